```python
import math
import jax, jax.numpy as jnp
from jax import lax
import numpy as np

D_MODEL = 2048
BATCH = 4
SEQ = 4096
DEPTH = 2

GRID_W = 64
CTX_LEN = 256
N_MIXERS = 2
N_CONV_LAYERS = (DEPTH + 1) // 2
N_MLSTM_LAYERS = DEPTH // 2
CONV_WIDTH = 3
MLSTM_HEADS = 8
MLSTM_DQK = D_MODEL // (2 * MLSTM_HEADS)
MLSTM_DV = D_MODEL // MLSTM_HEADS
MLSTM_CHUNK = 64
N_EXPERTS = 16
N_GROUPS = 4
EXPERTS_PER_GROUP = N_EXPERTS // N_GROUPS
TOP_K = 2
D_EXPERT = D_MODEL // 2
ALPHA = (2 * DEPTH) ** 0.25
BETA = (8 * DEPTH) ** -0.25
LN_EPS = 1e-5
HEAD_NORM_EPS = 1e-6

kernel_name = "hybrid_conv_mlstm_grouped_moe_deepnorm_dit"

F32 = jnp.float32


def _layer_norm(x, g, b):
    xf = x.astype(F32)
    mu = jnp.mean(xf, axis=-1, keepdims=True)
    var = jnp.mean(jnp.square(xf - mu), axis=-1, keepdims=True)
    return ((xf - mu) * lax.rsqrt(var + LN_EPS) * g + b).astype(x.dtype)


def _conv3(z, w, axis):
    n = z.shape[axis]
    pad = [(0, 0)] * z.ndim
    pad[axis] = (1, 1)
    zp = jnp.pad(z, pad)
    return (w[0] * lax.slice_in_dim(zp, 0, n, axis=axis)
            + w[1] * lax.slice_in_dim(zp, 1, n + 1, axis=axis)
            + w[2] * lax.slice_in_dim(zp, 2, n + 2, axis=axis))


def _conv_mixer_latent(h, w_in, w_conv, w_out):
    bsz, s, d = h.shape
    rows = s // GRID_W
    bg, cg, v = jnp.split(h @ w_in, 3, axis=-1)
    z = (cg * v).reshape(bsz, rows, GRID_W, d)
    half = d // 2
    z_row = _conv3(z[..., :half], w_conv[:, :half], axis=2)
    z_col = _conv3(z[..., half:], w_conv[:, half:], axis=1)
    z = jnp.concatenate([z_row, z_col], axis=-1).reshape(bsz, s, d)
    return (bg * z) @ w_out


def _conv_mixer_seq(h, w_in, w_conv, w_out):
    bg, cg, v = jnp.split(h @ w_in, 3, axis=-1)
    return (bg * _conv3(cg * v, w_conv, axis=1)) @ w_out


def _zero_state(bsz):
    return (jnp.zeros((bsz, MLSTM_HEADS, MLSTM_DV, MLSTM_DQK), F32),
            jnp.zeros((bsz, MLSTM_HEADS, MLSTM_DQK), F32),
            jnp.zeros((bsz, MLSTM_HEADS), F32))


def _mlstm_scan(q, k, v, i_pre, f_pre, state, with_outputs):
    bsz, s = q.shape[:2]
    nc = s // MLSTM_CHUNK

    def chunks(a):
        a = a.reshape((bsz, nc, MLSTM_CHUNK) + a.shape[2:])
        return jnp.swapaxes(jnp.moveaxis(a, 1, 0), 2, 3)

    qs = chunks(q.astype(F32) * (MLSTM_DQK ** -0.5))
    ks = chunks(k.astype(F32))
    vs = chunks(v.astype(F32))
    i_s = chunks(i_pre.astype(F32))
    lf_s = chunks(jax.nn.log_sigmoid(f_pre.astype(F32)))
    tril = jnp.tril(jnp.ones((MLSTM_CHUNK, MLSTM_CHUNK), dtype=bool))

    def step(carry, inp):
        c_st, n_st, m_st = carry
        qc, kc, vc, ic, lfc = inp
        b = jnp.cumsum(lfc, axis=-1)
        b_end = b[..., -1]
        w_log = b_end[..., None] - b + ic
        m_next = jnp.maximum(b_end + m_st, jnp.max(w_log, axis=-1))
        decay = jnp.exp(b_end + m_st - m_next)
        w = jnp.exp(w_log - m_next[..., None])
        c_next = decay[..., None, None] * c_st + jnp.einsum('bhl,bhlv,bhlk->bhvk', w, vc, kc)
        n_next = decay[..., None] * n_st + jnp.einsum('bhl,bhlk->bhk', w, kc)
        if not with_outputs:
            return (c_next, n_next, m_next), None
        d_log = jnp.where(tril, b[..., :, None] - b[..., None, :] + ic[..., None, :], -jnp.inf)
        a_inter = b + m_st[..., None]
        m_q = jnp.maximum(a_inter, jnp.max(d_log, axis=-1))
        p = jnp.exp(d_log - m_q[..., None]) * jnp.einsum('bhqk,bhsk->bhqs', qc, kc)
        inter = jnp.exp(a_inter - m_q)
        num = (jnp.einsum('bhqs,bhsv->bhqv', p, vc)
               + inter[..., None] * jnp.einsum('bhqk,bhvk->bhqv', qc, c_st))
        den = jnp.sum(p, axis=-1) + inter * jnp.einsum('bhqk,bhk->bhq', qc, n_st)
        h = num / jnp.maximum(jnp.abs(den), jnp.exp(-m_q))[..., None]
        return (c_next, n_next, m_next), h

    state, hs = lax.scan(step, state, (qs, ks, vs, i_s, lf_s))
    if not with_outputs:
        return state, None
    h = jnp.moveaxis(jnp.swapaxes(hs, 2, 3), 0, 1).reshape(bsz, s, MLSTM_HEADS, MLSTM_DV)
    return state, h


def _mlstm_bidir(q, k, v, gates, state_f, state_b, with_outputs):
    st_f, h_f = _mlstm_scan(q, k, v, gates[:, :, 0], gates[:, :, 1], state_f, with_outputs)
    fl = lambda a: jnp.flip(a, axis=1)
    st_b, h_b = _mlstm_scan(fl(q), fl(k), fl(v), fl(gates[:, :, 2]), fl(gates[:, :, 3]),
                            state_b, with_outputs)
    h = h_f + fl(h_b) if with_outputs else None
    return st_f, st_b, h


def _mlstm_project(h, w_in, w_gate, b_gate):
    bsz, s, _ = h.shape
    qk = MLSTM_HEADS * MLSTM_DQK
    vw = MLSTM_HEADS * MLSTM_DV
    u = h @ w_in
    q = u[..., :qk].reshape(bsz, s, MLSTM_HEADS, MLSTM_DQK)
    k = u[..., qk:2 * qk].reshape(bsz, s, MLSTM_HEADS, MLSTM_DQK)
    v = u[..., 2 * qk:2 * qk + vw].reshape(bsz, s, MLSTM_HEADS, MLSTM_DV)
    o = u[..., 2 * qk + vw:]
    gates = (h @ w_gate + b_gate).astype(F32).reshape(bsz, s, 4, MLSTM_HEADS)
    return q, k, v, o, gates


def _mlstm_out(h_sum, o, norm_g, w_out, dtype):
    bsz, s = h_sum.shape[:2]
    mu = jnp.mean(h_sum, axis=-1, keepdims=True)
    var = jnp.mean(jnp.square(h_sum - mu), axis=-1, keepdims=True)
    hn = ((h_sum - mu) * lax.rsqrt(var + HEAD_NORM_EPS)).reshape(bsz, s, MLSTM_HEADS * MLSTM_DV)
    y = (hn * norm_g * jax.nn.sigmoid(o.astype(F32))).astype(dtype)
    return y @ w_out


def _mlstm_mixer(hx, hc, w_in, w_gate, b_gate, norm_g, w_out, ctx_outputs):
    bsz = hx.shape[0]
    qc, kc, vc, oc, gc = _mlstm_project(hc, w_in, w_gate, b_gate)
    qx, kx, vx, ox, gx = _mlstm_project(hx, w_in, w_gate, b_gate)
    st_f, st_b, hc_sum = _mlstm_bidir(qc, kc, vc, gc, _zero_state(bsz), _zero_state(bsz), ctx_outputs)
    _, _, hx_sum = _mlstm_bidir(qx, kx, vx, gx, st_f, st_b, True)
    out_x = _mlstm_out(hx_sum, ox, norm_g, w_out, hx.dtype)
    out_c = _mlstm_out(hc_sum, oc, norm_g, w_out, hc.dtype) if ctx_outputs else None
    return out_x, out_c


def _moe(h, router_w, router_b, w_gate, w_up, w_down):
    shape = h.shape
    t = h.reshape(-1, shape[-1])
    s = jax.nn.sigmoid((t @ router_w).astype(F32))
    sel = s + router_b.astype(F32)
    grp = jnp.sum(lax.top_k(sel.reshape(-1, N_GROUPS, EXPERTS_PER_GROUP), 2)[0], axis=-1)
    g_best = jnp.argmax(grp, axis=-1)
    in_grp = (jnp.arange(N_EXPERTS) // EXPERTS_PER_GROUP)[None, :] == g_best[:, None]
    _, idx = lax.top_k(jnp.where(in_grp, sel, -jnp.inf), TOP_K)
    wts = jnp.take_along_axis(s, idx, axis=-1)
    wts = wts / jnp.sum(wts, axis=-1, keepdims=True)
    combine = jnp.sum(jax.nn.one_hot(idx, N_EXPERTS, dtype=F32) * wts[..., None], axis=1).astype(t.dtype)
    out = jnp.zeros_like(t)
    for e in range(N_EXPERTS):
        he = jax.nn.silu(t @ w_gate[e]) * (t @ w_up[e])
        out = out + combine[:, e:e + 1] * (he @ w_down[e])
    return out.reshape(shape)


def setup_inputs(seed: int = 0) -> dict:
    key = jax.random.key(seed)
    ks = jax.random.split(key, 24)
    nrm = lambda k, shape, scale: jax.random.normal(k, shape, F32) * scale
    d = D_MODEL
    h = MLSTM_HEADS
    fb = jnp.linspace(3.0, 6.0, h, dtype=F32)
    gate_bias_base = jnp.concatenate([jnp.zeros((h,), F32), fb, jnp.zeros((h,), F32), fb])
    ml_in_w = 2 * h * MLSTM_DQK + 2 * h * MLSTM_DV
    return {
        "x": nrm(ks[0], (BATCH, SEQ, d), 1.0),
        "c": nrm(ks[1], (BATCH, d), 1.0),
        "ctx": nrm(ks[2], (BATCH, CTX_LEN, d), 1.0),
        "c_ctx": nrm(ks[3], (d,), 1.0),
        "w_ada": nrm(ks[4], (DEPTH, d, 6 * d), 0.5 * d ** -0.5),
        "b_ada": nrm(ks[5], (DEPTH, 6 * d), 0.02),
        "ln_g": 1.0 + nrm(ks[6], (DEPTH, 2, d), 0.02),
        "ln_b": nrm(ks[7], (DEPTH, 2, d), 0.02),
        "conv_w_in": nrm(ks[8], (N_CONV_LAYERS, d, 3 * d), d ** -0.5),
        "conv_w": nrm(ks[9], (N_CONV_LAYERS, CONV_WIDTH, d), CONV_WIDTH ** -0.5),
        "conv_w_out": nrm(ks[10], (N_CONV_LAYERS, d, d), BETA * d ** -0.5),
        "ml_w_in": nrm(ks[11], (N_MLSTM_LAYERS, d, ml_in_w), d ** -0.5),
        "ml_w_gate": nrm(ks[12], (N_MLSTM_LAYERS, d, 4 * h), 0.5 * d ** -0.5),
        "ml_b_gate": gate_bias_base + nrm(ks[13], (N_MLSTM_LAYERS, 4 * h), 0.01),
        "ml_norm_g": 1.0 + nrm(ks[14], (N_MLSTM_LAYERS, h * MLSTM_DV), 0.02),
        "ml_w_out": nrm(ks[15], (N_MLSTM_LAYERS, h * MLSTM_DV, d), BETA * (h * MLSTM_DV) ** -0.5),
        "router_w": nrm(ks[16], (d, N_EXPERTS), d ** -0.5),
        "router_b": nrm(ks[17], (N_EXPERTS,), 0.01),
        "exp_w_gate": nrm(ks[18], (DEPTH, N_EXPERTS, d, D_EXPERT), d ** -0.5),
        "exp_w_up": nrm(ks[19], (DEPTH, N_EXPERTS, d, D_EXPERT), d ** -0.5),
        "exp_w_down": nrm(ks[20], (DEPTH, N_EXPERTS, D_EXPERT, d), BETA * D_EXPERT ** -0.5),
    }


def reference(x, c, ctx, c_ctx, w_ada, b_ada, ln_g, ln_b, conv_w_in, conv_w, conv_w_out,
              ml_w_in, ml_w_gate, ml_b_gate, ml_norm_g, ml_w_out, router_w, router_b,
              exp_w_gate, exp_w_up, exp_w_down):
    n_ctx = ctx.shape[1]
    for i in range(DEPTH):
        last = i == DEPTH - 1
        j = i // N_MIXERS
        mod_x = (jax.nn.silu(c) @ w_ada[i] + b_ada[i])[:, None, :]
        sh1, sc1, g1, sh2, sc2, g2 = jnp.split(mod_x, 6, axis=-1)
        mod_c = jax.nn.silu(c_ctx) @ w_ada[i] + b_ada[i]
        csh1, csc1, cg1, csh2, csc2, cg2 = jnp.split(mod_c, 6, axis=-1)

        hx = x * (1.0 + sc1) + sh1
        hc = ctx * (1.0 + csc1) + csh1
        if i % N_MIXERS == 0:
            mx = _conv_mixer_latent(hx, conv_w_in[j], conv_w[j], conv_w_out[j])
            mc = None if last else _conv_mixer_seq(hc, conv_w_in[j], conv_w[j], conv_w_out[j])
        else:
            mx, mc = _mlstm_mixer(hx, hc, ml_w_in[j], ml_w_gate[j], ml_b_gate[j],
                                  ml_norm_g[j], ml_w_out[j], not last)
        x = _layer_norm(ALPHA * x + g1 * mx, ln_g[i, 0], ln_b[i, 0])

        hx2 = x * (1.0 + sc2) + sh2
        if last:
            ex = _moe(hx2, router_w, router_b, exp_w_gate[i], exp_w_up[i], exp_w_down[i])
            x = _layer_norm(ALPHA * x + g2 * ex, ln_g[i, 1], ln_b[i, 1])
        else:
            ctx = _layer_norm(ALPHA * ctx + cg1 * mc, ln_g[i, 0], ln_b[i, 0])
            hc2 = ctx * (1.0 + csc2) + csh2
            e_all = _moe(jnp.concatenate([hc2, hx2], axis=1), router_w, router_b,
                         exp_w_gate[i], exp_w_up[i], exp_w_down[i])
            ctx = _layer_norm(ALPHA * ctx + cg2 * e_all[:, :n_ctx], ln_g[i, 1], ln_b[i, 1])
            x = _layer_norm(ALPHA * x + g2 * e_all[:, n_ctx:], ln_g[i, 1], ln_b[i, 1])
    return x
```

```python
import functools

import jax
import jax.numpy as jnp
from jax import lax
from jax.experimental import pallas as pl
from jax.experimental.pallas import tpu as pltpu

F32 = jnp.float32
BF16 = jnp.bfloat16
I32 = jnp.int32

GRID_W = 64
N_GROUPS = 4
TOP_K = 2
LN_EPS = 1e-5
HEAD_NORM_EPS = 1e-6

LANES = 128
SUBLANES = 8
VMEM_LIMIT_BYTES = 56 * 1024 * 1024

ROW_TILE = 256
PROJ_ROWS = 1024
MOE_ROWS = 256
ROUTE_COLS = 512
SCAN_CHUNK = 128
TBL_BLOCK = 1024


def _cparams(sem):
    return pltpu.CompilerParams(dimension_semantics=sem, vmem_limit_bytes=VMEM_LIMIT_BYTES)


def _layer_norm_rows(r, g, b):
    mu = jnp.mean(r, axis=-1, keepdims=True)
    c = r - mu
    var = jnp.mean(c * c, axis=-1, keepdims=True)
    return c * lax.rsqrt(var + LN_EPS) * g + b


def _ada_kernel(cc_ref, w_ref, b_ref, o_ref):
    a = cc_ref[...]
    a = (a * jax.nn.sigmoid(a)).astype(BF16)
    o_ref[0] = jnp.dot(a, w_ref[0].astype(BF16), preferred_element_type=F32) + b_ref[0]


def _ada(cc, w_ada, b_ada):
    depth, d, n6 = w_ada.shape
    tn = 1024
    return pl.pallas_call(
        _ada_kernel,
        grid=(depth, n6 // tn),
        in_specs=[
            pl.BlockSpec((SUBLANES, d), lambda l, j: (0, 0)),
            pl.BlockSpec((1, d, tn), lambda l, j: (l, 0, j)),
            pl.BlockSpec((1, 1, tn), lambda l, j: (l, 0, j)),
        ],
        out_specs=pl.BlockSpec((1, SUBLANES, tn), lambda l, j: (l, 0, j)),
        out_shape=jax.ShapeDtypeStruct((depth, SUBLANES, n6), F32),
        compiler_params=_cparams(("arbitrary", "arbitrary")),
        name="ada_mod",
    )(cc, w_ada, b_ada.reshape(depth, 1, n6))


def _mod_spec(chunk, d, tiles_per_batch, n_batch):
    return pl.BlockSpec(
        (1, 1, d),
        lambda i, *_: (jnp.minimum(i // tiles_per_batch, n_batch), 0, chunk))


def _conv_in_kernel(nxa, xa_ref, xb_ref, sh_ref, sc_ref, wb_ref, wc_ref, wv_ref,
                    bg_ref, z_ref, h_ref):
    i = pl.program_id(0)
    j = pl.program_id(1)

    @pl.when((j == 0) & (i < nxa))
    def _():
        h_ref[...] = (xa_ref[...] * (1.0 + sc_ref[0]) + sh_ref[0]).astype(BF16)

    @pl.when((j == 0) & (i >= nxa))
    def _():
        h_ref[...] = (xb_ref[...] * (1.0 + sc_ref[0]) + sh_ref[0]).astype(BF16)

    h = h_ref[...]
    bg = jnp.dot(h, wb_ref[...], preferred_element_type=F32)
    cg = jnp.dot(h, wc_ref[...], preferred_element_type=F32)
    v = jnp.dot(h, wv_ref[...], preferred_element_type=F32)
    bg_ref[...] = bg.astype(BF16)
    z_ref[...] = (cg * v).astype(BF16)


def _conv_in(x2d, c2d, mod, w_in_b, n_batch, seq):
    nx, d = x2d.shape
    nc = c2d.shape[0]
    bm = min(PROJ_ROWS, seq, nc)
    assert seq % bm == 0 and nc % bm == 0
    tn = 512 if d % 512 == 0 else d
    nj = d // tn
    nxa = nx // bm
    n_all = nx + nc
    tpb = seq // bm
    mspec = functools.partial(_mod_spec, d=d, tiles_per_batch=tpb, n_batch=n_batch)
    return pl.pallas_call(
        functools.partial(_conv_in_kernel, nxa),
        grid=(n_all // bm, nj),
        in_specs=[
            pl.BlockSpec((bm, d), lambda i, j: (jnp.minimum(i, nxa - 1), 0)),
            pl.BlockSpec((bm, d), lambda i, j: (jnp.maximum(i - nxa, 0), 0),
                         pipeline_mode=pl.Buffered(1)),
            mspec(0), mspec(1),
            pl.BlockSpec((d, tn), lambda i, j: (0, j)),
            pl.BlockSpec((d, tn), lambda i, j: (0, nj + j)),
            pl.BlockSpec((d, tn), lambda i, j: (0, 2 * nj + j)),
        ],
        out_specs=[
            pl.BlockSpec((bm, tn), lambda i, j: (i, j)),
            pl.BlockSpec((bm, tn), lambda i, j: (i, j)),
        ],
        out_shape=[jax.ShapeDtypeStruct((n_all, d), BF16),
                   jax.ShapeDtypeStruct((n_all, d), BF16)],
        scratch_shapes=[pltpu.VMEM((bm, d), BF16)],
        compiler_params=_cparams(("arbitrary", "arbitrary")),
        name="conv_in",
    )(x2d, c2d, mod, mod, w_in_b, w_in_b, w_in_b)


def _ml_in_kernel(x_ref, sh_ref, sc_ref, w_ref, wg_ref, bgate_ref, u_ref, g_ref, h_ref):
    j = pl.program_id(1)

    @pl.when(j == 0)
    def _():
        h = (x_ref[...] * (1.0 + sc_ref[0]) + sh_ref[0]).astype(BF16)
        h_ref[...] = h
        g_ref[...] = jnp.dot(h, wg_ref[...], preferred_element_type=F32) + bgate_ref[...]

    u_ref[...] = jnp.dot(h_ref[...], w_ref[...], preferred_element_type=F32).astype(BF16)


def _ml_in(xall, mod, w_in_b, w_gate_pad, b_gate_pad, n_batch, seq, n_ctx_rows):
    n_all, d = xall.shape
    nu = w_in_b.shape[1]
    bm = min(PROJ_ROWS, seq, n_ctx_rows)
    assert seq % bm == 0 and n_ctx_rows % bm == 0
    tn = 1024 if nu % 1024 == 0 else nu // 6
    gl = w_gate_pad.shape[1]
    mspec = functools.partial(_mod_spec, d=d, tiles_per_batch=seq // bm, n_batch=n_batch)
    return pl.pallas_call(
        _ml_in_kernel,
        grid=(n_all // bm, nu // tn),
        in_specs=[
            pl.BlockSpec((bm, d), lambda i, j: (i, 0)),
            mspec(0), mspec(1),
            pl.BlockSpec((d, tn), lambda i, j: (0, j)),
            pl.BlockSpec((d, gl), lambda i, j: (0, 0)),
            pl.BlockSpec((1, gl), lambda i, j: (0, 0)),
        ],
        out_specs=[
            pl.BlockSpec((bm, tn), lambda i, j: (i, j)),
            pl.BlockSpec((bm, gl), lambda i, j: (i, 0)),
        ],
        out_shape=[jax.ShapeDtypeStruct((n_all, nu), BF16),
                   jax.ShapeDtypeStruct((n_all, gl), F32)],
        scratch_shapes=[pltpu.VMEM((bm, d), BF16)],
        compiler_params=_cparams(("arbitrary", "arbitrary")),
        name="mlstm_in",
    )(xall, mod, mod, w_in_b, w_gate_pad, b_gate_pad)


def _scan_kernel(n_heads, dqk, dv, n_ctx_chunks, q_ref, k_ref, v_ref, g_ref, o_ref,
                 c_ref, n_ref, m_ref):
    d = pl.program_id(1)
    s = pl.program_id(2)
    L = q_ref.shape[0]
    scale = dqk ** -0.5

    @pl.when(s == 0)
    def _():
        c_ref[...] = jnp.zeros_like(c_ref)
        n_ref[...] = jnp.zeros_like(n_ref)
        m_ref[...] = jnp.zeros_like(m_ref)

    qi = lax.broadcasted_iota(I32, (L, L), 0)
    si = lax.broadcasted_iota(I32, (L, L), 1)
    fwd = d == 0
    mask = jnp.where(fwd, si - qi, qi - si) <= 0
    tri = mask.astype(F32)

    g = g_ref[...]
    lf = jax.nn.log_sigmoid(g)
    b_all = jnp.dot(tri, lf, precision=lax.Precision.HIGHEST, preferred_element_type=F32)
    b_end_row = jnp.where(fwd, b_all[L - 1:L, :], b_all[0:1, :])
    g_t = g.T
    b_t = b_all.T

    heads = range(n_heads)

    @pl.when(s >= n_ctx_chunks)
    def _():
        for h in heads:
            bq = b_all[:, n_heads + h:n_heads + h + 1]
            row = g_t[h:h + 1, :] - b_t[n_heads + h:n_heads + h + 1, :]
            dm = jnp.where(mask, bq + row, -jnp.inf)
            m_st = m_ref[h][0:1, 0:1]
            a_inter = bq + m_st
            m_q = jnp.maximum(a_inter, jnp.max(dm, axis=-1, keepdims=True))
            qh = q_ref[:, h * dqk:(h + 1) * dqk]
            kh = k_ref[:, h * dqk:(h + 1) * dqk]
            vh = v_ref[:, h * dv:(h + 1) * dv]
            sc = lax.dot_general(qh, kh, (((1,), (1,)), ((), ())),
                                 preferred_element_type=F32) * scale
            p = jnp.exp(dm - m_q) * sc
            inter = jnp.exp(a_inter - m_q)
            qc = lax.dot_general(qh, c_ref[h].astype(BF16), (((1,), (1,)), ((), ())),
                                 preferred_element_type=F32) * scale
            num = jnp.dot(p.astype(BF16), vh, preferred_element_type=F32) + inter * qc
            qn = jnp.sum(qh.astype(F32) * n_ref[h], axis=-1, keepdims=True) * scale
            den = jnp.sum(p, axis=-1, keepdims=True) + inter * qn
            o_ref[:, h * dv:(h + 1) * dv] = num / jnp.maximum(jnp.abs(den), jnp.exp(-m_q))

    for h in heads:
        bq = b_all[:, n_heads + h:n_heads + h + 1]
        i_col = g[:, h:h + 1]
        b_end = b_end_row[:, n_heads + h:n_heads + h + 1]
        m_st = m_ref[h][0:1, 0:1]
        wl = b_end - bq + i_col
        m_next = jnp.maximum(b_end + m_st, jnp.max(wl, axis=0, keepdims=True))
        decay = jnp.exp(b_end + m_st - m_next)
        w = jnp.exp(wl - m_next)
        kh = k_ref[:, h * dqk:(h + 1) * dqk]
        vh = v_ref[:, h * dv:(h + 1) * dv]
        vw = (vh.astype(F32) * w).astype(BF16)
        upd = lax.dot_general(vw, kh, (((0,), (0,)), ((), ())), preferred_element_type=F32)
        c_ref[h] = decay * c_ref[h] + upd
        n_ref[h] = decay * n_ref[h] + jnp.sum(kh.astype(F32) * w, axis=0, keepdims=True)
        m_ref[h] = jnp.broadcast_to(m_next, m_ref.shape[1:])


def _scan(u, gates, n_batch, seq, ctx_len, n_heads, dqk, dv):
    L = SCAN_CHUNK
    nx = n_batch * seq
    ncc = ctx_len // L
    nlc = seq // L
    assert ctx_len % L == 0 and seq % L == 0
    hq = n_heads * dqk
    hv = n_heads * dv
    assert hv == 2 * hq

    def row_blk(b, d, s):
        ctx = (nx + b * ctx_len) // L + jnp.where(d == 0, s, ncc - 1 - s)
        sl = s - ncc
        lat = (b * seq) // L + jnp.where(d == 0, sl, nlc - 1 - sl)
        return jnp.where(s < ncc, ctx, lat)

    def out_blk(b, d, s):
        sl = jnp.maximum(s - ncc, 0)
        return (b * seq) // L + jnp.where(d == 0, sl, nlc - 1 - sl)

    return pl.pallas_call(
        functools.partial(_scan_kernel, n_heads, dqk, dv, ncc),
        grid=(n_batch, 2, ncc + nlc),
        in_specs=[
            pl.BlockSpec((L, hq), lambda b, d, s: (row_blk(b, d, s), 0)),
            pl.BlockSpec((L, hq), lambda b, d, s: (row_blk(b, d, s), 1)),
            pl.BlockSpec((L, hv), lambda b, d, s: (row_blk(b, d, s), 1)),
            pl.BlockSpec((L, LANES), lambda b, d, s: (row_blk(b, d, s), d)),
        ],
        out_specs=pl.BlockSpec((None, L, hv), lambda b, d, s: (d, out_blk(b, d, s), 0)),
        out_shape=jax.ShapeDtypeStruct((2, nx, hv), F32),
        scratch_shapes=[pltpu.VMEM((n_heads, dv, dqk), F32),
                        pltpu.VMEM((n_heads, 1, dqk), F32),
                        pltpu.VMEM((n_heads, SUBLANES, LANES), F32)],
        compiler_params=_cparams(("arbitrary", "arbitrary", "arbitrary")),
        name="mlstm_scan",
    )(u, u, u, gates)


def _mix_epilogue(alpha, a_ref, xres, g1_ref, sh2_ref, sc2_ref, wout_ref, lng_ref, lnb_ref,
                  rwt_ref, x1_ref, hx2_ref, logt_ref):
    mx = jnp.dot(a_ref[...], wout_ref[...], preferred_element_type=F32)
    x1 = _layer_norm_rows(alpha * xres + g1_ref[0] * mx, lng_ref[...], lnb_ref[...])
    x1_ref[...] = x1
    hx2 = x1 * (1.0 + sc2_ref[0]) + sh2_ref[0]
    hx2_ref[...] = hx2
    logt_ref[...] = lax.dot_general(rwt_ref[...], hx2.astype(BF16), (((1,), (1,)), ((), ())),
                                    preferred_element_type=F32)


def _conv_out_kernel(alpha, nxa, tpb, z_ref, zp_ref, zn_ref, bg_ref, cw_ref, xa_ref, xb_ref,
                     g1_ref, sh2_ref, sc2_ref, wout_ref, lng_ref, lnb_ref, rwt_ref,
                     x1_ref, hx2_ref, logt_ref, a_ref):
    i = pl.program_id(0)
    tm, d = z_ref.shape
    half = d // 2
    is_ctx = i >= nxa
    cw = cw_ref[...]
    z = z_ref[...].astype(F32)
    bg = bg_ref[...].astype(F32)
    rows = lax.broadcasted_iota(I32, (tm, 1), 0)

    def conv1(zz, w3, period):
        pos = rows & (period - 1)
        prev = pltpu.roll(zz, 1, 0) * (pos != 0).astype(F32)
        nxt = pltpu.roll(zz, tm - 1, 0) * (pos != period - 1).astype(F32)
        return w3[0:1] * prev + w3[1:2] * zz + w3[2:3] * nxt

    period = jnp.where(is_ctx, tm, GRID_W)
    a_ref[:, :half] = (bg[:, :half] * conv1(z[:, :half], cw[:, :half], period)).astype(BF16)

    @pl.when(is_ctx)
    def _():
        a_ref[:, half:] = (bg[:, half:] * conv1(z[:, half:], cw[:, half:], tm)).astype(BF16)

    @pl.when(jnp.logical_not(is_ctx))
    def _():
        ti = i % tpb
        zh = z[:, half:]
        w3 = cw[:, half:]
        up = zp_ref[...].astype(F32) * (ti > 0).astype(F32)
        dn = zn_ref[...].astype(F32) * (ti < tpb - 1).astype(F32)
        prev = jnp.concatenate([up, zh[:tm - GRID_W]], axis=0)
        nxt = jnp.concatenate([zh[GRID_W:], dn], axis=0)
        y = w3[0:1] * prev + w3[1:2] * zh + w3[2:3] * nxt
        a_ref[:, half:] = (bg[:, half:] * y).astype(BF16)

    xres = jnp.where(is_ctx, xb_ref[...], xa_ref[...])
    _mix_epilogue(alpha, a_ref, xres, g1_ref, sh2_ref, sc2_ref, wout_ref, lng_ref, lnb_ref,
                  rwt_ref, x1_ref, hx2_ref, logt_ref)


def _ml_out_kernel(alpha, n_heads, dv, hf_ref, hb_ref, og_ref, ng_ref, x_ref,
                   g1_ref, sh2_ref, sc2_ref, wout_ref, lng_ref, lnb_ref, rwt_ref,
                   x1_ref, hx2_ref, logt_ref, a_ref):
    for h in range(n_heads):
        sl = slice(h * dv, (h + 1) * dv)
        hs = hf_ref[:, sl] + hb_ref[:, sl]
        mu = jnp.mean(hs, axis=-1, keepdims=True)
        c = hs - mu
        var = jnp.mean(c * c, axis=-1, keepdims=True)
        hn = c * lax.rsqrt(var + HEAD_NORM_EPS)
        gate = jax.nn.sigmoid(og_ref[:, sl].astype(F32))
        a_ref[:, sl] = (hn * ng_ref[:, sl] * gate).astype(BF16)
    _mix_epilogue(alpha, a_ref, x_ref[...], g1_ref, sh2_ref, sc2_ref, wout_ref, lng_ref,
                  lnb_ref, rwt_ref, x1_ref, hx2_ref, logt_ref)


def _mix_out_common(d, n_rows, n_exp, tpb, n_batch):
    tm = ROW_TILE
    mspec = functools.partial(_mod_spec, d=d, tiles_per_batch=tpb, n_batch=n_batch)
    const = lambda shape: pl.BlockSpec(shape, lambda i: (0,) * len(shape))
    in_specs = [mspec(2), mspec(3), mspec(4), const((d, d)), const((1, d)), const((1, d)),
                const((n_exp, d))]
    out_specs = [pl.BlockSpec((tm, d), lambda i: (i, 0)),
                 pl.BlockSpec((tm, d), lambda i: (i, 0)),
                 pl.BlockSpec((n_exp, tm), lambda i: (0, i))]
    out_shape = [jax.ShapeDtypeStruct((n_rows, d), F32),
                 jax.ShapeDtypeStruct((n_rows, d), F32),
                 jax.ShapeDtypeStruct((n_exp, n_rows), F32)]
    return in_specs, out_specs, out_shape


def _conv_out(alpha, z, bg, conv_w, x2d, c2d, mod, w_out_b, ln_g, ln_b, rwt_b, n_batch, seq,
              ctx_len):
    n_all, d = z.shape
    nx = x2d.shape[0]
    tm = ROW_TILE
    assert ctx_len == tm and seq % tm == 0 and tm % GRID_W == 0
    half = d // 2
    nxa = nx // tm
    tpb = seq // tm
    hpt = tm // GRID_W
    n_halo = n_all // GRID_W
    common_in, out_specs, out_shape = _mix_out_common(d, n_all, rwt_b.shape[0], tpb, n_batch)
    in_specs = [
        pl.BlockSpec((tm, d), lambda i: (i, 0)),
        pl.BlockSpec((GRID_W, half), lambda i: (jnp.maximum(i * hpt - 1, 0), 1)),
        pl.BlockSpec((GRID_W, half), lambda i: (jnp.minimum((i + 1) * hpt, n_halo - 1), 1)),
        pl.BlockSpec((tm, d), lambda i: (i, 0)),
        pl.BlockSpec((3, d), lambda i: (0, 0)),
        pl.BlockSpec((tm, d), lambda i: (jnp.minimum(i, nxa - 1), 0)),
        pl.BlockSpec((tm, d), lambda i: (jnp.maximum(i - nxa, 0), 0)),
    ] + common_in
    return pl.pallas_call(
        functools.partial(_conv_out_kernel, alpha, nxa, tpb),
        grid=(n_all // tm,),
        in_specs=in_specs,
        out_specs=out_specs,
        out_shape=out_shape,
        scratch_shapes=[pltpu.VMEM((tm, d), BF16)],
        compiler_params=_cparams(("arbitrary",)),
        name="conv_out",
    )(z, z, z, bg, conv_w, x2d, c2d, mod, mod, mod, w_out_b, ln_g, ln_b, rwt_b)


def _ml_out(alpha, hfb, u, norm_g, xall, mod, w_out_b, ln_g, ln_b, rwt_b, n_batch, seq,
            n_heads, dv):
    nx, d = hfb.shape[1:]
    tm = ROW_TILE
    tpb = seq // tm
    common_in, out_specs, out_shape = _mix_out_common(d, nx, rwt_b.shape[0], tpb, n_batch)
    o_blk = (u.shape[1] - d) // d
    in_specs = [
        pl.BlockSpec((None, tm, d), lambda i: (0, i, 0)),
        pl.BlockSpec((None, tm, d), lambda i: (1, i, 0)),
        pl.BlockSpec((tm, d), lambda i: (i, o_blk)),
        pl.BlockSpec((1, d), lambda i: (0, 0)),
        pl.BlockSpec((tm, d), lambda i: (i, 0)),
    ] + common_in
    return pl.pallas_call(
        functools.partial(_ml_out_kernel, alpha, n_heads, dv),
        grid=(nx // tm,),
        in_specs=in_specs,
        out_specs=out_specs,
        out_shape=out_shape,
        scratch_shapes=[pltpu.VMEM((tm, d), BF16)],
        compiler_params=_cparams(("arbitrary",)),
        name="mlstm_out",
    )(hfb, hfb, u, norm_g, xall, mod, mod, mod, w_out_b, ln_g, ln_b, rwt_b)


def _route_kernel(tile_rows, n_te, logt_ref, rb_ref, pos_ref, wcol_ref, te_ref, meta_ref,
                  carry_ref):
    ph = pl.program_id(0)
    i = pl.program_id(1)
    n_steps = pl.num_programs(1)
    n_exp, tr = logt_ref.shape
    epg = n_exp // N_GROUPS

    @pl.when((ph == 0) & (i == 0))
    def _():
        carry_ref[...] = jnp.zeros_like(carry_ref)

    s = jax.nn.sigmoid(logt_ref[...])
    sel = s + rb_ref[...]
    row = lax.broadcasted_iota(I32, (n_exp, tr), 0)
    member = row % epg
    group = row // epg

    def partner(x, k, idx, span, unit):
        wrapped = (idx + k) >= span
        up = pltpu.roll(x, n_exp - k * unit, 0)
        down = pltpu.roll(x, (span - k) * unit, 0)
        return jnp.where(wrapped, down, up), wrapped

    rank_in = jnp.zeros((n_exp, tr), F32)
    for k in range(1, epg):
        p, wrapped = partner(sel, k, member, epg, 1)
        beats = (p > sel) | ((p == sel) & wrapped)
        rank_in = rank_in + beats.astype(F32)
    top = (rank_in < TOP_K).astype(F32)
    gs = sel * top
    score = gs
    for k in range(1, epg):
        p, _ = partner(gs, k, member, epg, 1)
        score = score + p
    n_better = jnp.zeros((n_exp, tr), F32)
    for k in range(1, N_GROUPS):
        p, wrapped = partner(score, k, group, N_GROUPS, epg)
        beats = (p > score) | ((p == score) & wrapped)
        n_better = n_better + beats.astype(F32)
    best = n_better == 0.0
    m1 = (best & (rank_in == 0.0)).astype(F32)
    m2 = (best & (rank_in == 1.0)).astype(F32)
    s1 = jnp.sum(s * m1, axis=0, keepdims=True)
    s2 = jnp.sum(s * m2, axis=0, keepdims=True)
    den = s1 + s2
    oh = m1 + m2

    ti = lax.broadcasted_iota(I32, (tr, tr), 0)
    tj = lax.broadcasted_iota(I32, (tr, tr), 1)
    before = (ti < tj).astype(BF16)
    slot = jnp.dot(oh.astype(BF16), before, preferred_element_type=F32) + carry_ref[...]

    @pl.when(ph == 1)
    def _():
        pos_ref[0:1, :] = jnp.sum(m1 * slot, axis=0, keepdims=True).astype(I32)
        pos_ref[1:2, :] = jnp.sum(m2 * slot, axis=0, keepdims=True).astype(I32)
        w8 = jnp.concatenate([s1 / den, s2 / den, jnp.zeros((LANES - 2, tr), F32)], axis=0)
        wcol_ref[...] = w8.T

    carry_ref[...] = carry_ref[...] + jnp.sum(oh, axis=1, keepdims=True)

    @pl.when((ph == 0) & (i == n_steps - 1))
    def _():
        cnt = carry_ref[...]
        ntile = jnp.floor((cnt + (tile_rows - 1)) * (1.0 / tile_rows))
        er = lax.broadcasted_iota(I32, (n_exp, LANES), 0)
        el = lax.broadcasted_iota(I32, (n_exp, LANES), 1)
        eye = (er == el).astype(F32)
        nt_row = jnp.sum(ntile * eye, axis=0, keepdims=True)
        cnt_row = jnp.sum(cnt * eye, axis=0, keepdims=True)
        cum_excl = jnp.sum(nt_row * (el < er).astype(F32), axis=1, keepdims=True)
        off = cum_excl * tile_rows
        off_row = jnp.sum(off * eye, axis=0, keepdims=True)
        total = jnp.sum(nt_row, axis=1, keepdims=True)
        cum_incl = cum_excl + ntile
        tl = lax.broadcasted_iota(I32, (n_exp, n_te), 1).astype(F32)
        te = jnp.sum((cum_incl <= tl).astype(F32), axis=0, keepdims=True)
        te_ref[...] = jnp.minimum(te, n_exp - 1).astype(I32)
        meta_ref[...] = jnp.concatenate(
            [cnt_row, off_row, jnp.broadcast_to(total, (1, LANES)),
             jnp.zeros((SUBLANES - 3, LANES), F32)], axis=0).astype(I32)
        carry_ref[...] = off


def _route(logt, router_b, tile_rows):
    n_exp, n = logt.shape
    tr = ROUTE_COLS
    assert n % tr == 0
    n_te = 256
    assert (2 * n) // tile_rows + n_exp <= n_te
    return pl.pallas_call(
        functools.partial(_route_kernel, tile_rows, n_te),
        grid=(2, n // tr),
        in_specs=[pl.BlockSpec((n_exp, tr), lambda p, i: (0, i)),
                  pl.BlockSpec((n_exp, 1), lambda p, i: (0, 0))],
        out_specs=[pl.BlockSpec((2, tr), lambda p, i: (0, i * p)),
                   pl.BlockSpec((tr, LANES), lambda p, i: (i * p, 0)),
                   pl.BlockSpec((1, n_te), lambda p, i: (0, 0)),
                   pl.BlockSpec((SUBLANES, LANES), lambda p, i: (0, 0))],
        out_shape=[jax.ShapeDtypeStruct((2, n), I32),
                   jax.ShapeDtypeStruct((n, LANES), F32),
                   jax.ShapeDtypeStruct((1, n_te), I32),
                   jax.ShapeDtypeStruct((SUBLANES, LANES), I32)],
        scratch_shapes=[pltpu.VMEM((n_exp, 1), F32)],
        compiler_params=_cparams(("arbitrary", "arbitrary")),
        name="route",
    )(logt, router_b.reshape(n_exp, 1).astype(F32))


def _tbl_kernel(n_tok, n_exp, tile_rows, meta_ref, pos_ref, tbl_ref):
    i = pl.program_id(0)
    blk = pos_ref.shape[1]

    @pl.when(i == 0)
    def _():
        n_pad = 0
        for e in range(n_exp):
            cnt = meta_ref[e]
            off = meta_ref[n_exp + e]
            up = ((cnt + (tile_rows - 1)) // tile_rows) * tile_rows

            def fill(r, j):
                tbl_ref[off + r] = 2 * n_tok + j
                return j + 1

            n_pad = lax.fori_loop(cnt, up, fill, n_pad)

        def fill_tail(r, carry):
            tbl_ref[r] = 2 * n_tok
            return carry

        lax.fori_loop(2 * n_tok + n_pad, tbl_ref.shape[0], fill_tail, 0)

    base = i * blk

    def body(r, carry):
        tbl_ref[pos_ref[0, r]] = base + r
        return carry

    lax.fori_loop(0, blk, body, 0, unroll=8)


def _build_table(meta1d, pos, n_tok, n_exp, tile_rows, p_pad):
    blk = TBL_BLOCK
    n_pairs = 2 * n_tok
    assert n_pairs % blk == 0
    grid_spec = pltpu.PrefetchScalarGridSpec(
        num_scalar_prefetch=1,
        grid=(n_pairs // blk,),
        in_specs=[pl.BlockSpec((None, 1, blk), lambda i, m: (i, 0, 0), memory_space=pltpu.SMEM)],
        out_specs=pl.BlockSpec(memory_space=pltpu.SMEM),
    )
    return pl.pallas_call(
        functools.partial(_tbl_kernel, n_tok, n_exp, tile_rows),
        grid_spec=grid_spec,
        out_shape=jax.ShapeDtypeStruct((p_pad,), I32),
        compiler_params=_cparams(("arbitrary",)),
        name="slot_table",
    )(meta1d, pos.reshape(n_pairs // blk, 1, blk))


def _moe_kernel(n_tok, te_ref, nt_ref, tbl_ref, tbln_ref, hx_hbm, wg_ref, wu_ref, wd_ref,
                y_hbm, xbuf, ybuf, zbuf, gsem, ssem, zsem):
    t = pl.program_id(0)
    n = nt_ref[0]
    rows = tbl_ref.shape[1]
    slot = lax.rem(t, 2)

    def gather_copy(tref, r, sl):
        v = tref[0, r]
        src = jnp.where(v >= 2 * n_tok, 0, jnp.where(v >= n_tok, v - n_tok, v))
        return pltpu.make_async_copy(hx_hbm.at[pl.ds(src, 1), :],
                                     xbuf.at[sl, pl.ds(r, 1), :], gsem.at[sl])

    def scatter_copy(r, sl):
        v = tbl_ref[0, r]
        return pltpu.make_async_copy(ybuf.at[sl, pl.ds(r, 1), :],
                                     y_hbm.at[pl.ds(v, 1), :], ssem.at[sl])

    def each_row(fn):
        def body(r, carry):
            fn(r)
            return carry
        lax.fori_loop(0, rows, body, 0, unroll=8)

    @pl.when(t == 0)
    def _():
        zbuf[...] = jnp.zeros_like(zbuf)
        each_row(lambda r: gather_copy(tbl_ref, r, 0).start())

    @pl.when(t >= n)
    def _():
        dst = y_hbm.at[pl.ds(pl.multiple_of(t * rows, rows), rows), :]
        fill = pltpu.make_async_copy(zbuf, dst, zsem)
        fill.start()
        fill.wait()

    @pl.when(t < n)
    def _():
        each_row(lambda r: gather_copy(tbl_ref, r, slot).wait())

        @pl.when(t + 1 < n)
        def _():
            each_row(lambda r: gather_copy(tbln_ref, r, 1 - slot).start())

        xb = xbuf[slot].astype(BF16)
        g = jnp.dot(xb, wg_ref[0], preferred_element_type=F32)
        u = jnp.dot(xb, wu_ref[0], preferred_element_type=F32)
        h = (g * jax.nn.sigmoid(g) * u).astype(BF16)
        ybuf[slot] = jnp.dot(h, wd_ref[0], preferred_element_type=F32)

        @pl.when(t >= 1)
        def _():
            each_row(lambda r: scatter_copy(r, 1 - slot).wait())

        each_row(lambda r: scatter_copy(r, slot).start())

        @pl.when(t == n - 1)
        def _():
            each_row(lambda r: scatter_copy(r, slot).wait())


def _moe(te1d, nt1d, tbl, hx2, wg_b, wu_b, wd_b, n_tok, p_pad):
    tmm = MOE_ROWS
    n_exp, d, f = wg_b.shape
    t_max = (2 * n_tok) // tmm + n_exp
    last = lambda t, te, nt: jnp.minimum(t, nt[0] - 1)
    wspec = lambda shape: pl.BlockSpec(shape, lambda t, te, nt: (te[last(t, te, nt)], 0, 0))
    grid_spec = pltpu.PrefetchScalarGridSpec(
        num_scalar_prefetch=2,
        grid=(t_max,),
        in_specs=[
            pl.BlockSpec((None, 1, tmm), lambda t, te, nt: (last(t, te, nt), 0, 0),
                         memory_space=pltpu.SMEM),
            pl.BlockSpec((None, 1, tmm), lambda t, te, nt: (last(t + 1, te, nt), 0, 0),
                         memory_space=pltpu.SMEM),
            pl.BlockSpec(memory_space=pl.ANY),
            wspec((1, d, f)), wspec((1, d, f)), wspec((1, f, d)),
        ],
        out_specs=pl.BlockSpec(memory_space=pl.ANY),
        scratch_shapes=[pltpu.VMEM((2, tmm, d), F32), pltpu.VMEM((2, tmm, d), F32),
                        pltpu.VMEM((tmm, d), F32),
                        pltpu.SemaphoreType.DMA((2,)), pltpu.SemaphoreType.DMA((2,)),
                        pltpu.SemaphoreType.DMA(())],
    )
    tbl3 = tbl.reshape(p_pad // tmm, 1, tmm)
    return pl.pallas_call(
        functools.partial(_moe_kernel, n_tok),
        grid_spec=grid_spec,
        out_shape=jax.ShapeDtypeStruct((t_max * tmm, d), F32),
        compiler_params=_cparams(("arbitrary",)),
        name="moe_experts",
    )(te1d, nt1d, tbl3, tbl3, hx2, wg_b, wu_b, wd_b)


def _comb_kernel(alpha, x_ref, y0_ref, y1_ref, w_ref, g2_ref, lng_ref, lnb_ref, o_ref):
    w = w_ref[...]
    ex = w[:, 0:1] * y0_ref[...] + w[:, 1:2] * y1_ref[...]
    o_ref[...] = _layer_norm_rows(alpha * x_ref[...] + g2_ref[0] * ex, lng_ref[...], lnb_ref[...])


def _combine(alpha, x1, y, wcol, mod, ln_g, ln_b, n_tok, n_out, n_batch, seq):
    d = x1.shape[1]
    tm = ROW_TILE
    nblk = n_tok // tm
    mspec = functools.partial(_mod_spec, d=d, tiles_per_batch=seq // tm, n_batch=n_batch)
    return pl.pallas_call(
        functools.partial(_comb_kernel, alpha),
        grid=(n_out // tm,),
        in_specs=[
            pl.BlockSpec((tm, d), lambda i: (i, 0)),
            pl.BlockSpec((tm, d), lambda i: (i, 0)),
            pl.BlockSpec((tm, d), lambda i: (i + nblk, 0)),
            pl.BlockSpec((tm, LANES), lambda i: (i, 0)),
            mspec(5),
            pl.BlockSpec((1, d), lambda i: (0, 0)),
            pl.BlockSpec((1, d), lambda i: (0, 0)),
        ],
        out_specs=pl.BlockSpec((tm, d), lambda i: (i, 0)),
        out_shape=jax.ShapeDtypeStruct((n_out, d), F32),
        compiler_params=_cparams(("arbitrary",)),
        name="moe_combine",
    )(x1, y, y, wcol, mod, ln_g, ln_b)


def _moe_layer(alpha, x1, hx2, logt, mod, ln_g, ln_b, router_b, wg_b, wu_b, wd_b, n_out,
               n_batch, seq):
    n_tok = hx2.shape[0]
    n_exp = wg_b.shape[0]
    tmm = MOE_ROWS
    p_max = 2 * n_tok + n_exp * tmm
    p_pad = -(-p_max // TBL_BLOCK) * TBL_BLOCK
    pos, wcol, te, meta = _route(logt, router_b, tmm)
    meta1d = meta[:2, :n_exp].reshape(-1)
    tbl = _build_table(meta1d, pos, n_tok, n_exp, tmm, p_pad)
    y = _moe(te.reshape(-1), meta[2, :1], tbl, hx2, wg_b, wu_b, wd_b, n_tok, p_pad)
    return _combine(alpha, x1, y, wcol, mod, ln_g, ln_b, n_tok, n_out, n_batch, seq)


def kernel(x, c, ctx, c_ctx, w_ada, b_ada, ln_g, ln_b, conv_w_in, conv_w, conv_w_out, ml_w_in, ml_w_gate, ml_b_gate, ml_norm_g, ml_w_out, router_w, router_b, exp_w_gate, exp_w_up, exp_w_down):
    n_batch, seq, d = x.shape
    ctx_len = ctx.shape[1]
    depth = w_ada.shape[0]
    assert depth == 2, "layer 0 is the conv mixer, layer 1 the mLSTM mixer"
    alpha = (2 * depth) ** 0.25
    n_heads = ml_b_gate.shape[-1] // 4
    dqk = d // (2 * n_heads)
    dv = d // n_heads
    nx = n_batch * seq
    nc = n_batch * ctx_len
    assert n_batch < SUBLANES and 2 * n_heads <= LANES

    x2d = x.reshape(nx, d)
    c2d = ctx.reshape(nc, d)
    cc = jnp.zeros((SUBLANES, d), F32).at[:n_batch].set(c).at[n_batch].set(c_ctx)
    mod = _ada(cc, w_ada, b_ada).reshape(depth, SUBLANES, 1, 6 * d)
    rwt_b = router_w.T.astype(BF16)

    bg, z = _conv_in(x2d, c2d, mod[0], conv_w_in[0].astype(BF16), n_batch, seq)
    x1, hx2, logt = _conv_out(alpha, z, bg, conv_w[0], x2d, c2d, mod[0],
                              conv_w_out[0].astype(BF16), ln_g[0, 0:1], ln_b[0, 0:1], rwt_b,
                              n_batch, seq, ctx_len)
    xall = _moe_layer(alpha, x1, hx2, logt, mod[0], ln_g[0, 1:2], ln_b[0, 1:2], router_b,
                      exp_w_gate[0].astype(BF16), exp_w_up[0].astype(BF16),
                      exp_w_down[0].astype(BF16), nx + nc, n_batch, seq)

    w_gate = ml_w_gate[0]
    b_gate = ml_b_gate[0]
    h2 = 2 * n_heads
    w_gate_pad = (jnp.zeros((d, 2 * LANES), F32).at[:, :h2].set(w_gate[:, :h2])
                  .at[:, LANES:LANES + h2].set(w_gate[:, h2:])).astype(BF16)
    b_gate_pad = (jnp.zeros((1, 2 * LANES), F32).at[0, :h2].set(b_gate[:h2])
                  .at[0, LANES:LANES + h2].set(b_gate[h2:]))
    u, gates = _ml_in(xall, mod[1], ml_w_in[0].astype(BF16), w_gate_pad, b_gate_pad, n_batch,
                      seq, nc)
    hfb = _scan(u, gates, n_batch, seq, ctx_len, n_heads, dqk, dv)
    x1, hx2, logt = _ml_out(alpha, hfb, u, ml_norm_g[0:1], xall, mod[1],
                            ml_w_out[0].astype(BF16), ln_g[1, 0:1], ln_b[1, 0:1], rwt_b,
                            n_batch, seq, n_heads, dv)
    out = _moe_layer(alpha, x1, hx2, logt, mod[1], ln_g[1, 1:2], ln_b[1, 1:2], router_b,
                     exp_w_gate[1].astype(BF16), exp_w_up[1].astype(BF16),
                     exp_w_down[1].astype(BF16), nx, n_batch, seq)
    return out.reshape(n_batch, seq, d)
```

```python
import functools

import jax
import jax.numpy as jnp
from jax import lax
from jax.experimental import pallas as pl
from jax.experimental.pallas import tpu as pltpu

F32 = jnp.float32
BF16 = jnp.bfloat16
I32 = jnp.int32

GRID_W = 64
N_GROUPS = 4
TOP_K = 2
LN_EPS = 1e-5
HEAD_NORM_EPS = 1e-6

LANES = 128
SUBLANES = 8
VMEM_LIMIT_BYTES = 56 * 1024 * 1024

ROW_TILE = 256
PROJ_ROWS = 1024
MOE_ROWS = 256
ROUTE_COLS = 512
SCAN_CHUNK = 128
TBL_BLOCK = 1024
SLOT_SRC_BITS = 15
SLOT_SRC_MASK = (1 << SLOT_SRC_BITS) - 1
CAST_BLOCK_BYTES = 8 * 1024 * 1024


def _cparams(sem):
    return pltpu.CompilerParams(dimension_semantics=sem, vmem_limit_bytes=VMEM_LIMIT_BYTES)


def _layer_norm_rows(r, g, b):
    mu = jnp.mean(r, axis=-1, keepdims=True)
    c = r - mu
    var = jnp.mean(c * c, axis=-1, keepdims=True)
    return c * lax.rsqrt(var + LN_EPS) * g + b


def _ada_kernel(cc_ref, w_ref, b_ref, o_ref):
    a = cc_ref[...]
    a = (a * jax.nn.sigmoid(a)).astype(BF16)
    o_ref[0] = jnp.dot(a, w_ref[0].astype(BF16), preferred_element_type=F32) + b_ref[0]


def _ada(cc, w_ada, b_ada):
    depth, d, n6 = w_ada.shape
    tn = 1024
    return pl.pallas_call(
        _ada_kernel,
        grid=(depth, n6 // tn),
        in_specs=[
            pl.BlockSpec((SUBLANES, d), lambda l, j: (0, 0)),
            pl.BlockSpec((1, d, tn), lambda l, j: (l, 0, j)),
            pl.BlockSpec((1, 1, tn), lambda l, j: (l, 0, j)),
        ],
        out_specs=pl.BlockSpec((1, SUBLANES, tn), lambda l, j: (l, 0, j)),
        out_shape=jax.ShapeDtypeStruct((depth, SUBLANES, n6), F32),
        compiler_params=_cparams(("arbitrary", "arbitrary")),
        name="ada_mod",
    )(cc, w_ada, b_ada.reshape(depth, 1, n6))


def _cast_kernel(w_ref, o_ref):
    o_ref[...] = w_ref[...].astype(BF16)


def _to_bf16(w, layer):
    rows, cols = w.shape[-2:]
    w4 = w.reshape(w.shape[0], -1, rows, cols)
    m = w4.shape[1]
    rb = rows
    while rb * cols * 4 > CAST_BLOCK_BYTES and rb % (4 * SUBLANES) == 0:
        rb //= 2
    out = pl.pallas_call(
        _cast_kernel,
        grid=(m, rows // rb),
        in_specs=[pl.BlockSpec((None, None, rb, cols), lambda e, r: (layer, e, r, 0))],
        out_specs=pl.BlockSpec((None, rb, cols), lambda e, r: (e, r, 0)),
        out_shape=jax.ShapeDtypeStruct((m, rows, cols), BF16),
        compiler_params=_cparams(("arbitrary", "arbitrary")),
        name="to_bf16",
    )(w4)
    return out.reshape(w.shape[1:])


def _mod_spec(chunk, d, tiles_per_batch, n_batch):
    return pl.BlockSpec(
        (1, 1, d),
        lambda i, *_: (jnp.minimum(i // tiles_per_batch, n_batch), 0, chunk))


def _conv_in_kernel(nxa, xa_ref, xb_ref, sh_ref, sc_ref, wb_ref, wc_ref, wv_ref,
                    bg_ref, z_ref, h_ref):
    i = pl.program_id(0)
    j = pl.program_id(1)

    @pl.when((j == 0) & (i < nxa))
    def _():
        h_ref[...] = (xa_ref[...] * (1.0 + sc_ref[0]) + sh_ref[0]).astype(BF16)

    @pl.when((j == 0) & (i >= nxa))
    def _():
        h_ref[...] = (xb_ref[...] * (1.0 + sc_ref[0]) + sh_ref[0]).astype(BF16)

    h = h_ref[...]
    bg = jnp.dot(h, wb_ref[...], preferred_element_type=F32)
    cg = jnp.dot(h, wc_ref[...], preferred_element_type=F32)
    v = jnp.dot(h, wv_ref[...], preferred_element_type=F32)
    bg_ref[...] = bg.astype(BF16)
    z_ref[...] = (cg * v).astype(BF16)


def _conv_in(x2d, c2d, mod, w_in_b, n_batch, seq):
    nx, d = x2d.shape
    nc = c2d.shape[0]
    bm = min(PROJ_ROWS, seq, nc)
    assert seq % bm == 0 and nc % bm == 0
    tn = 512 if d % 512 == 0 else d
    nj = d // tn
    nxa = nx // bm
    n_all = nx + nc
    tpb = seq // bm
    mspec = functools.partial(_mod_spec, d=d, tiles_per_batch=tpb, n_batch=n_batch)
    return pl.pallas_call(
        functools.partial(_conv_in_kernel, nxa),
        grid=(n_all // bm, nj),
        in_specs=[
            pl.BlockSpec((bm, d), lambda i, j: (jnp.minimum(i, nxa - 1), 0)),
            pl.BlockSpec((bm, d), lambda i, j: (jnp.maximum(i - nxa, 0), 0),
                         pipeline_mode=pl.Buffered(1)),
            mspec(0), mspec(1),
            pl.BlockSpec((d, tn), lambda i, j: (0, j)),
            pl.BlockSpec((d, tn), lambda i, j: (0, nj + j)),
            pl.BlockSpec((d, tn), lambda i, j: (0, 2 * nj + j)),
        ],
        out_specs=[
            pl.BlockSpec((bm, tn), lambda i, j: (i, j)),
            pl.BlockSpec((bm, tn), lambda i, j: (i, j)),
        ],
        out_shape=[jax.ShapeDtypeStruct((n_all, d), BF16),
                   jax.ShapeDtypeStruct((n_all, d), BF16)],
        scratch_shapes=[pltpu.VMEM((bm, d), BF16)],
        compiler_params=_cparams(("arbitrary", "arbitrary")),
        name="conv_in",
    )(x2d, c2d, mod, mod, w_in_b, w_in_b, w_in_b)


def _ml_in_kernel(x_ref, sh_ref, sc_ref, w_ref, wg_ref, bgate_ref, u_ref, g_ref, h_ref):
    j = pl.program_id(1)

    @pl.when(j == 0)
    def _():
        h = (x_ref[...] * (1.0 + sc_ref[0]) + sh_ref[0]).astype(BF16)
        h_ref[...] = h
        g_ref[...] = jnp.dot(h, wg_ref[...], preferred_element_type=F32) + bgate_ref[...]

    u_ref[...] = jnp.dot(h_ref[...], w_ref[...], preferred_element_type=F32).astype(BF16)


def _ml_in(xall, mod, w_in_b, w_gate_pad, b_gate_pad, n_batch, seq, n_ctx_rows):
    n_all, d = xall.shape
    nu = w_in_b.shape[1]
    bm = min(PROJ_ROWS, seq, n_ctx_rows)
    assert seq % bm == 0 and n_ctx_rows % bm == 0
    tn = 1024 if nu % 1024 == 0 else nu // 6
    gl = w_gate_pad.shape[1]
    mspec = functools.partial(_mod_spec, d=d, tiles_per_batch=seq // bm, n_batch=n_batch)
    return pl.pallas_call(
        _ml_in_kernel,
        grid=(n_all // bm, nu // tn),
        in_specs=[
            pl.BlockSpec((bm, d), lambda i, j: (i, 0)),
            mspec(0), mspec(1),
            pl.BlockSpec((d, tn), lambda i, j: (0, j)),
            pl.BlockSpec((d, gl), lambda i, j: (0, 0)),
            pl.BlockSpec((1, gl), lambda i, j: (0, 0)),
        ],
        out_specs=[
            pl.BlockSpec((bm, tn), lambda i, j: (i, j)),
            pl.BlockSpec((bm, gl), lambda i, j: (i, 0)),
        ],
        out_shape=[jax.ShapeDtypeStruct((n_all, nu), BF16),
                   jax.ShapeDtypeStruct((n_all, gl), F32)],
        scratch_shapes=[pltpu.VMEM((bm, d), BF16)],
        compiler_params=_cparams(("arbitrary", "arbitrary")),
        name="mlstm_in",
    )(xall, mod, mod, w_in_b, w_gate_pad, b_gate_pad)


def _scan_kernel(n_heads, dqk, dv, n_ctx_chunks, q_ref, k_ref, v_ref, g_ref, o_ref,
                 c_ref, n_ref, m_ref):
    d = pl.program_id(1)
    s = pl.program_id(2)
    L = q_ref.shape[0]
    scale = dqk ** -0.5

    @pl.when(s == 0)
    def _():
        c_ref[...] = jnp.zeros_like(c_ref)
        n_ref[...] = jnp.zeros_like(n_ref)
        m_ref[...] = jnp.zeros_like(m_ref)

    qi = lax.broadcasted_iota(I32, (L, L), 0)
    si = lax.broadcasted_iota(I32, (L, L), 1)
    fwd = d == 0
    mask = jnp.where(fwd, si - qi, qi - si) <= 0
    tri = mask.astype(F32)

    g = g_ref[...]
    lf = jax.nn.log_sigmoid(g)
    b_all = jnp.dot(tri, lf, precision=lax.Precision.HIGHEST, preferred_element_type=F32)
    b_end_row = jnp.where(fwd, b_all[L - 1:L, :], b_all[0:1, :])
    g_t = g.T
    b_t = b_all.T

    heads = range(n_heads)

    @pl.when(s >= n_ctx_chunks)
    def _():
        for h in heads:
            bq = b_all[:, n_heads + h:n_heads + h + 1]
            row = g_t[h:h + 1, :] - b_t[n_heads + h:n_heads + h + 1, :]
            dm = jnp.where(mask, bq + row, -jnp.inf)
            m_st = m_ref[h][0:1, 0:1]
            a_inter = bq + m_st
            m_q = jnp.maximum(a_inter, jnp.max(dm, axis=-1, keepdims=True))
            qh = q_ref[:, h * dqk:(h + 1) * dqk]
            kh = k_ref[:, h * dqk:(h + 1) * dqk]
            vh = v_ref[:, h * dv:(h + 1) * dv]
            sc = lax.dot_general(qh, kh, (((1,), (1,)), ((), ())),
                                 preferred_element_type=F32) * scale
            p = jnp.exp(dm - m_q) * sc
            inter = jnp.exp(a_inter - m_q)
            qc = lax.dot_general(qh, c_ref[h].astype(BF16), (((1,), (1,)), ((), ())),
                                 preferred_element_type=F32) * scale
            num = jnp.dot(p.astype(BF16), vh, preferred_element_type=F32) + inter * qc
            qn = jnp.sum(qh.astype(F32) * n_ref[h], axis=-1, keepdims=True) * scale
            den = jnp.sum(p, axis=-1, keepdims=True) + inter * qn
            o_ref[:, h * dv:(h + 1) * dv] = num / jnp.maximum(jnp.abs(den), jnp.exp(-m_q))

    for h in heads:
        bq = b_all[:, n_heads + h:n_heads + h + 1]
        i_col = g[:, h:h + 1]
        b_end = b_end_row[:, n_heads + h:n_heads + h + 1]
        m_st = m_ref[h][0:1, 0:1]
        wl = b_end - bq + i_col
        m_next = jnp.maximum(b_end + m_st, jnp.max(wl, axis=0, keepdims=True))
        decay = jnp.exp(b_end + m_st - m_next)
        w = jnp.exp(wl - m_next)
        kh = k_ref[:, h * dqk:(h + 1) * dqk]
        vh = v_ref[:, h * dv:(h + 1) * dv]
        vw = (vh.astype(F32) * w).astype(BF16)
        upd = lax.dot_general(vw, kh, (((0,), (0,)), ((), ())), preferred_element_type=F32)
        c_ref[h] = decay * c_ref[h] + upd
        n_ref[h] = decay * n_ref[h] + jnp.sum(kh.astype(F32) * w, axis=0, keepdims=True)
        m_ref[h] = jnp.broadcast_to(m_next, m_ref.shape[1:])


def _scan(u, gates, n_batch, seq, ctx_len, n_heads, dqk, dv):
    L = SCAN_CHUNK
    nx = n_batch * seq
    ncc = ctx_len // L
    nlc = seq // L
    assert ctx_len % L == 0 and seq % L == 0
    hq = n_heads * dqk
    hv = n_heads * dv
    assert hv == 2 * hq

    def row_blk(b, d, s):
        ctx = (nx + b * ctx_len) // L + jnp.where(d == 0, s, ncc - 1 - s)
        sl = s - ncc
        lat = (b * seq) // L + jnp.where(d == 0, sl, nlc - 1 - sl)
        return jnp.where(s < ncc, ctx, lat)

    def out_blk(b, d, s):
        sl = jnp.maximum(s - ncc, 0)
        return (b * seq) // L + jnp.where(d == 0, sl, nlc - 1 - sl)

    return pl.pallas_call(
        functools.partial(_scan_kernel, n_heads, dqk, dv, ncc),
        grid=(n_batch, 2, ncc + nlc),
        in_specs=[
            pl.BlockSpec((L, hq), lambda b, d, s: (row_blk(b, d, s), 0)),
            pl.BlockSpec((L, hq), lambda b, d, s: (row_blk(b, d, s), 1)),
            pl.BlockSpec((L, hv), lambda b, d, s: (row_blk(b, d, s), 1)),
            pl.BlockSpec((L, LANES), lambda b, d, s: (row_blk(b, d, s), d)),
        ],
        out_specs=pl.BlockSpec((None, L, hv), lambda b, d, s: (d, out_blk(b, d, s), 0)),
        out_shape=jax.ShapeDtypeStruct((2, nx, hv), F32),
        scratch_shapes=[pltpu.VMEM((n_heads, dv, dqk), F32),
                        pltpu.VMEM((n_heads, 1, dqk), F32),
                        pltpu.VMEM((n_heads, SUBLANES, LANES), F32)],
        compiler_params=_cparams(("arbitrary", "arbitrary", "arbitrary")),
        name="mlstm_scan",
    )(u, u, u, gates)


def _mix_epilogue(alpha, a_ref, xres, g1_ref, sh2_ref, sc2_ref, wout_ref, lng_ref, lnb_ref,
                  rwt_ref, x1_ref, hx2_ref, logt_ref):
    mx = jnp.dot(a_ref[...], wout_ref[...], preferred_element_type=F32)
    x1 = _layer_norm_rows(alpha * xres + g1_ref[0] * mx, lng_ref[...], lnb_ref[...])
    x1_ref[...] = x1
    hx2 = x1 * (1.0 + sc2_ref[0]) + sh2_ref[0]
    hx2_ref[...] = hx2
    logt_ref[...] = lax.dot_general(rwt_ref[...], hx2.astype(BF16), (((1,), (1,)), ((), ())),
                                    preferred_element_type=F32)


def _conv_out_kernel(alpha, nxa, tpb, z_ref, zp_ref, zn_ref, bg_ref, cw_ref, xa_ref, xb_ref,
                     g1_ref, sh2_ref, sc2_ref, wout_ref, lng_ref, lnb_ref, rwt_ref,
                     x1_ref, hx2_ref, logt_ref, a_ref):
    i = pl.program_id(0)
    tm, d = z_ref.shape
    half = d // 2
    is_ctx = i >= nxa
    cw = cw_ref[...]
    z = z_ref[...].astype(F32)
    bg = bg_ref[...].astype(F32)
    rows = lax.broadcasted_iota(I32, (tm, 1), 0)

    def conv1(zz, w3, period):
        pos = rows & (period - 1)
        prev = pltpu.roll(zz, 1, 0) * (pos != 0).astype(F32)
        nxt = pltpu.roll(zz, tm - 1, 0) * (pos != period - 1).astype(F32)
        return w3[0:1] * prev + w3[1:2] * zz + w3[2:3] * nxt

    period = jnp.where(is_ctx, tm, GRID_W)
    a_ref[:, :half] = (bg[:, :half] * conv1(z[:, :half], cw[:, :half], period)).astype(BF16)

    @pl.when(is_ctx)
    def _():
        a_ref[:, half:] = (bg[:, half:] * conv1(z[:, half:], cw[:, half:], tm)).astype(BF16)

    @pl.when(jnp.logical_not(is_ctx))
    def _():
        ti = i % tpb
        zh = z[:, half:]
        w3 = cw[:, half:]
        up = zp_ref[...].astype(F32) * (ti > 0).astype(F32)
        dn = zn_ref[...].astype(F32) * (ti < tpb - 1).astype(F32)
        prev = jnp.concatenate([up, zh[:tm - GRID_W]], axis=0)
        nxt = jnp.concatenate([zh[GRID_W:], dn], axis=0)
        y = w3[0:1] * prev + w3[1:2] * zh + w3[2:3] * nxt
        a_ref[:, half:] = (bg[:, half:] * y).astype(BF16)

    xres = jnp.where(is_ctx, xb_ref[...], xa_ref[...])
    _mix_epilogue(alpha, a_ref, xres, g1_ref, sh2_ref, sc2_ref, wout_ref, lng_ref, lnb_ref,
                  rwt_ref, x1_ref, hx2_ref, logt_ref)


def _ml_out_kernel(alpha, n_heads, dv, hf_ref, hb_ref, og_ref, ng_ref, x_ref,
                   g1_ref, sh2_ref, sc2_ref, wout_ref, lng_ref, lnb_ref, rwt_ref,
                   x1_ref, hx2_ref, logt_ref, a_ref):
    for h in range(n_heads):
        sl = slice(h * dv, (h + 1) * dv)
        hs = hf_ref[:, sl] + hb_ref[:, sl]
        mu = jnp.mean(hs, axis=-1, keepdims=True)
        c = hs - mu
        var = jnp.mean(c * c, axis=-1, keepdims=True)
        hn = c * lax.rsqrt(var + HEAD_NORM_EPS)
        gate = jax.nn.sigmoid(og_ref[:, sl].astype(F32))
        a_ref[:, sl] = (hn * ng_ref[:, sl] * gate).astype(BF16)
    _mix_epilogue(alpha, a_ref, x_ref[...], g1_ref, sh2_ref, sc2_ref, wout_ref, lng_ref,
                  lnb_ref, rwt_ref, x1_ref, hx2_ref, logt_ref)


def _mix_out_common(d, n_rows, n_exp, tpb, n_batch):
    tm = ROW_TILE
    mspec = functools.partial(_mod_spec, d=d, tiles_per_batch=tpb, n_batch=n_batch)
    const = lambda shape: pl.BlockSpec(shape, lambda i: (0,) * len(shape))
    in_specs = [mspec(2), mspec(3), mspec(4), const((d, d)), const((1, d)), const((1, d)),
                const((n_exp, d))]
    out_specs = [pl.BlockSpec((tm, d), lambda i: (i, 0)),
                 pl.BlockSpec((tm, d), lambda i: (i, 0)),
                 pl.BlockSpec((n_exp, tm), lambda i: (0, i))]
    out_shape = [jax.ShapeDtypeStruct((n_rows, d), F32),
                 jax.ShapeDtypeStruct((n_rows, d), F32),
                 jax.ShapeDtypeStruct((n_exp, n_rows), F32)]
    return in_specs, out_specs, out_shape


def _conv_out(alpha, z, bg, conv_w, x2d, c2d, mod, w_out_b, ln_g, ln_b, rwt_b, n_batch, seq,
              ctx_len):
    n_all, d = z.shape
    nx = x2d.shape[0]
    tm = ROW_TILE
    assert ctx_len == tm and seq % tm == 0 and tm % GRID_W == 0
    half = d // 2
    nxa = nx // tm
    tpb = seq // tm
    hpt = tm // GRID_W
    n_halo = n_all // GRID_W
    common_in, out_specs, out_shape = _mix_out_common(d, n_all, rwt_b.shape[0], tpb, n_batch)
    in_specs = [
        pl.BlockSpec((tm, d), lambda i: (i, 0)),
        pl.BlockSpec((GRID_W, half), lambda i: (jnp.maximum(i * hpt - 1, 0), 1)),
        pl.BlockSpec((GRID_W, half), lambda i: (jnp.minimum((i + 1) * hpt, n_halo - 1), 1)),
        pl.BlockSpec((tm, d), lambda i: (i, 0)),
        pl.BlockSpec((3, d), lambda i: (0, 0)),
        pl.BlockSpec((tm, d), lambda i: (jnp.minimum(i, nxa - 1), 0)),
        pl.BlockSpec((tm, d), lambda i: (jnp.maximum(i - nxa, 0), 0)),
    ] + common_in
    return pl.pallas_call(
        functools.partial(_conv_out_kernel, alpha, nxa, tpb),
        grid=(n_all // tm,),
        in_specs=in_specs,
        out_specs=out_specs,
        out_shape=out_shape,
        scratch_shapes=[pltpu.VMEM((tm, d), BF16)],
        compiler_params=_cparams(("arbitrary",)),
        name="conv_out",
    )(z, z, z, bg, conv_w, x2d, c2d, mod, mod, mod, w_out_b, ln_g, ln_b, rwt_b)


def _ml_out(alpha, hfb, u, norm_g, xall, mod, w_out_b, ln_g, ln_b, rwt_b, n_batch, seq,
            n_heads, dv):
    nx, d = hfb.shape[1:]
    tm = ROW_TILE
    tpb = seq // tm
    common_in, out_specs, out_shape = _mix_out_common(d, nx, rwt_b.shape[0], tpb, n_batch)
    o_blk = (u.shape[1] - d) // d
    in_specs = [
        pl.BlockSpec((None, tm, d), lambda i: (0, i, 0)),
        pl.BlockSpec((None, tm, d), lambda i: (1, i, 0)),
        pl.BlockSpec((tm, d), lambda i: (i, o_blk)),
        pl.BlockSpec((1, d), lambda i: (0, 0)),
        pl.BlockSpec((tm, d), lambda i: (i, 0)),
    ] + common_in
    return pl.pallas_call(
        functools.partial(_ml_out_kernel, alpha, n_heads, dv),
        grid=(nx // tm,),
        in_specs=in_specs,
        out_specs=out_specs,
        out_shape=out_shape,
        scratch_shapes=[pltpu.VMEM((tm, d), BF16)],
        compiler_params=_cparams(("arbitrary",)),
        name="mlstm_out",
    )(hfb, hfb, u, norm_g, xall, mod, mod, mod, w_out_b, ln_g, ln_b, rwt_b)


def _route_kernel(tile_rows, n_te, logt_ref, rb_ref, pos_ref, wcol_ref, te_ref, meta_ref,
                  carry_ref):
    ph = pl.program_id(0)
    i = pl.program_id(1)
    n_steps = pl.num_programs(1)
    n_exp, tr = logt_ref.shape
    epg = n_exp // N_GROUPS

    @pl.when((ph == 0) & (i == 0))
    def _():
        carry_ref[...] = jnp.zeros_like(carry_ref)

    s = jax.nn.sigmoid(logt_ref[...])
    sel = s + rb_ref[...]
    row = lax.broadcasted_iota(I32, (n_exp, tr), 0)
    member = row % epg
    group = row // epg

    def partner(x, k, idx, span, unit):
        wrapped = (idx + k) >= span
        up = pltpu.roll(x, n_exp - k * unit, 0)
        down = pltpu.roll(x, (span - k) * unit, 0)
        return jnp.where(wrapped, down, up), wrapped

    rank_in = jnp.zeros((n_exp, tr), F32)
    for k in range(1, epg):
        p, wrapped = partner(sel, k, member, epg, 1)
        beats = (p > sel) | ((p == sel) & wrapped)
        rank_in = rank_in + beats.astype(F32)
    top = (rank_in < TOP_K).astype(F32)
    gs = sel * top
    score = gs
    for k in range(1, epg):
        p, _ = partner(gs, k, member, epg, 1)
        score = score + p
    n_better = jnp.zeros((n_exp, tr), F32)
    for k in range(1, N_GROUPS):
        p, wrapped = partner(score, k, group, N_GROUPS, epg)
        beats = (p > score) | ((p == score) & wrapped)
        n_better = n_better + beats.astype(F32)
    best = n_better == 0.0
    m1 = (best & (rank_in == 0.0)).astype(F32)
    m2 = (best & (rank_in == 1.0)).astype(F32)
    s1 = jnp.sum(s * m1, axis=0, keepdims=True)
    s2 = jnp.sum(s * m2, axis=0, keepdims=True)
    den = s1 + s2
    oh = m1 + m2

    ti = lax.broadcasted_iota(I32, (tr, tr), 0)
    tj = lax.broadcasted_iota(I32, (tr, tr), 1)
    before = (ti < tj).astype(BF16)
    slot = jnp.dot(oh.astype(BF16), before, preferred_element_type=F32) + carry_ref[...]

    @pl.when(ph == 1)
    def _():
        pos_ref[0:1, :] = jnp.sum(m1 * slot, axis=0, keepdims=True).astype(I32)
        pos_ref[1:2, :] = jnp.sum(m2 * slot, axis=0, keepdims=True).astype(I32)
        w8 = jnp.concatenate([s1 / den, s2 / den, jnp.zeros((LANES - 2, tr), F32)], axis=0)
        wcol_ref[...] = w8.T

    carry_ref[...] = carry_ref[...] + jnp.sum(oh, axis=1, keepdims=True)

    @pl.when((ph == 0) & (i == n_steps - 1))
    def _():
        cnt = carry_ref[...]
        ntile = jnp.floor((cnt + (tile_rows - 1)) * (1.0 / tile_rows))
        er = lax.broadcasted_iota(I32, (n_exp, LANES), 0)
        el = lax.broadcasted_iota(I32, (n_exp, LANES), 1)
        eye = (er == el).astype(F32)
        nt_row = jnp.sum(ntile * eye, axis=0, keepdims=True)
        cnt_row = jnp.sum(cnt * eye, axis=0, keepdims=True)
        cum_excl = jnp.sum(nt_row * (el < er).astype(F32), axis=1, keepdims=True)
        off = cum_excl * tile_rows
        off_row = jnp.sum(off * eye, axis=0, keepdims=True)
        total = jnp.sum(nt_row, axis=1, keepdims=True)
        cum_incl = cum_excl + ntile
        tl = lax.broadcasted_iota(I32, (n_exp, n_te), 1).astype(F32)
        te = jnp.sum((cum_incl <= tl).astype(F32), axis=0, keepdims=True)
        te_ref[...] = jnp.minimum(te, n_exp - 1).astype(I32)
        meta_ref[...] = jnp.concatenate(
            [cnt_row, off_row, jnp.broadcast_to(total, (1, LANES)),
             jnp.zeros((SUBLANES - 3, LANES), F32)], axis=0).astype(I32)
        carry_ref[...] = off


def _route(logt, router_b, tile_rows):
    n_exp, n = logt.shape
    tr = ROUTE_COLS
    assert n % tr == 0
    n_te = 256
    assert (2 * n) // tile_rows + n_exp <= n_te
    return pl.pallas_call(
        functools.partial(_route_kernel, tile_rows, n_te),
        grid=(2, n // tr),
        in_specs=[pl.BlockSpec((n_exp, tr), lambda p, i: (0, i)),
                  pl.BlockSpec((n_exp, 1), lambda p, i: (0, 0))],
        out_specs=[pl.BlockSpec((2, tr), lambda p, i: (0, i * p)),
                   pl.BlockSpec((tr, LANES), lambda p, i: (i * p, 0)),
                   pl.BlockSpec((1, n_te), lambda p, i: (0, 0)),
                   pl.BlockSpec((SUBLANES, LANES), lambda p, i: (0, 0))],
        out_shape=[jax.ShapeDtypeStruct((2, n), I32),
                   jax.ShapeDtypeStruct((n, LANES), F32),
                   jax.ShapeDtypeStruct((1, n_te), I32),
                   jax.ShapeDtypeStruct((SUBLANES, LANES), I32)],
        scratch_shapes=[pltpu.VMEM((n_exp, 1), F32)],
        compiler_params=_cparams(("arbitrary", "arbitrary")),
        name="route",
    )(logt, router_b.reshape(n_exp, 1).astype(F32))


def _tbl_kernel(n_tok, n_exp, tile_rows, meta_ref, pos_ref, tbl_ref):
    i = pl.program_id(0)
    blk = pos_ref.shape[1]

    @pl.when(i == 0)
    def _():
        n_pad = 0
        for e in range(n_exp):
            cnt = meta_ref[e]
            off = meta_ref[n_exp + e]
            up = ((cnt + (tile_rows - 1)) // tile_rows) * tile_rows

            def fill(r, j):
                tbl_ref[off + r] = (2 * n_tok + j) << SLOT_SRC_BITS
                return j + 1

            n_pad = lax.fori_loop(cnt, up, fill, n_pad)

        def fill_tail(r, carry):
            tbl_ref[r] = (2 * n_tok) << SLOT_SRC_BITS
            return carry

        lax.fori_loop(2 * n_tok + n_pad, tbl_ref.shape[0], fill_tail, 0)

    base = i * blk

    def body(r, carry):
        p = base + r
        tok = jnp.where(p >= n_tok, p - n_tok, p)
        tbl_ref[pos_ref[0, r]] = (p << SLOT_SRC_BITS) | tok
        return carry

    lax.fori_loop(0, blk, body, 0, unroll=8)


def _build_table(meta1d, pos, n_tok, n_exp, tile_rows, p_pad):
    blk = TBL_BLOCK
    n_pairs = 2 * n_tok
    assert n_pairs % blk == 0
    grid_spec = pltpu.PrefetchScalarGridSpec(
        num_scalar_prefetch=1,
        grid=(n_pairs // blk,),
        in_specs=[pl.BlockSpec((None, 1, blk), lambda i, m: (i, 0, 0), memory_space=pltpu.SMEM)],
        out_specs=pl.BlockSpec(memory_space=pltpu.SMEM),
    )
    return pl.pallas_call(
        functools.partial(_tbl_kernel, n_tok, n_exp, tile_rows),
        grid_spec=grid_spec,
        out_shape=jax.ShapeDtypeStruct((p_pad,), I32),
        compiler_params=_cparams(("arbitrary",)),
        name="slot_table",
    )(meta1d, pos.reshape(n_pairs // blk, 1, blk))


def _moe_kernel(te_ref, nt_ref, tbl_ref, tbln_ref, tblp_ref, hx_hbm, wg_ref, wu_ref, wd_ref,
                y_hbm, xbuf, ybuf, zbuf, gsem, ssem, zsem):
    t = pl.program_id(0)
    n = nt_ref[0]
    rows = tbl_ref.shape[1]
    slot = lax.rem(t, 2)
    other = 1 - slot

    def start_gather(tref, sl):
        for r in range(rows):
            src = tref[0, r] & SLOT_SRC_MASK
            pltpu.make_async_copy(hx_hbm.at[pl.ds(src, 1), :],
                                  xbuf.at[sl, pl.ds(r, 1), :], gsem.at[sl]).start()

    def start_scatter(tref, sl):
        for r in range(rows):
            dst = tref[0, r] >> SLOT_SRC_BITS
            pltpu.make_async_copy(ybuf.at[sl, pl.ds(r, 1), :],
                                  y_hbm.at[pl.ds(dst, 1), :], ssem.at[sl]).start()

    def wait_gather(sl):
        pltpu.make_async_copy(xbuf.at[sl], xbuf.at[sl], gsem.at[sl]).wait()

    def wait_scatter(sl):
        pltpu.make_async_copy(ybuf.at[sl], ybuf.at[sl], ssem.at[sl]).wait()

    @pl.when(t == 0)
    def _():
        zbuf[...] = jnp.zeros_like(zbuf)
        start_gather(tbl_ref, 0)

    @pl.when(t >= n)
    def _():
        dst = y_hbm.at[pl.ds(pl.multiple_of(t * rows, rows), rows), :]
        fill = pltpu.make_async_copy(zbuf, dst, zsem)
        fill.start()
        fill.wait()

    @pl.when(t < n)
    def _():
        wait_gather(slot)

        @pl.when(t >= 2)
        def _():
            wait_scatter(slot)

        start_gather(tbln_ref, other)

        @pl.when(t >= 1)
        def _():
            start_scatter(tblp_ref, other)

        xb = xbuf[slot].astype(BF16)
        g = jnp.dot(xb, wg_ref[0], preferred_element_type=F32)
        u = jnp.dot(xb, wu_ref[0], preferred_element_type=F32)
        h = (g * jax.nn.sigmoid(g) * u).astype(BF16)
        ybuf[slot] = jnp.dot(h, wd_ref[0], preferred_element_type=F32)

        @pl.when(t == n - 1)
        def _():
            start_scatter(tbl_ref, slot)
            wait_gather(other)

            @pl.when(t >= 1)
            def _():
                wait_scatter(other)

            wait_scatter(slot)


def _moe(te1d, nt1d, tbl, hx2, wg_b, wu_b, wd_b, n_tok, p_pad):
    tmm = MOE_ROWS
    n_exp, d, f = wg_b.shape
    t_max = (2 * n_tok) // tmm + n_exp
    last = lambda t, te, nt: jnp.minimum(t, nt[0] - 1)
    wspec = lambda shape: pl.BlockSpec(shape, lambda t, te, nt: (te[last(t, te, nt)], 0, 0))
    grid_spec = pltpu.PrefetchScalarGridSpec(
        num_scalar_prefetch=2,
        grid=(t_max,),
        in_specs=[
            pl.BlockSpec((None, 1, tmm), lambda t, te, nt: (last(t, te, nt), 0, 0),
                         memory_space=pltpu.SMEM),
            pl.BlockSpec((None, 1, tmm), lambda t, te, nt: (last(t + 1, te, nt), 0, 0),
                         memory_space=pltpu.SMEM),
            pl.BlockSpec((None, 1, tmm), lambda t, te, nt: (jnp.maximum(last(t, te, nt) - 1, 0), 0, 0),
                         memory_space=pltpu.SMEM),
            pl.BlockSpec(memory_space=pl.ANY),
            wspec((1, d, f)), wspec((1, d, f)), wspec((1, f, d)),
        ],
        out_specs=pl.BlockSpec(memory_space=pl.ANY),
        scratch_shapes=[pltpu.VMEM((2, tmm, d), F32), pltpu.VMEM((2, tmm, d), F32),
                        pltpu.VMEM((tmm, d), F32),
                        pltpu.SemaphoreType.DMA((2,)), pltpu.SemaphoreType.DMA((2,)),
                        pltpu.SemaphoreType.DMA(())],
    )
    tbl3 = tbl.reshape(p_pad // tmm, 1, tmm)
    return pl.pallas_call(
        _moe_kernel,
        grid_spec=grid_spec,
        out_shape=jax.ShapeDtypeStruct((t_max * tmm, d), F32),
        compiler_params=_cparams(("arbitrary",)),
        name="moe_experts",
    )(te1d, nt1d, tbl3, tbl3, tbl3, hx2, wg_b, wu_b, wd_b)


def _comb_kernel(alpha, x_ref, y0_ref, y1_ref, w_ref, g2_ref, lng_ref, lnb_ref, o_ref):
    w = w_ref[...]
    ex = w[:, 0:1] * y0_ref[...] + w[:, 1:2] * y1_ref[...]
    o_ref[...] = _layer_norm_rows(alpha * x_ref[...] + g2_ref[0] * ex, lng_ref[...], lnb_ref[...])


def _combine(alpha, x1, y, wcol, mod, ln_g, ln_b, n_tok, n_out, n_batch, seq):
    d = x1.shape[1]
    tm = ROW_TILE
    nblk = n_tok // tm
    mspec = functools.partial(_mod_spec, d=d, tiles_per_batch=seq // tm, n_batch=n_batch)
    return pl.pallas_call(
        functools.partial(_comb_kernel, alpha),
        grid=(n_out // tm,),
        in_specs=[
            pl.BlockSpec((tm, d), lambda i: (i, 0)),
            pl.BlockSpec((tm, d), lambda i: (i, 0)),
            pl.BlockSpec((tm, d), lambda i: (i + nblk, 0)),
            pl.BlockSpec((tm, LANES), lambda i: (i, 0)),
            mspec(5),
            pl.BlockSpec((1, d), lambda i: (0, 0)),
            pl.BlockSpec((1, d), lambda i: (0, 0)),
        ],
        out_specs=pl.BlockSpec((tm, d), lambda i: (i, 0)),
        out_shape=jax.ShapeDtypeStruct((n_out, d), F32),
        compiler_params=_cparams(("arbitrary",)),
        name="moe_combine",
    )(x1, y, y, wcol, mod, ln_g, ln_b)


def _moe_layer(alpha, x1, hx2, logt, mod, ln_g, ln_b, router_b, wg_b, wu_b, wd_b, n_out,
               n_batch, seq):
    n_tok = hx2.shape[0]
    n_exp = wg_b.shape[0]
    tmm = MOE_ROWS
    p_max = 2 * n_tok + n_exp * tmm
    p_pad = -(-p_max // TBL_BLOCK) * TBL_BLOCK
    assert n_tok <= 1 << SLOT_SRC_BITS and p_max < 1 << (31 - SLOT_SRC_BITS)
    pos, wcol, te, meta = _route(logt, router_b, tmm)
    meta1d = meta[:2, :n_exp].reshape(-1)
    tbl = _build_table(meta1d, pos, n_tok, n_exp, tmm, p_pad)
    y = _moe(te.reshape(-1), meta[2, :1], tbl, hx2, wg_b, wu_b, wd_b, n_tok, p_pad)
    return _combine(alpha, x1, y, wcol, mod, ln_g, ln_b, n_tok, n_out, n_batch, seq)


def kernel(x, c, ctx, c_ctx, w_ada, b_ada, ln_g, ln_b, conv_w_in, conv_w, conv_w_out, ml_w_in, ml_w_gate, ml_b_gate, ml_norm_g, ml_w_out, router_w, router_b, exp_w_gate, exp_w_up, exp_w_down):
    n_batch, seq, d = x.shape
    ctx_len = ctx.shape[1]
    depth = w_ada.shape[0]
    assert depth == 2, "layer 0 is the conv mixer, layer 1 the mLSTM mixer"
    alpha = (2 * depth) ** 0.25
    n_heads = ml_b_gate.shape[-1] // 4
    dqk = d // (2 * n_heads)
    dv = d // n_heads
    nx = n_batch * seq
    nc = n_batch * ctx_len
    assert n_batch < SUBLANES and 2 * n_heads <= LANES

    x2d = x.reshape(nx, d)
    c2d = ctx.reshape(nc, d)
    cc = jnp.zeros((SUBLANES, d), F32).at[:n_batch].set(c).at[n_batch].set(c_ctx)
    mod = _ada(cc, w_ada, b_ada).reshape(depth, SUBLANES, 1, 6 * d)
    rwt_b = router_w.T.astype(BF16)

    bg, z = _conv_in(x2d, c2d, mod[0], _to_bf16(conv_w_in, 0), n_batch, seq)
    x1, hx2, logt = _conv_out(alpha, z, bg, conv_w[0], x2d, c2d, mod[0],
                              _to_bf16(conv_w_out, 0), ln_g[0, 0:1], ln_b[0, 0:1], rwt_b,
                              n_batch, seq, ctx_len)
    xall = _moe_layer(alpha, x1, hx2, logt, mod[0], ln_g[0, 1:2], ln_b[0, 1:2], router_b,
                      _to_bf16(exp_w_gate, 0), _to_bf16(exp_w_up, 0), _to_bf16(exp_w_down, 0),
                      nx + nc, n_batch, seq)

    w_gate = ml_w_gate[0]
    b_gate = ml_b_gate[0]
    h2 = 2 * n_heads
    w_gate_pad = (jnp.zeros((d, 2 * LANES), F32).at[:, :h2].set(w_gate[:, :h2])
                  .at[:, LANES:LANES + h2].set(w_gate[:, h2:])).astype(BF16)
    b_gate_pad = (jnp.zeros((1, 2 * LANES), F32).at[0, :h2].set(b_gate[:h2])
                  .at[0, LANES:LANES + h2].set(b_gate[h2:]))
    u, gates = _ml_in(xall, mod[1], _to_bf16(ml_w_in, 0), w_gate_pad, b_gate_pad, n_batch,
                      seq, nc)
    hfb = _scan(u, gates, n_batch, seq, ctx_len, n_heads, dqk, dv)
    x1, hx2, logt = _ml_out(alpha, hfb, u, ml_norm_g[0:1], xall, mod[1],
                            _to_bf16(ml_w_out, 0), ln_g[1, 0:1], ln_b[1, 0:1], rwt_b,
                            n_batch, seq, n_heads, dv)
    out = _moe_layer(alpha, x1, hx2, logt, mod[1], ln_g[1, 1:2], ln_b[1, 1:2], router_b,
                     _to_bf16(exp_w_gate, 1), _to_bf16(exp_w_up, 1), _to_bf16(exp_w_down, 1),
                     nx, n_batch, seq)
    return out.reshape(n_batch, seq, d)
```

```python
import functools

import jax
import jax.numpy as jnp
from jax import lax
from jax.experimental import pallas as pl
from jax.experimental.pallas import tpu as pltpu

F32 = jnp.float32
BF16 = jnp.bfloat16
I32 = jnp.int32

GRID_W = 64
N_GROUPS = 4
TOP_K = 2
LN_EPS = 1e-5
HEAD_NORM_EPS = 1e-6

LANES = 128
SUBLANES = 8
VMEM_LIMIT_BYTES = 56 * 1024 * 1024

ROW_TILE = 256
MIX_SUB_ROWS = 128
PROJ_ROWS = 1024
MOE_ROWS = 256
ROUTE_COLS = 512
SCAN_CHUNK = 128
TBL_BLOCK = 1024
SLOT_SRC_BITS = 15
SLOT_SRC_MASK = (1 << SLOT_SRC_BITS) - 1
CAST_BLOCK_BYTES = 8 * 1024 * 1024


def _cparams(sem):
    return pltpu.CompilerParams(dimension_semantics=sem, vmem_limit_bytes=VMEM_LIMIT_BYTES)


def _layer_norm_rows(r, g, b):
    mu = jnp.mean(r, axis=-1, keepdims=True)
    c = r - mu
    var = jnp.mean(c * c, axis=-1, keepdims=True)
    return c * lax.rsqrt(var + LN_EPS) * g + b


def _ada_kernel(cc_ref, w_ref, b_ref, o_ref):
    a = cc_ref[...]
    a = (a * jax.nn.sigmoid(a)).astype(BF16)
    o_ref[0] = jnp.dot(a, w_ref[0].astype(BF16), preferred_element_type=F32) + b_ref[0]


def _ada(cc, w_ada, b_ada):
    depth, d, n6 = w_ada.shape
    tn = 1024
    return pl.pallas_call(
        _ada_kernel,
        grid=(depth, n6 // tn),
        in_specs=[
            pl.BlockSpec((SUBLANES, d), lambda l, j: (0, 0)),
            pl.BlockSpec((1, d, tn), lambda l, j: (l, 0, j)),
            pl.BlockSpec((1, 1, tn), lambda l, j: (l, 0, j)),
        ],
        out_specs=pl.BlockSpec((1, SUBLANES, tn), lambda l, j: (l, 0, j)),
        out_shape=jax.ShapeDtypeStruct((depth, SUBLANES, n6), F32),
        compiler_params=_cparams(("arbitrary", "arbitrary")),
        name="ada_mod",
    )(cc, w_ada, b_ada.reshape(depth, 1, n6))


def _cast_kernel(w_ref, o_ref):
    o_ref[...] = w_ref[...].astype(BF16)


def _to_bf16(w, layer):
    rows, cols = w.shape[-2:]
    w4 = w.reshape(w.shape[0], -1, rows, cols)
    m = w4.shape[1]
    rb = rows
    while rb * cols * 4 > CAST_BLOCK_BYTES and rb % (4 * SUBLANES) == 0:
        rb //= 2
    out = pl.pallas_call(
        _cast_kernel,
        grid=(m, rows // rb),
        in_specs=[pl.BlockSpec((None, None, rb, cols), lambda e, r: (layer, e, r, 0))],
        out_specs=pl.BlockSpec((None, rb, cols), lambda e, r: (e, r, 0)),
        out_shape=jax.ShapeDtypeStruct((m, rows, cols), BF16),
        compiler_params=_cparams(("arbitrary", "arbitrary")),
        name="to_bf16",
    )(w4)
    return out.reshape(w.shape[1:])


def _mod_spec(chunk, d, tiles_per_batch, n_batch):
    return pl.BlockSpec(
        (1, 1, d),
        lambda i, *_: (jnp.minimum(i // tiles_per_batch, n_batch), 0, chunk))


def _conv_in_kernel(nxa, xa_ref, xb_ref, sh_ref, sc_ref, wb_ref, wc_ref, wv_ref,
                    bg_ref, z_ref, h_ref):
    i = pl.program_id(0)
    j = pl.program_id(1)

    @pl.when((j == 0) & (i < nxa))
    def _():
        h_ref[...] = (xa_ref[...] * (1.0 + sc_ref[0]) + sh_ref[0]).astype(BF16)

    @pl.when((j == 0) & (i >= nxa))
    def _():
        h_ref[...] = (xb_ref[...] * (1.0 + sc_ref[0]) + sh_ref[0]).astype(BF16)

    h = h_ref[...]
    bg = jnp.dot(h, wb_ref[...], preferred_element_type=F32)
    cg = jnp.dot(h, wc_ref[...], preferred_element_type=F32)
    v = jnp.dot(h, wv_ref[...], preferred_element_type=F32)
    bg_ref[...] = bg.astype(BF16)
    z_ref[...] = (cg * v).astype(BF16)


def _conv_in(x2d, c2d, mod, w_in_b, n_batch, seq):
    nx, d = x2d.shape
    nc = c2d.shape[0]
    bm = min(PROJ_ROWS, seq, nc)
    assert seq % bm == 0 and nc % bm == 0
    tn = 512 if d % 512 == 0 else d
    nj = d // tn
    nxa = nx // bm
    n_all = nx + nc
    tpb = seq // bm
    mspec = functools.partial(_mod_spec, d=d, tiles_per_batch=tpb, n_batch=n_batch)
    return pl.pallas_call(
        functools.partial(_conv_in_kernel, nxa),
        grid=(n_all // bm, nj),
        in_specs=[
            pl.BlockSpec((bm, d), lambda i, j: (jnp.minimum(i, nxa - 1), 0)),
            pl.BlockSpec((bm, d), lambda i, j: (jnp.maximum(i - nxa, 0), 0),
                         pipeline_mode=pl.Buffered(1)),
            mspec(0), mspec(1),
            pl.BlockSpec((d, tn), lambda i, j: (0, j)),
            pl.BlockSpec((d, tn), lambda i, j: (0, nj + j)),
            pl.BlockSpec((d, tn), lambda i, j: (0, 2 * nj + j)),
        ],
        out_specs=[
            pl.BlockSpec((bm, tn), lambda i, j: (i, j)),
            pl.BlockSpec((bm, tn), lambda i, j: (i, j)),
        ],
        out_shape=[jax.ShapeDtypeStruct((n_all, d), BF16),
                   jax.ShapeDtypeStruct((n_all, d), BF16)],
        scratch_shapes=[pltpu.VMEM((bm, d), BF16)],
        compiler_params=_cparams(("arbitrary", "arbitrary")),
        name="conv_in",
    )(x2d, c2d, mod, mod, w_in_b, w_in_b, w_in_b)


def _ml_in_kernel(x_ref, sh_ref, sc_ref, w_ref, wkt_ref, wg_ref, bgate_ref,
                  u_ref, kt_ref, g_ref, h_ref):
    j = pl.program_id(1)

    @pl.when(j == 0)
    def _():
        h = (x_ref[...] * (1.0 + sc_ref[0]) + sh_ref[0]).astype(BF16)
        h_ref[...] = h
        g_ref[...] = jnp.dot(h, wg_ref[...], preferred_element_type=F32) + bgate_ref[...]
        kt_ref[...] = lax.dot_general(wkt_ref[...], h, (((1,), (1,)), ((), ())),
                                      preferred_element_type=F32).astype(BF16)

    u_ref[...] = jnp.dot(h_ref[...], w_ref[...], preferred_element_type=F32).astype(BF16)


def _ml_in(xall, mod, w_voq_b, w_kt_b, w_gate_pad, b_gate_pad, n_batch, seq, n_ctx_rows):
    n_all, d = xall.shape
    nu = w_voq_b.shape[1]
    hq = w_kt_b.shape[0]
    bm = min(PROJ_ROWS, seq, n_ctx_rows)
    assert seq % bm == 0 and n_ctx_rows % bm == 0
    tn = hq
    assert nu % tn == 0
    gl = w_gate_pad.shape[1]
    mspec = functools.partial(_mod_spec, d=d, tiles_per_batch=seq // bm, n_batch=n_batch)
    return pl.pallas_call(
        _ml_in_kernel,
        grid=(n_all // bm, nu // tn),
        in_specs=[
            pl.BlockSpec((bm, d), lambda i, j: (i, 0)),
            mspec(0), mspec(1),
            pl.BlockSpec((d, tn), lambda i, j: (0, j)),
            pl.BlockSpec((hq, d), lambda i, j: (0, 0)),
            pl.BlockSpec((d, gl), lambda i, j: (0, 0)),
            pl.BlockSpec((1, gl), lambda i, j: (0, 0)),
        ],
        out_specs=[
            pl.BlockSpec((bm, tn), lambda i, j: (i, j)),
            pl.BlockSpec((hq, bm), lambda i, j: (0, i)),
            pl.BlockSpec((bm, gl), lambda i, j: (i, 0)),
        ],
        out_shape=[jax.ShapeDtypeStruct((n_all, nu), BF16),
                   jax.ShapeDtypeStruct((hq, n_all), BF16),
                   jax.ShapeDtypeStruct((n_all, gl), F32)],
        scratch_shapes=[pltpu.VMEM((bm, d), BF16)],
        compiler_params=_cparams(("arbitrary", "arbitrary")),
        name="mlstm_in",
    )(xall, mod, mod, w_voq_b, w_kt_b, w_gate_pad, b_gate_pad)


def _dot_split3(a_b, x):
    hi = x.astype(BF16)
    r1 = x - hi.astype(F32)
    mid = r1.astype(BF16)
    lo = (r1 - mid.astype(F32)).astype(BF16)
    return (jnp.dot(a_b, hi, preferred_element_type=F32)
            + jnp.dot(a_b, mid, preferred_element_type=F32)
            + jnp.dot(a_b, lo, preferred_element_type=F32))


def _scan_kernel(n_heads, dqk, dv, q_ref, kt_ref, v_ref, g_ref, o_ref, ct_ref, m_ref):
    d = pl.program_id(1)
    s = pl.program_id(2)
    L = q_ref.shape[0]
    assert L == LANES
    scale = dqk ** -0.5

    @pl.when(s == 0)
    def _():
        ct_ref[...] = jnp.zeros_like(ct_ref)
        m_ref[...] = jnp.zeros_like(m_ref)

    H = n_heads
    heads = range(H)
    qi = lax.broadcasted_iota(I32, (L, L), 0)
    si = lax.broadcasted_iota(I32, (L, L), 1)
    fwd = d == 0
    mask = jnp.where(fwd, si - qi, qi - si) <= 0

    g = g_ref[...]
    b_all = _dot_split3(mask.astype(BF16), jax.nn.log_sigmoid(g))
    g_t = g.T
    b_t = b_all.T
    def lane_bcast(x, lane0):
        return jnp.stack([jnp.broadcast_to(x[:, lane0 + h:lane0 + h + 1], (L, LANES))
                          for h in heads])

    b_b = lane_bcast(b_all, H)
    i_b = lane_bcast(g, 0)
    b_end = jnp.where(fwd, b_b[:, L - 1:L, :], b_b[:, 0:1, :])
    m_st = m_ref[:, 0:1, :]
    tile = lambda x, n: jnp.concatenate([x] * (n // LANES), axis=-1)

    q3 = jnp.stack([q_ref[:, h * dqk:(h + 1) * dqk] for h in heads])
    kt3 = jnp.stack([kt_ref[h * dqk:(h + 1) * dqk, :] for h in heads])
    v3 = jnp.stack([v_ref[:, h * dv:(h + 1) * dv] for h in heads])

    r3 = jnp.stack([g_t[h:h + 1, :] - b_t[H + h:H + h + 1, :] for h in heads])
    dm = jnp.where(mask, b_b + r3, -jnp.inf)
    a_inter = b_b + m_st
    m_q = jnp.maximum(a_inter, jnp.max(dm, axis=-1, keepdims=True))
    inter = jnp.exp(a_inter - m_q) * scale
    sc = jnp.einsum("hqd,hds->hqs", q3, kt3, preferred_element_type=F32)
    p = jnp.exp(dm - m_q) * (sc * scale)
    ct = ct_ref[...]
    qc = jnp.einsum("hqd,hdv->hqv", q3, ct.astype(BF16), preferred_element_type=F32)
    v_ext = jnp.concatenate([v3, jnp.ones((H, L, LANES), BF16)], axis=-1)
    pv = jnp.einsum("hqs,hsv->hqv", p.astype(BF16), v_ext, preferred_element_type=F32)
    den = pv[:, :, dv:] + inter * qc[:, :, dv:]
    rden = 1.0 / jnp.maximum(jnp.abs(den), jnp.exp(-m_q))
    hout = (pv[:, :, :dv] + tile(inter, dv) * qc[:, :, :dv]) * tile(rden, dv)
    for h in heads:
        o_ref[:, h * dv:(h + 1) * dv] = hout[h]

    wl = b_end - b_b + i_b
    m_next = jnp.maximum(b_end + m_st, jnp.max(wl, axis=1, keepdims=True))
    decay = jnp.exp(b_end + m_st - m_next)
    w_b = jnp.exp(wl - m_next)
    vw = jnp.concatenate([v3.astype(F32) * tile(w_b, dv), w_b], axis=-1).astype(BF16)
    upd = jnp.einsum("hdl,hlv->hdv", kt3, vw, preferred_element_type=F32)
    ct_ref[...] = tile(decay, dv + LANES) * ct + upd
    m_ref[...] = jnp.broadcast_to(m_next, m_ref.shape)


def _scan(u, kt, gates, n_batch, seq, ctx_len, n_heads, dqk, dv):
    L = SCAN_CHUNK
    nx = n_batch * seq
    ncc = ctx_len // L
    nlc = seq // L
    assert ctx_len % L == 0 and seq % L == 0
    hq = n_heads * dqk
    hv = n_heads * dv
    assert hv == 2 * hq

    def row_blk(b, d, s):
        ctx = (nx + b * ctx_len) // L + jnp.where(d == 0, s, ncc - 1 - s)
        sl = s - ncc
        lat = (b * seq) // L + jnp.where(d == 0, sl, nlc - 1 - sl)
        return jnp.where(s < ncc, ctx, lat)

    def out_blk(b, d, s):
        sl = jnp.maximum(s - ncc, 0)
        return (b * seq) // L + jnp.where(d == 0, sl, nlc - 1 - sl)

    return pl.pallas_call(
        functools.partial(_scan_kernel, n_heads, dqk, dv),
        grid=(n_batch, 2, ncc + nlc),
        in_specs=[
            pl.BlockSpec((L, hq), lambda b, d, s: (row_blk(b, d, s), 2 * hv // hq)),
            pl.BlockSpec((hq, L), lambda b, d, s: (0, row_blk(b, d, s))),
            pl.BlockSpec((L, hv), lambda b, d, s: (row_blk(b, d, s), 0)),
            pl.BlockSpec((L, LANES), lambda b, d, s: (row_blk(b, d, s), d)),
        ],
        out_specs=pl.BlockSpec((None, L, hv), lambda b, d, s: (d, out_blk(b, d, s), 0)),
        out_shape=jax.ShapeDtypeStruct((2, nx, hv), F32),
        scratch_shapes=[pltpu.VMEM((n_heads, dqk, dv + LANES), F32),
                        pltpu.VMEM((n_heads, SUBLANES, LANES), F32)],
        compiler_params=_cparams(("arbitrary", "arbitrary", "arbitrary")),
        name="mlstm_scan",
    )(u, kt, u, gates)


def _mix_epilogue(alpha, rs, a, x_ref, g1_ref, sh2_ref, sc2_ref, wout_ref, lng_ref,
                  lnb_ref, rwt_ref, x1_ref, hx2_ref, logt_ref):
    mx = jnp.dot(a, wout_ref[...], preferred_element_type=F32)
    x1 = _layer_norm_rows(alpha * x_ref[rs, :] + g1_ref[0] * mx, lng_ref[...], lnb_ref[...])
    x1_ref[rs, :] = x1
    hx2 = x1 * (1.0 + sc2_ref[0]) + sh2_ref[0]
    hx2_ref[rs, :] = hx2
    logt_ref[:, rs] = lax.dot_general(rwt_ref[...], hx2.astype(BF16), (((1,), (1,)), ((), ())),
                                      preferred_element_type=F32)


def _sub_rows(tm, sub):
    return [slice(r, r + sub) for r in range(0, tm, sub)]


def _conv_out_kernel(alpha, ctx_mode, n_lat, tpb, sub, z_ref, zp_ref, zn_ref, bg_ref, cw_ref,
                     x_ref, g1_ref, sh2_ref, sc2_ref, wout_ref, lng_ref, lnb_ref, rwt_ref, *rest):
    x1_ref, hx2_ref, logt_ref = rest[-3:]
    i = pl.program_id(0)
    tm, d = z_ref.shape
    half = d // 2
    rows = lax.broadcasted_iota(I32, (sub, 1), 0)

    def conv1(zz, w3, period):
        pos = rows & (period - 1)
        prev = pltpu.roll(zz, 1, 0) * (pos != 0).astype(F32)
        nxt = pltpu.roll(zz, sub - 1, 0) * (pos != period - 1).astype(F32)
        return w3[0:1] * prev + w3[1:2] * zz + w3[2:3] * nxt

    def tile():
        cw = cw_ref[...]
        if not ctx_mode:
            ti = i % tpb
            up = zp_ref[...].astype(F32) * (ti > 0).astype(F32)
            dn = zn_ref[...].astype(F32) * (ti < tpb - 1).astype(F32)
            w3 = cw[:, half:]

        for rs in _sub_rows(tm, sub):
            z = z_ref[rs, :].astype(F32)
            bg = bg_ref[rs, :].astype(F32)
            if ctx_mode:
                a = (bg * conv1(z, cw, sub)).astype(BF16)
            else:
                a_row = (bg[:, :half] * conv1(z[:, :half], cw[:, :half], GRID_W)).astype(BF16)
                r0, r1 = rs.start, rs.stop
                f32_rows = lambda a, b: z_ref[a:b, half:].astype(F32)
                prev = (jnp.concatenate([up, f32_rows(0, r1 - GRID_W)], axis=0) if r0 == 0
                        else f32_rows(r0 - GRID_W, r1 - GRID_W))
                nxt = (jnp.concatenate([f32_rows(r0 + GRID_W, tm), dn], axis=0) if r1 == tm
                       else f32_rows(r0 + GRID_W, r1 + GRID_W))
                y = w3[0:1] * prev + w3[1:2] * z[:, half:] + w3[2:3] * nxt
                a = jnp.concatenate([a_row, (bg[:, half:] * y).astype(BF16)], axis=1)
            _mix_epilogue(alpha, rs, a, x_ref, g1_ref, sh2_ref, sc2_ref, wout_ref, lng_ref,
                          lnb_ref, rwt_ref, x1_ref, hx2_ref, logt_ref)

    if ctx_mode:
        tile()
    else:
        pl.when(i < n_lat)(tile)

        @pl.when(i >= n_lat)
        def _():
            x1_ref[...] = jnp.zeros_like(x1_ref)
            hx2_ref[...] = jnp.zeros_like(hx2_ref)
            logt_ref[...] = jnp.zeros_like(logt_ref)


def _ml_out_kernel(alpha, n_heads, dv, sub, hf_ref, hb_ref, og_ref, ng_ref, x_ref,
                   g1_ref, sh2_ref, sc2_ref, wout_ref, lng_ref, lnb_ref, rwt_ref,
                   x1_ref, hx2_ref, logt_ref):
    for rs in _sub_rows(hf_ref.shape[0], sub):
        parts = []
        for h in range(n_heads):
            sl = slice(h * dv, (h + 1) * dv)
            hs = hf_ref[rs, sl] + hb_ref[rs, sl]
            mu = jnp.mean(hs, axis=-1, keepdims=True)
            c = hs - mu
            var = jnp.mean(c * c, axis=-1, keepdims=True)
            hn = c * lax.rsqrt(var + HEAD_NORM_EPS)
            gate = jax.nn.sigmoid(og_ref[rs, sl].astype(F32))
            parts.append((hn * ng_ref[:, sl] * gate).astype(BF16))
        _mix_epilogue(alpha, rs, jnp.concatenate(parts, axis=1), x_ref, g1_ref, sh2_ref, sc2_ref, wout_ref, lng_ref,
                      lnb_ref, rwt_ref, x1_ref, hx2_ref, logt_ref)


def _mix_out_common(d, n_rows, n_exp, mod_row, tile0=0):
    tm = ROW_TILE
    mspec = lambda chunk: pl.BlockSpec((1, 1, d), lambda i: (mod_row(i), 0, chunk))
    const = lambda shape: pl.BlockSpec(shape, lambda i: (0,) * len(shape))
    in_specs = [mspec(2), mspec(3), mspec(4), const((d, d)), const((1, d)), const((1, d)),
                const((n_exp, d))]
    out_specs = [pl.BlockSpec((tm, d), lambda i: (i + tile0, 0)),
                 pl.BlockSpec((tm, d), lambda i: (i + tile0, 0)),
                 pl.BlockSpec((n_exp, tm), lambda i: (0, i + tile0))]
    out_shape = [jax.ShapeDtypeStruct((n_rows, d), F32),
                 jax.ShapeDtypeStruct((n_rows, d), F32),
                 jax.ShapeDtypeStruct((n_exp, n_rows), F32)]
    return in_specs, out_specs, out_shape


def _conv_out(alpha, z, bg, conv_w, x2d, c2d, mod, w_out_b, ln_g, ln_b, rwt_b, n_batch, seq,
              ctx_len):
    n_all, d = z.shape
    nx = x2d.shape[0]
    tm = ROW_TILE
    assert ctx_len == tm and seq % tm == 0 and MIX_SUB_ROWS % GRID_W == 0
    half = d // 2
    nxa = nx // tm
    tpb = seq // tm
    hpt = tm // GRID_W
    n_exp = rwt_b.shape[0]

    def z_specs(tile0):
        return [
            pl.BlockSpec((tm, d), lambda i: (i + tile0, 0)),
            pl.BlockSpec((GRID_W, half), lambda i: (jnp.maximum((i + tile0) * hpt - 1, 0), 1)),
            pl.BlockSpec((GRID_W, half), lambda i: ((i + tile0 + 1) * hpt - 1, 1)),
            pl.BlockSpec((tm, d), lambda i: (i + tile0, 0)),
            pl.BlockSpec((3, d), lambda i: (0, 0)),
            pl.BlockSpec((tm, d), lambda i: (i, 0)),
        ]

    lat = lambda i: jnp.minimum(i, nxa - 1)
    common_in, out_specs, out_shape = _mix_out_common(d, n_all, n_exp, lambda i: lat(i) // tpb)
    lat_specs = z_specs(0)
    lat_specs[2] = pl.BlockSpec((GRID_W, half), lambda i: ((lat(i) + 1) * hpt, 1))
    lat_specs[5] = pl.BlockSpec((tm, d), lambda i: (lat(i), 0))
    outs = pl.pallas_call(
        functools.partial(_conv_out_kernel, alpha, False, nxa, tpb, MIX_SUB_ROWS),
        grid=(n_all // tm,),
        in_specs=lat_specs + common_in,
        out_specs=out_specs,
        out_shape=out_shape,
        compiler_params=_cparams(("arbitrary",)),
        name="conv_out",
    )(z, z, z, bg, conv_w, x2d, mod, mod, mod, w_out_b, ln_g, ln_b, rwt_b)

    common_in, out_specs, out_shape = _mix_out_common(d, n_all, n_exp, lambda i: n_batch, nxa)
    n_in = 6 + len(common_in)
    keep = [pl.BlockSpec(memory_space=pl.ANY)] * 3
    return pl.pallas_call(
        functools.partial(_conv_out_kernel, alpha, True, None, tpb, tm),
        grid=(c2d.shape[0] // tm,),
        in_specs=z_specs(nxa) + common_in + keep,
        out_specs=out_specs,
        out_shape=out_shape,
        input_output_aliases={n_in: 0, n_in + 1: 1, n_in + 2: 2},
        compiler_params=_cparams(("arbitrary",)),
        name="conv_out_ctx",
    )(z, z, z, bg, conv_w, c2d, mod, mod, mod, w_out_b, ln_g, ln_b, rwt_b, *outs)


def _ml_out(alpha, hfb, u, norm_g, xall, mod, w_out_b, ln_g, ln_b, rwt_b, n_batch, seq,
            n_heads, dv):
    nx, d = hfb.shape[1:]
    tm = ROW_TILE
    tpb = seq // tm
    common_in, out_specs, out_shape = _mix_out_common(d, nx, rwt_b.shape[0], lambda i: i // tpb)
    o_blk = 1
    in_specs = [
        pl.BlockSpec((None, tm, d), lambda i: (0, i, 0)),
        pl.BlockSpec((None, tm, d), lambda i: (1, i, 0)),
        pl.BlockSpec((tm, d), lambda i: (i, o_blk)),
        pl.BlockSpec((1, d), lambda i: (0, 0)),
        pl.BlockSpec((tm, d), lambda i: (i, 0)),
    ] + common_in
    return pl.pallas_call(
        functools.partial(_ml_out_kernel, alpha, n_heads, dv, tm),
        grid=(nx // tm,),
        in_specs=in_specs,
        out_specs=out_specs,
        out_shape=out_shape,
        compiler_params=_cparams(("arbitrary",)),
        name="mlstm_out",
    )(hfb, hfb, u, norm_g, xall, mod, mod, mod, w_out_b, ln_g, ln_b, rwt_b)


def _route_kernel(tile_rows, n_te, logt_ref, rb_ref, pos_ref, wcol_ref, te_ref, meta_ref,
                  carry_ref):
    ph = pl.program_id(0)
    i = pl.program_id(1)
    n_steps = pl.num_programs(1)
    n_exp, tr = logt_ref.shape
    epg = n_exp // N_GROUPS

    @pl.when((ph == 0) & (i == 0))
    def _():
        carry_ref[...] = jnp.zeros_like(carry_ref)

    s = jax.nn.sigmoid(logt_ref[...])
    sel = s + rb_ref[...]
    row = lax.broadcasted_iota(I32, (n_exp, tr), 0)
    member = row % epg
    group = row // epg

    def partner(x, k, idx, span, unit):
        wrapped = (idx + k) >= span
        up = pltpu.roll(x, n_exp - k * unit, 0)
        down = pltpu.roll(x, (span - k) * unit, 0)
        return jnp.where(wrapped, down, up), wrapped

    rank_in = jnp.zeros((n_exp, tr), F32)
    for k in range(1, epg):
        p, wrapped = partner(sel, k, member, epg, 1)
        beats = (p > sel) | ((p == sel) & wrapped)
        rank_in = rank_in + beats.astype(F32)
    top = (rank_in < TOP_K).astype(F32)
    gs = sel * top
    score = gs
    for k in range(1, epg):
        p, _ = partner(gs, k, member, epg, 1)
        score = score + p
    n_better = jnp.zeros((n_exp, tr), F32)
    for k in range(1, N_GROUPS):
        p, wrapped = partner(score, k, group, N_GROUPS, epg)
        beats = (p > score) | ((p == score) & wrapped)
        n_better = n_better + beats.astype(F32)
    best = n_better == 0.0
    m1 = (best & (rank_in == 0.0)).astype(F32)
    m2 = (best & (rank_in == 1.0)).astype(F32)
    s1 = jnp.sum(s * m1, axis=0, keepdims=True)
    s2 = jnp.sum(s * m2, axis=0, keepdims=True)
    den = s1 + s2
    oh = m1 + m2

    ti = lax.broadcasted_iota(I32, (tr, tr), 0)
    tj = lax.broadcasted_iota(I32, (tr, tr), 1)
    before = (ti < tj).astype(BF16)
    slot = jnp.dot(oh.astype(BF16), before, preferred_element_type=F32) + carry_ref[...]

    @pl.when(ph == 1)
    def _():
        pos_ref[0:1, :] = jnp.sum(m1 * slot, axis=0, keepdims=True).astype(I32)
        pos_ref[1:2, :] = jnp.sum(m2 * slot, axis=0, keepdims=True).astype(I32)
        w8 = jnp.concatenate([s1 / den, s2 / den, jnp.zeros((LANES - 2, tr), F32)], axis=0)
        wcol_ref[...] = w8.T

    carry_ref[...] = carry_ref[...] + jnp.sum(oh, axis=1, keepdims=True)

    @pl.when((ph == 0) & (i == n_steps - 1))
    def _():
        cnt = carry_ref[...]
        ntile = jnp.floor((cnt + (tile_rows - 1)) * (1.0 / tile_rows))
        er = lax.broadcasted_iota(I32, (n_exp, LANES), 0)
        el = lax.broadcasted_iota(I32, (n_exp, LANES), 1)
        eye = (er == el).astype(F32)
        nt_row = jnp.sum(ntile * eye, axis=0, keepdims=True)
        cnt_row = jnp.sum(cnt * eye, axis=0, keepdims=True)
        cum_excl = jnp.sum(nt_row * (el < er).astype(F32), axis=1, keepdims=True)
        off = cum_excl * tile_rows
        off_row = jnp.sum(off * eye, axis=0, keepdims=True)
        total = jnp.sum(nt_row, axis=1, keepdims=True)
        cum_incl = cum_excl + ntile
        tl = lax.broadcasted_iota(I32, (n_exp, n_te), 1).astype(F32)
        te = jnp.sum((cum_incl <= tl).astype(F32), axis=0, keepdims=True)
        te_ref[...] = jnp.minimum(te, n_exp - 1).astype(I32)
        meta_ref[...] = jnp.concatenate(
            [cnt_row, off_row, jnp.broadcast_to(total, (1, LANES)),
             jnp.zeros((SUBLANES - 3, LANES), F32)], axis=0).astype(I32)
        carry_ref[...] = off


def _route(logt, router_b, tile_rows):
    n_exp, n = logt.shape
    tr = ROUTE_COLS
    assert n % tr == 0
    n_te = 256
    assert (2 * n) // tile_rows + n_exp <= n_te
    return pl.pallas_call(
        functools.partial(_route_kernel, tile_rows, n_te),
        grid=(2, n // tr),
        in_specs=[pl.BlockSpec((n_exp, tr), lambda p, i: (0, i)),
                  pl.BlockSpec((n_exp, 1), lambda p, i: (0, 0))],
        out_specs=[pl.BlockSpec((2, tr), lambda p, i: (0, i * p)),
                   pl.BlockSpec((tr, LANES), lambda p, i: (i * p, 0)),
                   pl.BlockSpec((1, n_te), lambda p, i: (0, 0)),
                   pl.BlockSpec((SUBLANES, LANES), lambda p, i: (0, 0))],
        out_shape=[jax.ShapeDtypeStruct((2, n), I32),
                   jax.ShapeDtypeStruct((n, LANES), F32),
                   jax.ShapeDtypeStruct((1, n_te), I32),
                   jax.ShapeDtypeStruct((SUBLANES, LANES), I32)],
        scratch_shapes=[pltpu.VMEM((n_exp, 1), F32)],
        compiler_params=_cparams(("arbitrary", "arbitrary")),
        name="route",
    )(logt, router_b.reshape(n_exp, 1).astype(F32))


def _tbl_kernel(n_tok, n_exp, tile_rows, meta_ref, pos_ref, tbl_ref):
    i = pl.program_id(0)
    blk = pos_ref.shape[1]

    @pl.when(i == 0)
    def _():
        n_pad = 0
        for e in range(n_exp):
            cnt = meta_ref[e]
            off = meta_ref[n_exp + e]
            up = ((cnt + (tile_rows - 1)) // tile_rows) * tile_rows

            def fill(r, j):
                tbl_ref[off + r] = (2 * n_tok + j) << SLOT_SRC_BITS
                return j + 1

            n_pad = lax.fori_loop(cnt, up, fill, n_pad)

        def fill_tail(r, carry):
            tbl_ref[r] = (2 * n_tok) << SLOT_SRC_BITS
            return carry

        lax.fori_loop(2 * n_tok + n_pad, tbl_ref.shape[0], fill_tail, 0)

    base = i * blk
    first = (base << SLOT_SRC_BITS) | jnp.where(base >= n_tok, base - n_tok, base)
    step = (1 << SLOT_SRC_BITS) + 1
    group = 8

    def body(q, carry):
        r0 = q * group
        slots = [pos_ref[0, r0 + j] for j in range(group)]
        for j in range(group):
            tbl_ref[slots[j]] = first + (r0 + j) * step
        return carry

    lax.fori_loop(0, blk // group, body, 0, unroll=4)


def _build_table(meta1d, pos, n_tok, n_exp, tile_rows, p_pad):
    blk = TBL_BLOCK
    while n_tok % blk:
        blk //= 2
    n_pairs = 2 * n_tok
    grid_spec = pltpu.PrefetchScalarGridSpec(
        num_scalar_prefetch=1,
        grid=(n_pairs // blk,),
        in_specs=[pl.BlockSpec((None, 1, blk), lambda i, m: (i, 0, 0), memory_space=pltpu.SMEM)],
        out_specs=pl.BlockSpec(memory_space=pltpu.SMEM),
    )
    return pl.pallas_call(
        functools.partial(_tbl_kernel, n_tok, n_exp, tile_rows),
        grid_spec=grid_spec,
        out_shape=jax.ShapeDtypeStruct((p_pad,), I32),
        compiler_params=_cparams(("arbitrary",)),
        name="slot_table",
    )(meta1d, pos.reshape(n_pairs // blk, 1, blk))


def _moe_kernel(te_ref, nt_ref, tbl_ref, tbln_ref, tblp_ref, hx_hbm, wg_ref, wu_ref, wd_ref,
                y_hbm, xbuf, ybuf, zbuf, gsem, ssem, zsem):
    t = pl.program_id(0)
    n = nt_ref[0]
    rows = tbl_ref.shape[1]

    def start_gather(tref, sl):
        for r in range(rows):
            src = tref[0, r] & SLOT_SRC_MASK
            pltpu.make_async_copy(hx_hbm.at[pl.ds(src, 1), :],
                                  xbuf.at[sl, pl.ds(r, 1), :], gsem.at[sl]).start()

    def start_scatter(tref, sl):
        for r in range(rows):
            dst = tref[0, r] >> SLOT_SRC_BITS
            pltpu.make_async_copy(ybuf.at[sl, pl.ds(r, 1), :],
                                  y_hbm.at[pl.ds(dst, 1), :], ssem.at[sl]).start()

    def wait_gather(sl):
        pltpu.make_async_copy(xbuf.at[sl], xbuf.at[sl], gsem.at[sl]).wait()

    def wait_scatter(sl):
        pltpu.make_async_copy(ybuf.at[sl], ybuf.at[sl], ssem.at[sl]).wait()

    @pl.when(t == 0)
    def _():
        zbuf[...] = jnp.zeros_like(zbuf)
        start_gather(tbl_ref, 0)

    @pl.when(t >= n)
    def _():
        dst = y_hbm.at[pl.ds(pl.multiple_of(t * rows, rows), rows), :]
        fill = pltpu.make_async_copy(zbuf, dst, zsem)
        fill.start()
        fill.wait()

    def tile(slot):
        other = 1 - slot
        wait_gather(slot)

        @pl.when(t >= 2)
        def _():
            wait_scatter(slot)

        start_gather(tbln_ref, other)

        @pl.when(t >= 1)
        def _():
            start_scatter(tblp_ref, other)

        xb = xbuf[slot].astype(BF16)
        g = jnp.dot(xb, wg_ref[0], preferred_element_type=F32)
        u = jnp.dot(xb, wu_ref[0], preferred_element_type=F32)
        h = (g * jax.nn.sigmoid(g) * u).astype(BF16)
        ybuf[slot] = jnp.dot(h, wd_ref[0], preferred_element_type=F32)

        @pl.when(t == n - 1)
        def _():
            start_scatter(tbl_ref, slot)
            wait_gather(other)

            @pl.when(t >= 1)
            def _():
                wait_scatter(other)

            wait_scatter(slot)

    for parity in (0, 1):
        pl.when((t < n) & (lax.rem(t, 2) == parity))(functools.partial(tile, parity))


def _moe(te1d, nt1d, tbl, hx2, wg_b, wu_b, wd_b, n_tok, p_pad):
    tmm = MOE_ROWS
    n_exp, d, f = wg_b.shape
    t_max = (2 * n_tok) // tmm + n_exp
    last = lambda t, te, nt: jnp.minimum(t, nt[0] - 1)
    wspec = lambda shape: pl.BlockSpec(shape, lambda t, te, nt: (te[last(t, te, nt)], 0, 0))
    grid_spec = pltpu.PrefetchScalarGridSpec(
        num_scalar_prefetch=2,
        grid=(t_max,),
        in_specs=[
            pl.BlockSpec((None, 1, tmm), lambda t, te, nt: (last(t, te, nt), 0, 0),
                         memory_space=pltpu.SMEM),
            pl.BlockSpec((None, 1, tmm), lambda t, te, nt: (last(t + 1, te, nt), 0, 0),
                         memory_space=pltpu.SMEM),
            pl.BlockSpec((None, 1, tmm), lambda t, te, nt: (jnp.maximum(last(t, te, nt) - 1, 0), 0, 0),
                         memory_space=pltpu.SMEM),
            pl.BlockSpec(memory_space=pl.ANY),
            wspec((1, d, f)), wspec((1, d, f)), wspec((1, f, d)),
        ],
        out_specs=pl.BlockSpec(memory_space=pl.ANY),
        scratch_shapes=[pltpu.VMEM((2, tmm, d), F32), pltpu.VMEM((2, tmm, d), F32),
                        pltpu.VMEM((tmm, d), F32),
                        pltpu.SemaphoreType.DMA((2,)), pltpu.SemaphoreType.DMA((2,)),
                        pltpu.SemaphoreType.DMA(())],
    )
    tbl3 = tbl.reshape(p_pad // tmm, 1, tmm)
    return pl.pallas_call(
        _moe_kernel,
        grid_spec=grid_spec,
        out_shape=jax.ShapeDtypeStruct((t_max * tmm, d), F32),
        compiler_params=_cparams(("arbitrary",)),
        name="moe_experts",
    )(te1d, nt1d, tbl3, tbl3, tbl3, hx2, wg_b, wu_b, wd_b)


def _comb_kernel(alpha, x_ref, y0_ref, y1_ref, w_ref, g2_ref, lng_ref, lnb_ref, o_ref):
    w = w_ref[...]
    ex = w[:, 0:1] * y0_ref[...] + w[:, 1:2] * y1_ref[...]
    o_ref[...] = _layer_norm_rows(alpha * x_ref[...] + g2_ref[0] * ex, lng_ref[...], lnb_ref[...])


def _combine(alpha, x1, y, wcol, mod, ln_g, ln_b, n_tok, n_out, n_batch, seq):
    d = x1.shape[1]
    tm = ROW_TILE
    nblk = n_tok // tm
    mspec = functools.partial(_mod_spec, d=d, tiles_per_batch=seq // tm, n_batch=n_batch)
    return pl.pallas_call(
        functools.partial(_comb_kernel, alpha),
        grid=(n_out // tm,),
        in_specs=[
            pl.BlockSpec((tm, d), lambda i: (i, 0)),
            pl.BlockSpec((tm, d), lambda i: (i, 0)),
            pl.BlockSpec((tm, d), lambda i: (i + nblk, 0)),
            pl.BlockSpec((tm, LANES), lambda i: (i, 0)),
            mspec(5),
            pl.BlockSpec((1, d), lambda i: (0, 0)),
            pl.BlockSpec((1, d), lambda i: (0, 0)),
        ],
        out_specs=pl.BlockSpec((tm, d), lambda i: (i, 0)),
        out_shape=jax.ShapeDtypeStruct((n_out, d), F32),
        compiler_params=_cparams(("arbitrary",)),
        name="moe_combine",
    )(x1, y, y, wcol, mod, ln_g, ln_b)


def _moe_layer(alpha, x1, hx2, logt, mod, ln_g, ln_b, router_b, wg_b, wu_b, wd_b, n_out,
               n_batch, seq):
    n_tok = hx2.shape[0]
    n_exp = wg_b.shape[0]
    tmm = MOE_ROWS
    p_max = 2 * n_tok + n_exp * tmm
    p_pad = -(-p_max // TBL_BLOCK) * TBL_BLOCK
    assert n_tok <= 1 << SLOT_SRC_BITS and p_max < 1 << (31 - SLOT_SRC_BITS)
    pos, wcol, te, meta = _route(logt, router_b, tmm)
    meta1d = meta[:2, :n_exp].reshape(-1)
    tbl = _build_table(meta1d, pos, n_tok, n_exp, tmm, p_pad)
    y = _moe(te.reshape(-1), meta[2, :1], tbl, hx2, wg_b, wu_b, wd_b, n_tok, p_pad)
    return _combine(alpha, x1, y, wcol, mod, ln_g, ln_b, n_tok, n_out, n_batch, seq)


def kernel(x, c, ctx, c_ctx, w_ada, b_ada, ln_g, ln_b, conv_w_in, conv_w, conv_w_out, ml_w_in, ml_w_gate, ml_b_gate, ml_norm_g, ml_w_out, router_w, router_b, exp_w_gate, exp_w_up, exp_w_down):
    n_batch, seq, d = x.shape
    ctx_len = ctx.shape[1]
    depth = w_ada.shape[0]
    assert depth == 2, "layer 0 is the conv mixer, layer 1 the mLSTM mixer"
    alpha = (2 * depth) ** 0.25
    n_heads = ml_b_gate.shape[-1] // 4
    dqk = d // (2 * n_heads)
    dv = d // n_heads
    nx = n_batch * seq
    nc = n_batch * ctx_len
    assert n_batch < SUBLANES and 2 * n_heads <= LANES

    x2d = x.reshape(nx, d)
    c2d = ctx.reshape(nc, d)
    cc = jnp.zeros((SUBLANES, d), F32).at[:n_batch].set(c).at[n_batch].set(c_ctx)
    mod = _ada(cc, w_ada, b_ada).reshape(depth, SUBLANES, 1, 6 * d)
    rwt_b = router_w.T.astype(BF16)

    bg, z = _conv_in(x2d, c2d, mod[0], _to_bf16(conv_w_in, 0), n_batch, seq)
    x1, hx2, logt = _conv_out(alpha, z, bg, conv_w[0], x2d, c2d, mod[0],
                              _to_bf16(conv_w_out, 0), ln_g[0, 0:1], ln_b[0, 0:1], rwt_b,
                              n_batch, seq, ctx_len)
    xall = _moe_layer(alpha, x1, hx2, logt, mod[0], ln_g[0, 1:2], ln_b[0, 1:2], router_b,
                      _to_bf16(exp_w_gate, 0), _to_bf16(exp_w_up, 0), _to_bf16(exp_w_down, 0),
                      nx + nc, n_batch, seq)

    w_gate = ml_w_gate[0]
    b_gate = ml_b_gate[0]
    h2 = 2 * n_heads
    w_gate_pad = (jnp.zeros((d, 2 * LANES), F32).at[:, :h2].set(w_gate[:, :h2])
                  .at[:, LANES:LANES + h2].set(w_gate[:, h2:])).astype(BF16)
    b_gate_pad = (jnp.zeros((1, 2 * LANES), F32).at[0, :h2].set(b_gate[:h2])
                  .at[0, LANES:LANES + h2].set(b_gate[h2:]))
    w_in_b = _to_bf16(ml_w_in, 0)
    hq = n_heads * dqk
    w_voq_b = jnp.concatenate([w_in_b[:, 2 * hq:], w_in_b[:, :hq]], axis=1)
    u, kt, gates = _ml_in(xall, mod[1], w_voq_b, w_in_b[:, hq:2 * hq].T, w_gate_pad,
                          b_gate_pad, n_batch, seq, nc)
    hfb = _scan(u, kt, gates, n_batch, seq, ctx_len, n_heads, dqk, dv)
    x1, hx2, logt = _ml_out(alpha, hfb, u, ml_norm_g[0:1], xall, mod[1],
                            _to_bf16(ml_w_out, 0), ln_g[1, 0:1], ln_b[1, 0:1], rwt_b,
                            n_batch, seq, n_heads, dv)
    out = _moe_layer(alpha, x1, hx2, logt, mod[1], ln_g[1, 1:2], ln_b[1, 1:2], router_b,
                     _to_bf16(exp_w_gate, 1), _to_bf16(exp_w_up, 1), _to_bf16(exp_w_down, 1),
                     nx, n_batch, seq)
    return out.reshape(n_batch, seq, d)
```

```python
import functools

import jax
import jax.numpy as jnp
from jax import lax
from jax.experimental import pallas as pl
from jax.experimental.pallas import tpu as pltpu

F32 = jnp.float32
BF16 = jnp.bfloat16
I32 = jnp.int32

GRID_W = 64
N_GROUPS = 4
TOP_K = 2
LN_EPS = 1e-5
HEAD_NORM_EPS = 1e-6

LANES = 128
SUBLANES = 8
VMEM_LIMIT_BYTES = 56 * 1024 * 1024

ROW_TILE = 256
MIX_SUB_ROWS = 128
PROJ_ROWS = 1024
MOE_ROWS = 256
ROUTE_COLS = 512
SCAN_CHUNK = 128
TBL_BLOCK = 1024
SLOT_SRC_BITS = 15
SLOT_SRC_MASK = (1 << SLOT_SRC_BITS) - 1
CAST_BLOCK_BYTES = 8 * 1024 * 1024


def _cparams(sem):
    return pltpu.CompilerParams(dimension_semantics=sem, vmem_limit_bytes=VMEM_LIMIT_BYTES)


def _layer_norm_rows(r, g, b):
    mu = jnp.mean(r, axis=-1, keepdims=True)
    c = r - mu
    var = jnp.mean(c * c, axis=-1, keepdims=True)
    return c * lax.rsqrt(var + LN_EPS) * g + b


def _ada_kernel(cc_ref, w_ref, b_ref, o_ref):
    a = cc_ref[...]
    a = (a * jax.nn.sigmoid(a)).astype(BF16)
    o_ref[0] = jnp.dot(a, w_ref[0].astype(BF16), preferred_element_type=F32) + b_ref[0]


def _ada(cc, w_ada, b_ada):
    depth, d, n6 = w_ada.shape
    tn = 1024
    return pl.pallas_call(
        _ada_kernel,
        grid=(depth, n6 // tn),
        in_specs=[
            pl.BlockSpec((SUBLANES, d), lambda l, j: (0, 0)),
            pl.BlockSpec((1, d, tn), lambda l, j: (l, 0, j)),
            pl.BlockSpec((1, 1, tn), lambda l, j: (l, 0, j)),
        ],
        out_specs=pl.BlockSpec((1, SUBLANES, tn), lambda l, j: (l, 0, j)),
        out_shape=jax.ShapeDtypeStruct((depth, SUBLANES, n6), F32),
        compiler_params=_cparams(("arbitrary", "arbitrary")),
        name="ada_mod",
    )(cc, w_ada, b_ada.reshape(depth, 1, n6))


def _cast_kernel(w_ref, o_ref):
    o_ref[...] = w_ref[...].astype(BF16)


def _to_bf16(w, layer):
    rows, cols = w.shape[-2:]
    w4 = w.reshape(w.shape[0], -1, rows, cols)
    m = w4.shape[1]
    rb = rows
    while rb * cols * 4 > CAST_BLOCK_BYTES and rb % (4 * SUBLANES) == 0:
        rb //= 2
    out = pl.pallas_call(
        _cast_kernel,
        grid=(m, rows // rb),
        in_specs=[pl.BlockSpec((None, None, rb, cols), lambda e, r: (layer, e, r, 0))],
        out_specs=pl.BlockSpec((None, rb, cols), lambda e, r: (e, r, 0)),
        out_shape=jax.ShapeDtypeStruct((m, rows, cols), BF16),
        compiler_params=_cparams(("arbitrary", "arbitrary")),
        name="to_bf16",
    )(w4)
    return out.reshape(w.shape[1:])


def _mod_spec(chunk, d, tiles_per_batch, n_batch):
    return pl.BlockSpec(
        (1, 1, d),
        lambda i, *_: (jnp.minimum(i // tiles_per_batch, n_batch), 0, chunk))


def _conv_in_kernel(nxa, xa_ref, xb_ref, sh_ref, sc_ref, wb_ref, wc_ref, wv_ref,
                    bg_ref, z_ref, h_ref):
    i = pl.program_id(0)
    j = pl.program_id(1)

    @pl.when((j == 0) & (i < nxa))
    def _():
        h_ref[...] = (xa_ref[...] * (1.0 + sc_ref[0]) + sh_ref[0]).astype(BF16)

    @pl.when((j == 0) & (i >= nxa))
    def _():
        h_ref[...] = (xb_ref[...] * (1.0 + sc_ref[0]) + sh_ref[0]).astype(BF16)

    h = h_ref[...]
    bg = jnp.dot(h, wb_ref[...], preferred_element_type=F32)
    cg = jnp.dot(h, wc_ref[...], preferred_element_type=F32)
    v = jnp.dot(h, wv_ref[...], preferred_element_type=F32)
    bg_ref[...] = bg.astype(BF16)
    z_ref[...] = (cg * v).astype(BF16)


def _conv_in(x2d, c2d, mod, w_in_b, n_batch, seq):
    nx, d = x2d.shape
    nc = c2d.shape[0]
    bm = min(PROJ_ROWS, seq, nc)
    assert seq % bm == 0 and nc % bm == 0
    tn = 512 if d % 512 == 0 else d
    nj = d // tn
    nxa = nx // bm
    n_all = nx + nc
    tpb = seq // bm
    mspec = functools.partial(_mod_spec, d=d, tiles_per_batch=tpb, n_batch=n_batch)
    return pl.pallas_call(
        functools.partial(_conv_in_kernel, nxa),
        grid=(n_all // bm, nj),
        in_specs=[
            pl.BlockSpec((bm, d), lambda i, j: (jnp.minimum(i, nxa - 1), 0)),
            pl.BlockSpec((bm, d), lambda i, j: (jnp.maximum(i - nxa, 0), 0),
                         pipeline_mode=pl.Buffered(1)),
            mspec(0), mspec(1),
            pl.BlockSpec((d, tn), lambda i, j: (0, j)),
            pl.BlockSpec((d, tn), lambda i, j: (0, nj + j)),
            pl.BlockSpec((d, tn), lambda i, j: (0, 2 * nj + j)),
        ],
        out_specs=[
            pl.BlockSpec((bm, tn), lambda i, j: (i, j)),
            pl.BlockSpec((bm, tn), lambda i, j: (i, j)),
        ],
        out_shape=[jax.ShapeDtypeStruct((n_all, d), BF16),
                   jax.ShapeDtypeStruct((n_all, d), BF16)],
        scratch_shapes=[pltpu.VMEM((bm, d), BF16)],
        compiler_params=_cparams(("arbitrary", "arbitrary")),
        name="conv_in",
    )(x2d, c2d, mod, mod, w_in_b, w_in_b, w_in_b)


def _ml_in_kernel(x_ref, sh_ref, sc_ref, w_ref, wkt_ref, wg_ref, bgate_ref,
                  u_ref, kt_ref, g_ref, h_ref):
    j = pl.program_id(1)

    @pl.when(j == 0)
    def _():
        h = (x_ref[...] * (1.0 + sc_ref[0]) + sh_ref[0]).astype(BF16)
        h_ref[...] = h
        g_ref[...] = jnp.dot(h, wg_ref[...], preferred_element_type=F32) + bgate_ref[...]
        kt_ref[...] = lax.dot_general(wkt_ref[...], h, (((1,), (1,)), ((), ())),
                                      preferred_element_type=F32).astype(BF16)

    u_ref[...] = jnp.dot(h_ref[...], w_ref[...], preferred_element_type=F32).astype(BF16)


def _ml_in(xall, mod, w_voq_b, w_kt_b, w_gate_pad, b_gate_pad, n_batch, seq, n_ctx_rows):
    n_all, d = xall.shape
    nu = w_voq_b.shape[1]
    hq = w_kt_b.shape[0]
    bm = min(PROJ_ROWS, seq, n_ctx_rows)
    assert seq % bm == 0 and n_ctx_rows % bm == 0
    tn = hq
    assert nu % tn == 0
    gl = w_gate_pad.shape[1]
    mspec = functools.partial(_mod_spec, d=d, tiles_per_batch=seq // bm, n_batch=n_batch)
    return pl.pallas_call(
        _ml_in_kernel,
        grid=(n_all // bm, nu // tn),
        in_specs=[
            pl.BlockSpec((bm, d), lambda i, j: (i, 0)),
            mspec(0), mspec(1),
            pl.BlockSpec((d, tn), lambda i, j: (0, j)),
            pl.BlockSpec((hq, d), lambda i, j: (0, 0)),
            pl.BlockSpec((d, gl), lambda i, j: (0, 0)),
            pl.BlockSpec((1, gl), lambda i, j: (0, 0)),
        ],
        out_specs=[
            pl.BlockSpec((bm, tn), lambda i, j: (i, j)),
            pl.BlockSpec((hq, bm), lambda i, j: (0, i)),
            pl.BlockSpec((bm, gl), lambda i, j: (i, 0)),
        ],
        out_shape=[jax.ShapeDtypeStruct((n_all, nu), BF16),
                   jax.ShapeDtypeStruct((hq, n_all), BF16),
                   jax.ShapeDtypeStruct((n_all, gl), F32)],
        scratch_shapes=[pltpu.VMEM((bm, d), BF16)],
        compiler_params=_cparams(("arbitrary", "arbitrary")),
        name="mlstm_in",
    )(xall, mod, mod, w_voq_b, w_kt_b, w_gate_pad, b_gate_pad)


def _dot_split3(a_b, x):
    hi = x.astype(BF16)
    r1 = x - hi.astype(F32)
    mid = r1.astype(BF16)
    lo = (r1 - mid.astype(F32)).astype(BF16)
    return (jnp.dot(a_b, hi, preferred_element_type=F32)
            + jnp.dot(a_b, mid, preferred_element_type=F32)
            + jnp.dot(a_b, lo, preferred_element_type=F32))


def _scan_kernel(n_heads, dqk, dv, q_ref, kt_ref, v_ref, g_ref, o_ref, ct_ref, m_ref):
    d = pl.program_id(1)
    s = pl.program_id(2)
    L = q_ref.shape[0]
    assert L == LANES
    scale = dqk ** -0.5

    @pl.when(s == 0)
    def _():
        ct_ref[...] = jnp.zeros_like(ct_ref)
        m_ref[...] = jnp.zeros_like(m_ref)

    H = n_heads
    heads = range(H)
    qi = lax.broadcasted_iota(I32, (L, L), 0)
    si = lax.broadcasted_iota(I32, (L, L), 1)
    fwd = d == 0
    mask = jnp.where(fwd, si - qi, qi - si) <= 0

    g = g_ref[...]
    b_all = _dot_split3(mask.astype(BF16), jax.nn.log_sigmoid(g))
    g_t = g.T
    b_t = b_all.T
    def lane_bcast(x, lane0):
        return jnp.stack([jnp.broadcast_to(x[:, lane0 + h:lane0 + h + 1], (L, LANES))
                          for h in heads])

    b_b = lane_bcast(b_all, H)
    i_b = lane_bcast(g, 0)
    b_end = jnp.where(fwd, b_b[:, L - 1:L, :], b_b[:, 0:1, :])
    m_st = m_ref[:, 0:1, :]
    tile = lambda x, n: jnp.concatenate([x] * (n // LANES), axis=-1)

    q3 = jnp.stack([q_ref[:, h * dqk:(h + 1) * dqk] for h in heads])
    kt3 = jnp.stack([kt_ref[h * dqk:(h + 1) * dqk, :] for h in heads])
    v3 = jnp.stack([v_ref[:, h * dv:(h + 1) * dv] for h in heads])

    r3 = jnp.stack([g_t[h:h + 1, :] - b_t[H + h:H + h + 1, :] for h in heads])
    dm = jnp.where(mask, b_b + r3, -jnp.inf)
    a_inter = b_b + m_st
    m_q = jnp.maximum(a_inter, jnp.max(dm, axis=-1, keepdims=True))
    inter = jnp.exp(a_inter - m_q) * scale
    sc = jnp.einsum("hqd,hds->hqs", q3, kt3, preferred_element_type=F32)
    p = jnp.exp(dm - m_q) * (sc * scale)
    ct = ct_ref[...]
    qc = jnp.einsum("hqd,hdv->hqv", q3, ct.astype(BF16), preferred_element_type=F32)
    v_ext = jnp.concatenate([v3, jnp.ones((H, L, LANES), BF16)], axis=-1)
    pv = jnp.einsum("hqs,hsv->hqv", p.astype(BF16), v_ext, preferred_element_type=F32)
    den = pv[:, :, dv:] + inter * qc[:, :, dv:]
    rden = 1.0 / jnp.maximum(jnp.abs(den), jnp.exp(-m_q))
    hout = (pv[:, :, :dv] + tile(inter, dv) * qc[:, :, :dv]) * tile(rden, dv)
    for h in heads:
        o_ref[:, h * dv:(h + 1) * dv] = hout[h]

    wl = b_end - b_b + i_b
    m_next = jnp.maximum(b_end + m_st, jnp.max(wl, axis=1, keepdims=True))
    decay = jnp.exp(b_end + m_st - m_next)
    w_b = jnp.exp(wl - m_next)
    vw = jnp.concatenate([v3.astype(F32) * tile(w_b, dv), w_b], axis=-1).astype(BF16)
    upd = jnp.einsum("hdl,hlv->hdv", kt3, vw, preferred_element_type=F32)
    ct_ref[...] = tile(decay, dv + LANES) * ct + upd
    m_ref[...] = jnp.broadcast_to(m_next, m_ref.shape)


def _scan(u, kt, gates, n_batch, seq, ctx_len, n_heads, dqk, dv):
    L = SCAN_CHUNK
    nx = n_batch * seq
    ncc = ctx_len // L
    nlc = seq // L
    assert ctx_len % L == 0 and seq % L == 0
    hq = n_heads * dqk
    hv = n_heads * dv
    assert hv == 2 * hq

    def row_blk(b, d, s):
        ctx = (nx + b * ctx_len) // L + jnp.where(d == 0, s, ncc - 1 - s)
        sl = s - ncc
        lat = (b * seq) // L + jnp.where(d == 0, sl, nlc - 1 - sl)
        return jnp.where(s < ncc, ctx, lat)

    def out_blk(b, d, s):
        sl = jnp.maximum(s - ncc, 0)
        return (b * seq) // L + jnp.where(d == 0, sl, nlc - 1 - sl)

    return pl.pallas_call(
        functools.partial(_scan_kernel, n_heads, dqk, dv),
        grid=(n_batch, 2, ncc + nlc),
        in_specs=[
            pl.BlockSpec((L, hq), lambda b, d, s: (row_blk(b, d, s), 2 * hv // hq)),
            pl.BlockSpec((hq, L), lambda b, d, s: (0, row_blk(b, d, s))),
            pl.BlockSpec((L, hv), lambda b, d, s: (row_blk(b, d, s), 0)),
            pl.BlockSpec((L, LANES), lambda b, d, s: (row_blk(b, d, s), d)),
        ],
        out_specs=pl.BlockSpec((None, L, hv), lambda b, d, s: (d, out_blk(b, d, s), 0)),
        out_shape=jax.ShapeDtypeStruct((2, nx, hv), F32),
        scratch_shapes=[pltpu.VMEM((n_heads, dqk, dv + LANES), F32),
                        pltpu.VMEM((n_heads, SUBLANES, LANES), F32)],
        compiler_params=_cparams(("arbitrary", "arbitrary", "arbitrary")),
        name="mlstm_scan",
    )(u, kt, u, gates)


def _mix_epilogue(alpha, rs, a, x_ref, g1_ref, sh2_ref, sc2_ref, wout_ref, lng_ref,
                  lnb_ref, rwt_ref, x1_ref, hx2_ref, logt_ref):
    mx = jnp.dot(a, wout_ref[...], preferred_element_type=F32)
    x1 = _layer_norm_rows(alpha * x_ref[rs, :] + g1_ref[0] * mx, lng_ref[...], lnb_ref[...])
    x1_ref[rs, :] = x1
    hx2 = x1 * (1.0 + sc2_ref[0]) + sh2_ref[0]
    hx2_ref[rs, :] = hx2
    logt_ref[:, rs] = lax.dot_general(rwt_ref[...], hx2.astype(BF16), (((1,), (1,)), ((), ())),
                                      preferred_element_type=F32)


def _sub_rows(tm, sub):
    return [slice(r, r + sub) for r in range(0, tm, sub)]


def _conv_out_kernel(alpha, ctx_mode, n_lat, tpb, sub, z_ref, zp_ref, zn_ref, bg_ref, cw_ref,
                     x_ref, g1_ref, sh2_ref, sc2_ref, wout_ref, lng_ref, lnb_ref, rwt_ref, *rest):
    x1_ref, hx2_ref, logt_ref = rest[-3:]
    i = pl.program_id(0)
    tm, d = z_ref.shape
    half = d // 2
    rows = lax.broadcasted_iota(I32, (sub, 1), 0)

    def conv1(zz, w3, period):
        pos = rows & (period - 1)
        prev = pltpu.roll(zz, 1, 0) * (pos != 0).astype(F32)
        nxt = pltpu.roll(zz, sub - 1, 0) * (pos != period - 1).astype(F32)
        return w3[0:1] * prev + w3[1:2] * zz + w3[2:3] * nxt

    def tile():
        cw = cw_ref[...]
        if not ctx_mode:
            ti = i % tpb
            up = zp_ref[...].astype(F32) * (ti > 0).astype(F32)
            dn = zn_ref[...].astype(F32) * (ti < tpb - 1).astype(F32)
            w3 = cw[:, half:]

        for rs in _sub_rows(tm, sub):
            z = z_ref[rs, :].astype(F32)
            bg = bg_ref[rs, :].astype(F32)
            if ctx_mode:
                a = (bg * conv1(z, cw, sub)).astype(BF16)
            else:
                a_row = (bg[:, :half] * conv1(z[:, :half], cw[:, :half], GRID_W)).astype(BF16)
                r0, r1 = rs.start, rs.stop
                f32_rows = lambda a, b: z_ref[a:b, half:].astype(F32)
                prev = (jnp.concatenate([up, f32_rows(0, r1 - GRID_W)], axis=0) if r0 == 0
                        else f32_rows(r0 - GRID_W, r1 - GRID_W))
                nxt = (jnp.concatenate([f32_rows(r0 + GRID_W, tm), dn], axis=0) if r1 == tm
                       else f32_rows(r0 + GRID_W, r1 + GRID_W))
                y = w3[0:1] * prev + w3[1:2] * z[:, half:] + w3[2:3] * nxt
                a = jnp.concatenate([a_row, (bg[:, half:] * y).astype(BF16)], axis=1)
            _mix_epilogue(alpha, rs, a, x_ref, g1_ref, sh2_ref, sc2_ref, wout_ref, lng_ref,
                          lnb_ref, rwt_ref, x1_ref, hx2_ref, logt_ref)

    if ctx_mode:
        tile()
    else:
        pl.when(i < n_lat)(tile)

        @pl.when(i >= n_lat)
        def _():
            x1_ref[...] = jnp.zeros_like(x1_ref)
            hx2_ref[...] = jnp.zeros_like(hx2_ref)
            logt_ref[...] = jnp.zeros_like(logt_ref)


def _ml_out_kernel(alpha, n_heads, dv, sub, hf_ref, hb_ref, og_ref, ng_ref, x_ref,
                   g1_ref, sh2_ref, sc2_ref, wout_ref, lng_ref, lnb_ref, rwt_ref,
                   x1_ref, hx2_ref, logt_ref):
    for rs in _sub_rows(hf_ref.shape[0], sub):
        parts = []
        for h in range(n_heads):
            sl = slice(h * dv, (h + 1) * dv)
            hs = hf_ref[rs, sl] + hb_ref[rs, sl]
            mu = jnp.mean(hs, axis=-1, keepdims=True)
            c = hs - mu
            var = jnp.mean(c * c, axis=-1, keepdims=True)
            hn = c * lax.rsqrt(var + HEAD_NORM_EPS)
            gate = jax.nn.sigmoid(og_ref[rs, sl].astype(F32))
            parts.append((hn * ng_ref[:, sl] * gate).astype(BF16))
        _mix_epilogue(alpha, rs, jnp.concatenate(parts, axis=1), x_ref, g1_ref, sh2_ref, sc2_ref, wout_ref, lng_ref,
                      lnb_ref, rwt_ref, x1_ref, hx2_ref, logt_ref)


def _mix_out_common(d, n_rows, n_exp, mod_row, tile0=0):
    tm = ROW_TILE
    mspec = lambda chunk: pl.BlockSpec((1, 1, d), lambda i: (mod_row(i), 0, chunk))
    const = lambda shape: pl.BlockSpec(shape, lambda i: (0,) * len(shape))
    in_specs = [mspec(2), mspec(3), mspec(4), const((d, d)), const((1, d)), const((1, d)),
                const((n_exp, d))]
    out_specs = [pl.BlockSpec((tm, d), lambda i: (i + tile0, 0)),
                 pl.BlockSpec((tm, d), lambda i: (i + tile0, 0)),
                 pl.BlockSpec((n_exp, tm), lambda i: (0, i + tile0))]
    out_shape = [jax.ShapeDtypeStruct((n_rows, d), F32),
                 jax.ShapeDtypeStruct((n_rows, d), F32),
                 jax.ShapeDtypeStruct((n_exp, n_rows), F32)]
    return in_specs, out_specs, out_shape


def _conv_out(alpha, z, bg, conv_w, x2d, c2d, mod, w_out_b, ln_g, ln_b, rwt_b, n_batch, seq,
              ctx_len):
    n_all, d = z.shape
    nx = x2d.shape[0]
    tm = ROW_TILE
    assert ctx_len == tm and seq % tm == 0 and MIX_SUB_ROWS % GRID_W == 0
    half = d // 2
    nxa = nx // tm
    tpb = seq // tm
    hpt = tm // GRID_W
    n_exp = rwt_b.shape[0]

    def z_specs(tile0):
        return [
            pl.BlockSpec((tm, d), lambda i: (i + tile0, 0)),
            pl.BlockSpec((GRID_W, half), lambda i: (jnp.maximum((i + tile0) * hpt - 1, 0), 1)),
            pl.BlockSpec((GRID_W, half), lambda i: ((i + tile0 + 1) * hpt - 1, 1)),
            pl.BlockSpec((tm, d), lambda i: (i + tile0, 0)),
            pl.BlockSpec((3, d), lambda i: (0, 0)),
            pl.BlockSpec((tm, d), lambda i: (i, 0)),
        ]

    lat = lambda i: jnp.minimum(i, nxa - 1)
    common_in, out_specs, out_shape = _mix_out_common(d, n_all, n_exp, lambda i: lat(i) // tpb)
    lat_specs = z_specs(0)
    lat_specs[2] = pl.BlockSpec((GRID_W, half), lambda i: ((lat(i) + 1) * hpt, 1))
    lat_specs[5] = pl.BlockSpec((tm, d), lambda i: (lat(i), 0))
    outs = pl.pallas_call(
        functools.partial(_conv_out_kernel, alpha, False, nxa, tpb, MIX_SUB_ROWS),
        grid=(n_all // tm,),
        in_specs=lat_specs + common_in,
        out_specs=out_specs,
        out_shape=out_shape,
        compiler_params=_cparams(("arbitrary",)),
        name="conv_out",
    )(z, z, z, bg, conv_w, x2d, mod, mod, mod, w_out_b, ln_g, ln_b, rwt_b)

    common_in, out_specs, out_shape = _mix_out_common(d, n_all, n_exp, lambda i: n_batch, nxa)
    n_in = 6 + len(common_in)
    keep = [pl.BlockSpec(memory_space=pl.ANY)] * 3
    return pl.pallas_call(
        functools.partial(_conv_out_kernel, alpha, True, None, tpb, tm),
        grid=(c2d.shape[0] // tm,),
        in_specs=z_specs(nxa) + common_in + keep,
        out_specs=out_specs,
        out_shape=out_shape,
        input_output_aliases={n_in: 0, n_in + 1: 1, n_in + 2: 2},
        compiler_params=_cparams(("arbitrary",)),
        name="conv_out_ctx",
    )(z, z, z, bg, conv_w, c2d, mod, mod, mod, w_out_b, ln_g, ln_b, rwt_b, *outs)


def _ml_out(alpha, hfb, u, norm_g, xall, mod, w_out_b, ln_g, ln_b, rwt_b, n_batch, seq,
            n_heads, dv):
    nx, d = hfb.shape[1:]
    tm = ROW_TILE
    tpb = seq // tm
    common_in, out_specs, out_shape = _mix_out_common(d, nx, rwt_b.shape[0], lambda i: i // tpb)
    o_blk = 1
    in_specs = [
        pl.BlockSpec((None, tm, d), lambda i: (0, i, 0)),
        pl.BlockSpec((None, tm, d), lambda i: (1, i, 0)),
        pl.BlockSpec((tm, d), lambda i: (i, o_blk)),
        pl.BlockSpec((1, d), lambda i: (0, 0)),
        pl.BlockSpec((tm, d), lambda i: (i, 0)),
    ] + common_in
    return pl.pallas_call(
        functools.partial(_ml_out_kernel, alpha, n_heads, dv, tm),
        grid=(nx // tm,),
        in_specs=in_specs,
        out_specs=out_specs,
        out_shape=out_shape,
        compiler_params=_cparams(("arbitrary",)),
        name="mlstm_out",
    )(hfb, hfb, u, norm_g, xall, mod, mod, mod, w_out_b, ln_g, ln_b, rwt_b)


def _route_kernel(tile_rows, n_te, logt_ref, rb_ref, pos_ref, wcol_ref, te_ref, meta_ref,
                  carry_ref):
    ph = pl.program_id(0)
    i = pl.program_id(1)
    n_steps = pl.num_programs(1)
    n_exp, tr = logt_ref.shape
    epg = n_exp // N_GROUPS

    @pl.when((ph == 0) & (i == 0))
    def _():
        carry_ref[...] = jnp.zeros_like(carry_ref)

    s = jax.nn.sigmoid(logt_ref[...])
    sel = s + rb_ref[...]
    row = lax.broadcasted_iota(I32, (n_exp, tr), 0)
    member = row % epg
    group = row // epg

    def partner(x, k, idx, span, unit):
        wrapped = (idx + k) >= span
        up = pltpu.roll(x, n_exp - k * unit, 0)
        down = pltpu.roll(x, (span - k) * unit, 0)
        return jnp.where(wrapped, down, up), wrapped

    rank_in = jnp.zeros((n_exp, tr), F32)
    for k in range(1, epg):
        p, wrapped = partner(sel, k, member, epg, 1)
        beats = (p > sel) | ((p == sel) & wrapped)
        rank_in = rank_in + beats.astype(F32)
    top = (rank_in < TOP_K).astype(F32)
    gs = sel * top
    score = gs
    for k in range(1, epg):
        p, _ = partner(gs, k, member, epg, 1)
        score = score + p
    n_better = jnp.zeros((n_exp, tr), F32)
    for k in range(1, N_GROUPS):
        p, wrapped = partner(score, k, group, N_GROUPS, epg)
        beats = (p > score) | ((p == score) & wrapped)
        n_better = n_better + beats.astype(F32)
    best = n_better == 0.0
    m1 = (best & (rank_in == 0.0)).astype(F32)
    m2 = (best & (rank_in == 1.0)).astype(F32)
    s1 = jnp.sum(s * m1, axis=0, keepdims=True)
    s2 = jnp.sum(s * m2, axis=0, keepdims=True)
    den = s1 + s2
    oh = m1 + m2

    ti = lax.broadcasted_iota(I32, (tr, tr), 0)
    tj = lax.broadcasted_iota(I32, (tr, tr), 1)
    before = (ti < tj).astype(BF16)
    slot = jnp.dot(oh.astype(BF16), before, preferred_element_type=F32) + carry_ref[...]

    @pl.when(ph == 1)
    def _():
        pos_ref[0:1, :] = jnp.sum(m1 * slot, axis=0, keepdims=True).astype(I32)
        pos_ref[1:2, :] = jnp.sum(m2 * slot, axis=0, keepdims=True).astype(I32)
        w8 = jnp.concatenate([s1 / den, s2 / den, jnp.zeros((LANES - 2, tr), F32)], axis=0)
        wcol_ref[...] = w8.T

    carry_ref[...] = carry_ref[...] + jnp.sum(oh, axis=1, keepdims=True)

    @pl.when((ph == 0) & (i == n_steps - 1))
    def _():
        cnt = carry_ref[...]
        ntile = jnp.floor((cnt + (tile_rows - 1)) * (1.0 / tile_rows))
        er = lax.broadcasted_iota(I32, (n_exp, LANES), 0)
        el = lax.broadcasted_iota(I32, (n_exp, LANES), 1)
        eye = (er == el).astype(F32)
        nt_row = jnp.sum(ntile * eye, axis=0, keepdims=True)
        cnt_row = jnp.sum(cnt * eye, axis=0, keepdims=True)
        cum_excl = jnp.sum(nt_row * (el < er).astype(F32), axis=1, keepdims=True)
        off = cum_excl * tile_rows
        off_row = jnp.sum(off * eye, axis=0, keepdims=True)
        total = jnp.sum(nt_row, axis=1, keepdims=True)
        cum_incl = cum_excl + ntile
        tl = lax.broadcasted_iota(I32, (n_exp, n_te), 1).astype(F32)
        te = jnp.sum((cum_incl <= tl).astype(F32), axis=0, keepdims=True)
        elf = el.astype(F32)
        later = (el > er) & (nt_row > 0.0) & (el < n_exp)
        nxt = jnp.min(jnp.where(later, elf, float(n_exp)), axis=1, keepdims=True)
        nxt = jnp.where(nxt == float(n_exp), er[:, 0:1].astype(F32), nxt)
        mine = (cum_excl <= tl) & (tl < cum_incl)
        te_next = jnp.sum(jnp.where(mine, nxt, 0.0), axis=0, keepdims=True)
        te_ref[...] = jnp.concatenate([jnp.minimum(te, n_exp - 1), te_next], axis=0).astype(I32)
        meta_ref[...] = jnp.concatenate(
            [cnt_row, off_row, jnp.broadcast_to(total, (1, LANES)),
             jnp.zeros((SUBLANES - 3, LANES), F32)], axis=0).astype(I32)
        carry_ref[...] = off


def _route(logt, router_b, tile_rows):
    n_exp, n = logt.shape
    tr = ROUTE_COLS
    assert n % tr == 0
    n_te = 256
    assert (2 * n) // tile_rows + n_exp <= n_te
    return pl.pallas_call(
        functools.partial(_route_kernel, tile_rows, n_te),
        grid=(2, n // tr),
        in_specs=[pl.BlockSpec((n_exp, tr), lambda p, i: (0, i)),
                  pl.BlockSpec((n_exp, 1), lambda p, i: (0, 0))],
        out_specs=[pl.BlockSpec((2, tr), lambda p, i: (0, i * p)),
                   pl.BlockSpec((tr, LANES), lambda p, i: (i * p, 0)),
                   pl.BlockSpec((2, n_te), lambda p, i: (0, 0)),
                   pl.BlockSpec((SUBLANES, LANES), lambda p, i: (0, 0))],
        out_shape=[jax.ShapeDtypeStruct((2, n), I32),
                   jax.ShapeDtypeStruct((n, LANES), F32),
                   jax.ShapeDtypeStruct((2, n_te), I32),
                   jax.ShapeDtypeStruct((SUBLANES, LANES), I32)],
        scratch_shapes=[pltpu.VMEM((n_exp, 1), F32)],
        compiler_params=_cparams(("arbitrary", "arbitrary")),
        name="route",
    )(logt, router_b.reshape(n_exp, 1).astype(F32))


def _tbl_kernel(n_tok, n_exp, tile_rows, meta_ref, pos_ref, tbl_ref):
    i = pl.program_id(0)
    blk = pos_ref.shape[1]

    @pl.when(i == 0)
    def _():
        n_pad = 0
        for e in range(n_exp):
            cnt = meta_ref[e]
            off = meta_ref[n_exp + e]
            up = ((cnt + (tile_rows - 1)) // tile_rows) * tile_rows

            def fill(r, j):
                tbl_ref[off + r] = (2 * n_tok + j) << SLOT_SRC_BITS
                return j + 1

            n_pad = lax.fori_loop(cnt, up, fill, n_pad)

        def fill_tail(r, carry):
            tbl_ref[r] = (2 * n_tok) << SLOT_SRC_BITS
            return carry

        lax.fori_loop(2 * n_tok + n_pad, tbl_ref.shape[0], fill_tail, 0)

    base = i * blk
    first = (base << SLOT_SRC_BITS) | jnp.where(base >= n_tok, base - n_tok, base)
    step = (1 << SLOT_SRC_BITS) + 1
    group = 8

    def body(q, carry):
        r0 = q * group
        slots = [pos_ref[0, r0 + j] for j in range(group)]
        for j in range(group):
            tbl_ref[slots[j]] = first + (r0 + j) * step
        return carry

    lax.fori_loop(0, blk // group, body, 0, unroll=True)


def _build_table(meta1d, pos, n_tok, n_exp, tile_rows, p_pad):
    blk = TBL_BLOCK
    while n_tok % blk:
        blk //= 2
    n_pairs = 2 * n_tok
    grid_spec = pltpu.PrefetchScalarGridSpec(
        num_scalar_prefetch=1,
        grid=(n_pairs // blk,),
        in_specs=[pl.BlockSpec((None, 1, blk), lambda i, m: (i, 0, 0), memory_space=pltpu.SMEM)],
        out_specs=pl.BlockSpec(memory_space=pltpu.SMEM),
    )
    return pl.pallas_call(
        functools.partial(_tbl_kernel, n_tok, n_exp, tile_rows),
        grid_spec=grid_spec,
        out_shape=jax.ShapeDtypeStruct((p_pad,), I32),
        compiler_params=_cparams(("arbitrary",)),
        name="slot_table",
    )(meta1d, pos.reshape(n_pairs // blk, 1, blk))


def _moe_kernel(layer, te_ref, ten_ref, nt_ref, tbl_ref, tbln_ref, tblp_ref, hx_hbm,
                wg_hbm, wu_hbm, wd_hbm, y_hbm, xbuf, ybuf, zbuf, wg_b, wu_b, wd_b,
                wg_s, wu_s, wd_s, gsem, ssem, zsem, wsem):
    t = pl.program_id(0)
    n = nt_ref[0]
    rows = tbl_ref.shape[1]
    staged = ((wg_hbm, wg_s, wg_b), (wu_hbm, wu_s, wu_b), (wd_hbm, wd_s, wd_b))

    def weight_copies(e):
        return [pltpu.make_async_copy(hbm.at[layer, e], stage, wsem.at[k])
                for k, (hbm, stage, _) in enumerate(staged)]

    @pl.when(t == 0)
    def _():
        for c in weight_copies(te_ref[0]):
            c.start()

    tl = jnp.minimum(t, n - 1)
    new_expert = (t < n) & ((t == 0) | (te_ref[tl] != te_ref[jnp.maximum(tl - 1, 0)]))

    @pl.when(new_expert)
    def _():
        for c, (_, stage, dst) in zip(weight_copies(te_ref[tl]), staged):
            c.wait()
            dst[...] = stage[...].astype(BF16)
        for c in weight_copies(ten_ref[tl]):
            c.start()

    def start_gather(tref, sl):
        for r in range(rows):
            src = tref[0, r] & SLOT_SRC_MASK
            pltpu.make_async_copy(hx_hbm.at[pl.ds(src, 1), :],
                                  xbuf.at[sl, pl.ds(r, 1), :], gsem.at[sl]).start()

    def start_scatter(tref, sl):
        for r in range(rows):
            dst = tref[0, r] >> SLOT_SRC_BITS
            pltpu.make_async_copy(ybuf.at[sl, pl.ds(r, 1), :],
                                  y_hbm.at[pl.ds(dst, 1), :], ssem.at[sl]).start()

    def wait_gather(sl):
        pltpu.make_async_copy(xbuf.at[sl], xbuf.at[sl], gsem.at[sl]).wait()

    def wait_scatter(sl):
        pltpu.make_async_copy(ybuf.at[sl], ybuf.at[sl], ssem.at[sl]).wait()

    @pl.when(t == 0)
    def _():
        zbuf[...] = jnp.zeros_like(zbuf)
        start_gather(tbl_ref, 0)

    @pl.when(t >= n)
    def _():
        dst = y_hbm.at[pl.ds(pl.multiple_of(t * rows, rows), rows), :]
        fill = pltpu.make_async_copy(zbuf, dst, zsem)
        fill.start()
        fill.wait()

    def tile(slot):
        other = 1 - slot
        wait_gather(slot)

        @pl.when(t >= 2)
        def _():
            wait_scatter(slot)

        start_gather(tbln_ref, other)

        @pl.when(t >= 1)
        def _():
            start_scatter(tblp_ref, other)

        xb = xbuf[slot].astype(BF16)
        g = jnp.dot(xb, wg_b[...], preferred_element_type=F32)
        u = jnp.dot(xb, wu_b[...], preferred_element_type=F32)
        h = (g * jax.nn.sigmoid(g) * u).astype(BF16)
        ybuf[slot] = jnp.dot(h, wd_b[...], preferred_element_type=F32)

        @pl.when(t == n - 1)
        def _():
            start_scatter(tbl_ref, slot)
            wait_gather(other)

            @pl.when(t >= 1)
            def _():
                wait_scatter(other)

            wait_scatter(slot)
            for c in weight_copies(ten_ref[tl]):
                c.wait()

    for parity in (0, 1):
        pl.when((t < n) & (lax.rem(t, 2) == parity))(functools.partial(tile, parity))


def _moe(te2, nt1d, tbl, hx2, w_gate, w_up, w_down, layer, n_tok, p_pad):
    tmm = MOE_ROWS
    _, n_exp, d, f = w_gate.shape
    t_max = (2 * n_tok) // tmm + n_exp
    last = lambda t, nt: jnp.minimum(t, nt[0] - 1)
    tspec = lambda fn: pl.BlockSpec((None, 1, tmm), lambda t, te, ten, nt: (fn(t, nt), 0, 0),
                                    memory_space=pltpu.SMEM)
    hbm = pl.BlockSpec(memory_space=pl.ANY)
    grid_spec = pltpu.PrefetchScalarGridSpec(
        num_scalar_prefetch=3,
        grid=(t_max,),
        in_specs=[
            tspec(last),
            tspec(lambda t, nt: last(t + 1, nt)),
            tspec(lambda t, nt: jnp.maximum(last(t, nt) - 1, 0)),
            hbm, hbm, hbm, hbm,
        ],
        out_specs=hbm,
        scratch_shapes=[pltpu.VMEM((2, tmm, d), F32), pltpu.VMEM((2, tmm, d), F32),
                        pltpu.VMEM((tmm, d), F32),
                        pltpu.VMEM((d, f), BF16), pltpu.VMEM((d, f), BF16),
                        pltpu.VMEM((f, d), BF16),
                        pltpu.VMEM((d, f), F32), pltpu.VMEM((d, f), F32),
                        pltpu.VMEM((f, d), F32),
                        pltpu.SemaphoreType.DMA((2,)), pltpu.SemaphoreType.DMA((2,)),
                        pltpu.SemaphoreType.DMA(()), pltpu.SemaphoreType.DMA((3,))],
    )
    tbl3 = tbl.reshape(p_pad // tmm, 1, tmm)
    return pl.pallas_call(
        functools.partial(_moe_kernel, layer),
        grid_spec=grid_spec,
        out_shape=jax.ShapeDtypeStruct((t_max * tmm, d), F32),
        compiler_params=_cparams(("arbitrary",)),
        name="moe_experts",
    )(te2[0], te2[1], nt1d, tbl3, tbl3, tbl3, hx2, w_gate, w_up, w_down)


def _comb_kernel(alpha, x_ref, y0_ref, y1_ref, w_ref, g2_ref, lng_ref, lnb_ref, o_ref):
    w = w_ref[...]
    ex = w[:, 0:1] * y0_ref[...] + w[:, 1:2] * y1_ref[...]
    o_ref[...] = _layer_norm_rows(alpha * x_ref[...] + g2_ref[0] * ex, lng_ref[...], lnb_ref[...])


def _combine(alpha, x1, y, wcol, mod, ln_g, ln_b, n_tok, n_out, n_batch, seq):
    d = x1.shape[1]
    tm = ROW_TILE
    nblk = n_tok // tm
    mspec = functools.partial(_mod_spec, d=d, tiles_per_batch=seq // tm, n_batch=n_batch)
    return pl.pallas_call(
        functools.partial(_comb_kernel, alpha),
        grid=(n_out // tm,),
        in_specs=[
            pl.BlockSpec((tm, d), lambda i: (i, 0)),
            pl.BlockSpec((tm, d), lambda i: (i, 0)),
            pl.BlockSpec((tm, d), lambda i: (i + nblk, 0)),
            pl.BlockSpec((tm, LANES), lambda i: (i, 0)),
            mspec(5),
            pl.BlockSpec((1, d), lambda i: (0, 0)),
            pl.BlockSpec((1, d), lambda i: (0, 0)),
        ],
        out_specs=pl.BlockSpec((tm, d), lambda i: (i, 0)),
        out_shape=jax.ShapeDtypeStruct((n_out, d), F32),
        compiler_params=_cparams(("arbitrary",)),
        name="moe_combine",
    )(x1, y, y, wcol, mod, ln_g, ln_b)


def _moe_layer(alpha, x1, hx2, logt, mod, ln_g, ln_b, router_b, w_gate, w_up, w_down, layer,
               n_out, n_batch, seq):
    n_tok = hx2.shape[0]
    n_exp = w_gate.shape[1]
    tmm = MOE_ROWS
    p_max = 2 * n_tok + n_exp * tmm
    p_pad = -(-p_max // TBL_BLOCK) * TBL_BLOCK
    assert n_tok <= 1 << SLOT_SRC_BITS and p_max < 1 << (31 - SLOT_SRC_BITS)
    pos, wcol, te, meta = _route(logt, router_b, tmm)
    meta1d = meta[:2, :n_exp].reshape(-1)
    tbl = _build_table(meta1d, pos, n_tok, n_exp, tmm, p_pad)
    y = _moe(te, meta[2, :1], tbl, hx2, w_gate, w_up, w_down, layer, n_tok, p_pad)
    return _combine(alpha, x1, y, wcol, mod, ln_g, ln_b, n_tok, n_out, n_batch, seq)


def kernel(x, c, ctx, c_ctx, w_ada, b_ada, ln_g, ln_b, conv_w_in, conv_w, conv_w_out, ml_w_in, ml_w_gate, ml_b_gate, ml_norm_g, ml_w_out, router_w, router_b, exp_w_gate, exp_w_up, exp_w_down):
    n_batch, seq, d = x.shape
    ctx_len = ctx.shape[1]
    depth = w_ada.shape[0]
    assert depth == 2, "layer 0 is the conv mixer, layer 1 the mLSTM mixer"
    alpha = (2 * depth) ** 0.25
    n_heads = ml_b_gate.shape[-1] // 4
    dqk = d // (2 * n_heads)
    dv = d // n_heads
    nx = n_batch * seq
    nc = n_batch * ctx_len
    assert n_batch < SUBLANES and 2 * n_heads <= LANES

    x2d = x.reshape(nx, d)
    c2d = ctx.reshape(nc, d)
    cc = jnp.zeros((SUBLANES, d), F32).at[:n_batch].set(c).at[n_batch].set(c_ctx)
    mod = _ada(cc, w_ada, b_ada).reshape(depth, SUBLANES, 1, 6 * d)
    rwt_b = router_w.T.astype(BF16)

    bg, z = _conv_in(x2d, c2d, mod[0], _to_bf16(conv_w_in, 0), n_batch, seq)
    x1, hx2, logt = _conv_out(alpha, z, bg, conv_w[0], x2d, c2d, mod[0],
                              _to_bf16(conv_w_out, 0), ln_g[0, 0:1], ln_b[0, 0:1], rwt_b,
                              n_batch, seq, ctx_len)
    xall = _moe_layer(alpha, x1, hx2, logt, mod[0], ln_g[0, 1:2], ln_b[0, 1:2], router_b,
                      exp_w_gate, exp_w_up, exp_w_down, 0, nx + nc, n_batch, seq)

    w_gate = ml_w_gate[0]
    b_gate = ml_b_gate[0]
    h2 = 2 * n_heads
    w_gate_pad = (jnp.zeros((d, 2 * LANES), F32).at[:, :h2].set(w_gate[:, :h2])
                  .at[:, LANES:LANES + h2].set(w_gate[:, h2:])).astype(BF16)
    b_gate_pad = (jnp.zeros((1, 2 * LANES), F32).at[0, :h2].set(b_gate[:h2])
                  .at[0, LANES:LANES + h2].set(b_gate[h2:]))
    w_in_b = _to_bf16(ml_w_in, 0)
    hq = n_heads * dqk
    w_voq_b = jnp.concatenate([w_in_b[:, 2 * hq:], w_in_b[:, :hq]], axis=1)
    u, kt, gates = _ml_in(xall, mod[1], w_voq_b, w_in_b[:, hq:2 * hq].T, w_gate_pad,
                          b_gate_pad, n_batch, seq, nc)
    hfb = _scan(u, kt, gates, n_batch, seq, ctx_len, n_heads, dqk, dv)
    x1, hx2, logt = _ml_out(alpha, hfb, u, ml_norm_g[0:1], xall, mod[1],
                            _to_bf16(ml_w_out, 0), ln_g[1, 0:1], ln_b[1, 0:1], rwt_b,
                            n_batch, seq, n_heads, dv)
    out = _moe_layer(alpha, x1, hx2, logt, mod[1], ln_g[1, 1:2], ln_b[1, 1:2], router_b,
                     exp_w_gate, exp_w_up, exp_w_down, 1, nx, n_batch, seq)
    return out.reshape(n_batch, seq, d)
```

```python
import functools

import jax
import jax.numpy as jnp
from jax import lax
from jax.experimental import pallas as pl
from jax.experimental.pallas import tpu as pltpu

F32 = jnp.float32
BF16 = jnp.bfloat16
I32 = jnp.int32

GRID_W = 64
N_GROUPS = 4
TOP_K = 2
LN_EPS = 1e-5
HEAD_NORM_EPS = 1e-6

LANES = 128
SUBLANES = 8
VMEM_LIMIT_BYTES = 56 * 1024 * 1024

ROW_TILE = 256
MIX_SUB_ROWS = 128
PROJ_ROWS = 1024
MOE_ROWS = 256
ROUTE_COLS = 1024
COMB_ROWS = 512
SCAN_CHUNK = 128
TBL_BLOCK = 1024
SLOT_SRC_BITS = 15
SLOT_SRC_MASK = (1 << SLOT_SRC_BITS) - 1
CAST_BLOCK_BYTES = 8 * 1024 * 1024


def _cparams(sem):
    return pltpu.CompilerParams(dimension_semantics=sem, vmem_limit_bytes=VMEM_LIMIT_BYTES)


def _layer_norm_rows(r, g, b):
    mu = jnp.mean(r, axis=-1, keepdims=True)
    c = r - mu
    var = jnp.mean(c * c, axis=-1, keepdims=True)
    return c * lax.rsqrt(var + LN_EPS) * g + b


def _ada_kernel(cc_ref, w_ref, b_ref, o_ref):
    a = cc_ref[...]
    a = (a * jax.nn.sigmoid(a)).astype(BF16)
    o_ref[0] = jnp.dot(a, w_ref[0].astype(BF16), preferred_element_type=F32) + b_ref[0]


def _ada(cc, w_ada, b_ada):
    depth, d, n6 = w_ada.shape
    tn = 1024
    return pl.pallas_call(
        _ada_kernel,
        grid=(depth, n6 // tn),
        in_specs=[
            pl.BlockSpec((SUBLANES, d), lambda l, j: (0, 0)),
            pl.BlockSpec((1, d, tn), lambda l, j: (l, 0, j)),
            pl.BlockSpec((1, 1, tn), lambda l, j: (l, 0, j)),
        ],
        out_specs=pl.BlockSpec((1, SUBLANES, tn), lambda l, j: (l, 0, j)),
        out_shape=jax.ShapeDtypeStruct((depth, SUBLANES, n6), F32),
        compiler_params=_cparams(("arbitrary", "arbitrary")),
        name="ada_mod",
    )(cc, w_ada, b_ada.reshape(depth, 1, n6))


def _cast_kernel(w_ref, o_ref):
    o_ref[...] = w_ref[...].astype(BF16)


def _to_bf16(w, layer):
    rows, cols = w.shape[-2:]
    w4 = w.reshape(w.shape[0], -1, rows, cols)
    m = w4.shape[1]
    rb = rows
    while rb * cols * 4 > CAST_BLOCK_BYTES and rb % (4 * SUBLANES) == 0:
        rb //= 2
    out = pl.pallas_call(
        _cast_kernel,
        grid=(m, rows // rb),
        in_specs=[pl.BlockSpec((None, None, rb, cols), lambda e, r: (layer, e, r, 0))],
        out_specs=pl.BlockSpec((None, rb, cols), lambda e, r: (e, r, 0)),
        out_shape=jax.ShapeDtypeStruct((m, rows, cols), BF16),
        compiler_params=_cparams(("arbitrary", "arbitrary")),
        name="to_bf16",
    )(w4)
    return out.reshape(w.shape[1:])


def _cast_t_kernel(w_ref, o_ref):
    o_ref[...] = w_ref[...].T.astype(BF16)


def _ml_weights(w, layer, hq):
    _, d, ncol = w.shape
    nb = ncol // hq
    rb = min(d, 512)
    voq = pl.pallas_call(
        _cast_kernel,
        grid=(d // rb, nb - 1),
        in_specs=[pl.BlockSpec((None, rb, hq), lambda r, k: (layer, r, (k + 2) % nb))],
        out_specs=pl.BlockSpec((rb, hq), lambda r, k: (r, k)),
        out_shape=jax.ShapeDtypeStruct((d, ncol - hq), BF16),
        compiler_params=_cparams(("arbitrary", "arbitrary")),
        name="to_bf16_voq",
    )(w)
    kt = pl.pallas_call(
        _cast_t_kernel,
        grid=(d // rb,),
        in_specs=[pl.BlockSpec((None, rb, hq), lambda r: (layer, r, 1))],
        out_specs=pl.BlockSpec((hq, rb), lambda r: (0, r)),
        out_shape=jax.ShapeDtypeStruct((hq, d), BF16),
        compiler_params=_cparams(("arbitrary",)),
        name="to_bf16_kt",
    )(w)
    return voq, kt


def _mod_spec(chunk, d, tiles_per_batch, n_batch):
    return pl.BlockSpec(
        (1, 1, d),
        lambda i, *_: (jnp.minimum(i // tiles_per_batch, n_batch), 0, chunk))


def _conv_in_kernel(nxa, xa_ref, xb_ref, sh_ref, sc_ref, wb_ref, wc_ref, wv_ref,
                    bg_ref, z_ref, h_ref):
    i = pl.program_id(0)
    j = pl.program_id(1)

    @pl.when((j == 0) & (i < nxa))
    def _():
        h_ref[...] = (xa_ref[...] * (1.0 + sc_ref[0]) + sh_ref[0]).astype(BF16)

    @pl.when((j == 0) & (i >= nxa))
    def _():
        h_ref[...] = (xb_ref[...] * (1.0 + sc_ref[0]) + sh_ref[0]).astype(BF16)

    h = h_ref[...]
    bg = jnp.dot(h, wb_ref[...], preferred_element_type=F32)
    cg = jnp.dot(h, wc_ref[...], preferred_element_type=F32)
    v = jnp.dot(h, wv_ref[...], preferred_element_type=F32)
    bg_ref[...] = bg.astype(BF16)
    z_ref[...] = (cg * v).astype(BF16)


def _conv_in(x2d, c2d, mod, w_in_b, n_batch, seq):
    nx, d = x2d.shape
    nc = c2d.shape[0]
    bm = min(PROJ_ROWS, seq, nc)
    assert seq % bm == 0 and nc % bm == 0
    tn = 512 if d % 512 == 0 else d
    nj = d // tn
    nxa = nx // bm
    n_all = nx + nc
    tpb = seq // bm
    mspec = functools.partial(_mod_spec, d=d, tiles_per_batch=tpb, n_batch=n_batch)
    return pl.pallas_call(
        functools.partial(_conv_in_kernel, nxa),
        grid=(n_all // bm, nj),
        in_specs=[
            pl.BlockSpec((bm, d), lambda i, j: (jnp.minimum(i, nxa - 1), 0)),
            pl.BlockSpec((bm, d), lambda i, j: (jnp.maximum(i - nxa, 0), 0),
                         pipeline_mode=pl.Buffered(1)),
            mspec(0), mspec(1),
            pl.BlockSpec((d, tn), lambda i, j: (0, j)),
            pl.BlockSpec((d, tn), lambda i, j: (0, nj + j)),
            pl.BlockSpec((d, tn), lambda i, j: (0, 2 * nj + j)),
        ],
        out_specs=[
            pl.BlockSpec((bm, tn), lambda i, j: (i, j)),
            pl.BlockSpec((bm, tn), lambda i, j: (i, j)),
        ],
        out_shape=[jax.ShapeDtypeStruct((n_all, d), BF16),
                   jax.ShapeDtypeStruct((n_all, d), BF16)],
        scratch_shapes=[pltpu.VMEM((bm, d), BF16)],
        compiler_params=_cparams(("arbitrary", "arbitrary")),
        name="conv_in",
    )(x2d, c2d, mod, mod, w_in_b, w_in_b, w_in_b)


def _ml_in_kernel(x_ref, sh_ref, sc_ref, w_ref, wkt_ref, wg_ref, bgate_ref,
                  u_ref, kt_ref, g_ref, h_ref):
    j = pl.program_id(1)

    @pl.when(j == 0)
    def _():
        h = (x_ref[...] * (1.0 + sc_ref[0]) + sh_ref[0]).astype(BF16)
        h_ref[...] = h
        g_ref[...] = jnp.dot(h, wg_ref[...], preferred_element_type=F32) + bgate_ref[...]
        kt_ref[...] = lax.dot_general(wkt_ref[...], h, (((1,), (1,)), ((), ())),
                                      preferred_element_type=F32).astype(BF16)

    u_ref[...] = jnp.dot(h_ref[...], w_ref[...], preferred_element_type=F32).astype(BF16)


def _ml_in(xall, mod, w_voq_b, w_kt_b, w_gate_pad, b_gate_pad, n_batch, seq, n_ctx_rows):
    n_all, d = xall.shape
    nu = w_voq_b.shape[1]
    hq = w_kt_b.shape[0]
    bm = min(PROJ_ROWS, seq, n_ctx_rows)
    assert seq % bm == 0 and n_ctx_rows % bm == 0
    tn = hq
    assert nu % tn == 0
    gl = w_gate_pad.shape[1]
    mspec = functools.partial(_mod_spec, d=d, tiles_per_batch=seq // bm, n_batch=n_batch)
    return pl.pallas_call(
        _ml_in_kernel,
        grid=(n_all // bm, nu // tn),
        in_specs=[
            pl.BlockSpec((bm, d), lambda i, j: (i, 0)),
            mspec(0), mspec(1),
            pl.BlockSpec((d, tn), lambda i, j: (0, j)),
            pl.BlockSpec((hq, d), lambda i, j: (0, 0)),
            pl.BlockSpec((d, gl), lambda i, j: (0, 0)),
            pl.BlockSpec((1, gl), lambda i, j: (0, 0)),
        ],
        out_specs=[
            pl.BlockSpec((bm, tn), lambda i, j: (i, j)),
            pl.BlockSpec((hq, bm), lambda i, j: (0, i)),
            pl.BlockSpec((bm, gl), lambda i, j: (i, 0)),
        ],
        out_shape=[jax.ShapeDtypeStruct((n_all, nu), BF16),
                   jax.ShapeDtypeStruct((hq, n_all), BF16),
                   jax.ShapeDtypeStruct((n_all, gl), F32)],
        scratch_shapes=[pltpu.VMEM((bm, d), BF16)],
        compiler_params=_cparams(("arbitrary", "arbitrary")),
        name="mlstm_in",
    )(xall, mod, mod, w_voq_b, w_kt_b, w_gate_pad, b_gate_pad)


def _dot_split3(a_b, x):
    hi = x.astype(BF16)
    r1 = x - hi.astype(F32)
    mid = r1.astype(BF16)
    lo = (r1 - mid.astype(F32)).astype(BF16)
    return (jnp.dot(a_b, hi, preferred_element_type=F32)
            + jnp.dot(a_b, mid, preferred_element_type=F32)
            + jnp.dot(a_b, lo, preferred_element_type=F32))


def _scan_kernel(n_heads, dqk, dv, q_ref, kt_ref, v_ref, g_ref, o_ref, ct_ref, m_ref):
    d = pl.program_id(1)
    s = pl.program_id(2)
    L = q_ref.shape[0]
    assert L == LANES
    scale = dqk ** -0.5

    @pl.when(s == 0)
    def _():
        ct_ref[...] = jnp.zeros_like(ct_ref)
        m_ref[...] = jnp.zeros_like(m_ref)

    H = n_heads
    heads = range(H)
    qi = lax.broadcasted_iota(I32, (L, L), 0)
    si = lax.broadcasted_iota(I32, (L, L), 1)
    fwd = d == 0
    mask = jnp.where(fwd, si - qi, qi - si) <= 0

    g = g_ref[...]
    b_all = _dot_split3(mask.astype(BF16), jax.nn.log_sigmoid(g))
    g_t = g.T
    b_t = b_all.T
    def lane_bcast(x, lane0):
        return jnp.stack([jnp.broadcast_to(x[:, lane0 + h:lane0 + h + 1], (L, LANES))
                          for h in heads])

    b_b = lane_bcast(b_all, H)
    i_b = lane_bcast(g, 0)
    b_end = jnp.where(fwd, b_b[:, L - 1:L, :], b_b[:, 0:1, :])
    m_st = m_ref[:, 0:1, :]
    tile = lambda x, n: jnp.concatenate([x] * (n // LANES), axis=-1)

    q3 = jnp.stack([q_ref[:, h * dqk:(h + 1) * dqk] for h in heads])
    kt3 = jnp.stack([kt_ref[h * dqk:(h + 1) * dqk, :] for h in heads])
    v3 = jnp.stack([v_ref[:, h * dv:(h + 1) * dv] for h in heads])

    r3 = jnp.stack([g_t[h:h + 1, :] - b_t[H + h:H + h + 1, :] for h in heads])
    dm = jnp.where(mask, b_b + r3, -jnp.inf)
    a_inter = b_b + m_st
    m_q = jnp.maximum(a_inter, jnp.max(dm, axis=-1, keepdims=True))
    inter = jnp.exp(a_inter - m_q) * scale
    sc = jnp.einsum("hqd,hds->hqs", q3, kt3, preferred_element_type=F32)
    p = jnp.exp(dm - m_q) * (sc * scale)
    ct = ct_ref[...]
    qc = jnp.einsum("hqd,hdv->hqv", q3, ct.astype(BF16), preferred_element_type=F32)
    v_ext = jnp.concatenate([v3, jnp.ones((H, L, LANES), BF16)], axis=-1)
    pv = jnp.einsum("hqs,hsv->hqv", p.astype(BF16), v_ext, preferred_element_type=F32)
    den = pv[:, :, dv:] + inter * qc[:, :, dv:]
    rden = 1.0 / jnp.maximum(jnp.abs(den), jnp.exp(-m_q))
    hout = (pv[:, :, :dv] + tile(inter, dv) * qc[:, :, :dv]) * tile(rden, dv)
    for h in heads:
        o_ref[:, h * dv:(h + 1) * dv] = hout[h]

    wl = b_end - b_b + i_b
    m_next = jnp.maximum(b_end + m_st, jnp.max(wl, axis=1, keepdims=True))
    decay = jnp.exp(b_end + m_st - m_next)
    w_b = jnp.exp(wl - m_next)
    vw = jnp.concatenate([v3.astype(F32) * tile(w_b, dv), w_b], axis=-1).astype(BF16)
    upd = jnp.einsum("hdl,hlv->hdv", kt3, vw, preferred_element_type=F32)
    ct_ref[...] = tile(decay, dv + LANES) * ct + upd
    m_ref[...] = jnp.broadcast_to(m_next, m_ref.shape)


def _scan(u, kt, gates, n_batch, seq, ctx_len, n_heads, dqk, dv):
    L = SCAN_CHUNK
    nx = n_batch * seq
    ncc = ctx_len // L
    nlc = seq // L
    assert ctx_len % L == 0 and seq % L == 0
    hq = n_heads * dqk
    hv = n_heads * dv
    assert hv == 2 * hq

    def row_blk(b, d, s):
        ctx = (nx + b * ctx_len) // L + jnp.where(d == 0, s, ncc - 1 - s)
        sl = s - ncc
        lat = (b * seq) // L + jnp.where(d == 0, sl, nlc - 1 - sl)
        return jnp.where(s < ncc, ctx, lat)

    def out_blk(b, d, s):
        sl = jnp.maximum(s - ncc, 0)
        return (b * seq) // L + jnp.where(d == 0, sl, nlc - 1 - sl)

    return pl.pallas_call(
        functools.partial(_scan_kernel, n_heads, dqk, dv),
        grid=(n_batch, 2, ncc + nlc),
        in_specs=[
            pl.BlockSpec((L, hq), lambda b, d, s: (row_blk(b, d, s), 2 * hv // hq)),
            pl.BlockSpec((hq, L), lambda b, d, s: (0, row_blk(b, d, s))),
            pl.BlockSpec((L, hv), lambda b, d, s: (row_blk(b, d, s), 0)),
            pl.BlockSpec((L, LANES), lambda b, d, s: (row_blk(b, d, s), d)),
        ],
        out_specs=pl.BlockSpec((None, L, hv), lambda b, d, s: (d, out_blk(b, d, s), 0)),
        out_shape=jax.ShapeDtypeStruct((2, nx, hv), F32),
        scratch_shapes=[pltpu.VMEM((n_heads, dqk, dv + LANES), F32),
                        pltpu.VMEM((n_heads, SUBLANES, LANES), F32)],
        compiler_params=_cparams(("arbitrary", "arbitrary", "arbitrary")),
        name="mlstm_scan",
    )(u, kt, u, gates)


def _mix_epilogue(alpha, rs, a, x_ref, g1_ref, sh2_ref, sc2_ref, wout_ref, lng_ref,
                  lnb_ref, rwt_ref, x1_ref, hx2_ref, logt_ref):
    mx = jnp.dot(a, wout_ref[...], preferred_element_type=F32)
    x1 = _layer_norm_rows(alpha * x_ref[rs, :] + g1_ref[0] * mx, lng_ref[...], lnb_ref[...])
    x1_ref[rs, :] = x1
    hx2 = x1 * (1.0 + sc2_ref[0]) + sh2_ref[0]
    hx2_ref[rs, :] = hx2
    logt_ref[:, rs] = lax.dot_general(rwt_ref[...], hx2.astype(BF16), (((1,), (1,)), ((), ())),
                                      preferred_element_type=F32)


def _sub_rows(tm, sub):
    return [slice(r, r + sub) for r in range(0, tm, sub)]


def _conv_out_kernel(alpha, ctx_mode, n_lat, tpb, sub, z_ref, zp_ref, zn_ref, bg_ref, cw_ref,
                     x_ref, g1_ref, sh2_ref, sc2_ref, wout_ref, lng_ref, lnb_ref, rwt_ref, *rest):
    x1_ref, hx2_ref, logt_ref = rest[-3:]
    i = pl.program_id(0)
    tm, d = z_ref.shape
    half = d // 2
    rows = lax.broadcasted_iota(I32, (sub, 1), 0)

    def conv1(zz, w3, period):
        pos = rows & (period - 1)
        prev = pltpu.roll(zz, 1, 0) * (pos != 0).astype(F32)
        nxt = pltpu.roll(zz, sub - 1, 0) * (pos != period - 1).astype(F32)
        return w3[0:1] * prev + w3[1:2] * zz + w3[2:3] * nxt

    def tile():
        cw = cw_ref[...]
        if not ctx_mode:
            ti = i % tpb
            up = zp_ref[...].astype(F32) * (ti > 0).astype(F32)
            dn = zn_ref[...].astype(F32) * (ti < tpb - 1).astype(F32)
            w3 = cw[:, half:]

        for rs in _sub_rows(tm, sub):
            z = z_ref[rs, :].astype(F32)
            bg = bg_ref[rs, :].astype(F32)
            if ctx_mode:
                a = (bg * conv1(z, cw, sub)).astype(BF16)
            else:
                a_row = (bg[:, :half] * conv1(z[:, :half], cw[:, :half], GRID_W)).astype(BF16)
                r0, r1 = rs.start, rs.stop
                f32_rows = lambda a, b: z_ref[a:b, half:].astype(F32)
                prev = (jnp.concatenate([up, f32_rows(0, r1 - GRID_W)], axis=0) if r0 == 0
                        else f32_rows(r0 - GRID_W, r1 - GRID_W))
                nxt = (jnp.concatenate([f32_rows(r0 + GRID_W, tm), dn], axis=0) if r1 == tm
                       else f32_rows(r0 + GRID_W, r1 + GRID_W))
                y = w3[0:1] * prev + w3[1:2] * z[:, half:] + w3[2:3] * nxt
                a = jnp.concatenate([a_row, (bg[:, half:] * y).astype(BF16)], axis=1)
            _mix_epilogue(alpha, rs, a, x_ref, g1_ref, sh2_ref, sc2_ref, wout_ref, lng_ref,
                          lnb_ref, rwt_ref, x1_ref, hx2_ref, logt_ref)

    if ctx_mode:
        tile()
    else:
        pl.when(i < n_lat)(tile)

        @pl.when(i >= n_lat)
        def _():
            x1_ref[...] = jnp.zeros_like(x1_ref)
            hx2_ref[...] = jnp.zeros_like(hx2_ref)
            logt_ref[...] = jnp.zeros_like(logt_ref)


def _ml_out_kernel(alpha, n_heads, dv, sub, hf_ref, hb_ref, og_ref, ng_ref, x_ref,
                   g1_ref, sh2_ref, sc2_ref, wout_ref, lng_ref, lnb_ref, rwt_ref,
                   x1_ref, hx2_ref, logt_ref):
    for rs in _sub_rows(hf_ref.shape[0], sub):
        parts = []
        for h in range(n_heads):
            sl = slice(h * dv, (h + 1) * dv)
            hs = hf_ref[rs, sl] + hb_ref[rs, sl]
            mu = jnp.mean(hs, axis=-1, keepdims=True)
            c = hs - mu
            var = jnp.mean(c * c, axis=-1, keepdims=True)
            hn = c * lax.rsqrt(var + HEAD_NORM_EPS)
            gate = jax.nn.sigmoid(og_ref[rs, sl].astype(F32))
            parts.append((hn * ng_ref[:, sl] * gate).astype(BF16))
        _mix_epilogue(alpha, rs, jnp.concatenate(parts, axis=1), x_ref, g1_ref, sh2_ref, sc2_ref, wout_ref, lng_ref,
                      lnb_ref, rwt_ref, x1_ref, hx2_ref, logt_ref)


def _mix_out_common(d, n_rows, n_exp, mod_row, tile0=0):
    tm = ROW_TILE
    mspec = lambda chunk: pl.BlockSpec((1, 1, d), lambda i: (mod_row(i), 0, chunk))
    const = lambda shape: pl.BlockSpec(shape, lambda i: (0,) * len(shape))
    in_specs = [mspec(2), mspec(3), mspec(4), const((d, d)), const((1, d)), const((1, d)),
                const((n_exp, d))]
    out_specs = [pl.BlockSpec((tm, d), lambda i: (i + tile0, 0)),
                 pl.BlockSpec((tm, d), lambda i: (i + tile0, 0)),
                 pl.BlockSpec((n_exp, tm), lambda i: (0, i + tile0))]
    out_shape = [jax.ShapeDtypeStruct((n_rows, d), F32),
                 jax.ShapeDtypeStruct((n_rows, d), F32),
                 jax.ShapeDtypeStruct((n_exp, n_rows), F32)]
    return in_specs, out_specs, out_shape


def _conv_out(alpha, z, bg, conv_w, x2d, c2d, mod, w_out_b, ln_g, ln_b, rwt_b, n_batch, seq,
              ctx_len):
    n_all, d = z.shape
    nx = x2d.shape[0]
    tm = ROW_TILE
    assert ctx_len == tm and seq % tm == 0 and MIX_SUB_ROWS % GRID_W == 0
    half = d // 2
    nxa = nx // tm
    tpb = seq // tm
    hpt = tm // GRID_W
    n_exp = rwt_b.shape[0]

    def z_specs(tile0):
        return [
            pl.BlockSpec((tm, d), lambda i: (i + tile0, 0)),
            pl.BlockSpec((GRID_W, half), lambda i: (jnp.maximum((i + tile0) * hpt - 1, 0), 1)),
            pl.BlockSpec((GRID_W, half), lambda i: ((i + tile0 + 1) * hpt - 1, 1)),
            pl.BlockSpec((tm, d), lambda i: (i + tile0, 0)),
            pl.BlockSpec((3, d), lambda i: (0, 0)),
            pl.BlockSpec((tm, d), lambda i: (i, 0)),
        ]

    lat = lambda i: jnp.minimum(i, nxa - 1)
    common_in, out_specs, out_shape = _mix_out_common(d, n_all, n_exp, lambda i: lat(i) // tpb)
    lat_specs = z_specs(0)
    lat_specs[2] = pl.BlockSpec((GRID_W, half), lambda i: ((lat(i) + 1) * hpt, 1))
    lat_specs[5] = pl.BlockSpec((tm, d), lambda i: (lat(i), 0))
    outs = pl.pallas_call(
        functools.partial(_conv_out_kernel, alpha, False, nxa, tpb, MIX_SUB_ROWS),
        grid=(n_all // tm,),
        in_specs=lat_specs + common_in,
        out_specs=out_specs,
        out_shape=out_shape,
        compiler_params=_cparams(("arbitrary",)),
        name="conv_out",
    )(z, z, z, bg, conv_w, x2d, mod, mod, mod, w_out_b, ln_g, ln_b, rwt_b)

    common_in, out_specs, out_shape = _mix_out_common(d, n_all, n_exp, lambda i: n_batch, nxa)
    n_in = 6 + len(common_in)
    keep = [pl.BlockSpec(memory_space=pl.ANY)] * 3
    return pl.pallas_call(
        functools.partial(_conv_out_kernel, alpha, True, None, tpb, tm),
        grid=(c2d.shape[0] // tm,),
        in_specs=z_specs(nxa) + common_in + keep,
        out_specs=out_specs,
        out_shape=out_shape,
        input_output_aliases={n_in: 0, n_in + 1: 1, n_in + 2: 2},
        compiler_params=_cparams(("arbitrary",)),
        name="conv_out_ctx",
    )(z, z, z, bg, conv_w, c2d, mod, mod, mod, w_out_b, ln_g, ln_b, rwt_b, *outs)


def _ml_out(alpha, hfb, u, norm_g, xall, mod, w_out_b, ln_g, ln_b, rwt_b, n_batch, seq,
            n_heads, dv):
    nx, d = hfb.shape[1:]
    tm = ROW_TILE
    tpb = seq // tm
    common_in, out_specs, out_shape = _mix_out_common(d, nx, rwt_b.shape[0], lambda i: i // tpb)
    o_blk = 1
    in_specs = [
        pl.BlockSpec((None, tm, d), lambda i: (0, i, 0)),
        pl.BlockSpec((None, tm, d), lambda i: (1, i, 0)),
        pl.BlockSpec((tm, d), lambda i: (i, o_blk)),
        pl.BlockSpec((1, d), lambda i: (0, 0)),
        pl.BlockSpec((tm, d), lambda i: (i, 0)),
    ] + common_in
    return pl.pallas_call(
        functools.partial(_ml_out_kernel, alpha, n_heads, dv, tm),
        grid=(nx // tm,),
        in_specs=in_specs,
        out_specs=out_specs,
        out_shape=out_shape,
        compiler_params=_cparams(("arbitrary",)),
        name="mlstm_out",
    )(hfb, hfb, u, norm_g, xall, mod, mod, mod, w_out_b, ln_g, ln_b, rwt_b)


def _route_kernel(tile_rows, n_te, logt_ref, rb_ref, pos_ref, wcol_ref, te_ref, meta_ref,
                  carry_ref, before_ref):
    ph = pl.program_id(0)
    i = pl.program_id(1)
    n_steps = pl.num_programs(1)
    n_exp, tr = logt_ref.shape
    epg = n_exp // N_GROUPS

    @pl.when((ph == 0) & (i == 0))
    def _():
        carry_ref[...] = jnp.zeros_like(carry_ref)
        ti = lax.broadcasted_iota(I32, (tr, tr), 0)
        tj = lax.broadcasted_iota(I32, (tr, tr), 1)
        before_ref[...] = (ti < tj).astype(BF16)

    s = jax.nn.sigmoid(logt_ref[...])
    sel = s + rb_ref[...]
    row = lax.broadcasted_iota(I32, (n_exp, tr), 0)
    member = row % epg
    group = row // epg

    def partner(x, k, idx, span, unit):
        wrapped = (idx + k) >= span
        up = pltpu.roll(x, n_exp - k * unit, 0)
        down = pltpu.roll(x, (span - k) * unit, 0)
        return jnp.where(wrapped, down, up), wrapped

    rank_in = jnp.zeros((n_exp, tr), F32)
    for k in range(1, epg):
        p, wrapped = partner(sel, k, member, epg, 1)
        beats = (p > sel) | ((p == sel) & wrapped)
        rank_in = rank_in + beats.astype(F32)
    top = (rank_in < TOP_K).astype(F32)
    gs = sel * top
    score = gs
    for k in range(1, epg):
        p, _ = partner(gs, k, member, epg, 1)
        score = score + p
    n_better = jnp.zeros((n_exp, tr), F32)
    for k in range(1, N_GROUPS):
        p, wrapped = partner(score, k, group, N_GROUPS, epg)
        beats = (p > score) | ((p == score) & wrapped)
        n_better = n_better + beats.astype(F32)
    best = n_better == 0.0
    m1 = (best & (rank_in == 0.0)).astype(F32)
    m2 = (best & (rank_in == 1.0)).astype(F32)
    s1 = jnp.sum(s * m1, axis=0, keepdims=True)
    s2 = jnp.sum(s * m2, axis=0, keepdims=True)
    den = s1 + s2
    oh = m1 + m2

    slot = jnp.dot(oh.astype(BF16), before_ref[...], preferred_element_type=F32) + carry_ref[...]

    @pl.when(ph == 1)
    def _():
        pos_ref[0:1, :] = jnp.sum(m1 * slot, axis=0, keepdims=True).astype(I32)
        pos_ref[1:2, :] = jnp.sum(m2 * slot, axis=0, keepdims=True).astype(I32)
        w8 = jnp.concatenate([s1 / den, s2 / den, jnp.zeros((LANES - 2, tr), F32)], axis=0)
        wcol_ref[...] = w8.T

    carry_ref[...] = carry_ref[...] + jnp.sum(oh, axis=1, keepdims=True)

    @pl.when((ph == 0) & (i == n_steps - 1))
    def _():
        cnt = carry_ref[...]
        ntile = jnp.floor((cnt + (tile_rows - 1)) * (1.0 / tile_rows))
        er = lax.broadcasted_iota(I32, (n_exp, LANES), 0)
        el = lax.broadcasted_iota(I32, (n_exp, LANES), 1)
        eye = (er == el).astype(F32)
        nt_row = jnp.sum(ntile * eye, axis=0, keepdims=True)
        cnt_row = jnp.sum(cnt * eye, axis=0, keepdims=True)
        cum_excl = jnp.sum(nt_row * (el < er).astype(F32), axis=1, keepdims=True)
        off = cum_excl * tile_rows
        off_row = jnp.sum(off * eye, axis=0, keepdims=True)
        total = jnp.sum(nt_row, axis=1, keepdims=True)
        cum_incl = cum_excl + ntile
        tl = lax.broadcasted_iota(I32, (n_exp, n_te), 1).astype(F32)
        te = jnp.sum((cum_incl <= tl).astype(F32), axis=0, keepdims=True)
        elf = el.astype(F32)
        later = (el > er) & (nt_row > 0.0) & (el < n_exp)
        nxt = jnp.min(jnp.where(later, elf, float(n_exp)), axis=1, keepdims=True)
        nxt = jnp.where(nxt == float(n_exp), er[:, 0:1].astype(F32), nxt)
        mine = (cum_excl <= tl) & (tl < cum_incl)
        te_next = jnp.sum(jnp.where(mine, nxt, 0.0), axis=0, keepdims=True)
        te_ref[...] = jnp.concatenate([jnp.minimum(te, n_exp - 1), te_next], axis=0).astype(I32)
        meta_ref[...] = jnp.concatenate(
            [cnt_row, off_row, jnp.broadcast_to(total, (1, LANES)),
             jnp.zeros((SUBLANES - 3, LANES), F32)], axis=0).astype(I32)
        carry_ref[...] = off


def _route(logt, router_b, tile_rows):
    n_exp, n = logt.shape
    tr = ROUTE_COLS
    while n % tr:
        tr //= 2
    n_te = 256
    assert (2 * n) // tile_rows + n_exp <= n_te
    return pl.pallas_call(
        functools.partial(_route_kernel, tile_rows, n_te),
        grid=(2, n // tr),
        in_specs=[pl.BlockSpec((n_exp, tr), lambda p, i: (0, i)),
                  pl.BlockSpec((n_exp, 1), lambda p, i: (0, 0))],
        out_specs=[pl.BlockSpec((2, tr), lambda p, i: (0, i * p)),
                   pl.BlockSpec((tr, LANES), lambda p, i: (i * p, 0)),
                   pl.BlockSpec((2, n_te), lambda p, i: (0, 0)),
                   pl.BlockSpec((SUBLANES, LANES), lambda p, i: (0, 0))],
        out_shape=[jax.ShapeDtypeStruct((2, n), I32),
                   jax.ShapeDtypeStruct((n, LANES), F32),
                   jax.ShapeDtypeStruct((2, n_te), I32),
                   jax.ShapeDtypeStruct((SUBLANES, LANES), I32)],
        scratch_shapes=[pltpu.VMEM((n_exp, 1), F32), pltpu.VMEM((tr, tr), BF16)],
        compiler_params=_cparams(("arbitrary", "arbitrary")),
        name="route",
    )(logt, router_b.reshape(n_exp, 1).astype(F32))


def _tbl_kernel(n_tok, n_exp, tile_rows, meta_ref, pos_ref, tbl_ref):
    i = pl.program_id(0)
    blk = pos_ref.shape[1]

    @pl.when(i == 0)
    def _():
        n_pad = 0
        for e in range(n_exp):
            cnt = meta_ref[e]
            off = meta_ref[n_exp + e]
            up = ((cnt + (tile_rows - 1)) // tile_rows) * tile_rows

            def fill(r, j):
                tbl_ref[off + r] = (2 * n_tok + j) << SLOT_SRC_BITS
                return j + 1

            n_pad = lax.fori_loop(cnt, up, fill, n_pad)

        def fill_tail(r, carry):
            tbl_ref[r] = (2 * n_tok) << SLOT_SRC_BITS
            return carry

        lax.fori_loop(2 * n_tok + n_pad, tbl_ref.shape[0], fill_tail, 0)

    base = i * blk
    first = (base << SLOT_SRC_BITS) | jnp.where(base >= n_tok, base - n_tok, base)
    step = (1 << SLOT_SRC_BITS) + 1
    group = 8

    def body(q, carry):
        r0 = q * group
        slots = [pos_ref[0, r0 + j] for j in range(group)]
        for j in range(group):
            tbl_ref[slots[j]] = first + (r0 + j) * step
        return carry

    lax.fori_loop(0, blk // group, body, 0, unroll=True)


def _build_table(meta1d, pos, n_tok, n_exp, tile_rows, p_pad):
    blk = TBL_BLOCK
    while n_tok % blk:
        blk //= 2
    n_pairs = 2 * n_tok
    grid_spec = pltpu.PrefetchScalarGridSpec(
        num_scalar_prefetch=1,
        grid=(n_pairs // blk,),
        in_specs=[pl.BlockSpec((None, 1, blk), lambda i, m: (i, 0, 0), memory_space=pltpu.SMEM)],
        out_specs=pl.BlockSpec(memory_space=pltpu.SMEM),
    )
    return pl.pallas_call(
        functools.partial(_tbl_kernel, n_tok, n_exp, tile_rows),
        grid_spec=grid_spec,
        out_shape=jax.ShapeDtypeStruct((p_pad,), I32),
        compiler_params=_cparams(("arbitrary",)),
        name="slot_table",
    )(meta1d, pos.reshape(n_pairs // blk, 1, blk))


def _moe_kernel(layer, te_ref, ten_ref, nt_ref, tbl_ref, tbln_ref, tblp_ref, hx_hbm,
                wg_hbm, wu_hbm, wd_hbm, y_hbm, xbuf, ybuf, zbuf, wg_b, wu_b, wd_b,
                wg_s, wu_s, wd_s, gsem, ssem, zsem, wsem):
    t = pl.program_id(0)
    n = nt_ref[0]
    rows = tbl_ref.shape[1]
    staged = ((wg_hbm, wg_s, wg_b), (wu_hbm, wu_s, wu_b), (wd_hbm, wd_s, wd_b))

    def weight_copies(e):
        return [pltpu.make_async_copy(hbm.at[layer, e], stage, wsem.at[k])
                for k, (hbm, stage, _) in enumerate(staged)]

    @pl.when(t == 0)
    def _():
        for c in weight_copies(te_ref[0]):
            c.start()

    tl = jnp.minimum(t, n - 1)
    new_expert = (t < n) & ((t == 0) | (te_ref[tl] != te_ref[jnp.maximum(tl - 1, 0)]))

    @pl.when(new_expert)
    def _():
        for c, (_, stage, dst) in zip(weight_copies(te_ref[tl]), staged):
            c.wait()
            dst[...] = stage[...].astype(BF16)
        for c in weight_copies(ten_ref[tl]):
            c.start()

    def start_gather(tref, sl):
        for r in range(rows):
            src = tref[0, r] & SLOT_SRC_MASK
            pltpu.make_async_copy(hx_hbm.at[pl.ds(src, 1), :],
                                  xbuf.at[sl, pl.ds(r, 1), :], gsem.at[sl]).start()

    def start_scatter(tref, sl):
        for r in range(rows):
            dst = tref[0, r] >> SLOT_SRC_BITS
            pltpu.make_async_copy(ybuf.at[sl, pl.ds(r, 1), :],
                                  y_hbm.at[pl.ds(dst, 1), :], ssem.at[sl]).start()

    def wait_gather(sl):
        pltpu.make_async_copy(xbuf.at[sl], xbuf.at[sl], gsem.at[sl]).wait()

    def wait_scatter(sl):
        pltpu.make_async_copy(ybuf.at[sl], ybuf.at[sl], ssem.at[sl]).wait()

    @pl.when(t == 0)
    def _():
        zbuf[...] = jnp.zeros_like(zbuf)
        start_gather(tbl_ref, 0)

    @pl.when(t >= n)
    def _():
        dst = y_hbm.at[pl.ds(pl.multiple_of(t * rows, rows), rows), :]
        fill = pltpu.make_async_copy(zbuf, dst, zsem)
        fill.start()
        fill.wait()

    def tile(slot):
        other = 1 - slot
        wait_gather(slot)

        @pl.when(t >= 2)
        def _():
            wait_scatter(slot)

        start_gather(tbln_ref, other)

        @pl.when(t >= 1)
        def _():
            start_scatter(tblp_ref, other)

        xb = xbuf[slot].astype(BF16)
        g = jnp.dot(xb, wg_b[...], preferred_element_type=F32)
        u = jnp.dot(xb, wu_b[...], preferred_element_type=F32)
        h = (g * jax.nn.sigmoid(g) * u).astype(BF16)
        ybuf[slot] = jnp.dot(h, wd_b[...], preferred_element_type=F32)

        @pl.when(t == n - 1)
        def _():
            start_scatter(tbl_ref, slot)
            wait_gather(other)

            @pl.when(t >= 1)
            def _():
                wait_scatter(other)

            wait_scatter(slot)
            for c in weight_copies(ten_ref[tl]):
                c.wait()

    for parity in (0, 1):
        pl.when((t < n) & (lax.rem(t, 2) == parity))(functools.partial(tile, parity))


def _moe(te2, nt1d, tbl, hx2, w_gate, w_up, w_down, layer, n_tok, p_pad):
    tmm = MOE_ROWS
    _, n_exp, d, f = w_gate.shape
    t_max = (2 * n_tok) // tmm + n_exp
    last = lambda t, nt: jnp.minimum(t, nt[0] - 1)
    tspec = lambda fn: pl.BlockSpec((None, 1, tmm), lambda t, te, ten, nt: (fn(t, nt), 0, 0),
                                    memory_space=pltpu.SMEM)
    hbm = pl.BlockSpec(memory_space=pl.ANY)
    grid_spec = pltpu.PrefetchScalarGridSpec(
        num_scalar_prefetch=3,
        grid=(t_max,),
        in_specs=[
            tspec(last),
            tspec(lambda t, nt: last(t + 1, nt)),
            tspec(lambda t, nt: jnp.maximum(last(t, nt) - 1, 0)),
            hbm, hbm, hbm, hbm,
        ],
        out_specs=hbm,
        scratch_shapes=[pltpu.VMEM((2, tmm, d), F32), pltpu.VMEM((2, tmm, d), F32),
                        pltpu.VMEM((tmm, d), F32),
                        pltpu.VMEM((d, f), BF16), pltpu.VMEM((d, f), BF16),
                        pltpu.VMEM((f, d), BF16),
                        pltpu.VMEM((d, f), F32), pltpu.VMEM((d, f), F32),
                        pltpu.VMEM((f, d), F32),
                        pltpu.SemaphoreType.DMA((2,)), pltpu.SemaphoreType.DMA((2,)),
                        pltpu.SemaphoreType.DMA(()), pltpu.SemaphoreType.DMA((3,))],
    )
    tbl3 = tbl.reshape(p_pad // tmm, 1, tmm)
    return pl.pallas_call(
        functools.partial(_moe_kernel, layer),
        grid_spec=grid_spec,
        out_shape=jax.ShapeDtypeStruct((t_max * tmm, d), F32),
        compiler_params=_cparams(("arbitrary",)),
        name="moe_experts",
    )(te2[0], te2[1], nt1d, tbl3, tbl3, tbl3, hx2, w_gate, w_up, w_down)


def _comb_kernel(alpha, x_ref, y0_ref, y1_ref, w_ref, g2_ref, lng_ref, lnb_ref, o_ref):
    w = w_ref[...]
    ex = w[:, 0:1] * y0_ref[...] + w[:, 1:2] * y1_ref[...]
    o_ref[...] = _layer_norm_rows(alpha * x_ref[...] + g2_ref[0] * ex, lng_ref[...], lnb_ref[...])


def _combine(alpha, x1, y, wcol, mod, ln_g, ln_b, n_tok, n_out, n_batch, seq):
    d = x1.shape[1]
    tm = COMB_ROWS
    assert n_tok % tm == 0 and n_out % tm == 0 and seq % tm == 0
    nblk = n_tok // tm
    mspec = functools.partial(_mod_spec, d=d, tiles_per_batch=seq // tm, n_batch=n_batch)
    return pl.pallas_call(
        functools.partial(_comb_kernel, alpha),
        grid=(n_out // tm,),
        in_specs=[
            pl.BlockSpec((tm, d), lambda i: (i, 0)),
            pl.BlockSpec((tm, d), lambda i: (i, 0)),
            pl.BlockSpec((tm, d), lambda i: (i + nblk, 0)),
            pl.BlockSpec((tm, LANES), lambda i: (i, 0)),
            mspec(5),
            pl.BlockSpec((1, d), lambda i: (0, 0)),
            pl.BlockSpec((1, d), lambda i: (0, 0)),
        ],
        out_specs=pl.BlockSpec((tm, d), lambda i: (i, 0)),
        out_shape=jax.ShapeDtypeStruct((n_out, d), F32),
        compiler_params=_cparams(("arbitrary",)),
        name="moe_combine",
    )(x1, y, y, wcol, mod, ln_g, ln_b)


def _moe_layer(alpha, x1, hx2, logt, mod, ln_g, ln_b, router_b, w_gate, w_up, w_down, layer,
               n_out, n_batch, seq):
    n_tok = hx2.shape[0]
    n_exp = w_gate.shape[1]
    tmm = MOE_ROWS
    p_max = 2 * n_tok + n_exp * tmm
    p_pad = -(-p_max // TBL_BLOCK) * TBL_BLOCK
    assert n_tok <= 1 << SLOT_SRC_BITS and p_max < 1 << (31 - SLOT_SRC_BITS)
    pos, wcol, te, meta = _route(logt, router_b, tmm)
    meta1d = meta[:2, :n_exp].reshape(-1)
    tbl = _build_table(meta1d, pos, n_tok, n_exp, tmm, p_pad)
    y = _moe(te, meta[2, :1], tbl, hx2, w_gate, w_up, w_down, layer, n_tok, p_pad)
    return _combine(alpha, x1, y, wcol, mod, ln_g, ln_b, n_tok, n_out, n_batch, seq)


def kernel(x, c, ctx, c_ctx, w_ada, b_ada, ln_g, ln_b, conv_w_in, conv_w, conv_w_out, ml_w_in, ml_w_gate, ml_b_gate, ml_norm_g, ml_w_out, router_w, router_b, exp_w_gate, exp_w_up, exp_w_down):
    n_batch, seq, d = x.shape
    ctx_len = ctx.shape[1]
    depth = w_ada.shape[0]
    assert depth == 2, "layer 0 is the conv mixer, layer 1 the mLSTM mixer"
    alpha = (2 * depth) ** 0.25
    n_heads = ml_b_gate.shape[-1] // 4
    dqk = d // (2 * n_heads)
    dv = d // n_heads
    nx = n_batch * seq
    nc = n_batch * ctx_len
    assert n_batch < SUBLANES and 2 * n_heads <= LANES

    x2d = x.reshape(nx, d)
    c2d = ctx.reshape(nc, d)
    cc = jnp.zeros((SUBLANES, d), F32).at[:n_batch].set(c).at[n_batch].set(c_ctx)
    mod = _ada(cc, w_ada, b_ada).reshape(depth, SUBLANES, 1, 6 * d)
    rwt_b = router_w.T.astype(BF16)

    bg, z = _conv_in(x2d, c2d, mod[0], _to_bf16(conv_w_in, 0), n_batch, seq)
    x1, hx2, logt = _conv_out(alpha, z, bg, conv_w[0], x2d, c2d, mod[0],
                              _to_bf16(conv_w_out, 0), ln_g[0, 0:1], ln_b[0, 0:1], rwt_b,
                              n_batch, seq, ctx_len)
    xall = _moe_layer(alpha, x1, hx2, logt, mod[0], ln_g[0, 1:2], ln_b[0, 1:2], router_b,
                      exp_w_gate, exp_w_up, exp_w_down, 0, nx + nc, n_batch, seq)

    w_gate = ml_w_gate[0]
    b_gate = ml_b_gate[0]
    h2 = 2 * n_heads
    w_gate_pad = (jnp.zeros((d, 2 * LANES), F32).at[:, :h2].set(w_gate[:, :h2])
                  .at[:, LANES:LANES + h2].set(w_gate[:, h2:])).astype(BF16)
    b_gate_pad = (jnp.zeros((1, 2 * LANES), F32).at[0, :h2].set(b_gate[:h2])
                  .at[0, LANES:LANES + h2].set(b_gate[h2:]))
    w_voq_b, w_kt_b = _ml_weights(ml_w_in, 0, n_heads * dqk)
    u, kt, gates = _ml_in(xall, mod[1], w_voq_b, w_kt_b, w_gate_pad, b_gate_pad, n_batch,
                          seq, nc)
    hfb = _scan(u, kt, gates, n_batch, seq, ctx_len, n_heads, dqk, dv)
    x1, hx2, logt = _ml_out(alpha, hfb, u, ml_norm_g[0:1], xall, mod[1],
                            _to_bf16(ml_w_out, 0), ln_g[1, 0:1], ln_b[1, 0:1], rwt_b,
                            n_batch, seq, n_heads, dv)
    out = _moe_layer(alpha, x1, hx2, logt, mod[1], ln_g[1, 1:2], ln_b[1, 1:2], router_b,
                     exp_w_gate, exp_w_up, exp_w_down, 1, nx, n_batch, seq)
    return out.reshape(n_batch, seq, d)
```

```python
import functools

import jax
import jax.numpy as jnp
from jax import lax
from jax.experimental import pallas as pl
from jax.experimental.pallas import tpu as pltpu

F32 = jnp.float32
BF16 = jnp.bfloat16
I32 = jnp.int32

GRID_W = 64
N_GROUPS = 4
TOP_K = 2
LN_EPS = 1e-5
HEAD_NORM_EPS = 1e-6

LANES = 128
SUBLANES = 8
VMEM_LIMIT_BYTES = 56 * 1024 * 1024

ROW_TILE = 256
MIX_SUB_ROWS = 128
PROJ_ROWS = 1024
MOE_ROWS = 256
MOE_UP_COLS = 256
MOE_DOWN_COLS = 512
ROUTE_COLS = 1024
COMB_ROWS = 512
SCAN_CHUNK = 128
TBL_BLOCK = 1024
SLOT_SRC_BITS = 15
SLOT_SRC_MASK = (1 << SLOT_SRC_BITS) - 1
CAST_BLOCK_BYTES = 8 * 1024 * 1024
WEIGHT_DMA_PRIORITY = 1


def _cparams(sem):
    return pltpu.CompilerParams(dimension_semantics=sem, vmem_limit_bytes=VMEM_LIMIT_BYTES)


def _layer_norm_rows(r, g, b):
    mu = jnp.mean(r, axis=-1, keepdims=True)
    c = r - mu
    var = jnp.mean(c * c, axis=-1, keepdims=True)
    return c * lax.rsqrt(var + LN_EPS) * g + b


def _ada_kernel(cc_ref, w_ref, b_ref, o_ref):
    a = cc_ref[...]
    a = (a * jax.nn.sigmoid(a)).astype(BF16)
    o_ref[0] = jnp.dot(a, w_ref[0].astype(BF16), preferred_element_type=F32) + b_ref[0]


def _ada(cc, w_ada, b_ada):
    depth, d, n6 = w_ada.shape
    tn = 1024
    return pl.pallas_call(
        _ada_kernel,
        grid=(depth, n6 // tn),
        in_specs=[
            pl.BlockSpec((SUBLANES, d), lambda l, j: (0, 0)),
            pl.BlockSpec((1, d, tn), lambda l, j: (l, 0, j)),
            pl.BlockSpec((1, 1, tn), lambda l, j: (l, 0, j)),
        ],
        out_specs=pl.BlockSpec((1, SUBLANES, tn), lambda l, j: (l, 0, j)),
        out_shape=jax.ShapeDtypeStruct((depth, SUBLANES, n6), F32),
        compiler_params=_cparams(("arbitrary", "arbitrary")),
        name="ada_mod",
    )(cc, w_ada, b_ada.reshape(depth, 1, n6))


def _cast_kernel(w_ref, o_ref):
    o_ref[...] = w_ref[...].astype(BF16)


def _to_bf16(w, layer):
    rows, cols = w.shape[-2:]
    w4 = w.reshape(w.shape[0], -1, rows, cols)
    m = w4.shape[1]
    rb = rows
    while rb * cols * 4 > CAST_BLOCK_BYTES and rb % (4 * SUBLANES) == 0:
        rb //= 2
    out = pl.pallas_call(
        _cast_kernel,
        grid=(m, rows // rb),
        in_specs=[pl.BlockSpec((None, None, rb, cols), lambda e, r: (layer, e, r, 0))],
        out_specs=pl.BlockSpec((None, rb, cols), lambda e, r: (e, r, 0)),
        out_shape=jax.ShapeDtypeStruct((m, rows, cols), BF16),
        compiler_params=_cparams(("arbitrary", "arbitrary")),
        name="to_bf16",
    )(w4)
    return out.reshape(w.shape[1:])


def _cast_t_kernel(w_ref, o_ref):
    o_ref[...] = w_ref[...].T.astype(BF16)


def _ml_weights(w, layer, hq):
    _, d, ncol = w.shape
    nb = ncol // hq
    rb = min(d, 512)
    voq = pl.pallas_call(
        _cast_kernel,
        grid=(d // rb, nb - 1),
        in_specs=[pl.BlockSpec((None, rb, hq), lambda r, k: (layer, r, (k + 2) % nb))],
        out_specs=pl.BlockSpec((rb, hq), lambda r, k: (r, k)),
        out_shape=jax.ShapeDtypeStruct((d, ncol - hq), BF16),
        compiler_params=_cparams(("arbitrary", "arbitrary")),
        name="to_bf16_voq",
    )(w)
    kt = pl.pallas_call(
        _cast_t_kernel,
        grid=(d // rb,),
        in_specs=[pl.BlockSpec((None, rb, hq), lambda r: (layer, r, 1))],
        out_specs=pl.BlockSpec((hq, rb), lambda r: (0, r)),
        out_shape=jax.ShapeDtypeStruct((hq, d), BF16),
        compiler_params=_cparams(("arbitrary",)),
        name="to_bf16_kt",
    )(w)
    return voq, kt


def _mod_spec(chunk, d, tiles_per_batch, n_batch):
    return pl.BlockSpec(
        (1, 1, d),
        lambda i, *_: (jnp.minimum(i // tiles_per_batch, n_batch), 0, chunk))


def _conv_in_kernel(nxa, xa_ref, xb_ref, sh_ref, sc_ref, wb_ref, wc_ref, wv_ref,
                    bg_ref, z_ref, h_ref):
    i = pl.program_id(0)
    j = pl.program_id(1)

    @pl.when((j == 0) & (i < nxa))
    def _():
        h_ref[...] = (xa_ref[...] * (1.0 + sc_ref[0]) + sh_ref[0]).astype(BF16)

    @pl.when((j == 0) & (i >= nxa))
    def _():
        h_ref[...] = (xb_ref[...] * (1.0 + sc_ref[0]) + sh_ref[0]).astype(BF16)

    h = h_ref[...]
    bg = jnp.dot(h, wb_ref[...], preferred_element_type=F32)
    cg = jnp.dot(h, wc_ref[...], preferred_element_type=F32)
    v = jnp.dot(h, wv_ref[...], preferred_element_type=F32)
    bg_ref[...] = bg.astype(BF16)
    z_ref[...] = (cg * v).astype(BF16)


def _conv_in(x2d, c2d, mod, w_in_b, n_batch, seq):
    nx, d = x2d.shape
    nc = c2d.shape[0]
    bm = min(PROJ_ROWS, seq, nc)
    assert seq % bm == 0 and nc % bm == 0
    tn = 512 if d % 512 == 0 else d
    nj = d // tn
    nxa = nx // bm
    n_all = nx + nc
    tpb = seq // bm
    mspec = functools.partial(_mod_spec, d=d, tiles_per_batch=tpb, n_batch=n_batch)
    return pl.pallas_call(
        functools.partial(_conv_in_kernel, nxa),
        grid=(n_all // bm, nj),
        in_specs=[
            pl.BlockSpec((bm, d), lambda i, j: (jnp.minimum(i, nxa - 1), 0)),
            pl.BlockSpec((bm, d), lambda i, j: (jnp.maximum(i - nxa, 0), 0),
                         pipeline_mode=pl.Buffered(1)),
            mspec(0), mspec(1),
            pl.BlockSpec((d, tn), lambda i, j: (0, j)),
            pl.BlockSpec((d, tn), lambda i, j: (0, nj + j)),
            pl.BlockSpec((d, tn), lambda i, j: (0, 2 * nj + j)),
        ],
        out_specs=[
            pl.BlockSpec((bm, tn), lambda i, j: (i, j)),
            pl.BlockSpec((bm, tn), lambda i, j: (i, j)),
        ],
        out_shape=[jax.ShapeDtypeStruct((n_all, d), BF16),
                   jax.ShapeDtypeStruct((n_all, d), BF16)],
        scratch_shapes=[pltpu.VMEM((bm, d), BF16)],
        compiler_params=_cparams(("arbitrary", "arbitrary")),
        name="conv_in",
    )(x2d, c2d, mod, mod, w_in_b, w_in_b, w_in_b)


def _ml_in_kernel(x_ref, sh_ref, sc_ref, w_ref, wkt_ref, wg_ref, bgate_ref,
                  u_ref, kt_ref, g_ref, h_ref):
    j = pl.program_id(1)

    @pl.when(j == 0)
    def _():
        h = (x_ref[...] * (1.0 + sc_ref[0]) + sh_ref[0]).astype(BF16)
        h_ref[...] = h
        g_ref[...] = jnp.dot(h, wg_ref[...], preferred_element_type=F32) + bgate_ref[...]
        kt_ref[...] = lax.dot_general(wkt_ref[...], h, (((1,), (1,)), ((), ())),
                                      preferred_element_type=F32).astype(BF16)

    u_ref[...] = jnp.dot(h_ref[...], w_ref[...], preferred_element_type=F32).astype(BF16)


def _ml_in(xall, mod, w_voq_b, w_kt_b, w_gate_pad, b_gate_pad, n_batch, seq, n_ctx_rows):
    n_all, d = xall.shape
    nu = w_voq_b.shape[1]
    hq = w_kt_b.shape[0]
    bm = min(PROJ_ROWS, seq, n_ctx_rows)
    assert seq % bm == 0 and n_ctx_rows % bm == 0
    tn = hq
    assert nu % tn == 0
    gl = w_gate_pad.shape[1]
    mspec = functools.partial(_mod_spec, d=d, tiles_per_batch=seq // bm, n_batch=n_batch)
    return pl.pallas_call(
        _ml_in_kernel,
        grid=(n_all // bm, nu // tn),
        in_specs=[
            pl.BlockSpec((bm, d), lambda i, j: (i, 0)),
            mspec(0), mspec(1),
            pl.BlockSpec((d, tn), lambda i, j: (0, j)),
            pl.BlockSpec((hq, d), lambda i, j: (0, 0)),
            pl.BlockSpec((d, gl), lambda i, j: (0, 0)),
            pl.BlockSpec((1, gl), lambda i, j: (0, 0)),
        ],
        out_specs=[
            pl.BlockSpec((bm, tn), lambda i, j: (i, j)),
            pl.BlockSpec((hq, bm), lambda i, j: (0, i)),
            pl.BlockSpec((bm, gl), lambda i, j: (i, 0)),
        ],
        out_shape=[jax.ShapeDtypeStruct((n_all, nu), BF16),
                   jax.ShapeDtypeStruct((hq, n_all), BF16),
                   jax.ShapeDtypeStruct((n_all, gl), F32)],
        scratch_shapes=[pltpu.VMEM((bm, d), BF16)],
        compiler_params=_cparams(("arbitrary", "arbitrary")),
        name="mlstm_in",
    )(xall, mod, mod, w_voq_b, w_kt_b, w_gate_pad, b_gate_pad)


def _dot_split3(a_b, x):
    hi = x.astype(BF16)
    r1 = x - hi.astype(F32)
    mid = r1.astype(BF16)
    lo = (r1 - mid.astype(F32)).astype(BF16)
    return (jnp.dot(a_b, hi, preferred_element_type=F32)
            + jnp.dot(a_b, mid, preferred_element_type=F32)
            + jnp.dot(a_b, lo, preferred_element_type=F32))


def _scan_kernel(n_heads, dqk, dv, q_ref, kt_ref, v_ref, g_ref, o_ref, ct_ref, m_ref):
    d = pl.program_id(1)
    s = pl.program_id(2)
    L = q_ref.shape[0]
    assert L == LANES
    scale = dqk ** -0.5

    @pl.when(s == 0)
    def _():
        ct_ref[...] = jnp.zeros_like(ct_ref)
        m_ref[...] = jnp.zeros_like(m_ref)

    H = n_heads
    heads = range(H)
    qi = lax.broadcasted_iota(I32, (L, L), 0)
    si = lax.broadcasted_iota(I32, (L, L), 1)
    fwd = d == 0
    mask = jnp.where(fwd, si - qi, qi - si) <= 0

    g = g_ref[...]
    b_all = _dot_split3(mask.astype(BF16), jax.nn.log_sigmoid(g))
    g_t = g.T
    b_t = b_all.T
    def lane_bcast(x, lane0):
        return jnp.stack([jnp.broadcast_to(x[:, lane0 + h:lane0 + h + 1], (L, LANES))
                          for h in heads])

    b_b = lane_bcast(b_all, H)
    i_b = lane_bcast(g, 0)
    b_end = jnp.where(fwd, b_b[:, L - 1:L, :], b_b[:, 0:1, :])
    m_st = m_ref[:, 0:1, :]
    tile = lambda x, n: jnp.concatenate([x] * (n // LANES), axis=-1)

    q3 = jnp.stack([q_ref[:, h * dqk:(h + 1) * dqk] for h in heads])
    kt3 = jnp.stack([kt_ref[h * dqk:(h + 1) * dqk, :] for h in heads])
    v3 = jnp.stack([v_ref[:, h * dv:(h + 1) * dv] for h in heads])

    r3 = jnp.stack([g_t[h:h + 1, :] - b_t[H + h:H + h + 1, :] for h in heads])
    dm = jnp.where(mask, b_b + r3, -jnp.inf)
    a_inter = b_b + m_st
    m_q = jnp.maximum(a_inter, jnp.max(dm, axis=-1, keepdims=True))
    inter = jnp.exp(a_inter - m_q) * scale
    sc = jnp.einsum("hqd,hds->hqs", q3, kt3, preferred_element_type=F32)
    p = jnp.exp(dm - m_q) * (sc * scale)
    ct = ct_ref[...]
    qc = jnp.einsum("hqd,hdv->hqv", q3, ct.astype(BF16), preferred_element_type=F32)
    v_ext = jnp.concatenate([v3, jnp.ones((H, L, LANES), BF16)], axis=-1)
    pv = jnp.einsum("hqs,hsv->hqv", p.astype(BF16), v_ext, preferred_element_type=F32)
    den = pv[:, :, dv:] + inter * qc[:, :, dv:]
    rden = 1.0 / jnp.maximum(jnp.abs(den), jnp.exp(-m_q))
    hout = (pv[:, :, :dv] + tile(inter, dv) * qc[:, :, :dv]) * tile(rden, dv)
    for h in heads:
        o_ref[:, h * dv:(h + 1) * dv] = hout[h]

    wl = b_end - b_b + i_b
    m_next = jnp.maximum(b_end + m_st, jnp.max(wl, axis=1, keepdims=True))
    decay = jnp.exp(b_end + m_st - m_next)
    w_b = jnp.exp(wl - m_next)
    vw = jnp.concatenate([v3.astype(F32) * tile(w_b, dv), w_b], axis=-1).astype(BF16)
    upd = jnp.einsum("hdl,hlv->hdv", kt3, vw, preferred_element_type=F32)
    ct_ref[...] = tile(decay, dv + LANES) * ct + upd
    m_ref[...] = jnp.broadcast_to(m_next, m_ref.shape)


def _scan(u, kt, gates, n_batch, seq, ctx_len, n_heads, dqk, dv):
    L = SCAN_CHUNK
    nx = n_batch * seq
    ncc = ctx_len // L
    nlc = seq // L
    assert ctx_len % L == 0 and seq % L == 0
    hq = n_heads * dqk
    hv = n_heads * dv
    assert hv == 2 * hq

    def row_blk(b, d, s):
        ctx = (nx + b * ctx_len) // L + jnp.where(d == 0, s, ncc - 1 - s)
        sl = s - ncc
        lat = (b * seq) // L + jnp.where(d == 0, sl, nlc - 1 - sl)
        return jnp.where(s < ncc, ctx, lat)

    def out_blk(b, d, s):
        sl = jnp.maximum(s - ncc, 0)
        return (b * seq) // L + jnp.where(d == 0, sl, nlc - 1 - sl)

    return pl.pallas_call(
        functools.partial(_scan_kernel, n_heads, dqk, dv),
        grid=(n_batch, 2, ncc + nlc),
        in_specs=[
            pl.BlockSpec((L, hq), lambda b, d, s: (row_blk(b, d, s), 2 * hv // hq)),
            pl.BlockSpec((hq, L), lambda b, d, s: (0, row_blk(b, d, s))),
            pl.BlockSpec((L, hv), lambda b, d, s: (row_blk(b, d, s), 0)),
            pl.BlockSpec((L, LANES), lambda b, d, s: (row_blk(b, d, s), d)),
        ],
        out_specs=pl.BlockSpec((None, L, hv), lambda b, d, s: (d, out_blk(b, d, s), 0)),
        out_shape=jax.ShapeDtypeStruct((2, nx, hv), F32),
        scratch_shapes=[pltpu.VMEM((n_heads, dqk, dv + LANES), F32),
                        pltpu.VMEM((n_heads, SUBLANES, LANES), F32)],
        compiler_params=_cparams(("arbitrary", "arbitrary", "arbitrary")),
        name="mlstm_scan",
    )(u, kt, u, gates)


def _mix_epilogue(alpha, rs, a, x_ref, g1_ref, sh2_ref, sc2_ref, wout_ref, lng_ref,
                  lnb_ref, rwt_ref, x1_ref, hx2_ref, logt_ref):
    mx = jnp.dot(a, wout_ref[...], preferred_element_type=F32)
    x1 = _layer_norm_rows(alpha * x_ref[rs, :] + g1_ref[0] * mx, lng_ref[...], lnb_ref[...])
    x1_ref[rs, :] = x1
    hx2 = x1 * (1.0 + sc2_ref[0]) + sh2_ref[0]
    hx2_ref[rs, :] = hx2
    logt_ref[:, rs] = lax.dot_general(rwt_ref[...], hx2.astype(BF16), (((1,), (1,)), ((), ())),
                                      preferred_element_type=F32)


def _sub_rows(tm, sub):
    return [slice(r, r + sub) for r in range(0, tm, sub)]


def _conv_out_kernel(alpha, ctx_mode, n_lat, tpb, sub, z_ref, zp_ref, zn_ref, bg_ref, cw_ref,
                     x_ref, g1_ref, sh2_ref, sc2_ref, wout_ref, lng_ref, lnb_ref, rwt_ref, *rest):
    x1_ref, hx2_ref, logt_ref = rest[-3:]
    i = pl.program_id(0)
    tm, d = z_ref.shape
    half = d // 2
    rows = lax.broadcasted_iota(I32, (sub, 1), 0)

    def conv1(zz, w3, period):
        pos = rows & (period - 1)
        prev = pltpu.roll(zz, 1, 0) * (pos != 0).astype(F32)
        nxt = pltpu.roll(zz, sub - 1, 0) * (pos != period - 1).astype(F32)
        return w3[0:1] * prev + w3[1:2] * zz + w3[2:3] * nxt

    def tile():
        cw = cw_ref[...]
        if not ctx_mode:
            ti = i % tpb
            up = zp_ref[...].astype(F32) * (ti > 0).astype(F32)
            dn = zn_ref[...].astype(F32) * (ti < tpb - 1).astype(F32)
            w3 = cw[:, half:]

        for rs in _sub_rows(tm, sub):
            z = z_ref[rs, :].astype(F32)
            bg = bg_ref[rs, :].astype(F32)
            if ctx_mode:
                a = (bg * conv1(z, cw, sub)).astype(BF16)
            else:
                a_row = (bg[:, :half] * conv1(z[:, :half], cw[:, :half], GRID_W)).astype(BF16)
                r0, r1 = rs.start, rs.stop
                f32_rows = lambda a, b: z_ref[a:b, half:].astype(F32)
                prev = (jnp.concatenate([up, f32_rows(0, r1 - GRID_W)], axis=0) if r0 == 0
                        else f32_rows(r0 - GRID_W, r1 - GRID_W))
                nxt = (jnp.concatenate([f32_rows(r0 + GRID_W, tm), dn], axis=0) if r1 == tm
                       else f32_rows(r0 + GRID_W, r1 + GRID_W))
                y = w3[0:1] * prev + w3[1:2] * z[:, half:] + w3[2:3] * nxt
                a = jnp.concatenate([a_row, (bg[:, half:] * y).astype(BF16)], axis=1)
            _mix_epilogue(alpha, rs, a, x_ref, g1_ref, sh2_ref, sc2_ref, wout_ref, lng_ref,
                          lnb_ref, rwt_ref, x1_ref, hx2_ref, logt_ref)

    if ctx_mode:
        tile()
    else:
        pl.when(i < n_lat)(tile)

        @pl.when(i >= n_lat)
        def _():
            x1_ref[...] = jnp.zeros_like(x1_ref)
            hx2_ref[...] = jnp.zeros_like(hx2_ref)
            logt_ref[...] = jnp.zeros_like(logt_ref)


def _ml_out_kernel(alpha, n_heads, dv, sub, hf_ref, hb_ref, og_ref, ng_ref, x_ref,
                   g1_ref, sh2_ref, sc2_ref, wout_ref, lng_ref, lnb_ref, rwt_ref,
                   x1_ref, hx2_ref, logt_ref):
    for rs in _sub_rows(hf_ref.shape[0], sub):
        parts = []
        for h in range(n_heads):
            sl = slice(h * dv, (h + 1) * dv)
            hs = hf_ref[rs, sl] + hb_ref[rs, sl]
            mu = jnp.mean(hs, axis=-1, keepdims=True)
            c = hs - mu
            var = jnp.mean(c * c, axis=-1, keepdims=True)
            hn = c * lax.rsqrt(var + HEAD_NORM_EPS)
            gate = jax.nn.sigmoid(og_ref[rs, sl].astype(F32))
            parts.append((hn * ng_ref[:, sl] * gate).astype(BF16))
        _mix_epilogue(alpha, rs, jnp.concatenate(parts, axis=1), x_ref, g1_ref, sh2_ref,
                      sc2_ref, wout_ref, lng_ref, lnb_ref, rwt_ref, x1_ref, hx2_ref, logt_ref)


def _mix_out_common(d, n_rows, n_exp, mod_row, tile0=0):
    tm = ROW_TILE
    mspec = lambda chunk: pl.BlockSpec((1, 1, d), lambda i: (mod_row(i), 0, chunk))
    const = lambda shape: pl.BlockSpec(shape, lambda i: (0,) * len(shape))
    in_specs = [mspec(2), mspec(3), mspec(4), const((d, d)), const((1, d)), const((1, d)),
                const((n_exp, d))]
    out_specs = [pl.BlockSpec((tm, d), lambda i: (i + tile0, 0)),
                 pl.BlockSpec((tm, d), lambda i: (i + tile0, 0)),
                 pl.BlockSpec((n_exp, tm), lambda i: (0, i + tile0))]
    out_shape = [jax.ShapeDtypeStruct((n_rows, d), F32),
                 jax.ShapeDtypeStruct((n_rows, d), F32),
                 jax.ShapeDtypeStruct((n_exp, n_rows), F32)]
    return in_specs, out_specs, out_shape


def _conv_out(alpha, z, bg, conv_w, x2d, c2d, mod, w_out_b, ln_g, ln_b, rwt_b, n_batch, seq,
              ctx_len):
    n_all, d = z.shape
    nx = x2d.shape[0]
    tm = ROW_TILE
    assert ctx_len == tm and seq % tm == 0 and MIX_SUB_ROWS % GRID_W == 0
    half = d // 2
    nxa = nx // tm
    tpb = seq // tm
    hpt = tm // GRID_W
    n_exp = rwt_b.shape[0]

    def z_specs(tile0):
        return [
            pl.BlockSpec((tm, d), lambda i: (i + tile0, 0)),
            pl.BlockSpec((GRID_W, half), lambda i: (jnp.maximum((i + tile0) * hpt - 1, 0), 1)),
            pl.BlockSpec((GRID_W, half), lambda i: ((i + tile0 + 1) * hpt - 1, 1)),
            pl.BlockSpec((tm, d), lambda i: (i + tile0, 0)),
            pl.BlockSpec((3, d), lambda i: (0, 0)),
            pl.BlockSpec((tm, d), lambda i: (i, 0)),
        ]

    lat = lambda i: jnp.minimum(i, nxa - 1)
    common_in, out_specs, out_shape = _mix_out_common(d, n_all, n_exp, lambda i: lat(i) // tpb)
    lat_specs = z_specs(0)
    lat_specs[2] = pl.BlockSpec((GRID_W, half), lambda i: ((lat(i) + 1) * hpt, 1))
    lat_specs[5] = pl.BlockSpec((tm, d), lambda i: (lat(i), 0))
    outs = pl.pallas_call(
        functools.partial(_conv_out_kernel, alpha, False, nxa, tpb, MIX_SUB_ROWS),
        grid=(n_all // tm,),
        in_specs=lat_specs + common_in,
        out_specs=out_specs,
        out_shape=out_shape,
        compiler_params=_cparams(("arbitrary",)),
        name="conv_out",
    )(z, z, z, bg, conv_w, x2d, mod, mod, mod, w_out_b, ln_g, ln_b, rwt_b)

    common_in, out_specs, out_shape = _mix_out_common(d, n_all, n_exp, lambda i: n_batch, nxa)
    n_in = 6 + len(common_in)
    keep = [pl.BlockSpec(memory_space=pl.ANY)] * 3
    return pl.pallas_call(
        functools.partial(_conv_out_kernel, alpha, True, None, tpb, tm),
        grid=(c2d.shape[0] // tm,),
        in_specs=z_specs(nxa) + common_in + keep,
        out_specs=out_specs,
        out_shape=out_shape,
        input_output_aliases={n_in: 0, n_in + 1: 1, n_in + 2: 2},
        compiler_params=_cparams(("arbitrary",)),
        name="conv_out_ctx",
    )(z, z, z, bg, conv_w, c2d, mod, mod, mod, w_out_b, ln_g, ln_b, rwt_b, *outs)


def _ml_out(alpha, hfb, u, norm_g, xall, mod, w_out_b, ln_g, ln_b, rwt_b, n_batch, seq,
            n_heads, dv):
    nx, d = hfb.shape[1:]
    tm = ROW_TILE
    tpb = seq // tm
    common_in, out_specs, out_shape = _mix_out_common(d, nx, rwt_b.shape[0], lambda i: i // tpb)
    o_blk = 1
    in_specs = [
        pl.BlockSpec((None, tm, d), lambda i: (0, i, 0)),
        pl.BlockSpec((None, tm, d), lambda i: (1, i, 0)),
        pl.BlockSpec((tm, d), lambda i: (i, o_blk)),
        pl.BlockSpec((1, d), lambda i: (0, 0)),
        pl.BlockSpec((tm, d), lambda i: (i, 0)),
    ] + common_in
    return pl.pallas_call(
        functools.partial(_ml_out_kernel, alpha, n_heads, dv, tm),
        grid=(nx // tm,),
        in_specs=in_specs,
        out_specs=out_specs,
        out_shape=out_shape,
        compiler_params=_cparams(("arbitrary",)),
        name="mlstm_out",
    )(hfb, hfb, u, norm_g, xall, mod, mod, mod, w_out_b, ln_g, ln_b, rwt_b)


def _route_kernel(tile_rows, n_te, logt_ref, rb_ref, pos_ref, wcol_ref, te_ref, meta_ref,
                  carry_ref, before_ref):
    ph = pl.program_id(0)
    i = pl.program_id(1)
    n_steps = pl.num_programs(1)
    n_exp, tr = logt_ref.shape
    epg = n_exp // N_GROUPS

    @pl.when((ph == 0) & (i == 0))
    def _():
        carry_ref[...] = jnp.zeros_like(carry_ref)
        ti = lax.broadcasted_iota(I32, (tr, tr), 0)
        tj = lax.broadcasted_iota(I32, (tr, tr), 1)
        before_ref[...] = (ti < tj).astype(BF16)

    s = jax.nn.sigmoid(logt_ref[...])
    sel = s + rb_ref[...]
    row = lax.broadcasted_iota(I32, (n_exp, tr), 0)
    member = row % epg
    group = row // epg

    def partner(x, k, idx, span, unit):
        wrapped = (idx + k) >= span
        up = pltpu.roll(x, n_exp - k * unit, 0)
        down = pltpu.roll(x, (span - k) * unit, 0)
        return jnp.where(wrapped, down, up), wrapped

    rank_in = jnp.zeros((n_exp, tr), F32)
    for k in range(1, epg):
        p, wrapped = partner(sel, k, member, epg, 1)
        beats = (p > sel) | ((p == sel) & wrapped)
        rank_in = rank_in + beats.astype(F32)
    top = (rank_in < TOP_K).astype(F32)
    gs = sel * top
    score = gs
    for k in range(1, epg):
        p, _ = partner(gs, k, member, epg, 1)
        score = score + p
    n_better = jnp.zeros((n_exp, tr), F32)
    for k in range(1, N_GROUPS):
        p, wrapped = partner(score, k, group, N_GROUPS, epg)
        beats = (p > score) | ((p == score) & wrapped)
        n_better = n_better + beats.astype(F32)
    best = n_better == 0.0
    m1 = (best & (rank_in == 0.0)).astype(F32)
    m2 = (best & (rank_in == 1.0)).astype(F32)
    s1 = jnp.sum(s * m1, axis=0, keepdims=True)
    s2 = jnp.sum(s * m2, axis=0, keepdims=True)
    den = s1 + s2
    oh = m1 + m2

    slot = jnp.dot(oh.astype(BF16), before_ref[...], preferred_element_type=F32) + carry_ref[...]

    @pl.when(ph == 1)
    def _():
        pos_ref[0:1, :] = jnp.sum(m1 * slot, axis=0, keepdims=True).astype(I32)
        pos_ref[1:2, :] = jnp.sum(m2 * slot, axis=0, keepdims=True).astype(I32)
        w8 = jnp.concatenate([s1 / den, s2 / den, jnp.zeros((LANES - 2, tr), F32)], axis=0)
        wcol_ref[...] = w8.T

    carry_ref[...] = carry_ref[...] + jnp.sum(oh, axis=1, keepdims=True)

    @pl.when((ph == 0) & (i == n_steps - 1))
    def _():
        cnt = carry_ref[...]
        ntile = jnp.floor((cnt + (tile_rows - 1)) * (1.0 / tile_rows))
        er = lax.broadcasted_iota(I32, (n_exp, LANES), 0)
        el = lax.broadcasted_iota(I32, (n_exp, LANES), 1)
        eye = (er == el).astype(F32)
        nt_row = jnp.sum(ntile * eye, axis=0, keepdims=True)
        cnt_row = jnp.sum(cnt * eye, axis=0, keepdims=True)
        cum_excl = jnp.sum(nt_row * (el < er).astype(F32), axis=1, keepdims=True)
        off = cum_excl * tile_rows
        off_row = jnp.sum(off * eye, axis=0, keepdims=True)
        total = jnp.sum(nt_row, axis=1, keepdims=True)
        cum_incl = cum_excl + ntile
        tl = lax.broadcasted_iota(I32, (n_exp, n_te), 1).astype(F32)
        te = jnp.sum((cum_incl <= tl).astype(F32), axis=0, keepdims=True)
        elf = el.astype(F32)
        later = (el > er) & (nt_row > 0.0) & (el < n_exp)
        nxt = jnp.min(jnp.where(later, elf, float(n_exp)), axis=1, keepdims=True)
        nxt = jnp.where(nxt == float(n_exp), er[:, 0:1].astype(F32), nxt)
        mine = (cum_excl <= tl) & (tl < cum_incl)
        te_next = jnp.sum(jnp.where(mine, nxt, 0.0), axis=0, keepdims=True)
        te_ref[...] = jnp.concatenate([jnp.minimum(te, n_exp - 1), te_next], axis=0).astype(I32)
        meta_ref[...] = jnp.concatenate(
            [cnt_row, off_row, jnp.broadcast_to(total, (1, LANES)),
             jnp.zeros((SUBLANES - 3, LANES), F32)], axis=0).astype(I32)
        carry_ref[...] = off


def _route(logt, router_b, tile_rows):
    n_exp, n = logt.shape
    tr = ROUTE_COLS
    while n % tr:
        tr //= 2
    n_te = 256
    assert (2 * n) // tile_rows + n_exp <= n_te
    return pl.pallas_call(
        functools.partial(_route_kernel, tile_rows, n_te),
        grid=(2, n // tr),
        in_specs=[pl.BlockSpec((n_exp, tr), lambda p, i: (0, i)),
                  pl.BlockSpec((n_exp, 1), lambda p, i: (0, 0))],
        out_specs=[pl.BlockSpec((2, tr), lambda p, i: (0, i * p)),
                   pl.BlockSpec((tr, LANES), lambda p, i: (i * p, 0)),
                   pl.BlockSpec((2, n_te), lambda p, i: (0, 0)),
                   pl.BlockSpec((SUBLANES, LANES), lambda p, i: (0, 0))],
        out_shape=[jax.ShapeDtypeStruct((2, n), I32),
                   jax.ShapeDtypeStruct((n, LANES), F32),
                   jax.ShapeDtypeStruct((2, n_te), I32),
                   jax.ShapeDtypeStruct((SUBLANES, LANES), I32)],
        scratch_shapes=[pltpu.VMEM((n_exp, 1), F32), pltpu.VMEM((tr, tr), BF16)],
        compiler_params=_cparams(("arbitrary", "arbitrary")),
        name="route",
    )(logt, router_b.reshape(n_exp, 1).astype(F32))


def _tbl_kernel(n_tok, n_exp, tile_rows, meta_ref, pos_ref, tbl_ref):
    i = pl.program_id(0)
    blk = pos_ref.shape[1]

    @pl.when(i == 0)
    def _():
        n_pad = 0
        for e in range(n_exp):
            cnt = meta_ref[e]
            off = meta_ref[n_exp + e]
            up = ((cnt + (tile_rows - 1)) // tile_rows) * tile_rows

            def fill(r, j):
                tbl_ref[off + r] = (2 * n_tok + j) << SLOT_SRC_BITS
                return j + 1

            n_pad = lax.fori_loop(cnt, up, fill, n_pad)

        def fill_tail(r, carry):
            tbl_ref[r] = (2 * n_tok) << SLOT_SRC_BITS
            return carry

        lax.fori_loop(2 * n_tok + n_pad, tbl_ref.shape[0], fill_tail, 0)

    base = i * blk
    first = (base << SLOT_SRC_BITS) | jnp.where(base >= n_tok, base - n_tok, base)
    step = (1 << SLOT_SRC_BITS) + 1
    group = 8

    def body(q, carry):
        r0 = q * group
        slots = [pos_ref[0, r0 + j] for j in range(group)]
        for j in range(group):
            tbl_ref[slots[j]] = first + (r0 + j) * step
        return carry

    lax.fori_loop(0, blk // group, body, 0, unroll=True)


def _build_table(meta1d, pos, n_tok, n_exp, tile_rows, p_pad):
    blk = TBL_BLOCK
    while n_tok % blk:
        blk //= 2
    n_pairs = 2 * n_tok
    grid_spec = pltpu.PrefetchScalarGridSpec(
        num_scalar_prefetch=1,
        grid=(n_pairs // blk,),
        in_specs=[pl.BlockSpec((None, 1, blk), lambda i, m: (i, 0, 0), memory_space=pltpu.SMEM)],
        out_specs=pl.BlockSpec(memory_space=pltpu.SMEM),
    )
    return pl.pallas_call(
        functools.partial(_tbl_kernel, n_tok, n_exp, tile_rows),
        grid_spec=grid_spec,
        out_shape=jax.ShapeDtypeStruct((p_pad,), I32),
        compiler_params=_cparams(("arbitrary",)),
        name="slot_table",
    )(meta1d, pos.reshape(n_pairs // blk, 1, blk))


def _moe_kernel(layer, te_ref, ten_ref, nt_ref, tbl_ref, tbln_ref, tblp_ref, hx_hbm,
                wg_hbm, wu_hbm, wd_hbm, y_hbm, xbuf, ybuf, zbuf, wg_b, wu_b, wd_b,
                wg_s, wu_s, wd_s, gsem, ssem, zsem, wsem):
    t = pl.program_id(0)
    n = nt_ref[0]
    rows = tbl_ref.shape[1]
    staged = ((wg_hbm, wg_s, wg_b), (wu_hbm, wu_s, wu_b), (wd_hbm, wd_s, wd_b))

    def weight_copies(e):
        return [pltpu.make_async_copy(hbm.at[layer, e], stage, wsem.at[k])
                for k, (hbm, stage, _) in enumerate(staged)]

    @pl.when(t == 0)
    def _():
        for c in weight_copies(te_ref[0]):
            c.start(priority=WEIGHT_DMA_PRIORITY)

    tl = jnp.minimum(t, n - 1)
    new_expert = (t < n) & ((t == 0) | (te_ref[tl] != te_ref[jnp.maximum(tl - 1, 0)]))

    @pl.when(new_expert)
    def _():
        for c, (_, stage, dst) in zip(weight_copies(te_ref[tl]), staged):
            c.wait()
            dst[...] = stage[...].astype(BF16)
        for c in weight_copies(ten_ref[tl]):
            c.start(priority=WEIGHT_DMA_PRIORITY)

    def start_gather(tref, sl, lo=0, hi=None):
        for r in range(lo, rows if hi is None else hi):
            src = tref[0, r] & SLOT_SRC_MASK
            pltpu.make_async_copy(hx_hbm.at[pl.ds(src, 1), :],
                                  xbuf.at[sl, pl.ds(r, 1), :], gsem.at[sl]).start()

    def start_scatter(tref, sl, lo=0, hi=None):
        for r in range(lo, rows if hi is None else hi):
            dst = tref[0, r] >> SLOT_SRC_BITS
            pltpu.make_async_copy(ybuf.at[sl, pl.ds(r, 1), :],
                                  y_hbm.at[pl.ds(dst, 1), :], ssem.at[sl]).start()

    def wait_gather(sl):
        pltpu.make_async_copy(xbuf.at[sl], xbuf.at[sl], gsem.at[sl]).wait()

    def wait_scatter(sl):
        pltpu.make_async_copy(ybuf.at[sl], ybuf.at[sl], ssem.at[sl]).wait()

    @pl.when(t == 0)
    def _():
        zbuf[...] = jnp.zeros_like(zbuf)
        start_gather(tbl_ref, 0)

    @pl.when(t >= n)
    def _():
        dst = y_hbm.at[pl.ds(pl.multiple_of(t * rows, rows), rows), :]
        fill = pltpu.make_async_copy(zbuf, dst, zsem)
        fill.start()
        fill.wait()

    def tile(slot, has_prev):
        other = 1 - slot
        wait_gather(slot)

        @pl.when(t >= 2)
        def _():
            wait_scatter(slot)

        f = wg_b.shape[1]
        d = wd_b.shape[1]
        fc, dc = MOE_UP_COLS, MOE_DOWN_COLS
        groups = f // fc + d // dc
        bounds = [(k * rows) // groups for k in range(groups + 1)]

        def copy_group(k):
            start_gather(tbln_ref, other, bounds[k], bounds[k + 1])
            if has_prev:
                start_scatter(tblp_ref, other, bounds[k], bounds[k + 1])

        xb = xbuf[slot].astype(BF16)
        hs = []
        for c in range(f // fc):
            cols = slice(c * fc, (c + 1) * fc)
            g = jnp.dot(xb, wg_b[:, cols], preferred_element_type=F32)
            u = jnp.dot(xb, wu_b[:, cols], preferred_element_type=F32)
            hs.append((g * jax.nn.sigmoid(g) * u).astype(BF16))
            copy_group(c)
        h = jnp.concatenate(hs, axis=1)
        for c in range(d // dc):
            cols = slice(c * dc, (c + 1) * dc)
            ybuf[slot, :, cols] = jnp.dot(h, wd_b[:, cols], preferred_element_type=F32)
            copy_group(f // fc + c)

        @pl.when(t == n - 1)
        def _():
            start_scatter(tbl_ref, slot)
            wait_gather(other)

            @pl.when(t >= 1)
            def _():
                wait_scatter(other)

            wait_scatter(slot)
            for c in weight_copies(ten_ref[tl]):
                c.wait()

    pl.when((t < n) & (t == 0))(functools.partial(tile, 0, False))
    for parity in (0, 1):
        pl.when((t < n) & (t > 0) & (lax.rem(t, 2) == parity))(
            functools.partial(tile, parity, True))


def _moe(te2, nt1d, tbl, hx2, w_gate, w_up, w_down, layer, n_tok, p_pad):
    tmm = MOE_ROWS
    _, n_exp, d, f = w_gate.shape
    t_max = (2 * n_tok) // tmm + n_exp
    last = lambda t, nt: jnp.minimum(t, nt[0] - 1)
    tspec = lambda fn: pl.BlockSpec((None, 1, tmm), lambda t, te, ten, nt: (fn(t, nt), 0, 0),
                                    memory_space=pltpu.SMEM)
    hbm = pl.BlockSpec(memory_space=pl.ANY)
    grid_spec = pltpu.PrefetchScalarGridSpec(
        num_scalar_prefetch=3,
        grid=(t_max,),
        in_specs=[
            tspec(last),
            tspec(lambda t, nt: last(t + 1, nt)),
            tspec(lambda t, nt: jnp.maximum(last(t, nt) - 1, 0)),
            hbm, hbm, hbm, hbm,
        ],
        out_specs=hbm,
        scratch_shapes=[pltpu.VMEM((2, tmm, d), F32), pltpu.VMEM((2, tmm, d), F32),
                        pltpu.VMEM((tmm, d), F32),
                        pltpu.VMEM((d, f), BF16), pltpu.VMEM((d, f), BF16),
                        pltpu.VMEM((f, d), BF16),
                        pltpu.VMEM((d, f), F32), pltpu.VMEM((d, f), F32),
                        pltpu.VMEM((f, d), F32),
                        pltpu.SemaphoreType.DMA((2,)), pltpu.SemaphoreType.DMA((2,)),
                        pltpu.SemaphoreType.DMA(()), pltpu.SemaphoreType.DMA((3,))],
    )
    tbl3 = tbl.reshape(p_pad // tmm, 1, tmm)
    return pl.pallas_call(
        functools.partial(_moe_kernel, layer),
        grid_spec=grid_spec,
        out_shape=jax.ShapeDtypeStruct((t_max * tmm, d), F32),
        compiler_params=_cparams(("arbitrary",)),
        name="moe_experts",
    )(te2[0], te2[1], nt1d, tbl3, tbl3, tbl3, hx2, w_gate, w_up, w_down)


def _comb_kernel(alpha, x_ref, y0_ref, y1_ref, w_ref, g2_ref, lng_ref, lnb_ref, o_ref):
    w = w_ref[...]
    ex = w[:, 0:1] * y0_ref[...] + w[:, 1:2] * y1_ref[...]
    o_ref[...] = _layer_norm_rows(alpha * x_ref[...] + g2_ref[0] * ex, lng_ref[...], lnb_ref[...])


def _combine(alpha, x1, y, wcol, mod, ln_g, ln_b, n_tok, n_out, n_batch, seq):
    d = x1.shape[1]
    tm = COMB_ROWS
    assert n_tok % tm == 0 and n_out % tm == 0 and seq % tm == 0
    nblk = n_tok // tm
    mspec = functools.partial(_mod_spec, d=d, tiles_per_batch=seq // tm, n_batch=n_batch)
    return pl.pallas_call(
        functools.partial(_comb_kernel, alpha),
        grid=(n_out // tm,),
        in_specs=[
            pl.BlockSpec((tm, d), lambda i: (i, 0)),
            pl.BlockSpec((tm, d), lambda i: (i, 0)),
            pl.BlockSpec((tm, d), lambda i: (i + nblk, 0)),
            pl.BlockSpec((tm, LANES), lambda i: (i, 0)),
            mspec(5),
            pl.BlockSpec((1, d), lambda i: (0, 0)),
            pl.BlockSpec((1, d), lambda i: (0, 0)),
        ],
        out_specs=pl.BlockSpec((tm, d), lambda i: (i, 0)),
        out_shape=jax.ShapeDtypeStruct((n_out, d), F32),
        compiler_params=_cparams(("arbitrary",)),
        name="moe_combine",
    )(x1, y, y, wcol, mod, ln_g, ln_b)


def _moe_layer(alpha, x1, hx2, logt, mod, ln_g, ln_b, router_b, w_gate, w_up, w_down, layer,
               n_out, n_batch, seq):
    n_tok = hx2.shape[0]
    n_exp = w_gate.shape[1]
    tmm = MOE_ROWS
    p_max = 2 * n_tok + n_exp * tmm
    p_pad = -(-p_max // TBL_BLOCK) * TBL_BLOCK
    assert n_tok <= 1 << SLOT_SRC_BITS and p_max < 1 << (31 - SLOT_SRC_BITS)
    pos, wcol, te, meta = _route(logt, router_b, tmm)
    meta1d = meta[:2, :n_exp].reshape(-1)
    tbl = _build_table(meta1d, pos, n_tok, n_exp, tmm, p_pad)
    y = _moe(te, meta[2, :1], tbl, hx2, w_gate, w_up, w_down, layer, n_tok, p_pad)
    return _combine(alpha, x1, y, wcol, mod, ln_g, ln_b, n_tok, n_out, n_batch, seq)


def kernel(x, c, ctx, c_ctx, w_ada, b_ada, ln_g, ln_b, conv_w_in, conv_w, conv_w_out, ml_w_in, ml_w_gate, ml_b_gate, ml_norm_g, ml_w_out, router_w, router_b, exp_w_gate, exp_w_up, exp_w_down):
    n_batch, seq, d = x.shape
    ctx_len = ctx.shape[1]
    depth = w_ada.shape[0]
    assert depth == 2, "layer 0 is the conv mixer, layer 1 the mLSTM mixer"
    alpha = (2 * depth) ** 0.25
    n_heads = ml_b_gate.shape[-1] // 4
    dqk = d // (2 * n_heads)
    dv = d // n_heads
    nx = n_batch * seq
    nc = n_batch * ctx_len
    assert n_batch < SUBLANES and 2 * n_heads <= LANES

    x2d = x.reshape(nx, d)
    c2d = ctx.reshape(nc, d)
    cc = jnp.zeros((SUBLANES, d), F32).at[:n_batch].set(c).at[n_batch].set(c_ctx)
    mod = _ada(cc, w_ada, b_ada).reshape(depth, SUBLANES, 1, 6 * d)
    rwt_b = router_w.T.astype(BF16)

    bg, z = _conv_in(x2d, c2d, mod[0], _to_bf16(conv_w_in, 0), n_batch, seq)
    x1, hx2, logt = _conv_out(alpha, z, bg, conv_w[0], x2d, c2d, mod[0],
                              _to_bf16(conv_w_out, 0), ln_g[0, 0:1], ln_b[0, 0:1], rwt_b,
                              n_batch, seq, ctx_len)
    xall = _moe_layer(alpha, x1, hx2, logt, mod[0], ln_g[0, 1:2], ln_b[0, 1:2], router_b,
                      exp_w_gate, exp_w_up, exp_w_down, 0, nx + nc, n_batch, seq)

    w_gate = ml_w_gate[0]
    b_gate = ml_b_gate[0]
    h2 = 2 * n_heads
    w_gate_pad = (jnp.zeros((d, 2 * LANES), F32).at[:, :h2].set(w_gate[:, :h2])
                  .at[:, LANES:LANES + h2].set(w_gate[:, h2:])).astype(BF16)
    b_gate_pad = (jnp.zeros((1, 2 * LANES), F32).at[0, :h2].set(b_gate[:h2])
                  .at[0, LANES:LANES + h2].set(b_gate[h2:]))
    w_voq_b, w_kt_b = _ml_weights(ml_w_in, 0, n_heads * dqk)
    u, kt, gates = _ml_in(xall, mod[1], w_voq_b, w_kt_b, w_gate_pad, b_gate_pad, n_batch,
                          seq, nc)
    hfb = _scan(u, kt, gates, n_batch, seq, ctx_len, n_heads, dqk, dv)
    x1, hx2, logt = _ml_out(alpha, hfb, u, ml_norm_g[0:1], xall, mod[1],
                            _to_bf16(ml_w_out, 0), ln_g[1, 0:1], ln_b[1, 0:1], rwt_b,
                            n_batch, seq, n_heads, dv)
    out = _moe_layer(alpha, x1, hx2, logt, mod[1], ln_g[1, 1:2], ln_b[1, 1:2], router_b,
                     exp_w_gate, exp_w_up, exp_w_down, 1, nx, n_batch, seq)
    return out.reshape(n_batch, seq, d)
```

```python
import functools

import jax
import jax.numpy as jnp
from jax import lax
from jax.experimental import pallas as pl
from jax.experimental.pallas import tpu as pltpu

F32 = jnp.float32
BF16 = jnp.bfloat16
I32 = jnp.int32

GRID_W = 64
N_GROUPS = 4
TOP_K = 2
LN_EPS = 1e-5
HEAD_NORM_EPS = 1e-6

LANES = 128
SUBLANES = 8
VMEM_LIMIT_BYTES = 56 * 1024 * 1024

ROW_TILE = 256
MIX_SUB_ROWS = 128
PROJ_ROWS = 1024
MOE_ROWS = 256
MOE_UP_COLS = 256
MOE_DOWN_COLS = 512
ROUTE_COLS = 1024
COMB_ROWS = 512
SCAN_CHUNK = 128
TBL_BLOCK = 1024
SLOT_SRC_BITS = 15
SLOT_SRC_MASK = (1 << SLOT_SRC_BITS) - 1
CAST_BLOCK_BYTES = 8 * 1024 * 1024
WEIGHT_DMA_PRIORITY = 1


def _cparams(sem):
    return pltpu.CompilerParams(dimension_semantics=sem, vmem_limit_bytes=VMEM_LIMIT_BYTES)


def _layer_norm_rows(r, g, b):
    mu = jnp.mean(r, axis=-1, keepdims=True)
    c = r - mu
    var = jnp.mean(c * c, axis=-1, keepdims=True)
    return c * lax.rsqrt(var + LN_EPS) * g + b


def _ada_kernel(cc_ref, w_ref, b_ref, o_ref):
    a = cc_ref[...]
    a = (a * jax.nn.sigmoid(a)).astype(BF16)
    o_ref[0] = jnp.dot(a, w_ref[0].astype(BF16), preferred_element_type=F32) + b_ref[0]


def _ada(cc, w_ada, b_ada):
    depth, d, n6 = w_ada.shape
    tn = 1024
    return pl.pallas_call(
        _ada_kernel,
        grid=(depth, n6 // tn),
        in_specs=[
            pl.BlockSpec((SUBLANES, d), lambda l, j: (0, 0)),
            pl.BlockSpec((1, d, tn), lambda l, j: (l, 0, j)),
            pl.BlockSpec((1, 1, tn), lambda l, j: (l, 0, j)),
        ],
        out_specs=pl.BlockSpec((1, SUBLANES, tn), lambda l, j: (l, 0, j)),
        out_shape=jax.ShapeDtypeStruct((depth, SUBLANES, n6), F32),
        compiler_params=_cparams(("arbitrary", "arbitrary")),
        name="ada_mod",
    )(cc, w_ada, b_ada.reshape(depth, 1, n6))


def _cast_kernel(w_ref, o_ref):
    o_ref[...] = w_ref[...].astype(BF16)


def _to_bf16(w, layer):
    rows, cols = w.shape[-2:]
    w4 = w.reshape(w.shape[0], -1, rows, cols)
    m = w4.shape[1]
    rb = rows
    while rb * cols * 4 > CAST_BLOCK_BYTES and rb % (4 * SUBLANES) == 0:
        rb //= 2
    out = pl.pallas_call(
        _cast_kernel,
        grid=(m, rows // rb),
        in_specs=[pl.BlockSpec((None, None, rb, cols), lambda e, r: (layer, e, r, 0))],
        out_specs=pl.BlockSpec((None, rb, cols), lambda e, r: (e, r, 0)),
        out_shape=jax.ShapeDtypeStruct((m, rows, cols), BF16),
        compiler_params=_cparams(("arbitrary", "arbitrary")),
        name="to_bf16",
    )(w4)
    return out.reshape(w.shape[1:])


def _cast_t_kernel(w_ref, o_ref):
    o_ref[...] = w_ref[...].T.astype(BF16)


def _ml_weights(w, layer, hq):
    _, d, ncol = w.shape
    nb = ncol // hq
    rb = min(d, 512)
    voq = pl.pallas_call(
        _cast_kernel,
        grid=(d // rb, nb - 1),
        in_specs=[pl.BlockSpec((None, rb, hq), lambda r, k: (layer, r, (k + 2) % nb))],
        out_specs=pl.BlockSpec((rb, hq), lambda r, k: (r, k)),
        out_shape=jax.ShapeDtypeStruct((d, ncol - hq), BF16),
        compiler_params=_cparams(("arbitrary", "arbitrary")),
        name="to_bf16_voq",
    )(w)
    kt = pl.pallas_call(
        _cast_t_kernel,
        grid=(d // rb,),
        in_specs=[pl.BlockSpec((None, rb, hq), lambda r: (layer, r, 1))],
        out_specs=pl.BlockSpec((hq, rb), lambda r: (0, r)),
        out_shape=jax.ShapeDtypeStruct((hq, d), BF16),
        compiler_params=_cparams(("arbitrary",)),
        name="to_bf16_kt",
    )(w)
    return voq, kt


def _mod_spec(chunk, d, tiles_per_batch, n_batch):
    return pl.BlockSpec(
        (1, 1, d),
        lambda i, *_: (jnp.minimum(i // tiles_per_batch, n_batch), 0, chunk))


def _conv_in_kernel(nxa, xa_ref, xb_ref, sh_ref, sc_ref, wb_ref, wc_ref, wv_ref,
                    bg_ref, z_ref, h_ref):
    i = pl.program_id(0)
    j = pl.program_id(1)

    @pl.when((j == 0) & (i < nxa))
    def _():
        h_ref[...] = (xa_ref[...] * (1.0 + sc_ref[0]) + sh_ref[0]).astype(BF16)

    @pl.when((j == 0) & (i >= nxa))
    def _():
        h_ref[...] = (xb_ref[...] * (1.0 + sc_ref[0]) + sh_ref[0]).astype(BF16)

    h = h_ref[...]
    bg = jnp.dot(h, wb_ref[...], preferred_element_type=F32)
    cg = jnp.dot(h, wc_ref[...], preferred_element_type=F32)
    v = jnp.dot(h, wv_ref[...], preferred_element_type=F32)
    bg_ref[...] = bg.astype(BF16)
    z_ref[...] = (cg * v).astype(BF16)


def _conv_in(x2d, c2d, mod, w_in_b, n_batch, seq):
    nx, d = x2d.shape
    nc = c2d.shape[0]
    bm = min(PROJ_ROWS, seq, nc)
    assert seq % bm == 0 and nc % bm == 0
    tn = 512 if d % 512 == 0 else d
    nj = d // tn
    nxa = nx // bm
    n_all = nx + nc
    tpb = seq // bm
    mspec = functools.partial(_mod_spec, d=d, tiles_per_batch=tpb, n_batch=n_batch)
    return pl.pallas_call(
        functools.partial(_conv_in_kernel, nxa),
        grid=(n_all // bm, nj),
        in_specs=[
            pl.BlockSpec((bm, d), lambda i, j: (jnp.minimum(i, nxa - 1), 0)),
            pl.BlockSpec((bm, d), lambda i, j: (jnp.maximum(i - nxa, 0), 0),
                         pipeline_mode=pl.Buffered(1)),
            mspec(0), mspec(1),
            pl.BlockSpec((d, tn), lambda i, j: (0, j)),
            pl.BlockSpec((d, tn), lambda i, j: (0, nj + j)),
            pl.BlockSpec((d, tn), lambda i, j: (0, 2 * nj + j)),
        ],
        out_specs=[
            pl.BlockSpec((bm, tn), lambda i, j: (i, j)),
            pl.BlockSpec((bm, tn), lambda i, j: (i, j)),
        ],
        out_shape=[jax.ShapeDtypeStruct((n_all, d), BF16),
                   jax.ShapeDtypeStruct((n_all, d), BF16)],
        scratch_shapes=[pltpu.VMEM((bm, d), BF16)],
        compiler_params=_cparams(("arbitrary", "arbitrary")),
        name="conv_in",
    )(x2d, c2d, mod, mod, w_in_b, w_in_b, w_in_b)


def _ml_in_kernel(x_ref, sh_ref, sc_ref, w_ref, wkt_ref, wg_ref, bgate_ref,
                  u_ref, kt_ref, g_ref, h_ref):
    j = pl.program_id(1)

    @pl.when(j == 0)
    def _():
        h = (x_ref[...] * (1.0 + sc_ref[0]) + sh_ref[0]).astype(BF16)
        h_ref[...] = h
        g_ref[...] = jnp.dot(h, wg_ref[...], preferred_element_type=F32) + bgate_ref[...]
        kt_ref[...] = lax.dot_general(wkt_ref[...], h, (((1,), (1,)), ((), ())),
                                      preferred_element_type=F32).astype(BF16)

    u_ref[...] = jnp.dot(h_ref[...], w_ref[...], preferred_element_type=F32).astype(BF16)


def _ml_in(xall, mod, w_voq_b, w_kt_b, w_gate_pad, b_gate_pad, n_batch, seq, n_ctx_rows):
    n_all, d = xall.shape
    nu = w_voq_b.shape[1]
    hq = w_kt_b.shape[0]
    bm = min(PROJ_ROWS, seq, n_ctx_rows)
    assert seq % bm == 0 and n_ctx_rows % bm == 0
    tn = hq
    assert nu % tn == 0
    gl = w_gate_pad.shape[1]
    mspec = functools.partial(_mod_spec, d=d, tiles_per_batch=seq // bm, n_batch=n_batch)
    return pl.pallas_call(
        _ml_in_kernel,
        grid=(n_all // bm, nu // tn),
        in_specs=[
            pl.BlockSpec((bm, d), lambda i, j: (i, 0)),
            mspec(0), mspec(1),
            pl.BlockSpec((d, tn), lambda i, j: (0, j)),
            pl.BlockSpec((hq, d), lambda i, j: (0, 0)),
            pl.BlockSpec((d, gl), lambda i, j: (0, 0)),
            pl.BlockSpec((1, gl), lambda i, j: (0, 0)),
        ],
        out_specs=[
            pl.BlockSpec((bm, tn), lambda i, j: (i, j)),
            pl.BlockSpec((hq, bm), lambda i, j: (0, i)),
            pl.BlockSpec((bm, gl), lambda i, j: (i, 0)),
        ],
        out_shape=[jax.ShapeDtypeStruct((n_all, nu), BF16),
                   jax.ShapeDtypeStruct((hq, n_all), BF16),
                   jax.ShapeDtypeStruct((n_all, gl), F32)],
        scratch_shapes=[pltpu.VMEM((bm, d), BF16)],
        compiler_params=_cparams(("arbitrary", "arbitrary")),
        name="mlstm_in",
    )(xall, mod, mod, w_voq_b, w_kt_b, w_gate_pad, b_gate_pad)


def _dot_split3(a_b, x):
    hi = x.astype(BF16)
    r1 = x - hi.astype(F32)
    mid = r1.astype(BF16)
    lo = (r1 - mid.astype(F32)).astype(BF16)
    return (jnp.dot(a_b, hi, preferred_element_type=F32)
            + jnp.dot(a_b, mid, preferred_element_type=F32)
            + jnp.dot(a_b, lo, preferred_element_type=F32))


def _scan_kernel(n_heads, dqk, dv, q_ref, kt_ref, v_ref, g_ref, o_ref, ct_ref, m_ref):
    d = pl.program_id(1)
    s = pl.program_id(2)
    L = q_ref.shape[0]
    assert L == LANES
    scale = dqk ** -0.5

    @pl.when(s == 0)
    def _():
        ct_ref[...] = jnp.zeros_like(ct_ref)
        m_ref[...] = jnp.zeros_like(m_ref)

    H = n_heads
    heads = range(H)
    qi = lax.broadcasted_iota(I32, (L, L), 0)
    si = lax.broadcasted_iota(I32, (L, L), 1)
    fwd = d == 0
    mask = jnp.where(fwd, si - qi, qi - si) <= 0

    g = g_ref[...]
    b_all = _dot_split3(mask.astype(BF16), jax.nn.log_sigmoid(g))
    g_t = g.T
    b_t = b_all.T
    def lane_bcast(x, lane0):
        return jnp.stack([jnp.broadcast_to(x[:, lane0 + h:lane0 + h + 1], (L, LANES))
                          for h in heads])

    b_b = lane_bcast(b_all, H)
    i_b = lane_bcast(g, 0)
    b_end = jnp.where(fwd, b_b[:, L - 1:L, :], b_b[:, 0:1, :])
    m_st = m_ref[:, 0:1, :]
    tile = lambda x, n: jnp.concatenate([x] * (n // LANES), axis=-1)

    q3 = jnp.stack([q_ref[:, h * dqk:(h + 1) * dqk] for h in heads])
    kt3 = jnp.stack([kt_ref[h * dqk:(h + 1) * dqk, :] for h in heads])
    v3 = jnp.stack([v_ref[:, h * dv:(h + 1) * dv] for h in heads])

    r3 = jnp.stack([g_t[h:h + 1, :] - b_t[H + h:H + h + 1, :] for h in heads])
    dm = jnp.where(mask, b_b + r3, -jnp.inf)
    a_inter = b_b + m_st
    m_q = jnp.maximum(a_inter, jnp.max(dm, axis=-1, keepdims=True))
    inter = jnp.exp(a_inter - m_q) * scale
    sc = jnp.einsum("hqd,hds->hqs", q3, kt3, preferred_element_type=F32)
    p = jnp.exp(dm - m_q) * (sc * scale)
    ct = ct_ref[...]
    qc = jnp.einsum("hqd,hdv->hqv", q3, ct.astype(BF16), preferred_element_type=F32)
    v_ext = jnp.concatenate([v3, jnp.ones((H, L, LANES), BF16)], axis=-1)
    pv = jnp.einsum("hqs,hsv->hqv", p.astype(BF16), v_ext, preferred_element_type=F32)
    den = pv[:, :, dv:] + inter * qc[:, :, dv:]
    rden = 1.0 / jnp.maximum(jnp.abs(den), jnp.exp(-m_q))
    hout = (pv[:, :, :dv] + tile(inter, dv) * qc[:, :, :dv]) * tile(rden, dv)
    for h in heads:
        o_ref[:, h * dv:(h + 1) * dv] = hout[h]

    wl = b_end - b_b + i_b
    m_next = jnp.maximum(b_end + m_st, jnp.max(wl, axis=1, keepdims=True))
    decay = jnp.exp(b_end + m_st - m_next)
    w_b = jnp.exp(wl - m_next)
    vw = jnp.concatenate([v3.astype(F32) * tile(w_b, dv), w_b], axis=-1).astype(BF16)
    upd = jnp.einsum("hdl,hlv->hdv", kt3, vw, preferred_element_type=F32)
    ct_ref[...] = tile(decay, dv + LANES) * ct + upd
    m_ref[...] = jnp.broadcast_to(m_next, m_ref.shape)


def _scan(u, kt, gates, n_batch, seq, ctx_len, n_heads, dqk, dv):
    L = SCAN_CHUNK
    nx = n_batch * seq
    ncc = ctx_len // L
    nlc = seq // L
    assert ctx_len % L == 0 and seq % L == 0
    hq = n_heads * dqk
    hv = n_heads * dv
    assert hv == 2 * hq

    def row_blk(b, d, s):
        ctx = (nx + b * ctx_len) // L + jnp.where(d == 0, s, ncc - 1 - s)
        sl = s - ncc
        lat = (b * seq) // L + jnp.where(d == 0, sl, nlc - 1 - sl)
        return jnp.where(s < ncc, ctx, lat)

    def out_blk(b, d, s):
        sl = jnp.maximum(s - ncc, 0)
        return (b * seq) // L + jnp.where(d == 0, sl, nlc - 1 - sl)

    return pl.pallas_call(
        functools.partial(_scan_kernel, n_heads, dqk, dv),
        grid=(n_batch, 2, ncc + nlc),
        in_specs=[
            pl.BlockSpec((L, hq), lambda b, d, s: (row_blk(b, d, s), 2 * hv // hq)),
            pl.BlockSpec((hq, L), lambda b, d, s: (0, row_blk(b, d, s))),
            pl.BlockSpec((L, hv), lambda b, d, s: (row_blk(b, d, s), 0)),
            pl.BlockSpec((L, LANES), lambda b, d, s: (row_blk(b, d, s), d)),
        ],
        out_specs=pl.BlockSpec((None, L, hv), lambda b, d, s: (d, out_blk(b, d, s), 0)),
        out_shape=jax.ShapeDtypeStruct((2, nx, hv), F32),
        scratch_shapes=[pltpu.VMEM((n_heads, dqk, dv + LANES), F32),
                        pltpu.VMEM((n_heads, SUBLANES, LANES), F32)],
        compiler_params=_cparams(("arbitrary", "arbitrary", "arbitrary")),
        name="mlstm_scan",
    )(u, kt, u, gates)


def _mix_epilogue(alpha, rs, a, x_ref, g1_ref, sh2_ref, sc2_ref, wout_ref, lng_ref,
                  lnb_ref, rwt_ref, x1_ref, hx2_ref, logt_ref):
    mx = jnp.dot(a, wout_ref[...], preferred_element_type=F32)
    x1 = _layer_norm_rows(alpha * x_ref[rs, :] + g1_ref[0] * mx, lng_ref[...], lnb_ref[...])
    x1_ref[rs, :] = x1
    hx2 = x1 * (1.0 + sc2_ref[0]) + sh2_ref[0]
    hx2_ref[rs, :] = hx2
    logt_ref[:, rs] = lax.dot_general(rwt_ref[...], hx2.astype(BF16), (((1,), (1,)), ((), ())),
                                      preferred_element_type=F32)


def _sub_rows(tm, sub):
    return [slice(r, r + sub) for r in range(0, tm, sub)]


def _conv_out_kernel(alpha, ctx_mode, n_lat, tpb, sub, z_ref, zp_ref, zn_ref, bg_ref, cw_ref,
                     x_ref, g1_ref, sh2_ref, sc2_ref, wout_ref, lng_ref, lnb_ref, rwt_ref, *rest):
    x1_ref, hx2_ref, logt_ref = rest[-3:]
    i = pl.program_id(0)
    tm, d = z_ref.shape
    half = d // 2
    rows = lax.broadcasted_iota(I32, (sub, 1), 0)

    def conv1(zz, w3, period):
        pos = rows & (period - 1)
        prev = pltpu.roll(zz, 1, 0) * (pos != 0).astype(F32)
        nxt = pltpu.roll(zz, sub - 1, 0) * (pos != period - 1).astype(F32)
        return w3[0:1] * prev + w3[1:2] * zz + w3[2:3] * nxt

    def tile():
        cw = cw_ref[...]
        if not ctx_mode:
            ti = i % tpb
            up = zp_ref[...].astype(F32) * (ti > 0).astype(F32)
            dn = zn_ref[...].astype(F32) * (ti < tpb - 1).astype(F32)
            w3 = cw[:, half:]

        for rs in _sub_rows(tm, sub):
            z = z_ref[rs, :].astype(F32)
            bg = bg_ref[rs, :].astype(F32)
            if ctx_mode:
                a = (bg * conv1(z, cw, sub)).astype(BF16)
            else:
                a_row = (bg[:, :half] * conv1(z[:, :half], cw[:, :half], GRID_W)).astype(BF16)
                r0, r1 = rs.start, rs.stop
                f32_rows = lambda a, b: z_ref[a:b, half:].astype(F32)
                prev = (jnp.concatenate([up, f32_rows(0, r1 - GRID_W)], axis=0) if r0 == 0
                        else f32_rows(r0 - GRID_W, r1 - GRID_W))
                nxt = (jnp.concatenate([f32_rows(r0 + GRID_W, tm), dn], axis=0) if r1 == tm
                       else f32_rows(r0 + GRID_W, r1 + GRID_W))
                y = w3[0:1] * prev + w3[1:2] * z[:, half:] + w3[2:3] * nxt
                a = jnp.concatenate([a_row, (bg[:, half:] * y).astype(BF16)], axis=1)
            _mix_epilogue(alpha, rs, a, x_ref, g1_ref, sh2_ref, sc2_ref, wout_ref, lng_ref,
                          lnb_ref, rwt_ref, x1_ref, hx2_ref, logt_ref)

    if ctx_mode:
        tile()
    else:
        pl.when(i < n_lat)(tile)

        @pl.when(i >= n_lat)
        def _():
            x1_ref[...] = jnp.zeros_like(x1_ref)
            hx2_ref[...] = jnp.zeros_like(hx2_ref)
            logt_ref[...] = jnp.zeros_like(logt_ref)


def _ml_out_kernel(alpha, n_heads, dv, sub, hf_ref, hb_ref, og_ref, ng_ref, x_ref,
                   g1_ref, sh2_ref, sc2_ref, wout_ref, lng_ref, lnb_ref, rwt_ref,
                   x1_ref, hx2_ref, logt_ref):
    for rs in _sub_rows(hf_ref.shape[0], sub):
        parts = []
        for h in range(n_heads):
            sl = slice(h * dv, (h + 1) * dv)
            hs = hf_ref[rs, sl] + hb_ref[rs, sl]
            mu = jnp.mean(hs, axis=-1, keepdims=True)
            c = hs - mu
            var = jnp.mean(c * c, axis=-1, keepdims=True)
            hn = c * lax.rsqrt(var + HEAD_NORM_EPS)
            gate = jax.nn.sigmoid(og_ref[rs, sl].astype(F32))
            parts.append((hn * ng_ref[:, sl] * gate).astype(BF16))
        _mix_epilogue(alpha, rs, jnp.concatenate(parts, axis=1), x_ref, g1_ref, sh2_ref,
                      sc2_ref, wout_ref, lng_ref, lnb_ref, rwt_ref, x1_ref, hx2_ref, logt_ref)


def _mix_out_common(d, n_rows, n_exp, mod_row, tile0=0):
    tm = ROW_TILE
    mspec = lambda chunk: pl.BlockSpec((1, 1, d), lambda i: (mod_row(i), 0, chunk))
    const = lambda shape: pl.BlockSpec(shape, lambda i: (0,) * len(shape))
    in_specs = [mspec(2), mspec(3), mspec(4), const((d, d)), const((1, d)), const((1, d)),
                const((n_exp, d))]
    out_specs = [pl.BlockSpec((tm, d), lambda i: (i + tile0, 0)),
                 pl.BlockSpec((tm, d), lambda i: (i + tile0, 0)),
                 pl.BlockSpec((n_exp, tm), lambda i: (0, i + tile0))]
    out_shape = [jax.ShapeDtypeStruct((n_rows, d), F32),
                 jax.ShapeDtypeStruct((n_rows, d), F32),
                 jax.ShapeDtypeStruct((n_exp, n_rows), F32)]
    return in_specs, out_specs, out_shape


def _conv_out(alpha, z, bg, conv_w, x2d, c2d, mod, w_out_b, ln_g, ln_b, rwt_b, n_batch, seq,
              ctx_len):
    n_all, d = z.shape
    nx = x2d.shape[0]
    tm = ROW_TILE
    assert ctx_len == tm and seq % tm == 0 and MIX_SUB_ROWS % GRID_W == 0
    half = d // 2
    nxa = nx // tm
    tpb = seq // tm
    hpt = tm // GRID_W
    n_exp = rwt_b.shape[0]

    def z_specs(tile0):
        return [
            pl.BlockSpec((tm, d), lambda i: (i + tile0, 0)),
            pl.BlockSpec((GRID_W, half), lambda i: (jnp.maximum((i + tile0) * hpt - 1, 0), 1)),
            pl.BlockSpec((GRID_W, half), lambda i: ((i + tile0 + 1) * hpt - 1, 1)),
            pl.BlockSpec((tm, d), lambda i: (i + tile0, 0)),
            pl.BlockSpec((3, d), lambda i: (0, 0)),
            pl.BlockSpec((tm, d), lambda i: (i, 0)),
        ]

    lat = lambda i: jnp.minimum(i, nxa - 1)
    common_in, out_specs, out_shape = _mix_out_common(d, n_all, n_exp, lambda i: lat(i) // tpb)
    lat_specs = z_specs(0)
    lat_specs[2] = pl.BlockSpec((GRID_W, half), lambda i: ((lat(i) + 1) * hpt, 1))
    lat_specs[5] = pl.BlockSpec((tm, d), lambda i: (lat(i), 0))
    outs = pl.pallas_call(
        functools.partial(_conv_out_kernel, alpha, False, nxa, tpb, MIX_SUB_ROWS),
        grid=(n_all // tm,),
        in_specs=lat_specs + common_in,
        out_specs=out_specs,
        out_shape=out_shape,
        compiler_params=_cparams(("arbitrary",)),
        name="conv_out",
    )(z, z, z, bg, conv_w, x2d, mod, mod, mod, w_out_b, ln_g, ln_b, rwt_b)

    common_in, out_specs, out_shape = _mix_out_common(d, n_all, n_exp, lambda i: n_batch, nxa)
    n_in = 6 + len(common_in)
    keep = [pl.BlockSpec(memory_space=pl.ANY)] * 3
    return pl.pallas_call(
        functools.partial(_conv_out_kernel, alpha, True, None, tpb, tm),
        grid=(c2d.shape[0] // tm,),
        in_specs=z_specs(nxa) + common_in + keep,
        out_specs=out_specs,
        out_shape=out_shape,
        input_output_aliases={n_in: 0, n_in + 1: 1, n_in + 2: 2},
        compiler_params=_cparams(("arbitrary",)),
        name="conv_out_ctx",
    )(z, z, z, bg, conv_w, c2d, mod, mod, mod, w_out_b, ln_g, ln_b, rwt_b, *outs)


def _ml_out(alpha, hfb, u, norm_g, xall, mod, w_out_b, ln_g, ln_b, rwt_b, n_batch, seq,
            n_heads, dv):
    nx, d = hfb.shape[1:]
    tm = ROW_TILE
    tpb = seq // tm
    common_in, out_specs, out_shape = _mix_out_common(d, nx, rwt_b.shape[0], lambda i: i // tpb)
    o_blk = 1
    in_specs = [
        pl.BlockSpec((None, tm, d), lambda i: (0, i, 0)),
        pl.BlockSpec((None, tm, d), lambda i: (1, i, 0)),
        pl.BlockSpec((tm, d), lambda i: (i, o_blk)),
        pl.BlockSpec((1, d), lambda i: (0, 0)),
        pl.BlockSpec((tm, d), lambda i: (i, 0)),
    ] + common_in
    return pl.pallas_call(
        functools.partial(_ml_out_kernel, alpha, n_heads, dv, tm),
        grid=(nx // tm,),
        in_specs=in_specs,
        out_specs=out_specs,
        out_shape=out_shape,
        compiler_params=_cparams(("arbitrary",)),
        name="mlstm_out",
    )(hfb, hfb, u, norm_g, xall, mod, mod, mod, w_out_b, ln_g, ln_b, rwt_b)


def _route_kernel(tile_rows, n_te, logt_ref, rb_ref, pos_ref, wcol_ref, te_ref, meta_ref,
                  carry_ref, before_ref):
    ph = pl.program_id(0)
    i = pl.program_id(1)
    n_steps = pl.num_programs(1)
    n_exp, tr = logt_ref.shape
    epg = n_exp // N_GROUPS

    @pl.when((ph == 0) & (i == 0))
    def _():
        carry_ref[...] = jnp.zeros_like(carry_ref)
        ti = lax.broadcasted_iota(I32, (tr, tr), 0)
        tj = lax.broadcasted_iota(I32, (tr, tr), 1)
        before_ref[...] = (ti < tj).astype(BF16)

    s = jax.nn.sigmoid(logt_ref[...])
    sel = s + rb_ref[...]
    row = lax.broadcasted_iota(I32, (n_exp, tr), 0)
    member = row % epg
    group = row // epg

    def partner(x, k, idx, span, unit):
        wrapped = (idx + k) >= span
        up = pltpu.roll(x, n_exp - k * unit, 0)
        down = pltpu.roll(x, (span - k) * unit, 0)
        return jnp.where(wrapped, down, up), wrapped

    rank_in = jnp.zeros((n_exp, tr), F32)
    for k in range(1, epg):
        p, wrapped = partner(sel, k, member, epg, 1)
        beats = (p > sel) | ((p == sel) & wrapped)
        rank_in = rank_in + beats.astype(F32)
    top = (rank_in < TOP_K).astype(F32)
    gs = sel * top
    score = gs
    for k in range(1, epg):
        p, _ = partner(gs, k, member, epg, 1)
        score = score + p
    n_better = jnp.zeros((n_exp, tr), F32)
    for k in range(1, N_GROUPS):
        p, wrapped = partner(score, k, group, N_GROUPS, epg)
        beats = (p > score) | ((p == score) & wrapped)
        n_better = n_better + beats.astype(F32)
    best = n_better == 0.0
    m1 = (best & (rank_in == 0.0)).astype(F32)
    m2 = (best & (rank_in == 1.0)).astype(F32)
    s1 = jnp.sum(s * m1, axis=0, keepdims=True)
    s2 = jnp.sum(s * m2, axis=0, keepdims=True)
    den = s1 + s2
    oh = m1 + m2

    slot = jnp.dot(oh.astype(BF16), before_ref[...], preferred_element_type=F32) + carry_ref[...]

    @pl.when(ph == 1)
    def _():
        pos_ref[0:1, :] = jnp.sum(m1 * slot, axis=0, keepdims=True).astype(I32)
        pos_ref[1:2, :] = jnp.sum(m2 * slot, axis=0, keepdims=True).astype(I32)
        w8 = jnp.concatenate([s1 / den, s2 / den, jnp.zeros((LANES - 2, tr), F32)], axis=0)
        wcol_ref[...] = w8.T

    carry_ref[...] = carry_ref[...] + jnp.sum(oh, axis=1, keepdims=True)

    @pl.when((ph == 0) & (i == n_steps - 1))
    def _():
        cnt = carry_ref[...]
        ntile = jnp.floor((cnt + (tile_rows - 1)) * (1.0 / tile_rows))
        er = lax.broadcasted_iota(I32, (n_exp, LANES), 0)
        el = lax.broadcasted_iota(I32, (n_exp, LANES), 1)
        eye = (er == el).astype(F32)
        nt_row = jnp.sum(ntile * eye, axis=0, keepdims=True)
        cnt_row = jnp.sum(cnt * eye, axis=0, keepdims=True)
        cum_excl = jnp.sum(nt_row * (el < er).astype(F32), axis=1, keepdims=True)
        off = cum_excl * tile_rows
        off_row = jnp.sum(off * eye, axis=0, keepdims=True)
        total = jnp.sum(nt_row, axis=1, keepdims=True)
        cum_incl = cum_excl + ntile
        tl = lax.broadcasted_iota(I32, (n_exp, n_te), 1).astype(F32)
        te = jnp.sum((cum_incl <= tl).astype(F32), axis=0, keepdims=True)
        elf = el.astype(F32)
        later = (el > er) & (nt_row > 0.0) & (el < n_exp)
        nxt = jnp.min(jnp.where(later, elf, float(n_exp)), axis=1, keepdims=True)
        nxt = jnp.where(nxt == float(n_exp), er[:, 0:1].astype(F32), nxt)
        mine = (cum_excl <= tl) & (tl < cum_incl)
        te_next = jnp.sum(jnp.where(mine, nxt, 0.0), axis=0, keepdims=True)
        te_ref[...] = jnp.concatenate([jnp.minimum(te, n_exp - 1), te_next], axis=0).astype(I32)
        meta_ref[...] = jnp.concatenate(
            [cnt_row, off_row, jnp.broadcast_to(total, (1, LANES)),
             jnp.zeros((SUBLANES - 3, LANES), F32)], axis=0).astype(I32)
        carry_ref[...] = off


def _route(logt, router_b, tile_rows):
    n_exp, n = logt.shape
    tr = ROUTE_COLS
    while n % tr:
        tr //= 2
    n_te = 256
    assert (2 * n) // tile_rows + n_exp <= n_te
    return pl.pallas_call(
        functools.partial(_route_kernel, tile_rows, n_te),
        grid=(2, n // tr),
        in_specs=[pl.BlockSpec((n_exp, tr), lambda p, i: (0, i)),
                  pl.BlockSpec((n_exp, 1), lambda p, i: (0, 0))],
        out_specs=[pl.BlockSpec((2, tr), lambda p, i: (0, i * p)),
                   pl.BlockSpec((tr, LANES), lambda p, i: (i * p, 0)),
                   pl.BlockSpec((2, n_te), lambda p, i: (0, 0)),
                   pl.BlockSpec((SUBLANES, LANES), lambda p, i: (0, 0))],
        out_shape=[jax.ShapeDtypeStruct((2, n), I32),
                   jax.ShapeDtypeStruct((n, LANES), F32),
                   jax.ShapeDtypeStruct((2, n_te), I32),
                   jax.ShapeDtypeStruct((SUBLANES, LANES), I32)],
        scratch_shapes=[pltpu.VMEM((n_exp, 1), F32), pltpu.VMEM((tr, tr), BF16)],
        compiler_params=_cparams(("arbitrary", "arbitrary")),
        name="route",
    )(logt, router_b.reshape(n_exp, 1).astype(F32))


def _tbl_kernel(n_tok, n_exp, tile_rows, meta_ref, pos_ref, tbl_ref):
    i = pl.program_id(0)
    blk = pos_ref.shape[1]

    @pl.when(i == 0)
    def _():
        n_pad = 0
        for e in range(n_exp):
            cnt = meta_ref[e]
            off = meta_ref[n_exp + e]
            up = ((cnt + (tile_rows - 1)) // tile_rows) * tile_rows

            def fill(r, j):
                tbl_ref[off + r] = (2 * n_tok + j) << SLOT_SRC_BITS
                return j + 1

            n_pad = lax.fori_loop(cnt, up, fill, n_pad)

        def fill_tail(r, carry):
            tbl_ref[r] = (2 * n_tok) << SLOT_SRC_BITS
            return carry

        lax.fori_loop(2 * n_tok + n_pad, tbl_ref.shape[0], fill_tail, 0)

    base = i * blk
    first = (base << SLOT_SRC_BITS) | jnp.where(base >= n_tok, base - n_tok, base)
    step = (1 << SLOT_SRC_BITS) + 1
    group = 8

    def body(q, carry):
        r0 = q * group
        slots = [pos_ref[0, r0 + j] for j in range(group)]
        for j in range(group):
            tbl_ref[slots[j]] = first + (r0 + j) * step
        return carry

    lax.fori_loop(0, blk // group, body, 0, unroll=True)


def _build_table(meta1d, pos, n_tok, n_exp, tile_rows, p_pad):
    blk = TBL_BLOCK
    while n_tok % blk:
        blk //= 2
    n_pairs = 2 * n_tok
    grid_spec = pltpu.PrefetchScalarGridSpec(
        num_scalar_prefetch=1,
        grid=(n_pairs // blk,),
        in_specs=[pl.BlockSpec((None, 1, blk), lambda i, m: (i, 0, 0), memory_space=pltpu.SMEM)],
        out_specs=pl.BlockSpec(memory_space=pltpu.SMEM),
    )
    return pl.pallas_call(
        functools.partial(_tbl_kernel, n_tok, n_exp, tile_rows),
        grid_spec=grid_spec,
        out_shape=jax.ShapeDtypeStruct((p_pad,), I32),
        compiler_params=_cparams(("arbitrary",)),
        name="slot_table",
    )(meta1d, pos.reshape(n_pairs // blk, 1, blk))


def _moe_kernel(layer, te_ref, ten_ref, nt_ref, tbl_ref, tbln_ref, tblp_ref, hx_hbm,
                wg_hbm, wu_hbm, wd_hbm, y_hbm, xbuf, ybuf, zbuf, wg_b, wu_b, wd_b,
                wg_s, wu_s, wd_s, gsem, ssem, zsem, wsem):
    t = pl.program_id(0)
    n = nt_ref[0]
    rows = tbl_ref.shape[1]
    staged = ((wg_hbm, wg_s, wg_b), (wu_hbm, wu_s, wu_b), (wd_hbm, wd_s, wd_b))

    def weight_copies(e):
        return [pltpu.make_async_copy(hbm.at[layer, e], stage, wsem.at[k])
                for k, (hbm, stage, _) in enumerate(staged)]

    @pl.when(t == 0)
    def _():
        for c in weight_copies(te_ref[0]):
            c.start(priority=WEIGHT_DMA_PRIORITY)

    tl = jnp.minimum(t, n - 1)
    new_expert = (t < n) & ((t == 0) | (te_ref[tl] != te_ref[jnp.maximum(tl - 1, 0)]))

    @pl.when(new_expert)
    def _():
        for c, (_, stage, dst) in zip(weight_copies(te_ref[tl]), staged):
            c.wait()
            dst[...] = stage[...].astype(BF16)
        for c in weight_copies(ten_ref[tl]):
            c.start(priority=WEIGHT_DMA_PRIORITY)

    def start_gather(tref, sl, lo=0, hi=None):
        for r in range(lo, rows if hi is None else hi):
            src = tref[0, r] & SLOT_SRC_MASK
            pltpu.make_async_copy(hx_hbm.at[pl.ds(src, 1), :],
                                  xbuf.at[sl, pl.ds(r, 1), :], gsem.at[sl]
                                  ).start(priority=r % 2)

    def start_scatter(tref, sl, lo=0, hi=None):
        for r in range(lo, rows if hi is None else hi):
            dst = tref[0, r] >> SLOT_SRC_BITS
            pltpu.make_async_copy(ybuf.at[sl, pl.ds(r, 1), :],
                                  y_hbm.at[pl.ds(dst, 1), :], ssem.at[sl]
                                  ).start(priority=(r + 1) % 2)

    def wait_gather(sl):
        pltpu.make_async_copy(xbuf.at[sl], xbuf.at[sl], gsem.at[sl]).wait()

    def wait_scatter(sl):
        pltpu.make_async_copy(ybuf.at[sl], ybuf.at[sl], ssem.at[sl]).wait()

    @pl.when(t == 0)
    def _():
        zbuf[...] = jnp.zeros_like(zbuf)
        start_gather(tbl_ref, 0)

    @pl.when(t >= n)
    def _():
        dst = y_hbm.at[pl.ds(pl.multiple_of(t * rows, rows), rows), :]
        fill = pltpu.make_async_copy(zbuf, dst, zsem)
        fill.start()
        fill.wait()

    def tile(slot, has_prev):
        other = 1 - slot
        wait_gather(slot)

        @pl.when(t >= 2)
        def _():
            wait_scatter(slot)

        f = wg_b.shape[1]
        d = wd_b.shape[1]
        fc, dc = MOE_UP_COLS, MOE_DOWN_COLS
        groups = f // fc + d // dc
        bounds = [(k * rows) // groups for k in range(groups + 1)]

        def copy_group(k):
            start_gather(tbln_ref, other, bounds[k], bounds[k + 1])
            if has_prev:
                start_scatter(tblp_ref, other, bounds[k], bounds[k + 1])

        xb = xbuf[slot].astype(BF16)
        hs = []
        for c in range(f // fc):
            cols = slice(c * fc, (c + 1) * fc)
            g = jnp.dot(xb, wg_b[:, cols], preferred_element_type=F32)
            u = jnp.dot(xb, wu_b[:, cols], preferred_element_type=F32)
            hs.append((g * jax.nn.sigmoid(g) * u).astype(BF16))
            copy_group(c)
        h = jnp.concatenate(hs, axis=1)
        for c in range(d // dc):
            cols = slice(c * dc, (c + 1) * dc)
            ybuf[slot, :, cols] = jnp.dot(h, wd_b[:, cols], preferred_element_type=F32)
            copy_group(f // fc + c)

        @pl.when(t == n - 1)
        def _():
            start_scatter(tbl_ref, slot)
            wait_gather(other)

            @pl.when(t >= 1)
            def _():
                wait_scatter(other)

            wait_scatter(slot)
            for c in weight_copies(ten_ref[tl]):
                c.wait()

    pl.when((t < n) & (t == 0))(functools.partial(tile, 0, False))
    for parity in (0, 1):
        pl.when((t < n) & (t > 0) & (lax.rem(t, 2) == parity))(
            functools.partial(tile, parity, True))


def _moe(te2, nt1d, tbl, hx2, w_gate, w_up, w_down, layer, n_tok, p_pad):
    tmm = MOE_ROWS
    _, n_exp, d, f = w_gate.shape
    t_max = (2 * n_tok) // tmm + n_exp
    last = lambda t, nt: jnp.minimum(t, nt[0] - 1)
    tspec = lambda fn: pl.BlockSpec((None, 1, tmm), lambda t, te, ten, nt: (fn(t, nt), 0, 0),
                                    memory_space=pltpu.SMEM)
    hbm = pl.BlockSpec(memory_space=pl.ANY)
    grid_spec = pltpu.PrefetchScalarGridSpec(
        num_scalar_prefetch=3,
        grid=(t_max,),
        in_specs=[
            tspec(last),
            tspec(lambda t, nt: last(t + 1, nt)),
            tspec(lambda t, nt: jnp.maximum(last(t, nt) - 1, 0)),
            hbm, hbm, hbm, hbm,
        ],
        out_specs=hbm,
        scratch_shapes=[pltpu.VMEM((2, tmm, d), F32), pltpu.VMEM((2, tmm, d), F32),
                        pltpu.VMEM((tmm, d), F32),
                        pltpu.VMEM((d, f), BF16), pltpu.VMEM((d, f), BF16),
                        pltpu.VMEM((f, d), BF16),
                        pltpu.VMEM((d, f), F32), pltpu.VMEM((d, f), F32),
                        pltpu.VMEM((f, d), F32),
                        pltpu.SemaphoreType.DMA((2,)), pltpu.SemaphoreType.DMA((2,)),
                        pltpu.SemaphoreType.DMA(()), pltpu.SemaphoreType.DMA((3,))],
    )
    tbl3 = tbl.reshape(p_pad // tmm, 1, tmm)
    return pl.pallas_call(
        functools.partial(_moe_kernel, layer),
        grid_spec=grid_spec,
        out_shape=jax.ShapeDtypeStruct((t_max * tmm, d), F32),
        compiler_params=_cparams(("arbitrary",)),
        name="moe_experts",
    )(te2[0], te2[1], nt1d, tbl3, tbl3, tbl3, hx2, w_gate, w_up, w_down)


def _comb_kernel(alpha, x_ref, y0_ref, y1_ref, w_ref, g2_ref, lng_ref, lnb_ref, o_ref):
    w = w_ref[...]
    ex = w[:, 0:1] * y0_ref[...] + w[:, 1:2] * y1_ref[...]
    o_ref[...] = _layer_norm_rows(alpha * x_ref[...] + g2_ref[0] * ex, lng_ref[...], lnb_ref[...])


def _combine(alpha, x1, y, wcol, mod, ln_g, ln_b, n_tok, n_out, n_batch, seq):
    d = x1.shape[1]
    tm = COMB_ROWS
    assert n_tok % tm == 0 and n_out % tm == 0 and seq % tm == 0
    nblk = n_tok // tm
    mspec = functools.partial(_mod_spec, d=d, tiles_per_batch=seq // tm, n_batch=n_batch)
    return pl.pallas_call(
        functools.partial(_comb_kernel, alpha),
        grid=(n_out // tm,),
        in_specs=[
            pl.BlockSpec((tm, d), lambda i: (i, 0)),
            pl.BlockSpec((tm, d), lambda i: (i, 0)),
            pl.BlockSpec((tm, d), lambda i: (i + nblk, 0)),
            pl.BlockSpec((tm, LANES), lambda i: (i, 0)),
            mspec(5),
            pl.BlockSpec((1, d), lambda i: (0, 0)),
            pl.BlockSpec((1, d), lambda i: (0, 0)),
        ],
        out_specs=pl.BlockSpec((tm, d), lambda i: (i, 0)),
        out_shape=jax.ShapeDtypeStruct((n_out, d), F32),
        compiler_params=_cparams(("arbitrary",)),
        name="moe_combine",
    )(x1, y, y, wcol, mod, ln_g, ln_b)


def _moe_layer(alpha, x1, hx2, logt, mod, ln_g, ln_b, router_b, w_gate, w_up, w_down, layer,
               n_out, n_batch, seq):
    n_tok = hx2.shape[0]
    n_exp = w_gate.shape[1]
    tmm = MOE_ROWS
    p_max = 2 * n_tok + n_exp * tmm
    p_pad = -(-p_max // TBL_BLOCK) * TBL_BLOCK
    assert n_tok <= 1 << SLOT_SRC_BITS and p_max < 1 << (31 - SLOT_SRC_BITS)
    pos, wcol, te, meta = _route(logt, router_b, tmm)
    meta1d = meta[:2, :n_exp].reshape(-1)
    tbl = _build_table(meta1d, pos, n_tok, n_exp, tmm, p_pad)
    y = _moe(te, meta[2, :1], tbl, hx2, w_gate, w_up, w_down, layer, n_tok, p_pad)
    return _combine(alpha, x1, y, wcol, mod, ln_g, ln_b, n_tok, n_out, n_batch, seq)


def kernel(x, c, ctx, c_ctx, w_ada, b_ada, ln_g, ln_b, conv_w_in, conv_w, conv_w_out, ml_w_in, ml_w_gate, ml_b_gate, ml_norm_g, ml_w_out, router_w, router_b, exp_w_gate, exp_w_up, exp_w_down):
    n_batch, seq, d = x.shape
    ctx_len = ctx.shape[1]
    depth = w_ada.shape[0]
    assert depth == 2, "layer 0 is the conv mixer, layer 1 the mLSTM mixer"
    alpha = (2 * depth) ** 0.25
    n_heads = ml_b_gate.shape[-1] // 4
    dqk = d // (2 * n_heads)
    dv = d // n_heads
    nx = n_batch * seq
    nc = n_batch * ctx_len
    assert n_batch < SUBLANES and 2 * n_heads <= LANES

    x2d = x.reshape(nx, d)
    c2d = ctx.reshape(nc, d)
    cc = jnp.zeros((SUBLANES, d), F32).at[:n_batch].set(c).at[n_batch].set(c_ctx)
    mod = _ada(cc, w_ada, b_ada).reshape(depth, SUBLANES, 1, 6 * d)
    rwt_b = router_w.T.astype(BF16)

    bg, z = _conv_in(x2d, c2d, mod[0], _to_bf16(conv_w_in, 0), n_batch, seq)
    x1, hx2, logt = _conv_out(alpha, z, bg, conv_w[0], x2d, c2d, mod[0],
                              _to_bf16(conv_w_out, 0), ln_g[0, 0:1], ln_b[0, 0:1], rwt_b,
                              n_batch, seq, ctx_len)
    xall = _moe_layer(alpha, x1, hx2, logt, mod[0], ln_g[0, 1:2], ln_b[0, 1:2], router_b,
                      exp_w_gate, exp_w_up, exp_w_down, 0, nx + nc, n_batch, seq)

    w_gate = ml_w_gate[0]
    b_gate = ml_b_gate[0]
    h2 = 2 * n_heads
    w_gate_pad = (jnp.zeros((d, 2 * LANES), F32).at[:, :h2].set(w_gate[:, :h2])
                  .at[:, LANES:LANES + h2].set(w_gate[:, h2:])).astype(BF16)
    b_gate_pad = (jnp.zeros((1, 2 * LANES), F32).at[0, :h2].set(b_gate[:h2])
                  .at[0, LANES:LANES + h2].set(b_gate[h2:]))
    w_voq_b, w_kt_b = _ml_weights(ml_w_in, 0, n_heads * dqk)
    u, kt, gates = _ml_in(xall, mod[1], w_voq_b, w_kt_b, w_gate_pad, b_gate_pad, n_batch,
                          seq, nc)
    hfb = _scan(u, kt, gates, n_batch, seq, ctx_len, n_heads, dqk, dv)
    x1, hx2, logt = _ml_out(alpha, hfb, u, ml_norm_g[0:1], xall, mod[1],
                            _to_bf16(ml_w_out, 0), ln_g[1, 0:1], ln_b[1, 0:1], rwt_b,
                            n_batch, seq, n_heads, dv)
    out = _moe_layer(alpha, x1, hx2, logt, mod[1], ln_g[1, 1:2], ln_b[1, 1:2], router_b,
                     exp_w_gate, exp_w_up, exp_w_down, 1, nx, n_batch, seq)
    return out.reshape(n_batch, seq, d)
```

```python
import functools

import jax
import jax.numpy as jnp
from jax import lax
from jax.experimental import pallas as pl
from jax.experimental.pallas import tpu as pltpu

F32 = jnp.float32
BF16 = jnp.bfloat16
I32 = jnp.int32

GRID_W = 64
N_GROUPS = 4
TOP_K = 2
LN_EPS = 1e-5
HEAD_NORM_EPS = 1e-6

LANES = 128
SUBLANES = 8
VMEM_LIMIT_BYTES = 56 * 1024 * 1024

ROW_TILE = 256
MIX_SUB_ROWS = 128
PROJ_ROWS = 1024
MOE_ROWS = 256
MOE_UP_COLS = 256
MOE_DOWN_COLS = 512
ROUTE_COLS = 1024
COMB_ROWS = 512
SCAN_CHUNK = 128
TBL_BLOCK = 1024
SLOT_SRC_BITS = 15
SLOT_SRC_MASK = (1 << SLOT_SRC_BITS) - 1
CAST_BLOCK_BYTES = 8 * 1024 * 1024
WEIGHT_DMA_PRIORITY = 1


def _cparams(sem):
    return pltpu.CompilerParams(dimension_semantics=sem, vmem_limit_bytes=VMEM_LIMIT_BYTES)


def _layer_norm_rows(r, g, b):
    mu = jnp.mean(r, axis=-1, keepdims=True)
    c = r - mu
    var = jnp.mean(c * c, axis=-1, keepdims=True)
    return c * lax.rsqrt(var + LN_EPS) * g + b


def _ada_kernel(cc_ref, w_ref, b_ref, o_ref):
    a = cc_ref[...]
    a = (a * jax.nn.sigmoid(a)).astype(BF16)
    o_ref[0] = jnp.dot(a, w_ref[0].astype(BF16), preferred_element_type=F32) + b_ref[0]


def _ada(cc, w_ada, b_ada):
    depth, d, n6 = w_ada.shape
    tn = 1024
    return pl.pallas_call(
        _ada_kernel,
        grid=(depth, n6 // tn),
        in_specs=[
            pl.BlockSpec((SUBLANES, d), lambda l, j: (0, 0)),
            pl.BlockSpec((1, d, tn), lambda l, j: (l, 0, j)),
            pl.BlockSpec((1, 1, tn), lambda l, j: (l, 0, j)),
        ],
        out_specs=pl.BlockSpec((1, SUBLANES, tn), lambda l, j: (l, 0, j)),
        out_shape=jax.ShapeDtypeStruct((depth, SUBLANES, n6), F32),
        compiler_params=_cparams(("arbitrary", "arbitrary")),
        name="ada_mod",
    )(cc, w_ada, b_ada.reshape(depth, 1, n6))


def _cast_kernel(w_ref, o_ref):
    o_ref[...] = w_ref[...].astype(BF16)


def _to_bf16(w, layer):
    rows, cols = w.shape[-2:]
    w4 = w.reshape(w.shape[0], -1, rows, cols)
    m = w4.shape[1]
    rb = rows
    while rb * cols * 4 > CAST_BLOCK_BYTES and rb % (4 * SUBLANES) == 0:
        rb //= 2
    out = pl.pallas_call(
        _cast_kernel,
        grid=(m, rows // rb),
        in_specs=[pl.BlockSpec((None, None, rb, cols), lambda e, r: (layer, e, r, 0))],
        out_specs=pl.BlockSpec((None, rb, cols), lambda e, r: (e, r, 0)),
        out_shape=jax.ShapeDtypeStruct((m, rows, cols), BF16),
        compiler_params=_cparams(("arbitrary", "arbitrary")),
        name="to_bf16",
    )(w4)
    return out.reshape(w.shape[1:])


def _cast_t_kernel(w_ref, o_ref):
    o_ref[...] = w_ref[...].T.astype(BF16)


def _ml_weights(w, layer, hq):
    _, d, ncol = w.shape
    nb = ncol // hq
    rb = min(d, 512)
    voq = pl.pallas_call(
        _cast_kernel,
        grid=(d // rb, nb - 1),
        in_specs=[pl.BlockSpec((None, rb, hq), lambda r, k: (layer, r, (k + 2) % nb))],
        out_specs=pl.BlockSpec((rb, hq), lambda r, k: (r, k)),
        out_shape=jax.ShapeDtypeStruct((d, ncol - hq), BF16),
        compiler_params=_cparams(("arbitrary", "arbitrary")),
        name="to_bf16_voq",
    )(w)
    kt = pl.pallas_call(
        _cast_t_kernel,
        grid=(d // rb,),
        in_specs=[pl.BlockSpec((None, rb, hq), lambda r: (layer, r, 1))],
        out_specs=pl.BlockSpec((hq, rb), lambda r: (0, r)),
        out_shape=jax.ShapeDtypeStruct((hq, d), BF16),
        compiler_params=_cparams(("arbitrary",)),
        name="to_bf16_kt",
    )(w)
    return voq, kt


def _mod_spec(chunk, d, tiles_per_batch, n_batch):
    return pl.BlockSpec(
        (1, 1, d),
        lambda i, *_: (jnp.minimum(i // tiles_per_batch, n_batch), 0, chunk))


def _conv_in_kernel(nxa, xa_ref, xb_ref, sh_ref, sc_ref, wb_ref, wc_ref, wv_ref,
                    bg_ref, z_ref, h_ref):
    i = pl.program_id(0)
    j = pl.program_id(1)

    @pl.when((j == 0) & (i < nxa))
    def _():
        h_ref[...] = (xa_ref[...] * (1.0 + sc_ref[0]) + sh_ref[0]).astype(BF16)

    @pl.when((j == 0) & (i >= nxa))
    def _():
        h_ref[...] = (xb_ref[...] * (1.0 + sc_ref[0]) + sh_ref[0]).astype(BF16)

    h = h_ref[...]
    bg = jnp.dot(h, wb_ref[...], preferred_element_type=F32)
    cg = jnp.dot(h, wc_ref[...], preferred_element_type=F32)
    v = jnp.dot(h, wv_ref[...], preferred_element_type=F32)
    bg_ref[...] = bg.astype(BF16)
    z_ref[...] = (cg * v).astype(BF16)


def _conv_in(x2d, c2d, mod, w_in_b, n_batch, seq):
    nx, d = x2d.shape
    nc = c2d.shape[0]
    bm = min(PROJ_ROWS, seq, nc)
    assert seq % bm == 0 and nc % bm == 0
    tn = 512 if d % 512 == 0 else d
    nj = d // tn
    nxa = nx // bm
    n_all = nx + nc
    tpb = seq // bm
    mspec = functools.partial(_mod_spec, d=d, tiles_per_batch=tpb, n_batch=n_batch)
    return pl.pallas_call(
        functools.partial(_conv_in_kernel, nxa),
        grid=(n_all // bm, nj),
        in_specs=[
            pl.BlockSpec((bm, d), lambda i, j: (jnp.minimum(i, nxa - 1), 0)),
            pl.BlockSpec((bm, d), lambda i, j: (jnp.maximum(i - nxa, 0), 0),
                         pipeline_mode=pl.Buffered(1)),
            mspec(0), mspec(1),
            pl.BlockSpec((d, tn), lambda i, j: (0, j)),
            pl.BlockSpec((d, tn), lambda i, j: (0, nj + j)),
            pl.BlockSpec((d, tn), lambda i, j: (0, 2 * nj + j)),
        ],
        out_specs=[
            pl.BlockSpec((bm, tn), lambda i, j: (i, j)),
            pl.BlockSpec((bm, tn), lambda i, j: (i, j)),
        ],
        out_shape=[jax.ShapeDtypeStruct((n_all, d), BF16),
                   jax.ShapeDtypeStruct((n_all, d), BF16)],
        scratch_shapes=[pltpu.VMEM((bm, d), BF16)],
        compiler_params=_cparams(("arbitrary", "arbitrary")),
        name="conv_in",
    )(x2d, c2d, mod, mod, w_in_b, w_in_b, w_in_b)


def _ml_in_kernel(x_ref, sh_ref, sc_ref, w_ref, wkt_ref, wg_ref, bgate_ref,
                  u_ref, kt_ref, g_ref, h_ref):
    j = pl.program_id(1)

    @pl.when(j == 0)
    def _():
        h = (x_ref[...] * (1.0 + sc_ref[0]) + sh_ref[0]).astype(BF16)
        h_ref[...] = h
        g_ref[...] = jnp.dot(h, wg_ref[...], preferred_element_type=F32) + bgate_ref[...]
        kt_ref[...] = lax.dot_general(wkt_ref[...], h, (((1,), (1,)), ((), ())),
                                      preferred_element_type=F32).astype(BF16)

    u_ref[...] = jnp.dot(h_ref[...], w_ref[...], preferred_element_type=F32).astype(BF16)


def _ml_in(xall, mod, w_voq_b, w_kt_b, w_gate_pad, b_gate_pad, n_batch, seq, n_ctx_rows):
    n_all, d = xall.shape
    nu = w_voq_b.shape[1]
    hq = w_kt_b.shape[0]
    bm = min(PROJ_ROWS, seq, n_ctx_rows)
    assert seq % bm == 0 and n_ctx_rows % bm == 0
    tn = hq
    assert nu % tn == 0
    gl = w_gate_pad.shape[1]
    mspec = functools.partial(_mod_spec, d=d, tiles_per_batch=seq // bm, n_batch=n_batch)
    return pl.pallas_call(
        _ml_in_kernel,
        grid=(n_all // bm, nu // tn),
        in_specs=[
            pl.BlockSpec((bm, d), lambda i, j: (i, 0)),
            mspec(0), mspec(1),
            pl.BlockSpec((d, tn), lambda i, j: (0, j)),
            pl.BlockSpec((hq, d), lambda i, j: (0, 0)),
            pl.BlockSpec((d, gl), lambda i, j: (0, 0)),
            pl.BlockSpec((1, gl), lambda i, j: (0, 0)),
        ],
        out_specs=[
            pl.BlockSpec((bm, tn), lambda i, j: (i, j)),
            pl.BlockSpec((hq, bm), lambda i, j: (0, i)),
            pl.BlockSpec((bm, gl), lambda i, j: (i, 0)),
        ],
        out_shape=[jax.ShapeDtypeStruct((n_all, nu), BF16),
                   jax.ShapeDtypeStruct((hq, n_all), BF16),
                   jax.ShapeDtypeStruct((n_all, gl), F32)],
        scratch_shapes=[pltpu.VMEM((bm, d), BF16)],
        compiler_params=_cparams(("arbitrary", "arbitrary")),
        name="mlstm_in",
    )(xall, mod, mod, w_voq_b, w_kt_b, w_gate_pad, b_gate_pad)


def _dot_split3(a_b, x):
    hi = x.astype(BF16)
    r1 = x - hi.astype(F32)
    mid = r1.astype(BF16)
    lo = (r1 - mid.astype(F32)).astype(BF16)
    return (jnp.dot(a_b, hi, preferred_element_type=F32)
            + jnp.dot(a_b, mid, preferred_element_type=F32)
            + jnp.dot(a_b, lo, preferred_element_type=F32))


def _scan_kernel(n_heads, dqk, dv, q_ref, kt_ref, v_ref, g_ref, o_ref, ct_ref, m_ref):
    d = pl.program_id(1)
    s = pl.program_id(2)
    L = q_ref.shape[0]
    assert L == LANES
    scale = dqk ** -0.5

    @pl.when(s == 0)
    def _():
        ct_ref[...] = jnp.zeros_like(ct_ref)
        m_ref[...] = jnp.zeros_like(m_ref)

    H = n_heads
    heads = range(H)
    qi = lax.broadcasted_iota(I32, (L, L), 0)
    si = lax.broadcasted_iota(I32, (L, L), 1)
    fwd = d == 0
    mask = jnp.where(fwd, si - qi, qi - si) <= 0

    g = g_ref[...]
    b_all = _dot_split3(mask.astype(BF16), jax.nn.log_sigmoid(g))
    g_t = g.T
    b_t = b_all.T
    def lane_bcast(x, lane0):
        return jnp.stack([jnp.broadcast_to(x[:, lane0 + h:lane0 + h + 1], (L, LANES))
                          for h in heads])

    b_b = lane_bcast(b_all, H)
    i_b = lane_bcast(g, 0)
    b_end = jnp.where(fwd, b_b[:, L - 1:L, :], b_b[:, 0:1, :])
    m_st = m_ref[:, 0:1, :]
    tile = lambda x, n: jnp.concatenate([x] * (n // LANES), axis=-1)

    q3 = jnp.stack([q_ref[:, h * dqk:(h + 1) * dqk] for h in heads])
    kt3 = jnp.stack([kt_ref[h * dqk:(h + 1) * dqk, :] for h in heads])
    v3 = jnp.stack([v_ref[:, h * dv:(h + 1) * dv] for h in heads])

    r3 = jnp.stack([g_t[h:h + 1, :] - b_t[H + h:H + h + 1, :] for h in heads])
    dm = jnp.where(mask, b_b + r3, -jnp.inf)
    a_inter = b_b + m_st
    m_q = jnp.maximum(a_inter, jnp.max(dm, axis=-1, keepdims=True))
    inter = jnp.exp(a_inter - m_q) * scale
    sc = jnp.einsum("hqd,hds->hqs", q3, kt3, preferred_element_type=F32)
    p = jnp.exp(dm - m_q) * (sc * scale)
    ct = ct_ref[...]
    qc = jnp.einsum("hqd,hdv->hqv", q3, ct.astype(BF16), preferred_element_type=F32)
    v_ext = jnp.concatenate([v3, jnp.ones((H, L, LANES), BF16)], axis=-1)
    pv = jnp.einsum("hqs,hsv->hqv", p.astype(BF16), v_ext, preferred_element_type=F32)
    den = pv[:, :, dv:] + inter * qc[:, :, dv:]
    rden = 1.0 / jnp.maximum(jnp.abs(den), jnp.exp(-m_q))
    hout = (pv[:, :, :dv] + tile(inter, dv) * qc[:, :, :dv]) * tile(rden, dv)
    for h in heads:
        o_ref[:, h * dv:(h + 1) * dv] = hout[h]

    wl = b_end - b_b + i_b
    m_next = jnp.maximum(b_end + m_st, jnp.max(wl, axis=1, keepdims=True))
    decay = jnp.exp(b_end + m_st - m_next)
    w_b = jnp.exp(wl - m_next)
    vw = jnp.concatenate([v3.astype(F32) * tile(w_b, dv), w_b], axis=-1).astype(BF16)
    upd = jnp.einsum("hdl,hlv->hdv", kt3, vw, preferred_element_type=F32)
    ct_ref[...] = tile(decay, dv + LANES) * ct + upd
    m_ref[...] = jnp.broadcast_to(m_next, m_ref.shape)


def _scan(u, kt, gates, n_batch, seq, ctx_len, n_heads, dqk, dv):
    L = SCAN_CHUNK
    nx = n_batch * seq
    ncc = ctx_len // L
    nlc = seq // L
    assert ctx_len % L == 0 and seq % L == 0
    hq = n_heads * dqk
    hv = n_heads * dv
    assert hv == 2 * hq

    def row_blk(b, d, s):
        ctx = (nx + b * ctx_len) // L + jnp.where(d == 0, s, ncc - 1 - s)
        sl = s - ncc
        lat = (b * seq) // L + jnp.where(d == 0, sl, nlc - 1 - sl)
        return jnp.where(s < ncc, ctx, lat)

    def out_blk(b, d, s):
        sl = jnp.maximum(s - ncc, 0)
        return (b * seq) // L + jnp.where(d == 0, sl, nlc - 1 - sl)

    return pl.pallas_call(
        functools.partial(_scan_kernel, n_heads, dqk, dv),
        grid=(n_batch, 2, ncc + nlc),
        in_specs=[
            pl.BlockSpec((L, hq), lambda b, d, s: (row_blk(b, d, s), 2 * hv // hq)),
            pl.BlockSpec((hq, L), lambda b, d, s: (0, row_blk(b, d, s))),
            pl.BlockSpec((L, hv), lambda b, d, s: (row_blk(b, d, s), 0)),
            pl.BlockSpec((L, LANES), lambda b, d, s: (row_blk(b, d, s), d)),
        ],
        out_specs=pl.BlockSpec((None, L, hv), lambda b, d, s: (d, out_blk(b, d, s), 0)),
        out_shape=jax.ShapeDtypeStruct((2, nx, hv), F32),
        scratch_shapes=[pltpu.VMEM((n_heads, dqk, dv + LANES), F32),
                        pltpu.VMEM((n_heads, SUBLANES, LANES), F32)],
        compiler_params=_cparams(("arbitrary", "arbitrary", "arbitrary")),
        name="mlstm_scan",
    )(u, kt, u, gates)


def _mix_epilogue(alpha, rs, a, x_ref, g1_ref, sh2_ref, sc2_ref, wout_ref, lng_ref,
                  lnb_ref, rwt_ref, x1_ref, hx2_ref, logt_ref):
    mx = jnp.dot(a, wout_ref[...], preferred_element_type=F32)
    x1 = _layer_norm_rows(alpha * x_ref[rs, :] + g1_ref[0] * mx, lng_ref[...], lnb_ref[...])
    x1_ref[rs, :] = x1
    hx2 = x1 * (1.0 + sc2_ref[0]) + sh2_ref[0]
    hx2_ref[rs, :] = hx2
    logt_ref[:, rs] = lax.dot_general(rwt_ref[...], hx2.astype(BF16), (((1,), (1,)), ((), ())),
                                      preferred_element_type=F32)


def _sub_rows(tm, sub):
    return [slice(r, r + sub) for r in range(0, tm, sub)]


def _conv_out_kernel(alpha, ctx_mode, n_lat, tpb, sub, z_ref, zp_ref, zn_ref, bg_ref, cw_ref,
                     x_ref, g1_ref, sh2_ref, sc2_ref, wout_ref, lng_ref, lnb_ref, rwt_ref, *rest):
    x1_ref, hx2_ref, logt_ref = rest[-3:]
    i = pl.program_id(0)
    tm, d = z_ref.shape
    half = d // 2
    rows = lax.broadcasted_iota(I32, (sub, 1), 0)

    def conv1(zz, w3, period):
        pos = rows & (period - 1)
        prev = pltpu.roll(zz, 1, 0) * (pos != 0).astype(F32)
        nxt = pltpu.roll(zz, sub - 1, 0) * (pos != period - 1).astype(F32)
        return w3[0:1] * prev + w3[1:2] * zz + w3[2:3] * nxt

    def tile():
        cw = cw_ref[...]
        if not ctx_mode:
            ti = i % tpb
            up = zp_ref[...].astype(F32) * (ti > 0).astype(F32)
            dn = zn_ref[...].astype(F32) * (ti < tpb - 1).astype(F32)
            w3 = cw[:, half:]

        for rs in _sub_rows(tm, sub):
            z = z_ref[rs, :].astype(F32)
            bg = bg_ref[rs, :].astype(F32)
            if ctx_mode:
                a = (bg * conv1(z, cw, sub)).astype(BF16)
            else:
                a_row = (bg[:, :half] * conv1(z[:, :half], cw[:, :half], GRID_W)).astype(BF16)
                r0, r1 = rs.start, rs.stop
                f32_rows = lambda a, b: z_ref[a:b, half:].astype(F32)
                prev = (jnp.concatenate([up, f32_rows(0, r1 - GRID_W)], axis=0) if r0 == 0
                        else f32_rows(r0 - GRID_W, r1 - GRID_W))
                nxt = (jnp.concatenate([f32_rows(r0 + GRID_W, tm), dn], axis=0) if r1 == tm
                       else f32_rows(r0 + GRID_W, r1 + GRID_W))
                y = w3[0:1] * prev + w3[1:2] * z[:, half:] + w3[2:3] * nxt
                a = jnp.concatenate([a_row, (bg[:, half:] * y).astype(BF16)], axis=1)
            _mix_epilogue(alpha, rs, a, x_ref, g1_ref, sh2_ref, sc2_ref, wout_ref, lng_ref,
                          lnb_ref, rwt_ref, x1_ref, hx2_ref, logt_ref)

    if ctx_mode:
        tile()
    else:
        pl.when(i < n_lat)(tile)

        @pl.when(i >= n_lat)
        def _():
            x1_ref[...] = jnp.zeros_like(x1_ref)
            hx2_ref[...] = jnp.zeros_like(hx2_ref)
            logt_ref[...] = jnp.zeros_like(logt_ref)


def _ml_out_kernel(alpha, n_heads, dv, sub, hf_ref, hb_ref, og_ref, ng_ref, x_ref,
                   g1_ref, sh2_ref, sc2_ref, wout_ref, lng_ref, lnb_ref, rwt_ref,
                   x1_ref, hx2_ref, logt_ref):
    for rs in _sub_rows(hf_ref.shape[0], sub):
        parts = []
        for h in range(n_heads):
            sl = slice(h * dv, (h + 1) * dv)
            hs = hf_ref[rs, sl] + hb_ref[rs, sl]
            mu = jnp.mean(hs, axis=-1, keepdims=True)
            c = hs - mu
            var = jnp.mean(c * c, axis=-1, keepdims=True)
            hn = c * lax.rsqrt(var + HEAD_NORM_EPS)
            gate = jax.nn.sigmoid(og_ref[rs, sl].astype(F32))
            parts.append((hn * ng_ref[:, sl] * gate).astype(BF16))
        _mix_epilogue(alpha, rs, jnp.concatenate(parts, axis=1), x_ref, g1_ref, sh2_ref,
                      sc2_ref, wout_ref, lng_ref, lnb_ref, rwt_ref, x1_ref, hx2_ref, logt_ref)


def _mix_out_common(d, n_rows, n_exp, mod_row, tile0=0):
    tm = ROW_TILE
    mspec = lambda chunk: pl.BlockSpec((1, 1, d), lambda i: (mod_row(i), 0, chunk))
    const = lambda shape: pl.BlockSpec(shape, lambda i: (0,) * len(shape))
    in_specs = [mspec(2), mspec(3), mspec(4), const((d, d)), const((1, d)), const((1, d)),
                const((n_exp, d))]
    out_specs = [pl.BlockSpec((tm, d), lambda i: (i + tile0, 0)),
                 pl.BlockSpec((tm, d), lambda i: (i + tile0, 0)),
                 pl.BlockSpec((n_exp, tm), lambda i: (0, i + tile0))]
    out_shape = [jax.ShapeDtypeStruct((n_rows, d), F32),
                 jax.ShapeDtypeStruct((n_rows, d), F32),
                 jax.ShapeDtypeStruct((n_exp, n_rows), F32)]
    return in_specs, out_specs, out_shape


def _conv_out(alpha, z, bg, conv_w, x2d, c2d, mod, w_out_b, ln_g, ln_b, rwt_b, n_batch, seq,
              ctx_len):
    n_all, d = z.shape
    nx = x2d.shape[0]
    tm = ROW_TILE
    assert ctx_len == tm and seq % tm == 0 and MIX_SUB_ROWS % GRID_W == 0
    half = d // 2
    nxa = nx // tm
    tpb = seq // tm
    hpt = tm // GRID_W
    n_exp = rwt_b.shape[0]

    def z_specs(tile0):
        return [
            pl.BlockSpec((tm, d), lambda i: (i + tile0, 0)),
            pl.BlockSpec((GRID_W, half), lambda i: (jnp.maximum((i + tile0) * hpt - 1, 0), 1)),
            pl.BlockSpec((GRID_W, half), lambda i: ((i + tile0 + 1) * hpt - 1, 1)),
            pl.BlockSpec((tm, d), lambda i: (i + tile0, 0)),
            pl.BlockSpec((3, d), lambda i: (0, 0)),
            pl.BlockSpec((tm, d), lambda i: (i, 0)),
        ]

    lat = lambda i: jnp.minimum(i, nxa - 1)
    common_in, out_specs, out_shape = _mix_out_common(d, n_all, n_exp, lambda i: lat(i) // tpb)
    lat_specs = z_specs(0)
    lat_specs[2] = pl.BlockSpec((GRID_W, half), lambda i: ((lat(i) + 1) * hpt, 1))
    lat_specs[5] = pl.BlockSpec((tm, d), lambda i: (lat(i), 0))
    outs = pl.pallas_call(
        functools.partial(_conv_out_kernel, alpha, False, nxa, tpb, MIX_SUB_ROWS),
        grid=(n_all // tm,),
        in_specs=lat_specs + common_in,
        out_specs=out_specs,
        out_shape=out_shape,
        compiler_params=_cparams(("arbitrary",)),
        name="conv_out",
    )(z, z, z, bg, conv_w, x2d, mod, mod, mod, w_out_b, ln_g, ln_b, rwt_b)

    common_in, out_specs, out_shape = _mix_out_common(d, n_all, n_exp, lambda i: n_batch, nxa)
    n_in = 6 + len(common_in)
    keep = [pl.BlockSpec(memory_space=pl.ANY)] * 3
    return pl.pallas_call(
        functools.partial(_conv_out_kernel, alpha, True, None, tpb, tm),
        grid=(c2d.shape[0] // tm,),
        in_specs=z_specs(nxa) + common_in + keep,
        out_specs=out_specs,
        out_shape=out_shape,
        input_output_aliases={n_in: 0, n_in + 1: 1, n_in + 2: 2},
        compiler_params=_cparams(("arbitrary",)),
        name="conv_out_ctx",
    )(z, z, z, bg, conv_w, c2d, mod, mod, mod, w_out_b, ln_g, ln_b, rwt_b, *outs)


def _ml_out(alpha, hfb, u, norm_g, xall, mod, w_out_b, ln_g, ln_b, rwt_b, n_batch, seq,
            n_heads, dv):
    nx, d = hfb.shape[1:]
    tm = ROW_TILE
    tpb = seq // tm
    common_in, out_specs, out_shape = _mix_out_common(d, nx, rwt_b.shape[0], lambda i: i // tpb)
    o_blk = 1
    in_specs = [
        pl.BlockSpec((None, tm, d), lambda i: (0, i, 0)),
        pl.BlockSpec((None, tm, d), lambda i: (1, i, 0)),
        pl.BlockSpec((tm, d), lambda i: (i, o_blk)),
        pl.BlockSpec((1, d), lambda i: (0, 0)),
        pl.BlockSpec((tm, d), lambda i: (i, 0)),
    ] + common_in
    return pl.pallas_call(
        functools.partial(_ml_out_kernel, alpha, n_heads, dv, tm),
        grid=(nx // tm,),
        in_specs=in_specs,
        out_specs=out_specs,
        out_shape=out_shape,
        compiler_params=_cparams(("arbitrary",)),
        name="mlstm_out",
    )(hfb, hfb, u, norm_g, xall, mod, mod, mod, w_out_b, ln_g, ln_b, rwt_b)


def _route_kernel(tile_rows, n_te, logt_ref, rb_ref, pos_ref, wcol_ref, te_ref, meta_ref,
                  carry_ref, before_ref):
    ph = pl.program_id(0)
    i = pl.program_id(1)
    n_steps = pl.num_programs(1)
    n_exp, tr = logt_ref.shape
    epg = n_exp // N_GROUPS

    @pl.when((ph == 0) & (i == 0))
    def _():
        carry_ref[...] = jnp.zeros_like(carry_ref)
        ti = lax.broadcasted_iota(I32, (tr, tr), 0)
        tj = lax.broadcasted_iota(I32, (tr, tr), 1)
        before_ref[...] = (ti < tj).astype(BF16)

    s = jax.nn.sigmoid(logt_ref[...])
    sel = s + rb_ref[...]
    row = lax.broadcasted_iota(I32, (n_exp, tr), 0)
    member = row % epg
    group = row // epg

    def partner(x, k, idx, span, unit):
        wrapped = (idx + k) >= span
        up = pltpu.roll(x, n_exp - k * unit, 0)
        down = pltpu.roll(x, (span - k) * unit, 0)
        return jnp.where(wrapped, down, up), wrapped

    rank_in = jnp.zeros((n_exp, tr), F32)
    for k in range(1, epg):
        p, wrapped = partner(sel, k, member, epg, 1)
        beats = (p > sel) | ((p == sel) & wrapped)
        rank_in = rank_in + beats.astype(F32)
    top = (rank_in < TOP_K).astype(F32)
    gs = sel * top
    score = gs
    for k in range(1, epg):
        p, _ = partner(gs, k, member, epg, 1)
        score = score + p
    n_better = jnp.zeros((n_exp, tr), F32)
    for k in range(1, N_GROUPS):
        p, wrapped = partner(score, k, group, N_GROUPS, epg)
        beats = (p > score) | ((p == score) & wrapped)
        n_better = n_better + beats.astype(F32)
    best = n_better == 0.0
    m1 = (best & (rank_in == 0.0)).astype(F32)
    m2 = (best & (rank_in == 1.0)).astype(F32)
    s1 = jnp.sum(s * m1, axis=0, keepdims=True)
    s2 = jnp.sum(s * m2, axis=0, keepdims=True)
    den = s1 + s2
    oh = m1 + m2

    slot = jnp.dot(oh.astype(BF16), before_ref[...], preferred_element_type=F32) + carry_ref[...]

    @pl.when(ph == 1)
    def _():
        pos_ref[0:1, :] = jnp.sum(m1 * slot, axis=0, keepdims=True).astype(I32)
        pos_ref[1:2, :] = jnp.sum(m2 * slot, axis=0, keepdims=True).astype(I32)
        w8 = jnp.concatenate([s1 / den, s2 / den, jnp.zeros((LANES - 2, tr), F32)], axis=0)
        wcol_ref[...] = w8.T

    carry_ref[...] = carry_ref[...] + jnp.sum(oh, axis=1, keepdims=True)

    @pl.when((ph == 0) & (i == n_steps - 1))
    def _():
        cnt = carry_ref[...]
        ntile = jnp.floor((cnt + (tile_rows - 1)) * (1.0 / tile_rows))
        er = lax.broadcasted_iota(I32, (n_exp, LANES), 0)
        el = lax.broadcasted_iota(I32, (n_exp, LANES), 1)
        eye = (er == el).astype(F32)
        nt_row = jnp.sum(ntile * eye, axis=0, keepdims=True)
        cnt_row = jnp.sum(cnt * eye, axis=0, keepdims=True)
        cum_excl = jnp.sum(nt_row * (el < er).astype(F32), axis=1, keepdims=True)
        off = cum_excl * tile_rows
        off_row = jnp.sum(off * eye, axis=0, keepdims=True)
        total = jnp.sum(nt_row, axis=1, keepdims=True)
        cum_incl = cum_excl + ntile
        tl = lax.broadcasted_iota(I32, (n_exp, n_te), 1).astype(F32)
        te = jnp.sum((cum_incl <= tl).astype(F32), axis=0, keepdims=True)
        elf = el.astype(F32)
        later = (el > er) & (nt_row > 0.0) & (el < n_exp)
        nxt = jnp.min(jnp.where(later, elf, float(n_exp)), axis=1, keepdims=True)
        nxt = jnp.where(nxt == float(n_exp), er[:, 0:1].astype(F32), nxt)
        mine = (cum_excl <= tl) & (tl < cum_incl)
        te_next = jnp.sum(jnp.where(mine, nxt, 0.0), axis=0, keepdims=True)
        te_ref[...] = jnp.concatenate([jnp.minimum(te, n_exp - 1), te_next], axis=0).astype(I32)
        meta_ref[...] = jnp.concatenate(
            [cnt_row, off_row, jnp.broadcast_to(total, (1, LANES)),
             jnp.zeros((SUBLANES - 3, LANES), F32)], axis=0).astype(I32)
        carry_ref[...] = off


def _route(logt, router_b, tile_rows):
    n_exp, n = logt.shape
    tr = ROUTE_COLS
    while n % tr:
        tr //= 2
    n_te = 256
    assert (2 * n) // tile_rows + n_exp <= n_te
    return pl.pallas_call(
        functools.partial(_route_kernel, tile_rows, n_te),
        grid=(2, n // tr),
        in_specs=[pl.BlockSpec((n_exp, tr), lambda p, i: (0, i)),
                  pl.BlockSpec((n_exp, 1), lambda p, i: (0, 0))],
        out_specs=[pl.BlockSpec((2, tr), lambda p, i: (0, i * p)),
                   pl.BlockSpec((tr, LANES), lambda p, i: (i * p, 0)),
                   pl.BlockSpec((2, n_te), lambda p, i: (0, 0)),
                   pl.BlockSpec((SUBLANES, LANES), lambda p, i: (0, 0))],
        out_shape=[jax.ShapeDtypeStruct((2, n), I32),
                   jax.ShapeDtypeStruct((n, LANES), F32),
                   jax.ShapeDtypeStruct((2, n_te), I32),
                   jax.ShapeDtypeStruct((SUBLANES, LANES), I32)],
        scratch_shapes=[pltpu.VMEM((n_exp, 1), F32), pltpu.VMEM((tr, tr), BF16)],
        compiler_params=_cparams(("arbitrary", "arbitrary")),
        name="route",
    )(logt, router_b.reshape(n_exp, 1).astype(F32))


def _tbl_kernel(n_tok, n_exp, tile_rows, meta_ref, pos_ref, tbl_ref):
    i = pl.program_id(0)
    blk = pos_ref.shape[1]

    @pl.when(i == 0)
    def _():
        n_pad = 0
        for e in range(n_exp):
            cnt = meta_ref[e]
            off = meta_ref[n_exp + e]
            up = ((cnt + (tile_rows - 1)) // tile_rows) * tile_rows

            def fill(r, j):
                tbl_ref[off + r] = (2 * n_tok + j) << SLOT_SRC_BITS
                return j + 1

            n_pad = lax.fori_loop(cnt, up, fill, n_pad)

        def fill_tail(r, carry):
            tbl_ref[r] = (2 * n_tok) << SLOT_SRC_BITS
            return carry

        lax.fori_loop(2 * n_tok + n_pad, tbl_ref.shape[0], fill_tail, 0)

    base = i * blk
    first = (base << SLOT_SRC_BITS) | jnp.where(base >= n_tok, base - n_tok, base)
    step = (1 << SLOT_SRC_BITS) + 1
    group = 8

    def body(q, carry):
        r0 = q * group
        slots = [pos_ref[0, r0 + j] for j in range(group)]
        for j in range(group):
            tbl_ref[slots[j]] = first + (r0 + j) * step
        return carry

    lax.fori_loop(0, blk // group, body, 0, unroll=True)


def _build_table(meta1d, pos, n_tok, n_exp, tile_rows, p_pad):
    blk = TBL_BLOCK
    while n_tok % blk:
        blk //= 2
    n_pairs = 2 * n_tok
    grid_spec = pltpu.PrefetchScalarGridSpec(
        num_scalar_prefetch=1,
        grid=(n_pairs // blk,),
        in_specs=[pl.BlockSpec((None, 1, blk), lambda i, m: (i, 0, 0), memory_space=pltpu.SMEM)],
        out_specs=pl.BlockSpec(memory_space=pltpu.SMEM),
    )
    return pl.pallas_call(
        functools.partial(_tbl_kernel, n_tok, n_exp, tile_rows),
        grid_spec=grid_spec,
        out_shape=jax.ShapeDtypeStruct((p_pad,), I32),
        compiler_params=_cparams(("arbitrary",)),
        name="slot_table",
    )(meta1d, pos.reshape(n_pairs // blk, 1, blk))


def _moe_kernel(layer, te_ref, ten_ref, nt_ref, tbl_ref, tbln_ref, tblp_ref, hx_hbm,
                wg_hbm, wu_hbm, wd_hbm, y_hbm, xbuf, ybuf, zbuf, wg_b, wu_b, wd_b,
                wg_s, wu_s, wd_s, gsem, ssem, zsem, wsem):
    t = pl.program_id(0)
    n = nt_ref[0]
    rows = tbl_ref.shape[1]
    lt = hx_hbm.shape[1]
    staged = ((wg_hbm, wg_s, wg_b), (wu_hbm, wu_s, wu_b), (wd_hbm, wd_s, wd_b))

    def weight_copies(e):
        return [pltpu.make_async_copy(hbm.at[layer, e], stage, wsem.at[k])
                for k, (hbm, stage, _) in enumerate(staged)]

    @pl.when(t == 0)
    def _():
        for c in weight_copies(te_ref[0]):
            c.start(priority=WEIGHT_DMA_PRIORITY)

    tl = jnp.minimum(t, n - 1)
    new_expert = (t < n) & ((t == 0) | (te_ref[tl] != te_ref[jnp.maximum(tl - 1, 0)]))

    @pl.when(new_expert)
    def _():
        for c, (_, stage, dst) in zip(weight_copies(te_ref[tl]), staged):
            c.wait()
            dst[...] = stage[...].astype(BF16)
        for c in weight_copies(ten_ref[tl]):
            c.start(priority=WEIGHT_DMA_PRIORITY)

    def start_gather(tref, sl, lo=0, hi=None):
        for r in range(lo, rows if hi is None else hi):
            src = tref[0, r] & SLOT_SRC_MASK
            pltpu.make_async_copy(hx_hbm.at[src], xbuf.at[sl, pl.ds(r * lt, lt), :],
                                  gsem.at[sl]).start()

    def start_scatter(tref, sl, lo=0, hi=None):
        for r in range(lo, rows if hi is None else hi):
            dst = tref[0, r] >> SLOT_SRC_BITS
            pltpu.make_async_copy(ybuf.at[sl, pl.ds(r, 1), :],
                                  y_hbm.at[pl.ds(dst, 1), :], ssem.at[sl]).start()

    def wait_gather(sl):
        pltpu.make_async_copy(xbuf.at[sl], xbuf.at[sl], gsem.at[sl]).wait()

    def wait_scatter(sl):
        pltpu.make_async_copy(ybuf.at[sl], ybuf.at[sl], ssem.at[sl]).wait()

    @pl.when(t == 0)
    def _():
        zbuf[...] = jnp.zeros_like(zbuf)
        start_gather(tbl_ref, 0)

    @pl.when(t >= n)
    def _():
        dst = y_hbm.at[pl.ds(pl.multiple_of(t * rows, rows), rows), :]
        fill = pltpu.make_async_copy(zbuf, dst, zsem)
        fill.start()
        fill.wait()

    def tile(slot, has_prev):
        other = 1 - slot
        wait_gather(slot)

        @pl.when(t >= 2)
        def _():
            wait_scatter(slot)

        f = wg_b.shape[1]
        d = wd_b.shape[1]
        fc, dc = MOE_UP_COLS, MOE_DOWN_COLS
        groups = f // fc + d // dc
        bounds = [(k * rows) // groups for k in range(groups + 1)]

        def copy_group(k):
            start_gather(tbln_ref, other, bounds[k], bounds[k + 1])
            if has_prev:
                start_scatter(tblp_ref, other, bounds[k], bounds[k + 1])

        xb = jnp.concatenate(
            [xbuf[slot, pl.ds(c, rows, stride=lt), :].astype(BF16) for c in range(lt)], axis=1)
        hs = []
        for c in range(f // fc):
            cols = slice(c * fc, (c + 1) * fc)
            g = jnp.dot(xb, wg_b[:, cols], preferred_element_type=F32)
            u = jnp.dot(xb, wu_b[:, cols], preferred_element_type=F32)
            hs.append((g * jax.nn.sigmoid(g) * u).astype(BF16))
            copy_group(c)
        h = jnp.concatenate(hs, axis=1)
        for c in range(d // dc):
            cols = slice(c * dc, (c + 1) * dc)
            ybuf[slot, :, cols] = jnp.dot(h, wd_b[:, cols], preferred_element_type=F32)
            copy_group(f // fc + c)

        @pl.when(t == n - 1)
        def _():
            start_scatter(tbl_ref, slot)
            wait_gather(other)

            @pl.when(t >= 1)
            def _():
                wait_scatter(other)

            wait_scatter(slot)
            for c in weight_copies(ten_ref[tl]):
                c.wait()

    pl.when((t < n) & (t == 0))(functools.partial(tile, 0, False))
    for parity in (0, 1):
        pl.when((t < n) & (t > 0) & (lax.rem(t, 2) == parity))(
            functools.partial(tile, parity, True))


def _moe(te2, nt1d, tbl, hx2, w_gate, w_up, w_down, layer, n_tok, p_pad):
    tmm = MOE_ROWS
    _, n_exp, d, f = w_gate.shape
    t_max = (2 * n_tok) // tmm + n_exp
    last = lambda t, nt: jnp.minimum(t, nt[0] - 1)
    tspec = lambda fn: pl.BlockSpec((None, 1, tmm), lambda t, te, ten, nt: (fn(t, nt), 0, 0),
                                    memory_space=pltpu.SMEM)
    hbm = pl.BlockSpec(memory_space=pl.ANY)
    grid_spec = pltpu.PrefetchScalarGridSpec(
        num_scalar_prefetch=3,
        grid=(t_max,),
        in_specs=[
            tspec(last),
            tspec(lambda t, nt: last(t + 1, nt)),
            tspec(lambda t, nt: jnp.maximum(last(t, nt) - 1, 0)),
            hbm, hbm, hbm, hbm,
        ],
        out_specs=hbm,
        scratch_shapes=[pltpu.VMEM((2, tmm * (d // LANES), LANES), F32),
                        pltpu.VMEM((2, tmm, d), F32),
                        pltpu.VMEM((tmm, d), F32),
                        pltpu.VMEM((d, f), BF16), pltpu.VMEM((d, f), BF16),
                        pltpu.VMEM((f, d), BF16),
                        pltpu.VMEM((d, f), F32), pltpu.VMEM((d, f), F32),
                        pltpu.VMEM((f, d), F32),
                        pltpu.SemaphoreType.DMA((2,)), pltpu.SemaphoreType.DMA((2,)),
                        pltpu.SemaphoreType.DMA(()), pltpu.SemaphoreType.DMA((3,))],
    )
    tbl3 = tbl.reshape(p_pad // tmm, 1, tmm)
    return pl.pallas_call(
        functools.partial(_moe_kernel, layer),
        grid_spec=grid_spec,
        out_shape=jax.ShapeDtypeStruct((t_max * tmm, d), F32),
        compiler_params=_cparams(("arbitrary",)),
        name="moe_experts",
    )(te2[0], te2[1], nt1d, tbl3, tbl3, tbl3, hx2.reshape(n_tok, d // LANES, LANES),
      w_gate, w_up, w_down)


def _comb_kernel(alpha, x_ref, y0_ref, y1_ref, w_ref, g2_ref, lng_ref, lnb_ref, o_ref):
    w = w_ref[...]
    ex = w[:, 0:1] * y0_ref[...] + w[:, 1:2] * y1_ref[...]
    o_ref[...] = _layer_norm_rows(alpha * x_ref[...] + g2_ref[0] * ex, lng_ref[...], lnb_ref[...])


def _combine(alpha, x1, y, wcol, mod, ln_g, ln_b, n_tok, n_out, n_batch, seq):
    d = x1.shape[1]
    tm = COMB_ROWS
    assert n_tok % tm == 0 and n_out % tm == 0 and seq % tm == 0
    nblk = n_tok // tm
    mspec = functools.partial(_mod_spec, d=d, tiles_per_batch=seq // tm, n_batch=n_batch)
    return pl.pallas_call(
        functools.partial(_comb_kernel, alpha),
        grid=(n_out // tm,),
        in_specs=[
            pl.BlockSpec((tm, d), lambda i: (i, 0)),
            pl.BlockSpec((tm, d), lambda i: (i, 0)),
            pl.BlockSpec((tm, d), lambda i: (i + nblk, 0)),
            pl.BlockSpec((tm, LANES), lambda i: (i, 0)),
            mspec(5),
            pl.BlockSpec((1, d), lambda i: (0, 0)),
            pl.BlockSpec((1, d), lambda i: (0, 0)),
        ],
        out_specs=pl.BlockSpec((tm, d), lambda i: (i, 0)),
        out_shape=jax.ShapeDtypeStruct((n_out, d), F32),
        compiler_params=_cparams(("arbitrary",)),
        name="moe_combine",
    )(x1, y, y, wcol, mod, ln_g, ln_b)


def _moe_layer(alpha, x1, hx2, logt, mod, ln_g, ln_b, router_b, w_gate, w_up, w_down, layer,
               n_out, n_batch, seq):
    n_tok = hx2.shape[0]
    n_exp = w_gate.shape[1]
    tmm = MOE_ROWS
    p_max = 2 * n_tok + n_exp * tmm
    p_pad = -(-p_max // TBL_BLOCK) * TBL_BLOCK
    assert n_tok <= 1 << SLOT_SRC_BITS and p_max < 1 << (31 - SLOT_SRC_BITS)
    pos, wcol, te, meta = _route(logt, router_b, tmm)
    meta1d = meta[:2, :n_exp].reshape(-1)
    tbl = _build_table(meta1d, pos, n_tok, n_exp, tmm, p_pad)
    y = _moe(te, meta[2, :1], tbl, hx2, w_gate, w_up, w_down, layer, n_tok, p_pad)
    return _combine(alpha, x1, y, wcol, mod, ln_g, ln_b, n_tok, n_out, n_batch, seq)


def kernel(x, c, ctx, c_ctx, w_ada, b_ada, ln_g, ln_b, conv_w_in, conv_w, conv_w_out, ml_w_in, ml_w_gate, ml_b_gate, ml_norm_g, ml_w_out, router_w, router_b, exp_w_gate, exp_w_up, exp_w_down):
    n_batch, seq, d = x.shape
    ctx_len = ctx.shape[1]
    depth = w_ada.shape[0]
    assert depth == 2, "layer 0 is the conv mixer, layer 1 the mLSTM mixer"
    alpha = (2 * depth) ** 0.25
    n_heads = ml_b_gate.shape[-1] // 4
    dqk = d // (2 * n_heads)
    dv = d // n_heads
    nx = n_batch * seq
    nc = n_batch * ctx_len
    assert n_batch < SUBLANES and 2 * n_heads <= LANES

    x2d = x.reshape(nx, d)
    c2d = ctx.reshape(nc, d)
    cc = jnp.zeros((SUBLANES, d), F32).at[:n_batch].set(c).at[n_batch].set(c_ctx)
    mod = _ada(cc, w_ada, b_ada).reshape(depth, SUBLANES, 1, 6 * d)
    rwt_b = router_w.T.astype(BF16)

    bg, z = _conv_in(x2d, c2d, mod[0], _to_bf16(conv_w_in, 0), n_batch, seq)
    x1, hx2, logt = _conv_out(alpha, z, bg, conv_w[0], x2d, c2d, mod[0],
                              _to_bf16(conv_w_out, 0), ln_g[0, 0:1], ln_b[0, 0:1], rwt_b,
                              n_batch, seq, ctx_len)
    xall = _moe_layer(alpha, x1, hx2, logt, mod[0], ln_g[0, 1:2], ln_b[0, 1:2], router_b,
                      exp_w_gate, exp_w_up, exp_w_down, 0, nx + nc, n_batch, seq)

    w_gate = ml_w_gate[0]
    b_gate = ml_b_gate[0]
    h2 = 2 * n_heads
    w_gate_pad = (jnp.zeros((d, 2 * LANES), F32).at[:, :h2].set(w_gate[:, :h2])
                  .at[:, LANES:LANES + h2].set(w_gate[:, h2:])).astype(BF16)
    b_gate_pad = (jnp.zeros((1, 2 * LANES), F32).at[0, :h2].set(b_gate[:h2])
                  .at[0, LANES:LANES + h2].set(b_gate[h2:]))
    w_voq_b, w_kt_b = _ml_weights(ml_w_in, 0, n_heads * dqk)
    u, kt, gates = _ml_in(xall, mod[1], w_voq_b, w_kt_b, w_gate_pad, b_gate_pad, n_batch,
                          seq, nc)
    hfb = _scan(u, kt, gates, n_batch, seq, ctx_len, n_heads, dqk, dv)
    x1, hx2, logt = _ml_out(alpha, hfb, u, ml_norm_g[0:1], xall, mod[1],
                            _to_bf16(ml_w_out, 0), ln_g[1, 0:1], ln_b[1, 0:1], rwt_b,
                            n_batch, seq, n_heads, dv)
    out = _moe_layer(alpha, x1, hx2, logt, mod[1], ln_g[1, 1:2], ln_b[1, 1:2], router_b,
                     exp_w_gate, exp_w_up, exp_w_down, 1, nx, n_batch, seq)
    return out.reshape(n_batch, seq, d)
```

```python
import functools

import jax
import jax.numpy as jnp
from jax import lax
from jax.experimental import pallas as pl
from jax.experimental.pallas import tpu as pltpu

F32 = jnp.float32
BF16 = jnp.bfloat16
I32 = jnp.int32

GRID_W = 64
N_GROUPS = 4
TOP_K = 2
LN_EPS = 1e-5
HEAD_NORM_EPS = 1e-6

LANES = 128
SUBLANES = 8
VMEM_LIMIT_BYTES = 56 * 1024 * 1024

ROW_TILE = 256
MIX_SUB_ROWS = 128
PROJ_ROWS = 1024
MOE_ROWS = 256
MOE_UP_COLS = 256
MOE_DOWN_COLS = 512
ROUTE_COLS = 1024
COMB_ROWS = 512
SCAN_CHUNK = 128
TBL_BLOCK = 1024
SLOT_SRC_BITS = 15
SLOT_SRC_MASK = (1 << SLOT_SRC_BITS) - 1
CAST_BLOCK_BYTES = 8 * 1024 * 1024
WEIGHT_DMA_PRIORITY = 1


def _cparams(sem):
    return pltpu.CompilerParams(dimension_semantics=sem, vmem_limit_bytes=VMEM_LIMIT_BYTES)


def _layer_norm_rows(r, g, b):
    mu = jnp.mean(r, axis=-1, keepdims=True)
    c = r - mu
    var = jnp.mean(c * c, axis=-1, keepdims=True)
    return c * lax.rsqrt(var + LN_EPS) * g + b


def _ada_kernel(cc_ref, w_ref, b_ref, o_ref):
    a = cc_ref[...]
    a = (a * jax.nn.sigmoid(a)).astype(BF16)
    o_ref[0] = jnp.dot(a, w_ref[0].astype(BF16), preferred_element_type=F32) + b_ref[0]


def _ada(cc, w_ada, b_ada):
    depth, d, n6 = w_ada.shape
    tn = 1024
    return pl.pallas_call(
        _ada_kernel,
        grid=(depth, n6 // tn),
        in_specs=[
            pl.BlockSpec((SUBLANES, d), lambda l, j: (0, 0)),
            pl.BlockSpec((1, d, tn), lambda l, j: (l, 0, j)),
            pl.BlockSpec((1, 1, tn), lambda l, j: (l, 0, j)),
        ],
        out_specs=pl.BlockSpec((1, SUBLANES, tn), lambda l, j: (l, 0, j)),
        out_shape=jax.ShapeDtypeStruct((depth, SUBLANES, n6), F32),
        compiler_params=_cparams(("arbitrary", "arbitrary")),
        name="ada_mod",
    )(cc, w_ada, b_ada.reshape(depth, 1, n6))


def _cast_kernel(w_ref, o_ref):
    o_ref[...] = w_ref[...].astype(BF16)


def _to_bf16(w, layer):
    rows, cols = w.shape[-2:]
    w4 = w.reshape(w.shape[0], -1, rows, cols)
    m = w4.shape[1]
    rb = rows
    while rb * cols * 4 > CAST_BLOCK_BYTES and rb % (4 * SUBLANES) == 0:
        rb //= 2
    out = pl.pallas_call(
        _cast_kernel,
        grid=(m, rows // rb),
        in_specs=[pl.BlockSpec((None, None, rb, cols), lambda e, r: (layer, e, r, 0))],
        out_specs=pl.BlockSpec((None, rb, cols), lambda e, r: (e, r, 0)),
        out_shape=jax.ShapeDtypeStruct((m, rows, cols), BF16),
        compiler_params=_cparams(("arbitrary", "arbitrary")),
        name="to_bf16",
    )(w4)
    return out.reshape(w.shape[1:])


def _cast_t_kernel(w_ref, o_ref):
    o_ref[...] = w_ref[...].T.astype(BF16)


def _ml_weights(w, layer, hq):
    _, d, ncol = w.shape
    nb = ncol // hq
    rb = min(d, 512)
    voq = pl.pallas_call(
        _cast_kernel,
        grid=(d // rb, nb - 1),
        in_specs=[pl.BlockSpec((None, rb, hq), lambda r, k: (layer, r, (k + 2) % nb))],
        out_specs=pl.BlockSpec((rb, hq), lambda r, k: (r, k)),
        out_shape=jax.ShapeDtypeStruct((d, ncol - hq), BF16),
        compiler_params=_cparams(("arbitrary", "arbitrary")),
        name="to_bf16_voq",
    )(w)
    kt = pl.pallas_call(
        _cast_t_kernel,
        grid=(d // rb,),
        in_specs=[pl.BlockSpec((None, rb, hq), lambda r: (layer, r, 1))],
        out_specs=pl.BlockSpec((hq, rb), lambda r: (0, r)),
        out_shape=jax.ShapeDtypeStruct((hq, d), BF16),
        compiler_params=_cparams(("arbitrary",)),
        name="to_bf16_kt",
    )(w)
    return voq, kt


def _mod_spec(chunk, d, tiles_per_batch, n_batch):
    return pl.BlockSpec(
        (1, 1, d),
        lambda i, *_: (jnp.minimum(i // tiles_per_batch, n_batch), 0, chunk))


def _conv_in_kernel(nxa, xa_ref, xb_ref, sh_ref, sc_ref, wb_ref, wc_ref, wv_ref,
                    bg_ref, z_ref, h_ref):
    i = pl.program_id(0)
    j = pl.program_id(1)

    @pl.when((j == 0) & (i < nxa))
    def _():
        h_ref[...] = (xa_ref[...] * (1.0 + sc_ref[0]) + sh_ref[0]).astype(BF16)

    @pl.when((j == 0) & (i >= nxa))
    def _():
        h_ref[...] = (xb_ref[...] * (1.0 + sc_ref[0]) + sh_ref[0]).astype(BF16)

    h = h_ref[...]
    bg = jnp.dot(h, wb_ref[...], preferred_element_type=F32)
    cg = jnp.dot(h, wc_ref[...], preferred_element_type=F32)
    v = jnp.dot(h, wv_ref[...], preferred_element_type=F32)
    bg_ref[...] = bg.astype(BF16)
    z_ref[...] = (cg * v).astype(BF16)


def _conv_in(x2d, c2d, mod, w_in_b, n_batch, seq):
    nx, d = x2d.shape
    nc = c2d.shape[0]
    bm = min(PROJ_ROWS, seq, nc)
    assert seq % bm == 0 and nc % bm == 0
    tn = 512 if d % 512 == 0 else d
    nj = d // tn
    nxa = nx // bm
    n_all = nx + nc
    tpb = seq // bm
    mspec = functools.partial(_mod_spec, d=d, tiles_per_batch=tpb, n_batch=n_batch)
    return pl.pallas_call(
        functools.partial(_conv_in_kernel, nxa),
        grid=(n_all // bm, nj),
        in_specs=[
            pl.BlockSpec((bm, d), lambda i, j: (jnp.minimum(i, nxa - 1), 0)),
            pl.BlockSpec((bm, d), lambda i, j: (jnp.maximum(i - nxa, 0), 0),
                         pipeline_mode=pl.Buffered(1)),
            mspec(0), mspec(1),
            pl.BlockSpec((d, tn), lambda i, j: (0, j)),
            pl.BlockSpec((d, tn), lambda i, j: (0, nj + j)),
            pl.BlockSpec((d, tn), lambda i, j: (0, 2 * nj + j)),
        ],
        out_specs=[
            pl.BlockSpec((bm, tn), lambda i, j: (i, j)),
            pl.BlockSpec((bm, tn), lambda i, j: (i, j)),
        ],
        out_shape=[jax.ShapeDtypeStruct((n_all, d), BF16),
                   jax.ShapeDtypeStruct((n_all, d), BF16)],
        scratch_shapes=[pltpu.VMEM((bm, d), BF16)],
        compiler_params=_cparams(("arbitrary", "arbitrary")),
        name="conv_in",
    )(x2d, c2d, mod, mod, w_in_b, w_in_b, w_in_b)


def _ml_in_kernel(x_ref, sh_ref, sc_ref, w_ref, wkt_ref, wg_ref, bgate_ref,
                  u_ref, kt_ref, g_ref, h_ref):
    j = pl.program_id(1)

    @pl.when(j == 0)
    def _():
        h = (x_ref[...] * (1.0 + sc_ref[0]) + sh_ref[0]).astype(BF16)
        h_ref[...] = h
        g_ref[...] = jnp.dot(h, wg_ref[...], preferred_element_type=F32) + bgate_ref[...]
        kt_ref[...] = lax.dot_general(wkt_ref[...], h, (((1,), (1,)), ((), ())),
                                      preferred_element_type=F32).astype(BF16)

    u_ref[...] = jnp.dot(h_ref[...], w_ref[...], preferred_element_type=F32).astype(BF16)


def _ml_in(xall, mod, w_voq_b, w_kt_b, w_gate_pad, b_gate_pad, n_batch, seq, n_ctx_rows):
    n_all, d = xall.shape
    nu = w_voq_b.shape[1]
    hq = w_kt_b.shape[0]
    bm = min(PROJ_ROWS, seq, n_ctx_rows)
    assert seq % bm == 0 and n_ctx_rows % bm == 0
    tn = hq
    assert nu % tn == 0
    gl = w_gate_pad.shape[1]
    mspec = functools.partial(_mod_spec, d=d, tiles_per_batch=seq // bm, n_batch=n_batch)
    return pl.pallas_call(
        _ml_in_kernel,
        grid=(n_all // bm, nu // tn),
        in_specs=[
            pl.BlockSpec((bm, d), lambda i, j: (i, 0)),
            mspec(0), mspec(1),
            pl.BlockSpec((d, tn), lambda i, j: (0, j)),
            pl.BlockSpec((hq, d), lambda i, j: (0, 0)),
            pl.BlockSpec((d, gl), lambda i, j: (0, 0)),
            pl.BlockSpec((1, gl), lambda i, j: (0, 0)),
        ],
        out_specs=[
            pl.BlockSpec((bm, tn), lambda i, j: (i, j)),
            pl.BlockSpec((hq, bm), lambda i, j: (0, i)),
            pl.BlockSpec((bm, gl), lambda i, j: (i, 0)),
        ],
        out_shape=[jax.ShapeDtypeStruct((n_all, nu), BF16),
                   jax.ShapeDtypeStruct((hq, n_all), BF16),
                   jax.ShapeDtypeStruct((n_all, gl), F32)],
        scratch_shapes=[pltpu.VMEM((bm, d), BF16)],
        compiler_params=_cparams(("arbitrary", "arbitrary")),
        name="mlstm_in",
    )(xall, mod, mod, w_voq_b, w_kt_b, w_gate_pad, b_gate_pad)


def _dot_split3(a_b, x):
    hi = x.astype(BF16)
    r1 = x - hi.astype(F32)
    mid = r1.astype(BF16)
    lo = (r1 - mid.astype(F32)).astype(BF16)
    return (jnp.dot(a_b, hi, preferred_element_type=F32)
            + jnp.dot(a_b, mid, preferred_element_type=F32)
            + jnp.dot(a_b, lo, preferred_element_type=F32))


def _scan_kernel(n_heads, dqk, dv, q_ref, kt_ref, v_ref, g_ref, o_ref, ct_ref, m_ref):
    d = pl.program_id(1)
    s = pl.program_id(2)
    L = q_ref.shape[0]
    assert L == LANES
    scale = dqk ** -0.5

    @pl.when(s == 0)
    def _():
        ct_ref[...] = jnp.zeros_like(ct_ref)
        m_ref[...] = jnp.zeros_like(m_ref)

    H = n_heads
    heads = range(H)
    qi = lax.broadcasted_iota(I32, (L, L), 0)
    si = lax.broadcasted_iota(I32, (L, L), 1)
    fwd = d == 0
    mask = jnp.where(fwd, si - qi, qi - si) <= 0

    g = g_ref[...]
    b_all = _dot_split3(mask.astype(BF16), jax.nn.log_sigmoid(g))
    g_t = g.T
    b_t = b_all.T
    def lane_bcast(x, lane0):
        return jnp.stack([jnp.broadcast_to(x[:, lane0 + h:lane0 + h + 1], (L, LANES))
                          for h in heads])

    b_b = lane_bcast(b_all, H)
    i_b = lane_bcast(g, 0)
    b_end = jnp.where(fwd, b_b[:, L - 1:L, :], b_b[:, 0:1, :])
    m_st = m_ref[:, 0:1, :]
    tile = lambda x, n: jnp.concatenate([x] * (n // LANES), axis=-1)

    q3 = jnp.stack([q_ref[:, h * dqk:(h + 1) * dqk] for h in heads])
    kt3 = jnp.stack([kt_ref[h * dqk:(h + 1) * dqk, :] for h in heads])
    v3 = jnp.stack([v_ref[:, h * dv:(h + 1) * dv] for h in heads])

    r3 = jnp.stack([g_t[h:h + 1, :] - b_t[H + h:H + h + 1, :] for h in heads])
    dm = jnp.where(mask, b_b + r3, -jnp.inf)
    a_inter = b_b + m_st
    m_q = jnp.maximum(a_inter, jnp.max(dm, axis=-1, keepdims=True))
    inter = jnp.exp(a_inter - m_q) * scale
    sc = jnp.einsum("hqd,hds->hqs", q3, kt3, preferred_element_type=F32)
    p = jnp.exp(dm - m_q) * (sc * scale)
    ct = ct_ref[...]
    qc = jnp.einsum("hqd,hdv->hqv", q3, ct.astype(BF16), preferred_element_type=F32)
    v_ext = jnp.concatenate([v3, jnp.ones((H, L, LANES), BF16)], axis=-1)
    pv = jnp.einsum("hqs,hsv->hqv", p.astype(BF16), v_ext, preferred_element_type=F32)
    den = pv[:, :, dv:] + inter * qc[:, :, dv:]
    rden = 1.0 / jnp.maximum(jnp.abs(den), jnp.exp(-m_q))
    hout = (pv[:, :, :dv] + tile(inter, dv) * qc[:, :, :dv]) * tile(rden, dv)
    for h in heads:
        o_ref[:, h * dv:(h + 1) * dv] = hout[h]

    wl = b_end - b_b + i_b
    m_next = jnp.maximum(b_end + m_st, jnp.max(wl, axis=1, keepdims=True))
    decay = jnp.exp(b_end + m_st - m_next)
    w_b = jnp.exp(wl - m_next)
    vw = jnp.concatenate([v3.astype(F32) * tile(w_b, dv), w_b], axis=-1).astype(BF16)
    upd = jnp.einsum("hdl,hlv->hdv", kt3, vw, preferred_element_type=F32)
    ct_ref[...] = tile(decay, dv + LANES) * ct + upd
    m_ref[...] = jnp.broadcast_to(m_next, m_ref.shape)


def _scan(u, kt, gates, n_batch, seq, ctx_len, n_heads, dqk, dv):
    L = SCAN_CHUNK
    nx = n_batch * seq
    ncc = ctx_len // L
    nlc = seq // L
    assert ctx_len % L == 0 and seq % L == 0
    hq = n_heads * dqk
    hv = n_heads * dv
    assert hv == 2 * hq

    def row_blk(b, d, s):
        ctx = (nx + b * ctx_len) // L + jnp.where(d == 0, s, ncc - 1 - s)
        sl = s - ncc
        lat = (b * seq) // L + jnp.where(d == 0, sl, nlc - 1 - sl)
        return jnp.where(s < ncc, ctx, lat)

    def out_blk(b, d, s):
        sl = jnp.maximum(s - ncc, 0)
        return (b * seq) // L + jnp.where(d == 0, sl, nlc - 1 - sl)

    return pl.pallas_call(
        functools.partial(_scan_kernel, n_heads, dqk, dv),
        grid=(n_batch, 2, ncc + nlc),
        in_specs=[
            pl.BlockSpec((L, hq), lambda b, d, s: (row_blk(b, d, s), 2 * hv // hq)),
            pl.BlockSpec((hq, L), lambda b, d, s: (0, row_blk(b, d, s))),
            pl.BlockSpec((L, hv), lambda b, d, s: (row_blk(b, d, s), 0)),
            pl.BlockSpec((L, LANES), lambda b, d, s: (row_blk(b, d, s), d)),
        ],
        out_specs=pl.BlockSpec((None, L, hv), lambda b, d, s: (d, out_blk(b, d, s), 0)),
        out_shape=jax.ShapeDtypeStruct((2, nx, hv), F32),
        scratch_shapes=[pltpu.VMEM((n_heads, dqk, dv + LANES), F32),
                        pltpu.VMEM((n_heads, SUBLANES, LANES), F32)],
        compiler_params=_cparams(("arbitrary", "arbitrary", "arbitrary")),
        name="mlstm_scan",
    )(u, kt, u, gates)


def _mix_epilogue(alpha, rs, a, x_ref, g1_ref, sh2_ref, sc2_ref, wout_ref, lng_ref,
                  lnb_ref, rwt_ref, x1_ref, hx2_ref, logt_ref):
    mx = jnp.dot(a, wout_ref[...], preferred_element_type=F32)
    x1 = _layer_norm_rows(alpha * x_ref[rs, :] + g1_ref[0] * mx, lng_ref[...], lnb_ref[...])
    x1_ref[rs, :] = x1
    hx2 = x1 * (1.0 + sc2_ref[0]) + sh2_ref[0]
    hx2_ref[rs, :] = hx2
    logt_ref[:, rs] = lax.dot_general(rwt_ref[...], hx2.astype(BF16), (((1,), (1,)), ((), ())),
                                      preferred_element_type=F32)


def _sub_rows(tm, sub):
    return [slice(r, r + sub) for r in range(0, tm, sub)]


def _conv_out_kernel(alpha, ctx_mode, n_lat, tpb, sub, z_ref, zp_ref, zn_ref, bg_ref, cw_ref,
                     x_ref, g1_ref, sh2_ref, sc2_ref, wout_ref, lng_ref, lnb_ref, rwt_ref, *rest):
    x1_ref, hx2_ref, logt_ref = rest[-3:]
    i = pl.program_id(0)
    tm, d = z_ref.shape
    half = d // 2
    rows = lax.broadcasted_iota(I32, (sub, 1), 0)

    def conv1(zz, w3, period):
        pos = rows & (period - 1)
        prev = pltpu.roll(zz, 1, 0) * (pos != 0).astype(F32)
        nxt = pltpu.roll(zz, sub - 1, 0) * (pos != period - 1).astype(F32)
        return w3[0:1] * prev + w3[1:2] * zz + w3[2:3] * nxt

    def tile():
        cw = cw_ref[...]
        if not ctx_mode:
            ti = i % tpb
            up = zp_ref[...].astype(F32) * (ti > 0).astype(F32)
            dn = zn_ref[...].astype(F32) * (ti < tpb - 1).astype(F32)
            w3 = cw[:, half:]

        for rs in _sub_rows(tm, sub):
            z = z_ref[rs, :].astype(F32)
            bg = bg_ref[rs, :].astype(F32)
            if ctx_mode:
                a = (bg * conv1(z, cw, sub)).astype(BF16)
            else:
                a_row = (bg[:, :half] * conv1(z[:, :half], cw[:, :half], GRID_W)).astype(BF16)
                r0, r1 = rs.start, rs.stop
                f32_rows = lambda a, b: z_ref[a:b, half:].astype(F32)
                prev = (jnp.concatenate([up, f32_rows(0, r1 - GRID_W)], axis=0) if r0 == 0
                        else f32_rows(r0 - GRID_W, r1 - GRID_W))
                nxt = (jnp.concatenate([f32_rows(r0 + GRID_W, tm), dn], axis=0) if r1 == tm
                       else f32_rows(r0 + GRID_W, r1 + GRID_W))
                y = w3[0:1] * prev + w3[1:2] * z[:, half:] + w3[2:3] * nxt
                a = jnp.concatenate([a_row, (bg[:, half:] * y).astype(BF16)], axis=1)
            _mix_epilogue(alpha, rs, a, x_ref, g1_ref, sh2_ref, sc2_ref, wout_ref, lng_ref,
                          lnb_ref, rwt_ref, x1_ref, hx2_ref, logt_ref)

    if ctx_mode:
        tile()
    else:
        pl.when(i < n_lat)(tile)

        @pl.when(i >= n_lat)
        def _():
            x1_ref[...] = jnp.zeros_like(x1_ref)
            hx2_ref[...] = jnp.zeros_like(hx2_ref)
            logt_ref[...] = jnp.zeros_like(logt_ref)


def _ml_out_kernel(alpha, n_heads, dv, sub, hf_ref, hb_ref, og_ref, ng_ref, x_ref,
                   g1_ref, sh2_ref, sc2_ref, wout_ref, lng_ref, lnb_ref, rwt_ref,
                   x1_ref, hx2_ref, logt_ref):
    for rs in _sub_rows(hf_ref.shape[0], sub):
        parts = []
        for h in range(n_heads):
            sl = slice(h * dv, (h + 1) * dv)
            hs = hf_ref[rs, sl] + hb_ref[rs, sl]
            mu = jnp.mean(hs, axis=-1, keepdims=True)
            c = hs - mu
            var = jnp.mean(c * c, axis=-1, keepdims=True)
            hn = c * lax.rsqrt(var + HEAD_NORM_EPS)
            gate = jax.nn.sigmoid(og_ref[rs, sl].astype(F32))
            parts.append((hn * ng_ref[:, sl] * gate).astype(BF16))
        _mix_epilogue(alpha, rs, jnp.concatenate(parts, axis=1), x_ref, g1_ref, sh2_ref,
                      sc2_ref, wout_ref, lng_ref, lnb_ref, rwt_ref, x1_ref, hx2_ref, logt_ref)


def _mix_out_common(d, n_rows, n_exp, mod_row, tile0=0):
    tm = ROW_TILE
    mspec = lambda chunk: pl.BlockSpec((1, 1, d), lambda i: (mod_row(i), 0, chunk))
    const = lambda shape: pl.BlockSpec(shape, lambda i: (0,) * len(shape))
    in_specs = [mspec(2), mspec(3), mspec(4), const((d, d)), const((1, d)), const((1, d)),
                const((n_exp, d))]
    out_specs = [pl.BlockSpec((tm, d), lambda i: (i + tile0, 0)),
                 pl.BlockSpec((tm, d), lambda i: (i + tile0, 0)),
                 pl.BlockSpec((n_exp, tm), lambda i: (0, i + tile0))]
    out_shape = [jax.ShapeDtypeStruct((n_rows, d), F32),
                 jax.ShapeDtypeStruct((n_rows, d), F32),
                 jax.ShapeDtypeStruct((n_exp, n_rows), F32)]
    return in_specs, out_specs, out_shape


def _conv_out(alpha, z, bg, conv_w, x2d, c2d, mod, w_out_b, ln_g, ln_b, rwt_b, n_batch, seq,
              ctx_len):
    n_all, d = z.shape
    nx = x2d.shape[0]
    tm = ROW_TILE
    assert ctx_len == tm and seq % tm == 0 and MIX_SUB_ROWS % GRID_W == 0
    half = d // 2
    nxa = nx // tm
    tpb = seq // tm
    hpt = tm // GRID_W
    n_exp = rwt_b.shape[0]

    def z_specs(tile0):
        return [
            pl.BlockSpec((tm, d), lambda i: (i + tile0, 0)),
            pl.BlockSpec((GRID_W, half), lambda i: (jnp.maximum((i + tile0) * hpt - 1, 0), 1)),
            pl.BlockSpec((GRID_W, half), lambda i: ((i + tile0 + 1) * hpt - 1, 1)),
            pl.BlockSpec((tm, d), lambda i: (i + tile0, 0)),
            pl.BlockSpec((3, d), lambda i: (0, 0)),
            pl.BlockSpec((tm, d), lambda i: (i, 0)),
        ]

    lat = lambda i: jnp.minimum(i, nxa - 1)
    common_in, out_specs, out_shape = _mix_out_common(d, n_all, n_exp, lambda i: lat(i) // tpb)
    lat_specs = z_specs(0)
    lat_specs[2] = pl.BlockSpec((GRID_W, half), lambda i: ((lat(i) + 1) * hpt, 1))
    lat_specs[5] = pl.BlockSpec((tm, d), lambda i: (lat(i), 0))
    outs = pl.pallas_call(
        functools.partial(_conv_out_kernel, alpha, False, nxa, tpb, MIX_SUB_ROWS),
        grid=(n_all // tm,),
        in_specs=lat_specs + common_in,
        out_specs=out_specs,
        out_shape=out_shape,
        compiler_params=_cparams(("arbitrary",)),
        name="conv_out",
    )(z, z, z, bg, conv_w, x2d, mod, mod, mod, w_out_b, ln_g, ln_b, rwt_b)

    common_in, out_specs, out_shape = _mix_out_common(d, n_all, n_exp, lambda i: n_batch, nxa)
    n_in = 6 + len(common_in)
    keep = [pl.BlockSpec(memory_space=pl.ANY)] * 3
    return pl.pallas_call(
        functools.partial(_conv_out_kernel, alpha, True, None, tpb, tm),
        grid=(c2d.shape[0] // tm,),
        in_specs=z_specs(nxa) + common_in + keep,
        out_specs=out_specs,
        out_shape=out_shape,
        input_output_aliases={n_in: 0, n_in + 1: 1, n_in + 2: 2},
        compiler_params=_cparams(("arbitrary",)),
        name="conv_out_ctx",
    )(z, z, z, bg, conv_w, c2d, mod, mod, mod, w_out_b, ln_g, ln_b, rwt_b, *outs)


def _ml_out(alpha, hfb, u, norm_g, xall, mod, w_out_b, ln_g, ln_b, rwt_b, n_batch, seq,
            n_heads, dv):
    nx, d = hfb.shape[1:]
    tm = ROW_TILE
    tpb = seq // tm
    common_in, out_specs, out_shape = _mix_out_common(d, nx, rwt_b.shape[0], lambda i: i // tpb)
    o_blk = 1
    in_specs = [
        pl.BlockSpec((None, tm, d), lambda i: (0, i, 0)),
        pl.BlockSpec((None, tm, d), lambda i: (1, i, 0)),
        pl.BlockSpec((tm, d), lambda i: (i, o_blk)),
        pl.BlockSpec((1, d), lambda i: (0, 0)),
        pl.BlockSpec((tm, d), lambda i: (i, 0)),
    ] + common_in
    return pl.pallas_call(
        functools.partial(_ml_out_kernel, alpha, n_heads, dv, tm),
        grid=(nx // tm,),
        in_specs=in_specs,
        out_specs=out_specs,
        out_shape=out_shape,
        compiler_params=_cparams(("arbitrary",)),
        name="mlstm_out",
    )(hfb, hfb, u, norm_g, xall, mod, mod, mod, w_out_b, ln_g, ln_b, rwt_b)


def _route_kernel(tile_rows, n_te, logt_ref, rb_ref, pos_ref, wcol_ref, te_ref, meta_ref,
                  carry_ref, before_ref):
    ph = pl.program_id(0)
    i = pl.program_id(1)
    n_steps = pl.num_programs(1)
    n_exp, tr = logt_ref.shape
    epg = n_exp // N_GROUPS

    @pl.when((ph == 0) & (i == 0))
    def _():
        carry_ref[...] = jnp.zeros_like(carry_ref)
        ti = lax.broadcasted_iota(I32, (tr, tr), 0)
        tj = lax.broadcasted_iota(I32, (tr, tr), 1)
        before_ref[...] = (ti < tj).astype(BF16)

    s = jax.nn.sigmoid(logt_ref[...])
    sel = s + rb_ref[...]
    row = lax.broadcasted_iota(I32, (n_exp, tr), 0)
    member = row % epg
    group = row // epg

    def partner(x, k, idx, span, unit):
        wrapped = (idx + k) >= span
        up = pltpu.roll(x, n_exp - k * unit, 0)
        down = pltpu.roll(x, (span - k) * unit, 0)
        return jnp.where(wrapped, down, up), wrapped

    rank_in = jnp.zeros((n_exp, tr), F32)
    for k in range(1, epg):
        p, wrapped = partner(sel, k, member, epg, 1)
        beats = (p > sel) | ((p == sel) & wrapped)
        rank_in = rank_in + beats.astype(F32)
    top = (rank_in < TOP_K).astype(F32)
    gs = sel * top
    score = gs
    for k in range(1, epg):
        p, _ = partner(gs, k, member, epg, 1)
        score = score + p
    n_better = jnp.zeros((n_exp, tr), F32)
    for k in range(1, N_GROUPS):
        p, wrapped = partner(score, k, group, N_GROUPS, epg)
        beats = (p > score) | ((p == score) & wrapped)
        n_better = n_better + beats.astype(F32)
    best = n_better == 0.0
    m1 = (best & (rank_in == 0.0)).astype(F32)
    m2 = (best & (rank_in == 1.0)).astype(F32)
    s1 = jnp.sum(s * m1, axis=0, keepdims=True)
    s2 = jnp.sum(s * m2, axis=0, keepdims=True)
    den = s1 + s2
    oh = m1 + m2

    slot = jnp.dot(oh.astype(BF16), before_ref[...], preferred_element_type=F32) + carry_ref[...]

    @pl.when(ph == 1)
    def _():
        pos_ref[0:1, :] = jnp.sum(m1 * slot, axis=0, keepdims=True).astype(I32)
        pos_ref[1:2, :] = jnp.sum(m2 * slot, axis=0, keepdims=True).astype(I32)
        w8 = jnp.concatenate([s1 / den, s2 / den, jnp.zeros((LANES - 2, tr), F32)], axis=0)
        wcol_ref[...] = w8.T

    carry_ref[...] = carry_ref[...] + jnp.sum(oh, axis=1, keepdims=True)

    @pl.when((ph == 0) & (i == n_steps - 1))
    def _():
        cnt = carry_ref[...]
        ntile = jnp.floor((cnt + (tile_rows - 1)) * (1.0 / tile_rows))
        er = lax.broadcasted_iota(I32, (n_exp, LANES), 0)
        el = lax.broadcasted_iota(I32, (n_exp, LANES), 1)
        eye = (er == el).astype(F32)
        nt_row = jnp.sum(ntile * eye, axis=0, keepdims=True)
        cnt_row = jnp.sum(cnt * eye, axis=0, keepdims=True)
        cum_excl = jnp.sum(nt_row * (el < er).astype(F32), axis=1, keepdims=True)
        off = cum_excl * tile_rows
        off_row = jnp.sum(off * eye, axis=0, keepdims=True)
        total = jnp.sum(nt_row, axis=1, keepdims=True)
        cum_incl = cum_excl + ntile
        tl = lax.broadcasted_iota(I32, (n_exp, n_te), 1).astype(F32)
        te = jnp.sum((cum_incl <= tl).astype(F32), axis=0, keepdims=True)
        elf = el.astype(F32)
        later = (el > er) & (nt_row > 0.0) & (el < n_exp)
        nxt = jnp.min(jnp.where(later, elf, float(n_exp)), axis=1, keepdims=True)
        nxt = jnp.where(nxt == float(n_exp), er[:, 0:1].astype(F32), nxt)
        mine = (cum_excl <= tl) & (tl < cum_incl)
        te_next = jnp.sum(jnp.where(mine, nxt, 0.0), axis=0, keepdims=True)
        te_ref[...] = jnp.concatenate([jnp.minimum(te, n_exp - 1), te_next], axis=0).astype(I32)
        meta_ref[...] = jnp.concatenate(
            [cnt_row, off_row, jnp.broadcast_to(total, (1, LANES)),
             jnp.zeros((SUBLANES - 3, LANES), F32)], axis=0).astype(I32)
        carry_ref[...] = off


def _route(logt, router_b, tile_rows):
    n_exp, n = logt.shape
    tr = ROUTE_COLS
    while n % tr:
        tr //= 2
    n_te = 256
    assert (2 * n) // tile_rows + n_exp <= n_te
    return pl.pallas_call(
        functools.partial(_route_kernel, tile_rows, n_te),
        grid=(2, n // tr),
        in_specs=[pl.BlockSpec((n_exp, tr), lambda p, i: (0, i)),
                  pl.BlockSpec((n_exp, 1), lambda p, i: (0, 0))],
        out_specs=[pl.BlockSpec((2, tr), lambda p, i: (0, i * p)),
                   pl.BlockSpec((tr, LANES), lambda p, i: (i * p, 0)),
                   pl.BlockSpec((2, n_te), lambda p, i: (0, 0)),
                   pl.BlockSpec((SUBLANES, LANES), lambda p, i: (0, 0))],
        out_shape=[jax.ShapeDtypeStruct((2, n), I32),
                   jax.ShapeDtypeStruct((n, LANES), F32),
                   jax.ShapeDtypeStruct((2, n_te), I32),
                   jax.ShapeDtypeStruct((SUBLANES, LANES), I32)],
        scratch_shapes=[pltpu.VMEM((n_exp, 1), F32), pltpu.VMEM((tr, tr), BF16)],
        compiler_params=_cparams(("arbitrary", "arbitrary")),
        name="route",
    )(logt, router_b.reshape(n_exp, 1).astype(F32))


def _tbl_kernel(n_tok, n_exp, tile_rows, meta_ref, pos_ref, tbl_ref):
    i = pl.program_id(0)
    blk = pos_ref.shape[1]

    @pl.when(i == 0)
    def _():
        n_pad = 0
        for e in range(n_exp):
            cnt = meta_ref[e]
            off = meta_ref[n_exp + e]
            up = ((cnt + (tile_rows - 1)) // tile_rows) * tile_rows

            def fill(r, j):
                tbl_ref[off + r] = (2 * n_tok + j) << SLOT_SRC_BITS
                return j + 1

            n_pad = lax.fori_loop(cnt, up, fill, n_pad)

        def fill_tail(r, carry):
            tbl_ref[r] = (2 * n_tok) << SLOT_SRC_BITS
            return carry

        lax.fori_loop(2 * n_tok + n_pad, tbl_ref.shape[0], fill_tail, 0)

    base = i * blk
    first = (base << SLOT_SRC_BITS) | jnp.where(base >= n_tok, base - n_tok, base)
    step = (1 << SLOT_SRC_BITS) + 1
    group = 8

    def body(q, carry):
        r0 = q * group
        slots = [pos_ref[0, r0 + j] for j in range(group)]
        for j in range(group):
            tbl_ref[slots[j]] = first + (r0 + j) * step
        return carry

    lax.fori_loop(0, blk // group, body, 0, unroll=True)


def _build_table(meta1d, pos, n_tok, n_exp, tile_rows, p_pad):
    blk = TBL_BLOCK
    while n_tok % blk:
        blk //= 2
    n_pairs = 2 * n_tok
    grid_spec = pltpu.PrefetchScalarGridSpec(
        num_scalar_prefetch=1,
        grid=(n_pairs // blk,),
        in_specs=[pl.BlockSpec((None, 1, blk), lambda i, m: (i, 0, 0), memory_space=pltpu.SMEM)],
        out_specs=pl.BlockSpec(memory_space=pltpu.SMEM),
    )
    return pl.pallas_call(
        functools.partial(_tbl_kernel, n_tok, n_exp, tile_rows),
        grid_spec=grid_spec,
        out_shape=jax.ShapeDtypeStruct((p_pad,), I32),
        compiler_params=_cparams(("arbitrary",)),
        name="slot_table",
    )(meta1d, pos.reshape(n_pairs // blk, 1, blk))


def _moe_kernel(layer, te_ref, ten_ref, nt_ref, tbl_ref, tbln_ref, tblp_ref, hx_hbm,
                wg_hbm, wu_hbm, wd_hbm, y_hbm, xbuf, ybuf, zbuf, wg_b, wu_b, wd_b,
                wg_s, wu_s, wd_s, gsem, ssem, zsem, wsem):
    t = pl.program_id(0)
    n = nt_ref[0]
    rows = tbl_ref.shape[1]
    lt = hx_hbm.shape[1]
    staged = ((wg_hbm, wg_s, wg_b), (wu_hbm, wu_s, wu_b), (wd_hbm, wd_s, wd_b))

    def weight_copies(e):
        return [pltpu.make_async_copy(hbm.at[layer, e], stage, wsem.at[k])
                for k, (hbm, stage, _) in enumerate(staged)]

    @pl.when(t == 0)
    def _():
        for c in weight_copies(te_ref[0]):
            c.start(priority=WEIGHT_DMA_PRIORITY)

    tl = jnp.minimum(t, n - 1)
    new_expert = (t < n) & ((t == 0) | (te_ref[tl] != te_ref[jnp.maximum(tl - 1, 0)]))

    @pl.when(new_expert)
    def _():
        for c, (_, stage, dst) in zip(weight_copies(te_ref[tl]), staged):
            c.wait()
            dst[...] = stage[...].astype(BF16)
        for c in weight_copies(ten_ref[tl]):
            c.start(priority=WEIGHT_DMA_PRIORITY)

    def start_gather(tref, sl, lo=0, hi=None):
        for r in range(lo, rows if hi is None else hi):
            src = tref[0, r] & SLOT_SRC_MASK
            pltpu.make_async_copy(hx_hbm.at[src], xbuf.at[sl, pl.ds(r * lt, lt), :],
                                  gsem.at[sl]).start()

    def start_scatter(tref, sl, lo=0, hi=None):
        for r in range(lo, rows if hi is None else hi):
            dst = tref[0, r] >> SLOT_SRC_BITS
            pltpu.make_async_copy(ybuf.at[sl, pl.ds(r, 1), :],
                                  y_hbm.at[pl.ds(dst, 1), :], ssem.at[sl]).start()

    def wait_gather(sl):
        pltpu.make_async_copy(xbuf.at[sl], xbuf.at[sl], gsem.at[sl]).wait()

    def wait_scatter(sl):
        pltpu.make_async_copy(ybuf.at[sl], ybuf.at[sl], ssem.at[sl]).wait()

    @pl.when(t == 0)
    def _():
        zbuf[...] = jnp.zeros_like(zbuf)
        start_gather(tbl_ref, 0)

    @pl.when(t >= n)
    def _():
        dst = y_hbm.at[pl.ds(pl.multiple_of(t * rows, rows), rows), :]
        fill = pltpu.make_async_copy(zbuf, dst, zsem)
        fill.start()
        fill.wait()

    def tile(slot, has_prev):
        other = 1 - slot
        wait_gather(slot)
        f = wg_b.shape[1]
        d = wd_b.shape[1]
        fc, dc = MOE_UP_COLS, MOE_DOWN_COLS
        n_up, n_down = f // fc, d // dc
        g_bounds = [(k * rows) // n_up for k in range(n_up + 1)]
        s_bounds = [(k * rows) // n_down for k in range(n_down + 1)]

        def copy_group(k):
            if k < n_up:
                start_gather(tbln_ref, other, g_bounds[k], g_bounds[k + 1])
            elif has_prev:
                start_scatter(tblp_ref, other, s_bounds[k - n_up], s_bounds[k - n_up + 1])

        xb = jnp.concatenate(
            [xbuf[slot, pl.ds(c, rows, stride=lt), :].astype(BF16) for c in range(lt)], axis=1)
        hs = []
        for c in range(f // fc):
            cols = slice(c * fc, (c + 1) * fc)
            g = jnp.dot(xb, wg_b[:, cols], preferred_element_type=F32)
            u = jnp.dot(xb, wu_b[:, cols], preferred_element_type=F32)
            hs.append((g * jax.nn.sigmoid(g) * u).astype(BF16))
            copy_group(c)
        h = jnp.concatenate(hs, axis=1)

        @pl.when(t >= 2)
        def _():
            wait_scatter(slot)

        for c in range(d // dc):
            cols = slice(c * dc, (c + 1) * dc)
            ybuf[slot, :, cols] = jnp.dot(h, wd_b[:, cols], preferred_element_type=F32)
            copy_group(f // fc + c)

        @pl.when(t == n - 1)
        def _():
            start_scatter(tbl_ref, slot)
            wait_gather(other)

            @pl.when(t >= 1)
            def _():
                wait_scatter(other)

            wait_scatter(slot)
            for c in weight_copies(ten_ref[tl]):
                c.wait()

    pl.when((t < n) & (t == 0))(functools.partial(tile, 0, False))
    for parity in (0, 1):
        pl.when((t < n) & (t > 0) & (lax.rem(t, 2) == parity))(
            functools.partial(tile, parity, True))


def _moe(te2, nt1d, tbl, hx2, w_gate, w_up, w_down, layer, n_tok, p_pad):
    tmm = MOE_ROWS
    _, n_exp, d, f = w_gate.shape
    t_max = (2 * n_tok) // tmm + n_exp
    last = lambda t, nt: jnp.minimum(t, nt[0] - 1)
    tspec = lambda fn: pl.BlockSpec((None, 1, tmm), lambda t, te, ten, nt: (fn(t, nt), 0, 0),
                                    memory_space=pltpu.SMEM)
    hbm = pl.BlockSpec(memory_space=pl.ANY)
    grid_spec = pltpu.PrefetchScalarGridSpec(
        num_scalar_prefetch=3,
        grid=(t_max,),
        in_specs=[
            tspec(last),
            tspec(lambda t, nt: last(t + 1, nt)),
            tspec(lambda t, nt: jnp.maximum(last(t, nt) - 1, 0)),
            hbm, hbm, hbm, hbm,
        ],
        out_specs=hbm,
        scratch_shapes=[pltpu.VMEM((2, tmm * (d // LANES), LANES), F32),
                        pltpu.VMEM((2, tmm, d), F32),
                        pltpu.VMEM((tmm, d), F32),
                        pltpu.VMEM((d, f), BF16), pltpu.VMEM((d, f), BF16),
                        pltpu.VMEM((f, d), BF16),
                        pltpu.VMEM((d, f), F32), pltpu.VMEM((d, f), F32),
                        pltpu.VMEM((f, d), F32),
                        pltpu.SemaphoreType.DMA((2,)), pltpu.SemaphoreType.DMA((2,)),
                        pltpu.SemaphoreType.DMA(()), pltpu.SemaphoreType.DMA((3,))],
    )
    tbl3 = tbl.reshape(p_pad // tmm, 1, tmm)
    return pl.pallas_call(
        functools.partial(_moe_kernel, layer),
        grid_spec=grid_spec,
        out_shape=jax.ShapeDtypeStruct((t_max * tmm, d), F32),
        compiler_params=_cparams(("arbitrary",)),
        name="moe_experts",
    )(te2[0], te2[1], nt1d, tbl3, tbl3, tbl3, hx2.reshape(n_tok, d // LANES, LANES),
      w_gate, w_up, w_down)


def _comb_kernel(alpha, x_ref, y0_ref, y1_ref, w_ref, g2_ref, lng_ref, lnb_ref, o_ref):
    w = w_ref[...]
    ex = w[:, 0:1] * y0_ref[...] + w[:, 1:2] * y1_ref[...]
    o_ref[...] = _layer_norm_rows(alpha * x_ref[...] + g2_ref[0] * ex, lng_ref[...], lnb_ref[...])


def _combine(alpha, x1, y, wcol, mod, ln_g, ln_b, n_tok, n_out, n_batch, seq):
    d = x1.shape[1]
    tm = COMB_ROWS
    assert n_tok % tm == 0 and n_out % tm == 0 and seq % tm == 0
    nblk = n_tok // tm
    mspec = functools.partial(_mod_spec, d=d, tiles_per_batch=seq // tm, n_batch=n_batch)
    return pl.pallas_call(
        functools.partial(_comb_kernel, alpha),
        grid=(n_out // tm,),
        in_specs=[
            pl.BlockSpec((tm, d), lambda i: (i, 0)),
            pl.BlockSpec((tm, d), lambda i: (i, 0)),
            pl.BlockSpec((tm, d), lambda i: (i + nblk, 0)),
            pl.BlockSpec((tm, LANES), lambda i: (i, 0)),
            mspec(5),
            pl.BlockSpec((1, d), lambda i: (0, 0)),
            pl.BlockSpec((1, d), lambda i: (0, 0)),
        ],
        out_specs=pl.BlockSpec((tm, d), lambda i: (i, 0)),
        out_shape=jax.ShapeDtypeStruct((n_out, d), F32),
        compiler_params=_cparams(("arbitrary",)),
        name="moe_combine",
    )(x1, y, y, wcol, mod, ln_g, ln_b)


def _moe_layer(alpha, x1, hx2, logt, mod, ln_g, ln_b, router_b, w_gate, w_up, w_down, layer,
               n_out, n_batch, seq):
    n_tok = hx2.shape[0]
    n_exp = w_gate.shape[1]
    tmm = MOE_ROWS
    p_max = 2 * n_tok + n_exp * tmm
    p_pad = -(-p_max // TBL_BLOCK) * TBL_BLOCK
    assert n_tok <= 1 << SLOT_SRC_BITS and p_max < 1 << (31 - SLOT_SRC_BITS)
    pos, wcol, te, meta = _route(logt, router_b, tmm)
    meta1d = meta[:2, :n_exp].reshape(-1)
    tbl = _build_table(meta1d, pos, n_tok, n_exp, tmm, p_pad)
    y = _moe(te, meta[2, :1], tbl, hx2, w_gate, w_up, w_down, layer, n_tok, p_pad)
    return _combine(alpha, x1, y, wcol, mod, ln_g, ln_b, n_tok, n_out, n_batch, seq)


def kernel(x, c, ctx, c_ctx, w_ada, b_ada, ln_g, ln_b, conv_w_in, conv_w, conv_w_out, ml_w_in, ml_w_gate, ml_b_gate, ml_norm_g, ml_w_out, router_w, router_b, exp_w_gate, exp_w_up, exp_w_down):
    n_batch, seq, d = x.shape
    ctx_len = ctx.shape[1]
    depth = w_ada.shape[0]
    assert depth == 2, "layer 0 is the conv mixer, layer 1 the mLSTM mixer"
    alpha = (2 * depth) ** 0.25
    n_heads = ml_b_gate.shape[-1] // 4
    dqk = d // (2 * n_heads)
    dv = d // n_heads
    nx = n_batch * seq
    nc = n_batch * ctx_len
    assert n_batch < SUBLANES and 2 * n_heads <= LANES

    x2d = x.reshape(nx, d)
    c2d = ctx.reshape(nc, d)
    cc = jnp.zeros((SUBLANES, d), F32).at[:n_batch].set(c).at[n_batch].set(c_ctx)
    mod = _ada(cc, w_ada, b_ada).reshape(depth, SUBLANES, 1, 6 * d)
    rwt_b = router_w.T.astype(BF16)

    bg, z = _conv_in(x2d, c2d, mod[0], _to_bf16(conv_w_in, 0), n_batch, seq)
    x1, hx2, logt = _conv_out(alpha, z, bg, conv_w[0], x2d, c2d, mod[0],
                              _to_bf16(conv_w_out, 0), ln_g[0, 0:1], ln_b[0, 0:1], rwt_b,
                              n_batch, seq, ctx_len)
    xall = _moe_layer(alpha, x1, hx2, logt, mod[0], ln_g[0, 1:2], ln_b[0, 1:2], router_b,
                      exp_w_gate, exp_w_up, exp_w_down, 0, nx + nc, n_batch, seq)

    w_gate = ml_w_gate[0]
    b_gate = ml_b_gate[0]
    h2 = 2 * n_heads
    w_gate_pad = (jnp.zeros((d, 2 * LANES), F32).at[:, :h2].set(w_gate[:, :h2])
                  .at[:, LANES:LANES + h2].set(w_gate[:, h2:])).astype(BF16)
    b_gate_pad = (jnp.zeros((1, 2 * LANES), F32).at[0, :h2].set(b_gate[:h2])
                  .at[0, LANES:LANES + h2].set(b_gate[h2:]))
    w_voq_b, w_kt_b = _ml_weights(ml_w_in, 0, n_heads * dqk)
    u, kt, gates = _ml_in(xall, mod[1], w_voq_b, w_kt_b, w_gate_pad, b_gate_pad, n_batch,
                          seq, nc)
    hfb = _scan(u, kt, gates, n_batch, seq, ctx_len, n_heads, dqk, dv)
    x1, hx2, logt = _ml_out(alpha, hfb, u, ml_norm_g[0:1], xall, mod[1],
                            _to_bf16(ml_w_out, 0), ln_g[1, 0:1], ln_b[1, 0:1], rwt_b,
                            n_batch, seq, n_heads, dv)
    out = _moe_layer(alpha, x1, hx2, logt, mod[1], ln_g[1, 1:2], ln_b[1, 1:2], router_b,
                     exp_w_gate, exp_w_up, exp_w_down, 1, nx, n_batch, seq)
    return out.reshape(n_batch, seq, d)
```

```python
import functools

import jax
import jax.numpy as jnp
from jax import lax
from jax.experimental import pallas as pl
from jax.experimental.pallas import tpu as pltpu

F32 = jnp.float32
BF16 = jnp.bfloat16
I32 = jnp.int32

GRID_W = 64
N_GROUPS = 4
TOP_K = 2
LN_EPS = 1e-5
HEAD_NORM_EPS = 1e-6

LANES = 128
SUBLANES = 8
VMEM_LIMIT_BYTES = 56 * 1024 * 1024

ROW_TILE = 256
MIX_SUB_ROWS = 128
PROJ_ROWS = 1024
MOE_ROWS = 256
MOE_UP_COLS = 256
MOE_DOWN_COLS = 512
ROUTE_COLS = 1024
COMB_ROWS = 512
SCAN_CHUNK = 128
TBL_BLOCK = 1024
SLOT_SRC_BITS = 15
SLOT_SRC_MASK = (1 << SLOT_SRC_BITS) - 1
CAST_BLOCK_BYTES = 8 * 1024 * 1024
WEIGHT_DMA_PRIORITY = 1


def _cparams(sem):
    return pltpu.CompilerParams(dimension_semantics=sem, vmem_limit_bytes=VMEM_LIMIT_BYTES)


def _layer_norm_rows(r, g, b):
    mu = jnp.mean(r, axis=-1, keepdims=True)
    c = r - mu
    var = jnp.mean(c * c, axis=-1, keepdims=True)
    return c * lax.rsqrt(var + LN_EPS) * g + b


def _ada_kernel(cc_ref, w_ref, b_ref, o_ref):
    a = cc_ref[...]
    a = (a * jax.nn.sigmoid(a)).astype(BF16)
    o_ref[0] = jnp.dot(a, w_ref[0].astype(BF16), preferred_element_type=F32) + b_ref[0]


def _ada(cc, w_ada, b_ada):
    depth, d, n6 = w_ada.shape
    tn = 1024
    return pl.pallas_call(
        _ada_kernel,
        grid=(depth, n6 // tn),
        in_specs=[
            pl.BlockSpec((SUBLANES, d), lambda l, j: (0, 0)),
            pl.BlockSpec((1, d, tn), lambda l, j: (l, 0, j)),
            pl.BlockSpec((1, 1, tn), lambda l, j: (l, 0, j)),
        ],
        out_specs=pl.BlockSpec((1, SUBLANES, tn), lambda l, j: (l, 0, j)),
        out_shape=jax.ShapeDtypeStruct((depth, SUBLANES, n6), F32),
        compiler_params=_cparams(("arbitrary", "arbitrary")),
        name="ada_mod",
    )(cc, w_ada, b_ada.reshape(depth, 1, n6))


def _cast_kernel(w_ref, o_ref):
    o_ref[...] = w_ref[...].astype(BF16)


def _to_bf16(w, layer):
    rows, cols = w.shape[-2:]
    w4 = w.reshape(w.shape[0], -1, rows, cols)
    m = w4.shape[1]
    rb = rows
    while rb * cols * 4 > CAST_BLOCK_BYTES and rb % (4 * SUBLANES) == 0:
        rb //= 2
    out = pl.pallas_call(
        _cast_kernel,
        grid=(m, rows // rb),
        in_specs=[pl.BlockSpec((None, None, rb, cols), lambda e, r: (layer, e, r, 0))],
        out_specs=pl.BlockSpec((None, rb, cols), lambda e, r: (e, r, 0)),
        out_shape=jax.ShapeDtypeStruct((m, rows, cols), BF16),
        compiler_params=_cparams(("arbitrary", "arbitrary")),
        name="to_bf16",
    )(w4)
    return out.reshape(w.shape[1:])


def _cast_t_kernel(w_ref, o_ref):
    o_ref[...] = w_ref[...].T.astype(BF16)


def _ml_weights(w, layer, hq):
    _, d, ncol = w.shape
    nb = ncol // hq
    rb = min(d, 512)
    voq = pl.pallas_call(
        _cast_kernel,
        grid=(d // rb, nb - 1),
        in_specs=[pl.BlockSpec((None, rb, hq), lambda r, k: (layer, r, (k + 2) % nb))],
        out_specs=pl.BlockSpec((rb, hq), lambda r, k: (r, k)),
        out_shape=jax.ShapeDtypeStruct((d, ncol - hq), BF16),
        compiler_params=_cparams(("arbitrary", "arbitrary")),
        name="to_bf16_voq",
    )(w)
    kt = pl.pallas_call(
        _cast_t_kernel,
        grid=(d // rb,),
        in_specs=[pl.BlockSpec((None, rb, hq), lambda r: (layer, r, 1))],
        out_specs=pl.BlockSpec((hq, rb), lambda r: (0, r)),
        out_shape=jax.ShapeDtypeStruct((hq, d), BF16),
        compiler_params=_cparams(("arbitrary",)),
        name="to_bf16_kt",
    )(w)
    return voq, kt


def _mod_spec(chunk, d, tiles_per_batch, n_batch):
    return pl.BlockSpec(
        (1, 1, d),
        lambda i, *_: (jnp.minimum(i // tiles_per_batch, n_batch), 0, chunk))


def _conv_in_kernel(nxa, xa_ref, xb_ref, sh_ref, sc_ref, wb_ref, wc_ref, wv_ref,
                    bg_ref, z_ref, h_ref):
    i = pl.program_id(0)
    j = pl.program_id(1)

    @pl.when((j == 0) & (i < nxa))
    def _():
        h_ref[...] = (xa_ref[...] * (1.0 + sc_ref[0]) + sh_ref[0]).astype(BF16)

    @pl.when((j == 0) & (i >= nxa))
    def _():
        h_ref[...] = (xb_ref[...] * (1.0 + sc_ref[0]) + sh_ref[0]).astype(BF16)

    h = h_ref[...]
    bg = jnp.dot(h, wb_ref[...], preferred_element_type=F32)
    cg = jnp.dot(h, wc_ref[...], preferred_element_type=F32)
    v = jnp.dot(h, wv_ref[...], preferred_element_type=F32)
    bg_ref[...] = bg.astype(BF16)
    z_ref[...] = (cg * v).astype(BF16)


def _conv_in(x2d, c2d, mod, w_in_b, n_batch, seq):
    nx, d = x2d.shape
    nc = c2d.shape[0]
    bm = min(PROJ_ROWS, seq, nc)
    assert seq % bm == 0 and nc % bm == 0
    tn = 512 if d % 512 == 0 else d
    nj = d // tn
    nxa = nx // bm
    n_all = nx + nc
    tpb = seq // bm
    mspec = functools.partial(_mod_spec, d=d, tiles_per_batch=tpb, n_batch=n_batch)
    return pl.pallas_call(
        functools.partial(_conv_in_kernel, nxa),
        grid=(n_all // bm, nj),
        in_specs=[
            pl.BlockSpec((bm, d), lambda i, j: (jnp.minimum(i, nxa - 1), 0)),
            pl.BlockSpec((bm, d), lambda i, j: (jnp.maximum(i - nxa, 0), 0),
                         pipeline_mode=pl.Buffered(1)),
            mspec(0), mspec(1),
            pl.BlockSpec((d, tn), lambda i, j: (0, j)),
            pl.BlockSpec((d, tn), lambda i, j: (0, nj + j)),
            pl.BlockSpec((d, tn), lambda i, j: (0, 2 * nj + j)),
        ],
        out_specs=[
            pl.BlockSpec((bm, tn), lambda i, j: (i, j)),
            pl.BlockSpec((bm, tn), lambda i, j: (i, j)),
        ],
        out_shape=[jax.ShapeDtypeStruct((n_all, d), BF16),
                   jax.ShapeDtypeStruct((n_all, d), BF16)],
        scratch_shapes=[pltpu.VMEM((bm, d), BF16)],
        compiler_params=_cparams(("arbitrary", "arbitrary")),
        name="conv_in",
    )(x2d, c2d, mod, mod, w_in_b, w_in_b, w_in_b)


def _ml_in_kernel(x_ref, sh_ref, sc_ref, w_ref, wkt_ref, wg_ref, bgate_ref,
                  u_ref, kt_ref, g_ref, h_ref):
    j = pl.program_id(1)

    @pl.when(j == 0)
    def _():
        h = (x_ref[...] * (1.0 + sc_ref[0]) + sh_ref[0]).astype(BF16)
        h_ref[...] = h
        g_ref[...] = jnp.dot(h, wg_ref[...], preferred_element_type=F32) + bgate_ref[...]
        kt_ref[...] = lax.dot_general(wkt_ref[...], h, (((1,), (1,)), ((), ())),
                                      preferred_element_type=F32).astype(BF16)

    u_ref[...] = jnp.dot(h_ref[...], w_ref[...], preferred_element_type=F32).astype(BF16)


def _ml_in(xall, mod, w_voq_b, w_kt_b, w_gate_pad, b_gate_pad, n_batch, seq, n_ctx_rows):
    n_all, d = xall.shape
    nu = w_voq_b.shape[1]
    hq = w_kt_b.shape[0]
    bm = min(PROJ_ROWS, seq, n_ctx_rows)
    assert seq % bm == 0 and n_ctx_rows % bm == 0
    tn = hq
    assert nu % tn == 0
    gl = w_gate_pad.shape[1]
    mspec = functools.partial(_mod_spec, d=d, tiles_per_batch=seq // bm, n_batch=n_batch)
    return pl.pallas_call(
        _ml_in_kernel,
        grid=(n_all // bm, nu // tn),
        in_specs=[
            pl.BlockSpec((bm, d), lambda i, j: (i, 0)),
            mspec(0), mspec(1),
            pl.BlockSpec((d, tn), lambda i, j: (0, j)),
            pl.BlockSpec((hq, d), lambda i, j: (0, 0)),
            pl.BlockSpec((d, gl), lambda i, j: (0, 0)),
            pl.BlockSpec((1, gl), lambda i, j: (0, 0)),
        ],
        out_specs=[
            pl.BlockSpec((bm, tn), lambda i, j: (i, j)),
            pl.BlockSpec((hq, bm), lambda i, j: (0, i)),
            pl.BlockSpec((bm, gl), lambda i, j: (i, 0)),
        ],
        out_shape=[jax.ShapeDtypeStruct((n_all, nu), BF16),
                   jax.ShapeDtypeStruct((hq, n_all), BF16),
                   jax.ShapeDtypeStruct((n_all, gl), F32)],
        scratch_shapes=[pltpu.VMEM((bm, d), BF16)],
        compiler_params=_cparams(("arbitrary", "arbitrary")),
        name="mlstm_in",
    )(xall, mod, mod, w_voq_b, w_kt_b, w_gate_pad, b_gate_pad)


def _dot_split3(a_b, x):
    hi = x.astype(BF16)
    r1 = x - hi.astype(F32)
    mid = r1.astype(BF16)
    lo = (r1 - mid.astype(F32)).astype(BF16)
    return (jnp.dot(a_b, hi, preferred_element_type=F32)
            + jnp.dot(a_b, mid, preferred_element_type=F32)
            + jnp.dot(a_b, lo, preferred_element_type=F32))


def _scan_kernel(n_heads, dqk, dv, q_ref, kt_ref, v_ref, g_ref, o_ref, ct_ref, m_ref):
    d = pl.program_id(1)
    s = pl.program_id(2)
    L = q_ref.shape[0]
    assert L == LANES
    scale = dqk ** -0.5

    @pl.when(s == 0)
    def _():
        ct_ref[...] = jnp.zeros_like(ct_ref)
        m_ref[...] = jnp.zeros_like(m_ref)

    H = n_heads
    heads = range(H)
    qi = lax.broadcasted_iota(I32, (L, L), 0)
    si = lax.broadcasted_iota(I32, (L, L), 1)
    fwd = d == 0
    mask = jnp.where(fwd, si - qi, qi - si) <= 0

    g = g_ref[...]
    b_all = _dot_split3(mask.astype(BF16), jax.nn.log_sigmoid(g))
    g_t = g.T
    b_t = b_all.T
    def lane_bcast(x, lane0):
        return jnp.stack([jnp.broadcast_to(x[:, lane0 + h:lane0 + h + 1], (L, LANES))
                          for h in heads])

    b_b = lane_bcast(b_all, H)
    i_b = lane_bcast(g, 0)
    b_end = jnp.where(fwd, b_b[:, L - 1:L, :], b_b[:, 0:1, :])
    m_st = m_ref[:, 0:1, :]
    tile = lambda x, n: jnp.concatenate([x] * (n // LANES), axis=-1)

    q3 = jnp.stack([q_ref[:, h * dqk:(h + 1) * dqk] for h in heads])
    kt3 = jnp.stack([kt_ref[h * dqk:(h + 1) * dqk, :] for h in heads])
    v3 = jnp.stack([v_ref[:, h * dv:(h + 1) * dv] for h in heads])

    r3 = jnp.stack([g_t[h:h + 1, :] - b_t[H + h:H + h + 1, :] for h in heads])
    dm = jnp.where(mask, b_b + r3, -jnp.inf)
    a_inter = b_b + m_st
    m_q = jnp.maximum(a_inter, jnp.max(dm, axis=-1, keepdims=True))
    inter = jnp.exp(a_inter - m_q) * scale
    sc = jnp.einsum("hqd,hds->hqs", q3, kt3, preferred_element_type=F32)
    p = jnp.exp(dm - m_q) * (sc * scale)
    ct = ct_ref[...]
    qc = jnp.einsum("hqd,hdv->hqv", q3, ct.astype(BF16), preferred_element_type=F32)
    v_ext = jnp.concatenate([v3, jnp.ones((H, L, LANES), BF16)], axis=-1)
    pv = jnp.einsum("hqs,hsv->hqv", p.astype(BF16), v_ext, preferred_element_type=F32)
    den = pv[:, :, dv:] + inter * qc[:, :, dv:]
    rden = 1.0 / jnp.maximum(jnp.abs(den), jnp.exp(-m_q))
    hout = (pv[:, :, :dv] + tile(inter, dv) * qc[:, :, :dv]) * tile(rden, dv)
    for h in heads:
        o_ref[:, h * dv:(h + 1) * dv] = hout[h]

    wl = b_end - b_b + i_b
    m_next = jnp.maximum(b_end + m_st, jnp.max(wl, axis=1, keepdims=True))
    decay = jnp.exp(b_end + m_st - m_next)
    w_b = jnp.exp(wl - m_next)
    vw = jnp.concatenate([v3.astype(F32) * tile(w_b, dv), w_b], axis=-1).astype(BF16)
    upd = jnp.einsum("hdl,hlv->hdv", kt3, vw, preferred_element_type=F32)
    ct_ref[...] = tile(decay, dv + LANES) * ct + upd
    m_ref[...] = jnp.broadcast_to(m_next, m_ref.shape)


def _scan(u, kt, gates, n_batch, seq, ctx_len, n_heads, dqk, dv):
    L = SCAN_CHUNK
    nx = n_batch * seq
    ncc = ctx_len // L
    nlc = seq // L
    assert ctx_len % L == 0 and seq % L == 0
    hq = n_heads * dqk
    hv = n_heads * dv
    assert hv == 2 * hq

    def row_blk(b, d, s):
        ctx = (nx + b * ctx_len) // L + jnp.where(d == 0, s, ncc - 1 - s)
        sl = s - ncc
        lat = (b * seq) // L + jnp.where(d == 0, sl, nlc - 1 - sl)
        return jnp.where(s < ncc, ctx, lat)

    def out_blk(b, d, s):
        sl = jnp.maximum(s - ncc, 0)
        return (b * seq) // L + jnp.where(d == 0, sl, nlc - 1 - sl)

    return pl.pallas_call(
        functools.partial(_scan_kernel, n_heads, dqk, dv),
        grid=(n_batch, 2, ncc + nlc),
        in_specs=[
            pl.BlockSpec((L, hq), lambda b, d, s: (row_blk(b, d, s), 2 * hv // hq)),
            pl.BlockSpec((hq, L), lambda b, d, s: (0, row_blk(b, d, s))),
            pl.BlockSpec((L, hv), lambda b, d, s: (row_blk(b, d, s), 0)),
            pl.BlockSpec((L, LANES), lambda b, d, s: (row_blk(b, d, s), d)),
        ],
        out_specs=pl.BlockSpec((None, L, hv), lambda b, d, s: (d, out_blk(b, d, s), 0)),
        out_shape=jax.ShapeDtypeStruct((2, nx, hv), F32),
        scratch_shapes=[pltpu.VMEM((n_heads, dqk, dv + LANES), F32),
                        pltpu.VMEM((n_heads, SUBLANES, LANES), F32)],
        compiler_params=_cparams(("arbitrary", "arbitrary", "arbitrary")),
        name="mlstm_scan",
    )(u, kt, u, gates)


def _mix_epilogue(alpha, rs, a, x_ref, g1_ref, sh2_ref, sc2_ref, wout_ref, lng_ref,
                  lnb_ref, rwt_ref, x1_ref, hx2_ref, logt_ref):
    mx = jnp.dot(a, wout_ref[...], preferred_element_type=F32)
    x1 = _layer_norm_rows(alpha * x_ref[rs, :] + g1_ref[0] * mx, lng_ref[...], lnb_ref[...])
    x1_ref[rs, :] = x1
    hx2 = x1 * (1.0 + sc2_ref[0]) + sh2_ref[0]
    hx2_ref[rs, :] = hx2
    logt_ref[:, rs] = lax.dot_general(rwt_ref[...], hx2.astype(BF16), (((1,), (1,)), ((), ())),
                                      preferred_element_type=F32)


def _sub_rows(tm, sub):
    return [slice(r, r + sub) for r in range(0, tm, sub)]


def _conv_out_kernel(alpha, ctx_mode, n_lat, tpb, sub, z_ref, zp_ref, zn_ref, bg_ref, cw_ref,
                     x_ref, g1_ref, sh2_ref, sc2_ref, wout_ref, lng_ref, lnb_ref, rwt_ref, *rest):
    x1_ref, hx2_ref, logt_ref = rest[-3:]
    i = pl.program_id(0)
    tm, d = z_ref.shape
    half = d // 2
    rows = lax.broadcasted_iota(I32, (sub, 1), 0)

    def conv1(zz, w3, period):
        pos = rows & (period - 1)
        prev = pltpu.roll(zz, 1, 0) * (pos != 0).astype(F32)
        nxt = pltpu.roll(zz, sub - 1, 0) * (pos != period - 1).astype(F32)
        return w3[0:1] * prev + w3[1:2] * zz + w3[2:3] * nxt

    def tile():
        cw = cw_ref[...]
        if not ctx_mode:
            ti = i % tpb
            up = zp_ref[...].astype(F32) * (ti > 0).astype(F32)
            dn = zn_ref[...].astype(F32) * (ti < tpb - 1).astype(F32)
            w3 = cw[:, half:]

        for rs in _sub_rows(tm, sub):
            z = z_ref[rs, :].astype(F32)
            bg = bg_ref[rs, :].astype(F32)
            if ctx_mode:
                a = (bg * conv1(z, cw, sub)).astype(BF16)
            else:
                a_row = (bg[:, :half] * conv1(z[:, :half], cw[:, :half], GRID_W)).astype(BF16)
                r0, r1 = rs.start, rs.stop
                f32_rows = lambda a, b: z_ref[a:b, half:].astype(F32)
                prev = (jnp.concatenate([up, f32_rows(0, r1 - GRID_W)], axis=0) if r0 == 0
                        else f32_rows(r0 - GRID_W, r1 - GRID_W))
                nxt = (jnp.concatenate([f32_rows(r0 + GRID_W, tm), dn], axis=0) if r1 == tm
                       else f32_rows(r0 + GRID_W, r1 + GRID_W))
                y = w3[0:1] * prev + w3[1:2] * z[:, half:] + w3[2:3] * nxt
                a = jnp.concatenate([a_row, (bg[:, half:] * y).astype(BF16)], axis=1)
            _mix_epilogue(alpha, rs, a, x_ref, g1_ref, sh2_ref, sc2_ref, wout_ref, lng_ref,
                          lnb_ref, rwt_ref, x1_ref, hx2_ref, logt_ref)

    if ctx_mode:
        tile()
    else:
        pl.when(i < n_lat)(tile)

        @pl.when(i >= n_lat)
        def _():
            x1_ref[...] = jnp.zeros_like(x1_ref)
            hx2_ref[...] = jnp.zeros_like(hx2_ref)
            logt_ref[...] = jnp.zeros_like(logt_ref)


def _ml_out_kernel(alpha, n_heads, dv, sub, hf_ref, hb_ref, og_ref, ng_ref, x_ref,
                   g1_ref, sh2_ref, sc2_ref, wout_ref, lng_ref, lnb_ref, rwt_ref,
                   x1_ref, hx2_ref, logt_ref):
    for rs in _sub_rows(hf_ref.shape[0], sub):
        parts = []
        for h in range(n_heads):
            sl = slice(h * dv, (h + 1) * dv)
            hs = hf_ref[rs, sl] + hb_ref[rs, sl]
            mu = jnp.mean(hs, axis=-1, keepdims=True)
            c = hs - mu
            var = jnp.mean(c * c, axis=-1, keepdims=True)
            hn = c * lax.rsqrt(var + HEAD_NORM_EPS)
            gate = jax.nn.sigmoid(og_ref[rs, sl].astype(F32))
            parts.append((hn * ng_ref[:, sl] * gate).astype(BF16))
        _mix_epilogue(alpha, rs, jnp.concatenate(parts, axis=1), x_ref, g1_ref, sh2_ref,
                      sc2_ref, wout_ref, lng_ref, lnb_ref, rwt_ref, x1_ref, hx2_ref, logt_ref)


def _mix_out_common(d, n_rows, n_exp, mod_row, tile0=0):
    tm = ROW_TILE
    mspec = lambda chunk: pl.BlockSpec((1, 1, d), lambda i: (mod_row(i), 0, chunk))
    const = lambda shape: pl.BlockSpec(shape, lambda i: (0,) * len(shape))
    in_specs = [mspec(2), mspec(3), mspec(4), const((d, d)), const((1, d)), const((1, d)),
                const((n_exp, d))]
    out_specs = [pl.BlockSpec((tm, d), lambda i: (i + tile0, 0)),
                 pl.BlockSpec((tm, d), lambda i: (i + tile0, 0)),
                 pl.BlockSpec((n_exp, tm), lambda i: (0, i + tile0))]
    out_shape = [jax.ShapeDtypeStruct((n_rows, d), F32),
                 jax.ShapeDtypeStruct((n_rows, d), F32),
                 jax.ShapeDtypeStruct((n_exp, n_rows), F32)]
    return in_specs, out_specs, out_shape


def _conv_out(alpha, z, bg, conv_w, x2d, c2d, mod, w_out_b, ln_g, ln_b, rwt_b, n_batch, seq,
              ctx_len):
    n_all, d = z.shape
    nx = x2d.shape[0]
    tm = ROW_TILE
    assert ctx_len == tm and seq % tm == 0 and MIX_SUB_ROWS % GRID_W == 0
    half = d // 2
    nxa = nx // tm
    tpb = seq // tm
    hpt = tm // GRID_W
    n_exp = rwt_b.shape[0]

    def z_specs(tile0):
        return [
            pl.BlockSpec((tm, d), lambda i: (i + tile0, 0)),
            pl.BlockSpec((GRID_W, half), lambda i: (jnp.maximum((i + tile0) * hpt - 1, 0), 1)),
            pl.BlockSpec((GRID_W, half), lambda i: ((i + tile0 + 1) * hpt - 1, 1)),
            pl.BlockSpec((tm, d), lambda i: (i + tile0, 0)),
            pl.BlockSpec((3, d), lambda i: (0, 0)),
            pl.BlockSpec((tm, d), lambda i: (i, 0)),
        ]

    lat = lambda i: jnp.minimum(i, nxa - 1)
    common_in, out_specs, out_shape = _mix_out_common(d, n_all, n_exp, lambda i: lat(i) // tpb)
    lat_specs = z_specs(0)
    lat_specs[2] = pl.BlockSpec((GRID_W, half), lambda i: ((lat(i) + 1) * hpt, 1))
    lat_specs[5] = pl.BlockSpec((tm, d), lambda i: (lat(i), 0))
    outs = pl.pallas_call(
        functools.partial(_conv_out_kernel, alpha, False, nxa, tpb, MIX_SUB_ROWS),
        grid=(n_all // tm,),
        in_specs=lat_specs + common_in,
        out_specs=out_specs,
        out_shape=out_shape,
        compiler_params=_cparams(("arbitrary",)),
        name="conv_out",
    )(z, z, z, bg, conv_w, x2d, mod, mod, mod, w_out_b, ln_g, ln_b, rwt_b)

    common_in, out_specs, out_shape = _mix_out_common(d, n_all, n_exp, lambda i: n_batch, nxa)
    n_in = 6 + len(common_in)
    keep = [pl.BlockSpec(memory_space=pl.ANY)] * 3
    return pl.pallas_call(
        functools.partial(_conv_out_kernel, alpha, True, None, tpb, tm),
        grid=(c2d.shape[0] // tm,),
        in_specs=z_specs(nxa) + common_in + keep,
        out_specs=out_specs,
        out_shape=out_shape,
        input_output_aliases={n_in: 0, n_in + 1: 1, n_in + 2: 2},
        compiler_params=_cparams(("arbitrary",)),
        name="conv_out_ctx",
    )(z, z, z, bg, conv_w, c2d, mod, mod, mod, w_out_b, ln_g, ln_b, rwt_b, *outs)


def _ml_out(alpha, hfb, u, norm_g, xall, mod, w_out_b, ln_g, ln_b, rwt_b, n_batch, seq,
            n_heads, dv):
    nx, d = hfb.shape[1:]
    tm = ROW_TILE
    tpb = seq // tm
    common_in, out_specs, out_shape = _mix_out_common(d, nx, rwt_b.shape[0], lambda i: i // tpb)
    o_blk = 1
    in_specs = [
        pl.BlockSpec((None, tm, d), lambda i: (0, i, 0)),
        pl.BlockSpec((None, tm, d), lambda i: (1, i, 0)),
        pl.BlockSpec((tm, d), lambda i: (i, o_blk)),
        pl.BlockSpec((1, d), lambda i: (0, 0)),
        pl.BlockSpec((tm, d), lambda i: (i, 0)),
    ] + common_in
    return pl.pallas_call(
        functools.partial(_ml_out_kernel, alpha, n_heads, dv, tm),
        grid=(nx // tm,),
        in_specs=in_specs,
        out_specs=out_specs,
        out_shape=out_shape,
        compiler_params=_cparams(("arbitrary",)),
        name="mlstm_out",
    )(hfb, hfb, u, norm_g, xall, mod, mod, mod, w_out_b, ln_g, ln_b, rwt_b)


def _route_kernel(tile_rows, n_te, logt_ref, rb_ref, pos_ref, wcol_ref, te_ref, meta_ref,
                  carry_ref, before_ref):
    ph = pl.program_id(0)
    i = pl.program_id(1)
    n_steps = pl.num_programs(1)
    n_exp, tr = logt_ref.shape
    epg = n_exp // N_GROUPS

    @pl.when((ph == 0) & (i == 0))
    def _():
        carry_ref[...] = jnp.zeros_like(carry_ref)
        ti = lax.broadcasted_iota(I32, (tr, tr), 0)
        tj = lax.broadcasted_iota(I32, (tr, tr), 1)
        before_ref[...] = (ti < tj).astype(BF16)

    s = jax.nn.sigmoid(logt_ref[...])
    sel = s + rb_ref[...]
    row = lax.broadcasted_iota(I32, (n_exp, tr), 0)
    member = row % epg
    group = row // epg

    def partner(x, k, idx, span, unit):
        wrapped = (idx + k) >= span
        up = pltpu.roll(x, n_exp - k * unit, 0)
        down = pltpu.roll(x, (span - k) * unit, 0)
        return jnp.where(wrapped, down, up), wrapped

    rank_in = jnp.zeros((n_exp, tr), F32)
    for k in range(1, epg):
        p, wrapped = partner(sel, k, member, epg, 1)
        beats = (p > sel) | ((p == sel) & wrapped)
        rank_in = rank_in + beats.astype(F32)
    top = (rank_in < TOP_K).astype(F32)
    gs = sel * top
    score = gs
    for k in range(1, epg):
        p, _ = partner(gs, k, member, epg, 1)
        score = score + p
    n_better = jnp.zeros((n_exp, tr), F32)
    for k in range(1, N_GROUPS):
        p, wrapped = partner(score, k, group, N_GROUPS, epg)
        beats = (p > score) | ((p == score) & wrapped)
        n_better = n_better + beats.astype(F32)
    best = n_better == 0.0
    m1 = (best & (rank_in == 0.0)).astype(F32)
    m2 = (best & (rank_in == 1.0)).astype(F32)
    s1 = jnp.sum(s * m1, axis=0, keepdims=True)
    s2 = jnp.sum(s * m2, axis=0, keepdims=True)
    den = s1 + s2
    oh = m1 + m2

    slot = jnp.dot(oh.astype(BF16), before_ref[...], preferred_element_type=F32) + carry_ref[...]

    @pl.when(ph == 1)
    def _():
        pos_ref[0:1, :] = jnp.sum(m1 * slot, axis=0, keepdims=True).astype(I32)
        pos_ref[1:2, :] = jnp.sum(m2 * slot, axis=0, keepdims=True).astype(I32)
        w8 = jnp.concatenate([s1 / den, s2 / den, jnp.zeros((LANES - 2, tr), F32)], axis=0)
        wcol_ref[...] = w8.T

    carry_ref[...] = carry_ref[...] + jnp.sum(oh, axis=1, keepdims=True)

    @pl.when((ph == 0) & (i == n_steps - 1))
    def _():
        cnt = carry_ref[...]
        ntile = jnp.floor((cnt + (tile_rows - 1)) * (1.0 / tile_rows))
        er = lax.broadcasted_iota(I32, (n_exp, LANES), 0)
        el = lax.broadcasted_iota(I32, (n_exp, LANES), 1)
        eye = (er == el).astype(F32)
        nt_row = jnp.sum(ntile * eye, axis=0, keepdims=True)
        cnt_row = jnp.sum(cnt * eye, axis=0, keepdims=True)
        cum_excl = jnp.sum(nt_row * (el < er).astype(F32), axis=1, keepdims=True)
        off = cum_excl * tile_rows
        off_row = jnp.sum(off * eye, axis=0, keepdims=True)
        total = jnp.sum(nt_row, axis=1, keepdims=True)
        cum_incl = cum_excl + ntile
        tl = lax.broadcasted_iota(I32, (n_exp, n_te), 1).astype(F32)
        te = jnp.sum((cum_incl <= tl).astype(F32), axis=0, keepdims=True)
        elf = el.astype(F32)
        later = (el > er) & (nt_row > 0.0) & (el < n_exp)
        nxt = jnp.min(jnp.where(later, elf, float(n_exp)), axis=1, keepdims=True)
        nxt = jnp.where(nxt == float(n_exp), er[:, 0:1].astype(F32), nxt)
        mine = (cum_excl <= tl) & (tl < cum_incl)
        te_next = jnp.sum(jnp.where(mine, nxt, 0.0), axis=0, keepdims=True)
        te_ref[...] = jnp.concatenate([jnp.minimum(te, n_exp - 1), te_next], axis=0).astype(I32)
        meta_ref[...] = jnp.concatenate(
            [cnt_row, off_row, jnp.broadcast_to(total, (1, LANES)),
             jnp.zeros((SUBLANES - 3, LANES), F32)], axis=0).astype(I32)
        carry_ref[...] = off


def _route(logt, router_b, tile_rows):
    n_exp, n = logt.shape
    tr = ROUTE_COLS
    while n % tr:
        tr //= 2
    n_te = 256
    assert (2 * n) // tile_rows + n_exp <= n_te
    return pl.pallas_call(
        functools.partial(_route_kernel, tile_rows, n_te),
        grid=(2, n // tr),
        in_specs=[pl.BlockSpec((n_exp, tr), lambda p, i: (0, i)),
                  pl.BlockSpec((n_exp, 1), lambda p, i: (0, 0))],
        out_specs=[pl.BlockSpec((2, tr), lambda p, i: (0, i * p)),
                   pl.BlockSpec((tr, LANES), lambda p, i: (i * p, 0)),
                   pl.BlockSpec((2, n_te), lambda p, i: (0, 0)),
                   pl.BlockSpec((SUBLANES, LANES), lambda p, i: (0, 0))],
        out_shape=[jax.ShapeDtypeStruct((2, n), I32),
                   jax.ShapeDtypeStruct((n, LANES), F32),
                   jax.ShapeDtypeStruct((2, n_te), I32),
                   jax.ShapeDtypeStruct((SUBLANES, LANES), I32)],
        scratch_shapes=[pltpu.VMEM((n_exp, 1), F32), pltpu.VMEM((tr, tr), BF16)],
        compiler_params=_cparams(("arbitrary", "arbitrary")),
        name="route",
    )(logt, router_b.reshape(n_exp, 1).astype(F32))


def _tbl_kernel(n_tok, n_exp, tile_rows, meta_ref, pos_ref, tbl_ref):
    i = pl.program_id(0)
    blk = pos_ref.shape[1]

    @pl.when(i == 0)
    def _():
        n_pad = 0
        for e in range(n_exp):
            cnt = meta_ref[e]
            off = meta_ref[n_exp + e]
            up = ((cnt + (tile_rows - 1)) // tile_rows) * tile_rows

            def fill(r, j):
                tbl_ref[off + r] = (2 * n_tok + j) << SLOT_SRC_BITS
                return j + 1

            n_pad = lax.fori_loop(cnt, up, fill, n_pad)

        def fill_tail(r, carry):
            tbl_ref[r] = (2 * n_tok) << SLOT_SRC_BITS
            return carry

        lax.fori_loop(2 * n_tok + n_pad, tbl_ref.shape[0], fill_tail, 0)

    base = i * blk
    first = (base << SLOT_SRC_BITS) | jnp.where(base >= n_tok, base - n_tok, base)
    step = (1 << SLOT_SRC_BITS) + 1
    group = 8

    def body(q, carry):
        r0 = q * group
        slots = [pos_ref[0, r0 + j] for j in range(group)]
        for j in range(group):
            tbl_ref[slots[j]] = first + (r0 + j) * step
        return carry

    lax.fori_loop(0, blk // group, body, 0, unroll=True)


def _build_table(meta1d, pos, n_tok, n_exp, tile_rows, p_pad):
    blk = TBL_BLOCK
    while n_tok % blk:
        blk //= 2
    n_pairs = 2 * n_tok
    grid_spec = pltpu.PrefetchScalarGridSpec(
        num_scalar_prefetch=1,
        grid=(n_pairs // blk,),
        in_specs=[pl.BlockSpec((None, 1, blk), lambda i, m: (i, 0, 0), memory_space=pltpu.SMEM)],
        out_specs=pl.BlockSpec(memory_space=pltpu.SMEM),
    )
    return pl.pallas_call(
        functools.partial(_tbl_kernel, n_tok, n_exp, tile_rows),
        grid_spec=grid_spec,
        out_shape=jax.ShapeDtypeStruct((p_pad,), I32),
        compiler_params=_cparams(("arbitrary",)),
        name="slot_table",
    )(meta1d, pos.reshape(n_pairs // blk, 1, blk))


def _moe_kernel(layer, te_ref, ten_ref, nt_ref, tbl_ref, tbln_ref, tblp_ref, hx_hbm,
                wg_hbm, wu_hbm, wd_hbm, y_hbm, xbuf, ybuf, zbuf, wg_b, wu_b, wd_b,
                wg_s, wu_s, wd_s, gsem, ssem, zsem, wsem):
    t = pl.program_id(0)
    n = nt_ref[0]
    rows = tbl_ref.shape[1]
    lt = hx_hbm.shape[1]
    staged = ((wg_hbm, wg_s, wg_b), (wu_hbm, wu_s, wu_b), (wd_hbm, wd_s, wd_b))

    def weight_copies(e):
        return [pltpu.make_async_copy(hbm.at[layer, e], stage, wsem.at[k])
                for k, (hbm, stage, _) in enumerate(staged)]

    @pl.when(t == 0)
    def _():
        for c in weight_copies(te_ref[0]):
            c.start(priority=WEIGHT_DMA_PRIORITY)

    tl = jnp.minimum(t, n - 1)
    new_expert = (t < n) & ((t == 0) | (te_ref[tl] != te_ref[jnp.maximum(tl - 1, 0)]))

    @pl.when(new_expert)
    def _():
        for c, (_, stage, dst) in zip(weight_copies(te_ref[tl]), staged):
            c.wait()
            dst[...] = stage[...].astype(BF16)
        for c in weight_copies(ten_ref[tl]):
            c.start(priority=WEIGHT_DMA_PRIORITY)

    def start_gather(tref, sl, lo=0, hi=None):
        for r in range(lo, rows if hi is None else hi):
            src = tref[0, r] & SLOT_SRC_MASK
            pltpu.make_async_copy(hx_hbm.at[src], xbuf.at[sl, pl.ds(r * lt, lt), :],
                                  gsem.at[sl]).start()

    def start_scatter(tref, sl, lo=0, hi=None):
        for r in range(lo, rows if hi is None else hi):
            dst = tref[0, r] >> SLOT_SRC_BITS
            pltpu.make_async_copy(ybuf.at[sl, pl.ds(r, 1), :],
                                  y_hbm.at[pl.ds(dst, 1), :], ssem.at[sl]).start()

    def wait_gather(sl):
        pltpu.make_async_copy(xbuf.at[sl], xbuf.at[sl], gsem.at[sl]).wait()

    def wait_scatter(sl):
        pltpu.make_async_copy(ybuf.at[sl], ybuf.at[sl], ssem.at[sl]).wait()

    @pl.when(t == 0)
    def _():
        zbuf[...] = jnp.zeros_like(zbuf)
        start_gather(tbl_ref, 0)

    @pl.when(t >= n)
    def _():
        dst = y_hbm.at[pl.ds(pl.multiple_of(t * rows, rows), rows), :]
        fill = pltpu.make_async_copy(zbuf, dst, zsem)

        @pl.when(t > n)
        def _():
            fill.wait()

        fill.start()

        @pl.when(t == pl.num_programs(0) - 1)
        def _():
            fill.wait()

    def tile(slot, has_prev):
        other = 1 - slot
        wait_gather(slot)
        f = wg_b.shape[1]
        d = wd_b.shape[1]
        fc, dc = MOE_UP_COLS, MOE_DOWN_COLS
        n_up, n_down = f // fc, d // dc
        g_bounds = [(k * rows) // n_up for k in range(n_up + 1)]
        s_bounds = [(k * rows) // n_down for k in range(n_down + 1)]

        def copy_group(k):
            if k < n_up:
                start_gather(tbln_ref, other, g_bounds[k], g_bounds[k + 1])
            elif has_prev:
                start_scatter(tblp_ref, other, s_bounds[k - n_up], s_bounds[k - n_up + 1])

        xb = jnp.concatenate(
            [xbuf[slot, pl.ds(c, rows, stride=lt), :].astype(BF16) for c in range(lt)], axis=1)
        hs = []
        for c in range(f // fc):
            cols = slice(c * fc, (c + 1) * fc)
            g = jnp.dot(xb, wg_b[:, cols], preferred_element_type=F32)
            u = jnp.dot(xb, wu_b[:, cols], preferred_element_type=F32)
            hs.append((g * jax.nn.sigmoid(g) * u).astype(BF16))
            copy_group(c)
        h = jnp.concatenate(hs, axis=1)

        @pl.when(t >= 2)
        def _():
            wait_scatter(slot)

        for c in range(d // dc):
            cols = slice(c * dc, (c + 1) * dc)
            ybuf[slot, :, cols] = jnp.dot(h, wd_b[:, cols], preferred_element_type=F32)
            copy_group(f // fc + c)

        @pl.when(t == n - 1)
        def _():
            start_scatter(tbl_ref, slot)
            wait_gather(other)

            @pl.when(t >= 1)
            def _():
                wait_scatter(other)

            wait_scatter(slot)
            for c in weight_copies(ten_ref[tl]):
                c.wait()

    pl.when((t < n) & (t == 0))(functools.partial(tile, 0, False))
    for parity in (0, 1):
        pl.when((t < n) & (t > 0) & (lax.rem(t, 2) == parity))(
            functools.partial(tile, parity, True))


def _moe(te2, nt1d, tbl, hx2, w_gate, w_up, w_down, layer, n_tok, p_pad):
    tmm = MOE_ROWS
    _, n_exp, d, f = w_gate.shape
    t_max = (2 * n_tok) // tmm + n_exp
    last = lambda t, nt: jnp.minimum(t, nt[0] - 1)
    tspec = lambda fn: pl.BlockSpec((None, 1, tmm), lambda t, te, ten, nt: (fn(t, nt), 0, 0),
                                    memory_space=pltpu.SMEM)
    hbm = pl.BlockSpec(memory_space=pl.ANY)
    grid_spec = pltpu.PrefetchScalarGridSpec(
        num_scalar_prefetch=3,
        grid=(t_max,),
        in_specs=[
            tspec(last),
            tspec(lambda t, nt: last(t + 1, nt)),
            tspec(lambda t, nt: jnp.maximum(last(t, nt) - 1, 0)),
            hbm, hbm, hbm, hbm,
        ],
        out_specs=hbm,
        scratch_shapes=[pltpu.VMEM((2, tmm * (d // LANES), LANES), F32),
                        pltpu.VMEM((2, tmm, d), F32),
                        pltpu.VMEM((tmm, d), F32),
                        pltpu.VMEM((d, f), BF16), pltpu.VMEM((d, f), BF16),
                        pltpu.VMEM((f, d), BF16),
                        pltpu.VMEM((d, f), F32), pltpu.VMEM((d, f), F32),
                        pltpu.VMEM((f, d), F32),
                        pltpu.SemaphoreType.DMA((2,)), pltpu.SemaphoreType.DMA((2,)),
                        pltpu.SemaphoreType.DMA(()), pltpu.SemaphoreType.DMA((3,))],
    )
    tbl3 = tbl.reshape(p_pad // tmm, 1, tmm)
    return pl.pallas_call(
        functools.partial(_moe_kernel, layer),
        grid_spec=grid_spec,
        out_shape=jax.ShapeDtypeStruct((t_max * tmm, d), F32),
        compiler_params=_cparams(("arbitrary",)),
        name="moe_experts",
    )(te2[0], te2[1], nt1d, tbl3, tbl3, tbl3, hx2.reshape(n_tok, d // LANES, LANES),
      w_gate, w_up, w_down)


def _comb_kernel(alpha, x_ref, y0_ref, y1_ref, w_ref, g2_ref, lng_ref, lnb_ref, o_ref):
    w = w_ref[...]
    ex = w[:, 0:1] * y0_ref[...] + w[:, 1:2] * y1_ref[...]
    o_ref[...] = _layer_norm_rows(alpha * x_ref[...] + g2_ref[0] * ex, lng_ref[...], lnb_ref[...])


def _combine(alpha, x1, y, wcol, mod, ln_g, ln_b, n_tok, n_out, n_batch, seq):
    d = x1.shape[1]
    tm = COMB_ROWS
    assert n_tok % tm == 0 and n_out % tm == 0 and seq % tm == 0
    nblk = n_tok // tm
    mspec = functools.partial(_mod_spec, d=d, tiles_per_batch=seq // tm, n_batch=n_batch)
    return pl.pallas_call(
        functools.partial(_comb_kernel, alpha),
        grid=(n_out // tm,),
        in_specs=[
            pl.BlockSpec((tm, d), lambda i: (i, 0)),
            pl.BlockSpec((tm, d), lambda i: (i, 0)),
            pl.BlockSpec((tm, d), lambda i: (i + nblk, 0)),
            pl.BlockSpec((tm, LANES), lambda i: (i, 0)),
            mspec(5),
            pl.BlockSpec((1, d), lambda i: (0, 0)),
            pl.BlockSpec((1, d), lambda i: (0, 0)),
        ],
        out_specs=pl.BlockSpec((tm, d), lambda i: (i, 0)),
        out_shape=jax.ShapeDtypeStruct((n_out, d), F32),
        compiler_params=_cparams(("arbitrary",)),
        name="moe_combine",
    )(x1, y, y, wcol, mod, ln_g, ln_b)


def _moe_layer(alpha, x1, hx2, logt, mod, ln_g, ln_b, router_b, w_gate, w_up, w_down, layer,
               n_out, n_batch, seq):
    n_tok = hx2.shape[0]
    n_exp = w_gate.shape[1]
    tmm = MOE_ROWS
    p_max = 2 * n_tok + n_exp * tmm
    p_pad = -(-p_max // TBL_BLOCK) * TBL_BLOCK
    assert n_tok <= 1 << SLOT_SRC_BITS and p_max < 1 << (31 - SLOT_SRC_BITS)
    pos, wcol, te, meta = _route(logt, router_b, tmm)
    meta1d = meta[:2, :n_exp].reshape(-1)
    tbl = _build_table(meta1d, pos, n_tok, n_exp, tmm, p_pad)
    y = _moe(te, meta[2, :1], tbl, hx2, w_gate, w_up, w_down, layer, n_tok, p_pad)
    return _combine(alpha, x1, y, wcol, mod, ln_g, ln_b, n_tok, n_out, n_batch, seq)


def kernel(x, c, ctx, c_ctx, w_ada, b_ada, ln_g, ln_b, conv_w_in, conv_w, conv_w_out, ml_w_in, ml_w_gate, ml_b_gate, ml_norm_g, ml_w_out, router_w, router_b, exp_w_gate, exp_w_up, exp_w_down):
    n_batch, seq, d = x.shape
    ctx_len = ctx.shape[1]
    depth = w_ada.shape[0]
    assert depth == 2, "layer 0 is the conv mixer, layer 1 the mLSTM mixer"
    alpha = (2 * depth) ** 0.25
    n_heads = ml_b_gate.shape[-1] // 4
    dqk = d // (2 * n_heads)
    dv = d // n_heads
    nx = n_batch * seq
    nc = n_batch * ctx_len
    assert n_batch < SUBLANES and 2 * n_heads <= LANES

    x2d = x.reshape(nx, d)
    c2d = ctx.reshape(nc, d)
    cc = jnp.zeros((SUBLANES, d), F32).at[:n_batch].set(c).at[n_batch].set(c_ctx)
    mod = _ada(cc, w_ada, b_ada).reshape(depth, SUBLANES, 1, 6 * d)
    rwt_b = router_w.T.astype(BF16)

    bg, z = _conv_in(x2d, c2d, mod[0], _to_bf16(conv_w_in, 0), n_batch, seq)
    x1, hx2, logt = _conv_out(alpha, z, bg, conv_w[0], x2d, c2d, mod[0],
                              _to_bf16(conv_w_out, 0), ln_g[0, 0:1], ln_b[0, 0:1], rwt_b,
                              n_batch, seq, ctx_len)
    xall = _moe_layer(alpha, x1, hx2, logt, mod[0], ln_g[0, 1:2], ln_b[0, 1:2], router_b,
                      exp_w_gate, exp_w_up, exp_w_down, 0, nx + nc, n_batch, seq)

    w_gate = ml_w_gate[0]
    b_gate = ml_b_gate[0]
    h2 = 2 * n_heads
    w_gate_pad = (jnp.zeros((d, 2 * LANES), F32).at[:, :h2].set(w_gate[:, :h2])
                  .at[:, LANES:LANES + h2].set(w_gate[:, h2:])).astype(BF16)
    b_gate_pad = (jnp.zeros((1, 2 * LANES), F32).at[0, :h2].set(b_gate[:h2])
                  .at[0, LANES:LANES + h2].set(b_gate[h2:]))
    w_voq_b, w_kt_b = _ml_weights(ml_w_in, 0, n_heads * dqk)
    u, kt, gates = _ml_in(xall, mod[1], w_voq_b, w_kt_b, w_gate_pad, b_gate_pad, n_batch,
                          seq, nc)
    hfb = _scan(u, kt, gates, n_batch, seq, ctx_len, n_heads, dqk, dv)
    x1, hx2, logt = _ml_out(alpha, hfb, u, ml_norm_g[0:1], xall, mod[1],
                            _to_bf16(ml_w_out, 0), ln_g[1, 0:1], ln_b[1, 0:1], rwt_b,
                            n_batch, seq, n_heads, dv)
    out = _moe_layer(alpha, x1, hx2, logt, mod[1], ln_g[1, 1:2], ln_b[1, 1:2], router_b,
                     exp_w_gate, exp_w_up, exp_w_down, 1, nx, n_batch, seq)
    return out.reshape(n_batch, seq, d)
```

```python
import functools

import jax
import jax.numpy as jnp
from jax import lax
from jax.experimental import pallas as pl
from jax.experimental.pallas import tpu as pltpu

F32 = jnp.float32
BF16 = jnp.bfloat16
I32 = jnp.int32

GRID_W = 64
N_GROUPS = 4
TOP_K = 2
LN_EPS = 1e-5
HEAD_NORM_EPS = 1e-6

LANES = 128
SUBLANES = 8
VMEM_LIMIT_BYTES = 56 * 1024 * 1024

ROW_TILE = 256
MIX_SUB_ROWS = 128
PROJ_ROWS = 1024
MOE_ROWS = 256
MOE_UP_COLS = 256
MOE_DOWN_COLS = 512
ROUTE_COLS = 1024
COMB_ROWS = 512
SCAN_CHUNK = 128
TBL_BLOCK = 1024
SLOT_SRC_BITS = 15
SLOT_SRC_MASK = (1 << SLOT_SRC_BITS) - 1
CAST_BLOCK_BYTES = 8 * 1024 * 1024
WEIGHT_DMA_PRIORITY = 1
SCATTER_DMA_PRIORITY = 1


def _cparams(sem):
    return pltpu.CompilerParams(dimension_semantics=sem, vmem_limit_bytes=VMEM_LIMIT_BYTES)


def _layer_norm_rows(r, g, b):
    mu = jnp.mean(r, axis=-1, keepdims=True)
    c = r - mu
    var = jnp.mean(c * c, axis=-1, keepdims=True)
    return c * lax.rsqrt(var + LN_EPS) * g + b


def _ada_kernel(cc_ref, w_ref, b_ref, o_ref):
    a = cc_ref[...]
    a = (a * jax.nn.sigmoid(a)).astype(BF16)
    o_ref[0] = jnp.dot(a, w_ref[0].astype(BF16), preferred_element_type=F32) + b_ref[0]


def _ada(cc, w_ada, b_ada):
    depth, d, n6 = w_ada.shape
    tn = 1024
    return pl.pallas_call(
        _ada_kernel,
        grid=(depth, n6 // tn),
        in_specs=[
            pl.BlockSpec((SUBLANES, d), lambda l, j: (0, 0)),
            pl.BlockSpec((1, d, tn), lambda l, j: (l, 0, j)),
            pl.BlockSpec((1, 1, tn), lambda l, j: (l, 0, j)),
        ],
        out_specs=pl.BlockSpec((1, SUBLANES, tn), lambda l, j: (l, 0, j)),
        out_shape=jax.ShapeDtypeStruct((depth, SUBLANES, n6), F32),
        compiler_params=_cparams(("arbitrary", "arbitrary")),
        name="ada_mod",
    )(cc, w_ada, b_ada.reshape(depth, 1, n6))


def _cast_kernel(w_ref, o_ref):
    o_ref[...] = w_ref[...].astype(BF16)


def _to_bf16(w, layer):
    rows, cols = w.shape[-2:]
    w4 = w.reshape(w.shape[0], -1, rows, cols)
    m = w4.shape[1]
    rb = rows
    while rb * cols * 4 > CAST_BLOCK_BYTES and rb % (4 * SUBLANES) == 0:
        rb //= 2
    out = pl.pallas_call(
        _cast_kernel,
        grid=(m, rows // rb),
        in_specs=[pl.BlockSpec((None, None, rb, cols), lambda e, r: (layer, e, r, 0))],
        out_specs=pl.BlockSpec((None, rb, cols), lambda e, r: (e, r, 0)),
        out_shape=jax.ShapeDtypeStruct((m, rows, cols), BF16),
        compiler_params=_cparams(("arbitrary", "arbitrary")),
        name="to_bf16",
    )(w4)
    return out.reshape(w.shape[1:])


def _cast_t_kernel(w_ref, o_ref):
    o_ref[...] = w_ref[...].T.astype(BF16)


def _ml_weights(w, layer, hq):
    _, d, ncol = w.shape
    nb = ncol // hq
    rb = min(d, 512)
    voq = pl.pallas_call(
        _cast_kernel,
        grid=(d // rb, nb - 1),
        in_specs=[pl.BlockSpec((None, rb, hq), lambda r, k: (layer, r, (k + 2) % nb))],
        out_specs=pl.BlockSpec((rb, hq), lambda r, k: (r, k)),
        out_shape=jax.ShapeDtypeStruct((d, ncol - hq), BF16),
        compiler_params=_cparams(("arbitrary", "arbitrary")),
        name="to_bf16_voq",
    )(w)
    kt = pl.pallas_call(
        _cast_t_kernel,
        grid=(d // rb,),
        in_specs=[pl.BlockSpec((None, rb, hq), lambda r: (layer, r, 1))],
        out_specs=pl.BlockSpec((hq, rb), lambda r: (0, r)),
        out_shape=jax.ShapeDtypeStruct((hq, d), BF16),
        compiler_params=_cparams(("arbitrary",)),
        name="to_bf16_kt",
    )(w)
    return voq, kt


def _mod_spec(chunk, d, tiles_per_batch, n_batch):
    return pl.BlockSpec(
        (1, 1, d),
        lambda i, *_: (jnp.minimum(i // tiles_per_batch, n_batch), 0, chunk))


def _conv_in_kernel(nxa, xa_ref, xb_ref, sh_ref, sc_ref, wb_ref, wc_ref, wv_ref,
                    bg_ref, z_ref, h_ref):
    i = pl.program_id(0)
    j = pl.program_id(1)

    @pl.when((j == 0) & (i < nxa))
    def _():
        h_ref[...] = (xa_ref[...] * (1.0 + sc_ref[0]) + sh_ref[0]).astype(BF16)

    @pl.when((j == 0) & (i >= nxa))
    def _():
        h_ref[...] = (xb_ref[...] * (1.0 + sc_ref[0]) + sh_ref[0]).astype(BF16)

    h = h_ref[...]
    bg = jnp.dot(h, wb_ref[...], preferred_element_type=F32)
    cg = jnp.dot(h, wc_ref[...], preferred_element_type=F32)
    v = jnp.dot(h, wv_ref[...], preferred_element_type=F32)
    bg_ref[...] = bg.astype(BF16)
    z_ref[...] = (cg * v).astype(BF16)


def _conv_in(x2d, c2d, mod, w_in_b, n_batch, seq):
    nx, d = x2d.shape
    nc = c2d.shape[0]
    bm = min(PROJ_ROWS, seq, nc)
    assert seq % bm == 0 and nc % bm == 0
    tn = 512 if d % 512 == 0 else d
    nj = d // tn
    nxa = nx // bm
    n_all = nx + nc
    tpb = seq // bm
    mspec = functools.partial(_mod_spec, d=d, tiles_per_batch=tpb, n_batch=n_batch)
    return pl.pallas_call(
        functools.partial(_conv_in_kernel, nxa),
        grid=(n_all // bm, nj),
        in_specs=[
            pl.BlockSpec((bm, d), lambda i, j: (jnp.minimum(i, nxa - 1), 0)),
            pl.BlockSpec((bm, d), lambda i, j: (jnp.maximum(i - nxa, 0), 0),
                         pipeline_mode=pl.Buffered(1)),
            mspec(0), mspec(1),
            pl.BlockSpec((d, tn), lambda i, j: (0, j)),
            pl.BlockSpec((d, tn), lambda i, j: (0, nj + j)),
            pl.BlockSpec((d, tn), lambda i, j: (0, 2 * nj + j)),
        ],
        out_specs=[
            pl.BlockSpec((bm, tn), lambda i, j: (i, j)),
            pl.BlockSpec((bm, tn), lambda i, j: (i, j)),
        ],
        out_shape=[jax.ShapeDtypeStruct((n_all, d), BF16),
                   jax.ShapeDtypeStruct((n_all, d), BF16)],
        scratch_shapes=[pltpu.VMEM((bm, d), BF16)],
        compiler_params=_cparams(("arbitrary", "arbitrary")),
        name="conv_in",
    )(x2d, c2d, mod, mod, w_in_b, w_in_b, w_in_b)


def _ml_in_kernel(x_ref, sh_ref, sc_ref, w_ref, wkt_ref, wg_ref, bgate_ref,
                  u_ref, kt_ref, g_ref, h_ref):
    j = pl.program_id(1)

    @pl.when(j == 0)
    def _():
        h = (x_ref[...] * (1.0 + sc_ref[0]) + sh_ref[0]).astype(BF16)
        h_ref[...] = h
        g_ref[...] = jnp.dot(h, wg_ref[...], preferred_element_type=F32) + bgate_ref[...]
        kt_ref[...] = lax.dot_general(wkt_ref[...], h, (((1,), (1,)), ((), ())),
                                      preferred_element_type=F32).astype(BF16)

    u_ref[...] = jnp.dot(h_ref[...], w_ref[...], preferred_element_type=F32).astype(BF16)


def _ml_in(xall, mod, w_voq_b, w_kt_b, w_gate_pad, b_gate_pad, n_batch, seq, n_ctx_rows):
    n_all, d = xall.shape
    nu = w_voq_b.shape[1]
    hq = w_kt_b.shape[0]
    bm = min(PROJ_ROWS, seq, n_ctx_rows)
    assert seq % bm == 0 and n_ctx_rows % bm == 0
    tn = hq
    assert nu % tn == 0
    gl = w_gate_pad.shape[1]
    mspec = functools.partial(_mod_spec, d=d, tiles_per_batch=seq // bm, n_batch=n_batch)
    return pl.pallas_call(
        _ml_in_kernel,
        grid=(n_all // bm, nu // tn),
        in_specs=[
            pl.BlockSpec((bm, d), lambda i, j: (i, 0)),
            mspec(0), mspec(1),
            pl.BlockSpec((d, tn), lambda i, j: (0, j)),
            pl.BlockSpec((hq, d), lambda i, j: (0, 0)),
            pl.BlockSpec((d, gl), lambda i, j: (0, 0)),
            pl.BlockSpec((1, gl), lambda i, j: (0, 0)),
        ],
        out_specs=[
            pl.BlockSpec((bm, tn), lambda i, j: (i, j)),
            pl.BlockSpec((hq, bm), lambda i, j: (0, i)),
            pl.BlockSpec((bm, gl), lambda i, j: (i, 0)),
        ],
        out_shape=[jax.ShapeDtypeStruct((n_all, nu), BF16),
                   jax.ShapeDtypeStruct((hq, n_all), BF16),
                   jax.ShapeDtypeStruct((n_all, gl), F32)],
        scratch_shapes=[pltpu.VMEM((bm, d), BF16)],
        compiler_params=_cparams(("arbitrary", "arbitrary")),
        name="mlstm_in",
    )(xall, mod, mod, w_voq_b, w_kt_b, w_gate_pad, b_gate_pad)


def _dot_split3(a_b, x):
    hi = x.astype(BF16)
    r1 = x - hi.astype(F32)
    mid = r1.astype(BF16)
    lo = (r1 - mid.astype(F32)).astype(BF16)
    return (jnp.dot(a_b, hi, preferred_element_type=F32)
            + jnp.dot(a_b, mid, preferred_element_type=F32)
            + jnp.dot(a_b, lo, preferred_element_type=F32))


def _scan_kernel(n_heads, dqk, dv, q_ref, kt_ref, v_ref, g_ref, o_ref, ct_ref, m_ref):
    d = pl.program_id(1)
    s = pl.program_id(2)
    L = q_ref.shape[0]
    assert L == LANES
    scale = dqk ** -0.5

    @pl.when(s == 0)
    def _():
        ct_ref[...] = jnp.zeros_like(ct_ref)
        m_ref[...] = jnp.zeros_like(m_ref)

    H = n_heads
    heads = range(H)
    qi = lax.broadcasted_iota(I32, (L, L), 0)
    si = lax.broadcasted_iota(I32, (L, L), 1)
    fwd = d == 0
    mask = jnp.where(fwd, si - qi, qi - si) <= 0

    g = g_ref[...]
    b_all = _dot_split3(mask.astype(BF16), jax.nn.log_sigmoid(g))
    g_t = g.T
    b_t = b_all.T
    def lane_bcast(x, lane0):
        return jnp.stack([jnp.broadcast_to(x[:, lane0 + h:lane0 + h + 1], (L, LANES))
                          for h in heads])

    b_b = lane_bcast(b_all, H)
    i_b = lane_bcast(g, 0)
    b_end = jnp.where(fwd, b_b[:, L - 1:L, :], b_b[:, 0:1, :])
    m_st = m_ref[:, 0:1, :]
    tile = lambda x, n: jnp.concatenate([x] * (n // LANES), axis=-1)

    q3 = jnp.stack([q_ref[:, h * dqk:(h + 1) * dqk] for h in heads])
    kt3 = jnp.stack([kt_ref[h * dqk:(h + 1) * dqk, :] for h in heads])
    v3 = jnp.stack([v_ref[:, h * dv:(h + 1) * dv] for h in heads])

    r3 = jnp.stack([g_t[h:h + 1, :] - b_t[H + h:H + h + 1, :] for h in heads])
    dm = jnp.where(mask, b_b + r3, -jnp.inf)
    a_inter = b_b + m_st
    m_q = jnp.maximum(a_inter, jnp.max(dm, axis=-1, keepdims=True))
    inter = jnp.exp(a_inter - m_q) * scale
    sc = jnp.einsum("hqd,hds->hqs", q3, kt3, preferred_element_type=F32)
    p = jnp.exp(dm - m_q) * (sc * scale)
    ct = ct_ref[...]
    qc = jnp.einsum("hqd,hdv->hqv", q3, ct.astype(BF16), preferred_element_type=F32)
    v_ext = jnp.concatenate([v3, jnp.ones((H, L, LANES), BF16)], axis=-1)
    pv = jnp.einsum("hqs,hsv->hqv", p.astype(BF16), v_ext, preferred_element_type=F32)
    den = pv[:, :, dv:] + inter * qc[:, :, dv:]
    rden = 1.0 / jnp.maximum(jnp.abs(den), jnp.exp(-m_q))
    hout = (pv[:, :, :dv] + tile(inter, dv) * qc[:, :, :dv]) * tile(rden, dv)
    for h in heads:
        o_ref[:, h * dv:(h + 1) * dv] = hout[h]

    wl = b_end - b_b + i_b
    m_next = jnp.maximum(b_end + m_st, jnp.max(wl, axis=1, keepdims=True))
    decay = jnp.exp(b_end + m_st - m_next)
    w_b = jnp.exp(wl - m_next)
    vw = jnp.concatenate([v3.astype(F32) * tile(w_b, dv), w_b], axis=-1).astype(BF16)
    upd = jnp.einsum("hdl,hlv->hdv", kt3, vw, preferred_element_type=F32)
    ct_ref[...] = tile(decay, dv + LANES) * ct + upd
    m_ref[...] = jnp.broadcast_to(m_next, m_ref.shape)


def _scan(u, kt, gates, n_batch, seq, ctx_len, n_heads, dqk, dv):
    L = SCAN_CHUNK
    nx = n_batch * seq
    ncc = ctx_len // L
    nlc = seq // L
    assert ctx_len % L == 0 and seq % L == 0
    hq = n_heads * dqk
    hv = n_heads * dv
    assert hv == 2 * hq

    def row_blk(b, d, s):
        ctx = (nx + b * ctx_len) // L + jnp.where(d == 0, s, ncc - 1 - s)
        sl = s - ncc
        lat = (b * seq) // L + jnp.where(d == 0, sl, nlc - 1 - sl)
        return jnp.where(s < ncc, ctx, lat)

    def out_blk(b, d, s):
        sl = jnp.maximum(s - ncc, 0)
        return (b * seq) // L + jnp.where(d == 0, sl, nlc - 1 - sl)

    return pl.pallas_call(
        functools.partial(_scan_kernel, n_heads, dqk, dv),
        grid=(n_batch, 2, ncc + nlc),
        in_specs=[
            pl.BlockSpec((L, hq), lambda b, d, s: (row_blk(b, d, s), 2 * hv // hq)),
            pl.BlockSpec((hq, L), lambda b, d, s: (0, row_blk(b, d, s))),
            pl.BlockSpec((L, hv), lambda b, d, s: (row_blk(b, d, s), 0)),
            pl.BlockSpec((L, LANES), lambda b, d, s: (row_blk(b, d, s), d)),
        ],
        out_specs=pl.BlockSpec((None, L, hv), lambda b, d, s: (d, out_blk(b, d, s), 0)),
        out_shape=jax.ShapeDtypeStruct((2, nx, hv), F32),
        scratch_shapes=[pltpu.VMEM((n_heads, dqk, dv + LANES), F32),
                        pltpu.VMEM((n_heads, SUBLANES, LANES), F32)],
        compiler_params=_cparams(("arbitrary", "arbitrary", "arbitrary")),
        name="mlstm_scan",
    )(u, kt, u, gates)


def _mix_epilogue(alpha, rs, a, x_ref, g1_ref, sh2_ref, sc2_ref, wout_ref, lng_ref,
                  lnb_ref, rwt_ref, x1_ref, hx2_ref, logt_ref):
    mx = jnp.dot(a, wout_ref[...], preferred_element_type=F32)
    x1 = _layer_norm_rows(alpha * x_ref[rs, :] + g1_ref[0] * mx, lng_ref[...], lnb_ref[...])
    x1_ref[rs, :] = x1
    hx2 = x1 * (1.0 + sc2_ref[0]) + sh2_ref[0]
    hx2_ref[rs, :] = hx2
    logt_ref[:, rs] = lax.dot_general(rwt_ref[...], hx2.astype(BF16), (((1,), (1,)), ((), ())),
                                      preferred_element_type=F32)


def _sub_rows(tm, sub):
    return [slice(r, r + sub) for r in range(0, tm, sub)]


def _conv_out_kernel(alpha, ctx_mode, n_lat, tpb, sub, z_ref, zp_ref, zn_ref, bg_ref, cw_ref,
                     x_ref, g1_ref, sh2_ref, sc2_ref, wout_ref, lng_ref, lnb_ref, rwt_ref, *rest):
    x1_ref, hx2_ref, logt_ref = rest[-3:]
    i = pl.program_id(0)
    tm, d = z_ref.shape
    half = d // 2
    rows = lax.broadcasted_iota(I32, (sub, 1), 0)

    def conv1(zz, w3, period):
        pos = rows & (period - 1)
        prev = pltpu.roll(zz, 1, 0) * (pos != 0).astype(F32)
        nxt = pltpu.roll(zz, sub - 1, 0) * (pos != period - 1).astype(F32)
        return w3[0:1] * prev + w3[1:2] * zz + w3[2:3] * nxt

    def tile():
        cw = cw_ref[...]
        if not ctx_mode:
            ti = i % tpb
            up = zp_ref[...].astype(F32) * (ti > 0).astype(F32)
            dn = zn_ref[...].astype(F32) * (ti < tpb - 1).astype(F32)
            w3 = cw[:, half:]

        for rs in _sub_rows(tm, sub):
            z = z_ref[rs, :].astype(F32)
            bg = bg_ref[rs, :].astype(F32)
            if ctx_mode:
                a = (bg * conv1(z, cw, sub)).astype(BF16)
            else:
                a_row = (bg[:, :half] * conv1(z[:, :half], cw[:, :half], GRID_W)).astype(BF16)
                r0, r1 = rs.start, rs.stop
                f32_rows = lambda a, b: z_ref[a:b, half:].astype(F32)
                prev = (jnp.concatenate([up, f32_rows(0, r1 - GRID_W)], axis=0) if r0 == 0
                        else f32_rows(r0 - GRID_W, r1 - GRID_W))
                nxt = (jnp.concatenate([f32_rows(r0 + GRID_W, tm), dn], axis=0) if r1 == tm
                       else f32_rows(r0 + GRID_W, r1 + GRID_W))
                y = w3[0:1] * prev + w3[1:2] * z[:, half:] + w3[2:3] * nxt
                a = jnp.concatenate([a_row, (bg[:, half:] * y).astype(BF16)], axis=1)
            _mix_epilogue(alpha, rs, a, x_ref, g1_ref, sh2_ref, sc2_ref, wout_ref, lng_ref,
                          lnb_ref, rwt_ref, x1_ref, hx2_ref, logt_ref)

    if ctx_mode:
        tile()
    else:
        pl.when(i < n_lat)(tile)

        @pl.when(i >= n_lat)
        def _():
            x1_ref[...] = jnp.zeros_like(x1_ref)
            hx2_ref[...] = jnp.zeros_like(hx2_ref)
            logt_ref[...] = jnp.zeros_like(logt_ref)


def _ml_out_kernel(alpha, n_heads, dv, sub, hf_ref, hb_ref, og_ref, ng_ref, x_ref,
                   g1_ref, sh2_ref, sc2_ref, wout_ref, lng_ref, lnb_ref, rwt_ref,
                   x1_ref, hx2_ref, logt_ref):
    for rs in _sub_rows(hf_ref.shape[0], sub):
        parts = []
        for h in range(n_heads):
            sl = slice(h * dv, (h + 1) * dv)
            hs = hf_ref[rs, sl] + hb_ref[rs, sl]
            mu = jnp.mean(hs, axis=-1, keepdims=True)
            c = hs - mu
            var = jnp.mean(c * c, axis=-1, keepdims=True)
            hn = c * lax.rsqrt(var + HEAD_NORM_EPS)
            gate = jax.nn.sigmoid(og_ref[rs, sl].astype(F32))
            parts.append((hn * ng_ref[:, sl] * gate).astype(BF16))
        _mix_epilogue(alpha, rs, jnp.concatenate(parts, axis=1), x_ref, g1_ref, sh2_ref,
                      sc2_ref, wout_ref, lng_ref, lnb_ref, rwt_ref, x1_ref, hx2_ref, logt_ref)


def _mix_out_common(d, n_rows, n_exp, mod_row, tile0=0):
    tm = ROW_TILE
    mspec = lambda chunk: pl.BlockSpec((1, 1, d), lambda i: (mod_row(i), 0, chunk))
    const = lambda shape: pl.BlockSpec(shape, lambda i: (0,) * len(shape))
    in_specs = [mspec(2), mspec(3), mspec(4), const((d, d)), const((1, d)), const((1, d)),
                const((n_exp, d))]
    out_specs = [pl.BlockSpec((tm, d), lambda i: (i + tile0, 0)),
                 pl.BlockSpec((tm, d), lambda i: (i + tile0, 0)),
                 pl.BlockSpec((n_exp, tm), lambda i: (0, i + tile0))]
    out_shape = [jax.ShapeDtypeStruct((n_rows, d), F32),
                 jax.ShapeDtypeStruct((n_rows, d), F32),
                 jax.ShapeDtypeStruct((n_exp, n_rows), F32)]
    return in_specs, out_specs, out_shape


def _conv_out(alpha, z, bg, conv_w, x2d, c2d, mod, w_out_b, ln_g, ln_b, rwt_b, n_batch, seq,
              ctx_len):
    n_all, d = z.shape
    nx = x2d.shape[0]
    tm = ROW_TILE
    assert ctx_len == tm and seq % tm == 0 and MIX_SUB_ROWS % GRID_W == 0
    half = d // 2
    nxa = nx // tm
    tpb = seq // tm
    hpt = tm // GRID_W
    n_exp = rwt_b.shape[0]

    def z_specs(tile0):
        return [
            pl.BlockSpec((tm, d), lambda i: (i + tile0, 0)),
            pl.BlockSpec((GRID_W, half), lambda i: (jnp.maximum((i + tile0) * hpt - 1, 0), 1)),
            pl.BlockSpec((GRID_W, half), lambda i: ((i + tile0 + 1) * hpt - 1, 1)),
            pl.BlockSpec((tm, d), lambda i: (i + tile0, 0)),
            pl.BlockSpec((3, d), lambda i: (0, 0)),
            pl.BlockSpec((tm, d), lambda i: (i, 0)),
        ]

    lat = lambda i: jnp.minimum(i, nxa - 1)
    common_in, out_specs, out_shape = _mix_out_common(d, n_all, n_exp, lambda i: lat(i) // tpb)
    lat_specs = z_specs(0)
    lat_specs[2] = pl.BlockSpec((GRID_W, half), lambda i: ((lat(i) + 1) * hpt, 1))
    lat_specs[5] = pl.BlockSpec((tm, d), lambda i: (lat(i), 0))
    outs = pl.pallas_call(
        functools.partial(_conv_out_kernel, alpha, False, nxa, tpb, MIX_SUB_ROWS),
        grid=(n_all // tm,),
        in_specs=lat_specs + common_in,
        out_specs=out_specs,
        out_shape=out_shape,
        compiler_params=_cparams(("arbitrary",)),
        name="conv_out",
    )(z, z, z, bg, conv_w, x2d, mod, mod, mod, w_out_b, ln_g, ln_b, rwt_b)

    common_in, out_specs, out_shape = _mix_out_common(d, n_all, n_exp, lambda i: n_batch, nxa)
    n_in = 6 + len(common_in)
    keep = [pl.BlockSpec(memory_space=pl.ANY)] * 3
    return pl.pallas_call(
        functools.partial(_conv_out_kernel, alpha, True, None, tpb, tm),
        grid=(c2d.shape[0] // tm,),
        in_specs=z_specs(nxa) + common_in + keep,
        out_specs=out_specs,
        out_shape=out_shape,
        input_output_aliases={n_in: 0, n_in + 1: 1, n_in + 2: 2},
        compiler_params=_cparams(("arbitrary",)),
        name="conv_out_ctx",
    )(z, z, z, bg, conv_w, c2d, mod, mod, mod, w_out_b, ln_g, ln_b, rwt_b, *outs)


def _ml_out(alpha, hfb, u, norm_g, xall, mod, w_out_b, ln_g, ln_b, rwt_b, n_batch, seq,
            n_heads, dv):
    nx, d = hfb.shape[1:]
    tm = ROW_TILE
    tpb = seq // tm
    common_in, out_specs, out_shape = _mix_out_common(d, nx, rwt_b.shape[0], lambda i: i // tpb)
    o_blk = 1
    in_specs = [
        pl.BlockSpec((None, tm, d), lambda i: (0, i, 0)),
        pl.BlockSpec((None, tm, d), lambda i: (1, i, 0)),
        pl.BlockSpec((tm, d), lambda i: (i, o_blk)),
        pl.BlockSpec((1, d), lambda i: (0, 0)),
        pl.BlockSpec((tm, d), lambda i: (i, 0)),
    ] + common_in
    return pl.pallas_call(
        functools.partial(_ml_out_kernel, alpha, n_heads, dv, tm),
        grid=(nx // tm,),
        in_specs=in_specs,
        out_specs=out_specs,
        out_shape=out_shape,
        compiler_params=_cparams(("arbitrary",)),
        name="mlstm_out",
    )(hfb, hfb, u, norm_g, xall, mod, mod, mod, w_out_b, ln_g, ln_b, rwt_b)


def _route_kernel(tile_rows, n_te, logt_ref, rb_ref, pos_ref, wcol_ref, te_ref, meta_ref,
                  carry_ref, before_ref):
    ph = pl.program_id(0)
    i = pl.program_id(1)
    n_steps = pl.num_programs(1)
    n_exp, tr = logt_ref.shape
    epg = n_exp // N_GROUPS

    @pl.when((ph == 0) & (i == 0))
    def _():
        carry_ref[...] = jnp.zeros_like(carry_ref)
        ti = lax.broadcasted_iota(I32, (tr, tr), 0)
        tj = lax.broadcasted_iota(I32, (tr, tr), 1)
        before_ref[...] = (ti < tj).astype(BF16)

    s = jax.nn.sigmoid(logt_ref[...])
    sel = s + rb_ref[...]
    row = lax.broadcasted_iota(I32, (n_exp, tr), 0)
    member = row % epg
    group = row // epg

    def partner(x, k, idx, span, unit):
        wrapped = (idx + k) >= span
        up = pltpu.roll(x, n_exp - k * unit, 0)
        down = pltpu.roll(x, (span - k) * unit, 0)
        return jnp.where(wrapped, down, up), wrapped

    rank_in = jnp.zeros((n_exp, tr), F32)
    for k in range(1, epg):
        p, wrapped = partner(sel, k, member, epg, 1)
        beats = (p > sel) | ((p == sel) & wrapped)
        rank_in = rank_in + beats.astype(F32)
    top = (rank_in < TOP_K).astype(F32)
    gs = sel * top
    score = gs
    for k in range(1, epg):
        p, _ = partner(gs, k, member, epg, 1)
        score = score + p
    n_better = jnp.zeros((n_exp, tr), F32)
    for k in range(1, N_GROUPS):
        p, wrapped = partner(score, k, group, N_GROUPS, epg)
        beats = (p > score) | ((p == score) & wrapped)
        n_better = n_better + beats.astype(F32)
    best = n_better == 0.0
    m1 = (best & (rank_in == 0.0)).astype(F32)
    m2 = (best & (rank_in == 1.0)).astype(F32)
    s1 = jnp.sum(s * m1, axis=0, keepdims=True)
    s2 = jnp.sum(s * m2, axis=0, keepdims=True)
    den = s1 + s2
    oh = m1 + m2

    slot = jnp.dot(oh.astype(BF16), before_ref[...], preferred_element_type=F32) + carry_ref[...]

    @pl.when(ph == 1)
    def _():
        pos_ref[0:1, :] = jnp.sum(m1 * slot, axis=0, keepdims=True).astype(I32)
        pos_ref[1:2, :] = jnp.sum(m2 * slot, axis=0, keepdims=True).astype(I32)
        w8 = jnp.concatenate([s1 / den, s2 / den, jnp.zeros((LANES - 2, tr), F32)], axis=0)
        wcol_ref[...] = w8.T

    carry_ref[...] = carry_ref[...] + jnp.sum(oh, axis=1, keepdims=True)

    @pl.when((ph == 0) & (i == n_steps - 1))
    def _():
        cnt = carry_ref[...]
        ntile = jnp.floor((cnt + (tile_rows - 1)) * (1.0 / tile_rows))
        er = lax.broadcasted_iota(I32, (n_exp, LANES), 0)
        el = lax.broadcasted_iota(I32, (n_exp, LANES), 1)
        eye = (er == el).astype(F32)
        nt_row = jnp.sum(ntile * eye, axis=0, keepdims=True)
        cnt_row = jnp.sum(cnt * eye, axis=0, keepdims=True)
        cum_excl = jnp.sum(nt_row * (el < er).astype(F32), axis=1, keepdims=True)
        off = cum_excl * tile_rows
        off_row = jnp.sum(off * eye, axis=0, keepdims=True)
        total = jnp.sum(nt_row, axis=1, keepdims=True)
        cum_incl = cum_excl + ntile
        tl = lax.broadcasted_iota(I32, (n_exp, n_te), 1).astype(F32)
        te = jnp.sum((cum_incl <= tl).astype(F32), axis=0, keepdims=True)
        elf = el.astype(F32)
        later = (el > er) & (nt_row > 0.0) & (el < n_exp)
        nxt = jnp.min(jnp.where(later, elf, float(n_exp)), axis=1, keepdims=True)
        nxt = jnp.where(nxt == float(n_exp), er[:, 0:1].astype(F32), nxt)
        mine = (cum_excl <= tl) & (tl < cum_incl)
        te_next = jnp.sum(jnp.where(mine, nxt, 0.0), axis=0, keepdims=True)
        te_ref[...] = jnp.concatenate([jnp.minimum(te, n_exp - 1), te_next], axis=0).astype(I32)
        meta_ref[...] = jnp.concatenate(
            [cnt_row, off_row, jnp.broadcast_to(total, (1, LANES)),
             jnp.zeros((SUBLANES - 3, LANES), F32)], axis=0).astype(I32)
        carry_ref[...] = off


def _route(logt, router_b, tile_rows):
    n_exp, n = logt.shape
    tr = ROUTE_COLS
    while n % tr:
        tr //= 2
    n_te = 256
    assert (2 * n) // tile_rows + n_exp <= n_te
    return pl.pallas_call(
        functools.partial(_route_kernel, tile_rows, n_te),
        grid=(2, n // tr),
        in_specs=[pl.BlockSpec((n_exp, tr), lambda p, i: (0, i)),
                  pl.BlockSpec((n_exp, 1), lambda p, i: (0, 0))],
        out_specs=[pl.BlockSpec((2, tr), lambda p, i: (0, i * p)),
                   pl.BlockSpec((tr, LANES), lambda p, i: (i * p, 0)),
                   pl.BlockSpec((2, n_te), lambda p, i: (0, 0)),
                   pl.BlockSpec((SUBLANES, LANES), lambda p, i: (0, 0))],
        out_shape=[jax.ShapeDtypeStruct((2, n), I32),
                   jax.ShapeDtypeStruct((n, LANES), F32),
                   jax.ShapeDtypeStruct((2, n_te), I32),
                   jax.ShapeDtypeStruct((SUBLANES, LANES), I32)],
        scratch_shapes=[pltpu.VMEM((n_exp, 1), F32), pltpu.VMEM((tr, tr), BF16)],
        compiler_params=_cparams(("arbitrary", "arbitrary")),
        name="route",
    )(logt, router_b.reshape(n_exp, 1).astype(F32))


def _tbl_kernel(n_tok, n_exp, tile_rows, meta_ref, pos_ref, tbl_ref):
    i = pl.program_id(0)
    blk = pos_ref.shape[1]

    @pl.when(i == 0)
    def _():
        n_pad = 0
        for e in range(n_exp):
            cnt = meta_ref[e]
            off = meta_ref[n_exp + e]
            up = ((cnt + (tile_rows - 1)) // tile_rows) * tile_rows

            def fill(r, j):
                tbl_ref[off + r] = (2 * n_tok + j) << SLOT_SRC_BITS
                return j + 1

            n_pad = lax.fori_loop(cnt, up, fill, n_pad)

        def fill_tail(r, carry):
            tbl_ref[r] = (2 * n_tok) << SLOT_SRC_BITS
            return carry

        lax.fori_loop(2 * n_tok + n_pad, tbl_ref.shape[0], fill_tail, 0)

    base = i * blk
    first = (base << SLOT_SRC_BITS) | jnp.where(base >= n_tok, base - n_tok, base)
    step = (1 << SLOT_SRC_BITS) + 1
    group = 8

    def body(q, carry):
        r0 = q * group
        slots = [pos_ref[0, r0 + j] for j in range(group)]
        for j in range(group):
            tbl_ref[slots[j]] = first + (r0 + j) * step
        return carry

    lax.fori_loop(0, blk // group, body, 0, unroll=True)


def _build_table(meta1d, pos, n_tok, n_exp, tile_rows, p_pad):
    blk = TBL_BLOCK
    while n_tok % blk:
        blk //= 2
    n_pairs = 2 * n_tok
    grid_spec = pltpu.PrefetchScalarGridSpec(
        num_scalar_prefetch=1,
        grid=(n_pairs // blk,),
        in_specs=[pl.BlockSpec((None, 1, blk), lambda i, m: (i, 0, 0), memory_space=pltpu.SMEM)],
        out_specs=pl.BlockSpec(memory_space=pltpu.SMEM),
    )
    return pl.pallas_call(
        functools.partial(_tbl_kernel, n_tok, n_exp, tile_rows),
        grid_spec=grid_spec,
        out_shape=jax.ShapeDtypeStruct((p_pad,), I32),
        compiler_params=_cparams(("arbitrary",)),
        name="slot_table",
    )(meta1d, pos.reshape(n_pairs // blk, 1, blk))


def _moe_kernel(layer, te_ref, ten_ref, nt_ref, tbl_ref, tbln_ref, tblp_ref, hx_hbm,
                wg_hbm, wu_hbm, wd_hbm, y_hbm, xbuf, ybuf, zbuf, wg_b, wu_b, wd_b,
                wg_s, wu_s, wd_s, gsem, ssem, zsem, wsem):
    t = pl.program_id(0)
    n = nt_ref[0]
    rows = tbl_ref.shape[1]
    lt = hx_hbm.shape[1]
    staged = ((wg_hbm, wg_s, wg_b), (wu_hbm, wu_s, wu_b), (wd_hbm, wd_s, wd_b))

    def weight_copies(e):
        return [pltpu.make_async_copy(hbm.at[layer, e], stage, wsem.at[k])
                for k, (hbm, stage, _) in enumerate(staged)]

    @pl.when(t == 0)
    def _():
        for c in weight_copies(te_ref[0]):
            c.start(priority=WEIGHT_DMA_PRIORITY)

    tl = jnp.minimum(t, n - 1)
    new_expert = (t < n) & ((t == 0) | (te_ref[tl] != te_ref[jnp.maximum(tl - 1, 0)]))

    @pl.when(new_expert)
    def _():
        for c, (_, stage, dst) in zip(weight_copies(te_ref[tl]), staged):
            c.wait()
            dst[...] = stage[...].astype(BF16)
        for c in weight_copies(ten_ref[tl]):
            c.start(priority=WEIGHT_DMA_PRIORITY)

    def start_gather(tref, sl, lo=0, hi=None):
        for r in range(lo, rows if hi is None else hi):
            src = tref[0, r] & SLOT_SRC_MASK
            pltpu.make_async_copy(hx_hbm.at[src], xbuf.at[sl, pl.ds(r * lt, lt), :],
                                  gsem.at[sl]).start()

    def start_scatter(tref, sl, lo=0, hi=None):
        for r in range(lo, rows if hi is None else hi):
            dst = tref[0, r] >> SLOT_SRC_BITS
            pltpu.make_async_copy(ybuf.at[sl, pl.ds(r, 1), :],
                                  y_hbm.at[pl.ds(dst, 1), :], ssem.at[sl]
                                  ).start(priority=SCATTER_DMA_PRIORITY)

    def wait_gather(sl):
        pltpu.make_async_copy(xbuf.at[sl], xbuf.at[sl], gsem.at[sl]).wait()

    def wait_scatter(sl):
        pltpu.make_async_copy(ybuf.at[sl], ybuf.at[sl], ssem.at[sl]).wait()

    @pl.when(t == 0)
    def _():
        zbuf[...] = jnp.zeros_like(zbuf)
        start_gather(tbl_ref, 0)

    @pl.when(t >= n)
    def _():
        dst = y_hbm.at[pl.ds(pl.multiple_of(t * rows, rows), rows), :]
        fill = pltpu.make_async_copy(zbuf, dst, zsem)
        fill.start()
        fill.wait()

    def tile(slot, has_prev):
        other = 1 - slot
        wait_gather(slot)
        f = wg_b.shape[1]
        d = wd_b.shape[1]
        fc, dc = MOE_UP_COLS, MOE_DOWN_COLS
        n_up, n_down = f // fc, d // dc
        g_bounds = [(k * rows) // n_up for k in range(n_up + 1)]
        s_bounds = [(k * rows) // n_down for k in range(n_down + 1)]

        def copy_group(k):
            if k < n_up:
                start_gather(tbln_ref, other, g_bounds[k], g_bounds[k + 1])
            elif has_prev:
                start_scatter(tblp_ref, other, s_bounds[k - n_up], s_bounds[k - n_up + 1])

        xb = jnp.concatenate(
            [xbuf[slot, pl.ds(c, rows, stride=lt), :].astype(BF16) for c in range(lt)], axis=1)
        hs = []
        for c in range(f // fc):
            cols = slice(c * fc, (c + 1) * fc)
            g = jnp.dot(xb, wg_b[:, cols], preferred_element_type=F32)
            u = jnp.dot(xb, wu_b[:, cols], preferred_element_type=F32)
            hs.append((g * jax.nn.sigmoid(g) * u).astype(BF16))
            copy_group(c)
        h = jnp.concatenate(hs, axis=1)

        @pl.when(t >= 2)
        def _():
            wait_scatter(slot)

        for c in range(d // dc):
            cols = slice(c * dc, (c + 1) * dc)
            ybuf[slot, :, cols] = jnp.dot(h, wd_b[:, cols], preferred_element_type=F32)
            copy_group(f // fc + c)

        @pl.when(t == n - 1)
        def _():
            start_scatter(tbl_ref, slot)
            wait_gather(other)

            @pl.when(t >= 1)
            def _():
                wait_scatter(other)

            wait_scatter(slot)
            for c in weight_copies(ten_ref[tl]):
                c.wait()

    pl.when((t < n) & (t == 0))(functools.partial(tile, 0, False))
    for parity in (0, 1):
        pl.when((t < n) & (t > 0) & (lax.rem(t, 2) == parity))(
            functools.partial(tile, parity, True))


def _moe(te2, nt1d, tbl, hx2, w_gate, w_up, w_down, layer, n_tok, p_pad):
    tmm = MOE_ROWS
    _, n_exp, d, f = w_gate.shape
    t_max = (2 * n_tok) // tmm + n_exp
    last = lambda t, nt: jnp.minimum(t, nt[0] - 1)
    tspec = lambda fn: pl.BlockSpec((None, 1, tmm), lambda t, te, ten, nt: (fn(t, nt), 0, 0),
                                    memory_space=pltpu.SMEM)
    hbm = pl.BlockSpec(memory_space=pl.ANY)
    grid_spec = pltpu.PrefetchScalarGridSpec(
        num_scalar_prefetch=3,
        grid=(t_max,),
        in_specs=[
            tspec(last),
            tspec(lambda t, nt: last(t + 1, nt)),
            tspec(lambda t, nt: jnp.maximum(last(t, nt) - 1, 0)),
            hbm, hbm, hbm, hbm,
        ],
        out_specs=hbm,
        scratch_shapes=[pltpu.VMEM((2, tmm * (d // LANES), LANES), F32),
                        pltpu.VMEM((2, tmm, d), F32),
                        pltpu.VMEM((tmm, d), F32),
                        pltpu.VMEM((d, f), BF16), pltpu.VMEM((d, f), BF16),
                        pltpu.VMEM((f, d), BF16),
                        pltpu.VMEM((d, f), F32), pltpu.VMEM((d, f), F32),
                        pltpu.VMEM((f, d), F32),
                        pltpu.SemaphoreType.DMA((2,)), pltpu.SemaphoreType.DMA((2,)),
                        pltpu.SemaphoreType.DMA(()), pltpu.SemaphoreType.DMA((3,))],
    )
    tbl3 = tbl.reshape(p_pad // tmm, 1, tmm)
    return pl.pallas_call(
        functools.partial(_moe_kernel, layer),
        grid_spec=grid_spec,
        out_shape=jax.ShapeDtypeStruct((t_max * tmm, d), F32),
        compiler_params=_cparams(("arbitrary",)),
        name="moe_experts",
    )(te2[0], te2[1], nt1d, tbl3, tbl3, tbl3, hx2.reshape(n_tok, d // LANES, LANES),
      w_gate, w_up, w_down)


def _comb_kernel(alpha, x_ref, y0_ref, y1_ref, w_ref, g2_ref, lng_ref, lnb_ref, o_ref):
    w = w_ref[...]
    ex = w[:, 0:1] * y0_ref[...] + w[:, 1:2] * y1_ref[...]
    o_ref[...] = _layer_norm_rows(alpha * x_ref[...] + g2_ref[0] * ex, lng_ref[...], lnb_ref[...])


def _combine(alpha, x1, y, wcol, mod, ln_g, ln_b, n_tok, n_out, n_batch, seq):
    d = x1.shape[1]
    tm = COMB_ROWS
    assert n_tok % tm == 0 and n_out % tm == 0 and seq % tm == 0
    nblk = n_tok // tm
    mspec = functools.partial(_mod_spec, d=d, tiles_per_batch=seq // tm, n_batch=n_batch)
    return pl.pallas_call(
        functools.partial(_comb_kernel, alpha),
        grid=(n_out // tm,),
        in_specs=[
            pl.BlockSpec((tm, d), lambda i: (i, 0)),
            pl.BlockSpec((tm, d), lambda i: (i, 0)),
            pl.BlockSpec((tm, d), lambda i: (i + nblk, 0)),
            pl.BlockSpec((tm, LANES), lambda i: (i, 0)),
            mspec(5),
            pl.BlockSpec((1, d), lambda i: (0, 0)),
            pl.BlockSpec((1, d), lambda i: (0, 0)),
        ],
        out_specs=pl.BlockSpec((tm, d), lambda i: (i, 0)),
        out_shape=jax.ShapeDtypeStruct((n_out, d), F32),
        compiler_params=_cparams(("arbitrary",)),
        name="moe_combine",
    )(x1, y, y, wcol, mod, ln_g, ln_b)


def _moe_layer(alpha, x1, hx2, logt, mod, ln_g, ln_b, router_b, w_gate, w_up, w_down, layer,
               n_out, n_batch, seq):
    n_tok = hx2.shape[0]
    n_exp = w_gate.shape[1]
    tmm = MOE_ROWS
    p_max = 2 * n_tok + n_exp * tmm
    p_pad = -(-p_max // TBL_BLOCK) * TBL_BLOCK
    assert n_tok <= 1 << SLOT_SRC_BITS and p_max < 1 << (31 - SLOT_SRC_BITS)
    pos, wcol, te, meta = _route(logt, router_b, tmm)
    meta1d = meta[:2, :n_exp].reshape(-1)
    tbl = _build_table(meta1d, pos, n_tok, n_exp, tmm, p_pad)
    y = _moe(te, meta[2, :1], tbl, hx2, w_gate, w_up, w_down, layer, n_tok, p_pad)
    return _combine(alpha, x1, y, wcol, mod, ln_g, ln_b, n_tok, n_out, n_batch, seq)


def kernel(x, c, ctx, c_ctx, w_ada, b_ada, ln_g, ln_b, conv_w_in, conv_w, conv_w_out, ml_w_in, ml_w_gate, ml_b_gate, ml_norm_g, ml_w_out, router_w, router_b, exp_w_gate, exp_w_up, exp_w_down):
    n_batch, seq, d = x.shape
    ctx_len = ctx.shape[1]
    depth = w_ada.shape[0]
    assert depth == 2, "layer 0 is the conv mixer, layer 1 the mLSTM mixer"
    alpha = (2 * depth) ** 0.25
    n_heads = ml_b_gate.shape[-1] // 4
    dqk = d // (2 * n_heads)
    dv = d // n_heads
    nx = n_batch * seq
    nc = n_batch * ctx_len
    assert n_batch < SUBLANES and 2 * n_heads <= LANES

    x2d = x.reshape(nx, d)
    c2d = ctx.reshape(nc, d)
    cc = jnp.zeros((SUBLANES, d), F32).at[:n_batch].set(c).at[n_batch].set(c_ctx)
    mod = _ada(cc, w_ada, b_ada).reshape(depth, SUBLANES, 1, 6 * d)
    rwt_b = router_w.T.astype(BF16)

    bg, z = _conv_in(x2d, c2d, mod[0], _to_bf16(conv_w_in, 0), n_batch, seq)
    x1, hx2, logt = _conv_out(alpha, z, bg, conv_w[0], x2d, c2d, mod[0],
                              _to_bf16(conv_w_out, 0), ln_g[0, 0:1], ln_b[0, 0:1], rwt_b,
                              n_batch, seq, ctx_len)
    xall = _moe_layer(alpha, x1, hx2, logt, mod[0], ln_g[0, 1:2], ln_b[0, 1:2], router_b,
                      exp_w_gate, exp_w_up, exp_w_down, 0, nx + nc, n_batch, seq)

    w_gate = ml_w_gate[0]
    b_gate = ml_b_gate[0]
    h2 = 2 * n_heads
    w_gate_pad = (jnp.zeros((d, 2 * LANES), F32).at[:, :h2].set(w_gate[:, :h2])
                  .at[:, LANES:LANES + h2].set(w_gate[:, h2:])).astype(BF16)
    b_gate_pad = (jnp.zeros((1, 2 * LANES), F32).at[0, :h2].set(b_gate[:h2])
                  .at[0, LANES:LANES + h2].set(b_gate[h2:]))
    w_voq_b, w_kt_b = _ml_weights(ml_w_in, 0, n_heads * dqk)
    u, kt, gates = _ml_in(xall, mod[1], w_voq_b, w_kt_b, w_gate_pad, b_gate_pad, n_batch,
                          seq, nc)
    hfb = _scan(u, kt, gates, n_batch, seq, ctx_len, n_heads, dqk, dv)
    x1, hx2, logt = _ml_out(alpha, hfb, u, ml_norm_g[0:1], xall, mod[1],
                            _to_bf16(ml_w_out, 0), ln_g[1, 0:1], ln_b[1, 0:1], rwt_b,
                            n_batch, seq, n_heads, dv)
    out = _moe_layer(alpha, x1, hx2, logt, mod[1], ln_g[1, 1:2], ln_b[1, 1:2], router_b,
                     exp_w_gate, exp_w_up, exp_w_down, 1, nx, n_batch, seq)
    return out.reshape(n_batch, seq, d)
```

```python
import functools

import jax
import jax.numpy as jnp
from jax import lax
from jax.experimental import pallas as pl
from jax.experimental.pallas import tpu as pltpu

F32 = jnp.float32
BF16 = jnp.bfloat16
I32 = jnp.int32

GRID_W = 64
N_GROUPS = 4
TOP_K = 2
LN_EPS = 1e-5
HEAD_NORM_EPS = 1e-6

LANES = 128
SUBLANES = 8
VMEM_LIMIT_BYTES = 56 * 1024 * 1024

ROW_TILE = 256
MIX_SUB_ROWS = 128
PROJ_ROWS = 1024
MOE_ROWS = 256
MOE_UP_COLS = 256
MOE_DOWN_COLS = 512
MOE_GATHER_GROUPS = 2
ROUTE_COLS = 1024
COMB_ROWS = 512
SCAN_CHUNK = 128
TBL_BLOCK = 1024
SLOT_SRC_BITS = 15
SLOT_SRC_MASK = (1 << SLOT_SRC_BITS) - 1
CAST_BLOCK_BYTES = 8 * 1024 * 1024
WEIGHT_DMA_PRIORITY = 1
SCATTER_DMA_PRIORITY = 1


def _cparams(sem):
    return pltpu.CompilerParams(dimension_semantics=sem, vmem_limit_bytes=VMEM_LIMIT_BYTES)


def _layer_norm_rows(r, g, b):
    mu = jnp.mean(r, axis=-1, keepdims=True)
    c = r - mu
    var = jnp.mean(c * c, axis=-1, keepdims=True)
    return c * lax.rsqrt(var + LN_EPS) * g + b


def _ada_kernel(cc_ref, w_ref, b_ref, o_ref):
    a = cc_ref[...]
    a = (a * jax.nn.sigmoid(a)).astype(BF16)
    o_ref[0] = jnp.dot(a, w_ref[0].astype(BF16), preferred_element_type=F32) + b_ref[0]


def _ada(cc, w_ada, b_ada):
    depth, d, n6 = w_ada.shape
    tn = 1024
    return pl.pallas_call(
        _ada_kernel,
        grid=(depth, n6 // tn),
        in_specs=[
            pl.BlockSpec((SUBLANES, d), lambda l, j: (0, 0)),
            pl.BlockSpec((1, d, tn), lambda l, j: (l, 0, j)),
            pl.BlockSpec((1, 1, tn), lambda l, j: (l, 0, j)),
        ],
        out_specs=pl.BlockSpec((1, SUBLANES, tn), lambda l, j: (l, 0, j)),
        out_shape=jax.ShapeDtypeStruct((depth, SUBLANES, n6), F32),
        compiler_params=_cparams(("arbitrary", "arbitrary")),
        name="ada_mod",
    )(cc, w_ada, b_ada.reshape(depth, 1, n6))


def _cast_kernel(w_ref, o_ref):
    o_ref[...] = w_ref[...].astype(BF16)


def _to_bf16(w, layer):
    rows, cols = w.shape[-2:]
    w4 = w.reshape(w.shape[0], -1, rows, cols)
    m = w4.shape[1]
    rb = rows
    while rb * cols * 4 > CAST_BLOCK_BYTES and rb % (4 * SUBLANES) == 0:
        rb //= 2
    out = pl.pallas_call(
        _cast_kernel,
        grid=(m, rows // rb),
        in_specs=[pl.BlockSpec((None, None, rb, cols), lambda e, r: (layer, e, r, 0))],
        out_specs=pl.BlockSpec((None, rb, cols), lambda e, r: (e, r, 0)),
        out_shape=jax.ShapeDtypeStruct((m, rows, cols), BF16),
        compiler_params=_cparams(("arbitrary", "arbitrary")),
        name="to_bf16",
    )(w4)
    return out.reshape(w.shape[1:])


def _cast_t_kernel(w_ref, o_ref):
    o_ref[...] = w_ref[...].T.astype(BF16)


def _ml_weights(w, layer, hq):
    _, d, ncol = w.shape
    nb = ncol // hq
    rb = min(d, 512)
    voq = pl.pallas_call(
        _cast_kernel,
        grid=(d // rb, nb - 1),
        in_specs=[pl.BlockSpec((None, rb, hq), lambda r, k: (layer, r, (k + 2) % nb))],
        out_specs=pl.BlockSpec((rb, hq), lambda r, k: (r, k)),
        out_shape=jax.ShapeDtypeStruct((d, ncol - hq), BF16),
        compiler_params=_cparams(("arbitrary", "arbitrary")),
        name="to_bf16_voq",
    )(w)
    kt = pl.pallas_call(
        _cast_t_kernel,
        grid=(d // rb,),
        in_specs=[pl.BlockSpec((None, rb, hq), lambda r: (layer, r, 1))],
        out_specs=pl.BlockSpec((hq, rb), lambda r: (0, r)),
        out_shape=jax.ShapeDtypeStruct((hq, d), BF16),
        compiler_params=_cparams(("arbitrary",)),
        name="to_bf16_kt",
    )(w)
    return voq, kt


def _mod_spec(chunk, d, tiles_per_batch, n_batch):
    return pl.BlockSpec(
        (1, 1, d),
        lambda i, *_: (jnp.minimum(i // tiles_per_batch, n_batch), 0, chunk))


def _conv_in_kernel(nxa, xa_ref, xb_ref, sh_ref, sc_ref, wb_ref, wc_ref, wv_ref,
                    bg_ref, z_ref, h_ref):
    i = pl.program_id(0)
    j = pl.program_id(1)

    @pl.when((j == 0) & (i < nxa))
    def _():
        h_ref[...] = (xa_ref[...] * (1.0 + sc_ref[0]) + sh_ref[0]).astype(BF16)

    @pl.when((j == 0) & (i >= nxa))
    def _():
        h_ref[...] = (xb_ref[...] * (1.0 + sc_ref[0]) + sh_ref[0]).astype(BF16)

    h = h_ref[...]
    bg = jnp.dot(h, wb_ref[...], preferred_element_type=F32)
    cg = jnp.dot(h, wc_ref[...], preferred_element_type=F32)
    v = jnp.dot(h, wv_ref[...], preferred_element_type=F32)
    bg_ref[...] = bg.astype(BF16)
    z_ref[...] = (cg * v).astype(BF16)


def _conv_in(x2d, c2d, mod, w_in_b, n_batch, seq):
    nx, d = x2d.shape
    nc = c2d.shape[0]
    bm = min(PROJ_ROWS, seq, nc)
    assert seq % bm == 0 and nc % bm == 0
    tn = 512 if d % 512 == 0 else d
    nj = d // tn
    nxa = nx // bm
    n_all = nx + nc
    tpb = seq // bm
    mspec = functools.partial(_mod_spec, d=d, tiles_per_batch=tpb, n_batch=n_batch)
    return pl.pallas_call(
        functools.partial(_conv_in_kernel, nxa),
        grid=(n_all // bm, nj),
        in_specs=[
            pl.BlockSpec((bm, d), lambda i, j: (jnp.minimum(i, nxa - 1), 0)),
            pl.BlockSpec((bm, d), lambda i, j: (jnp.maximum(i - nxa, 0), 0),
                         pipeline_mode=pl.Buffered(1)),
            mspec(0), mspec(1),
            pl.BlockSpec((d, tn), lambda i, j: (0, j)),
            pl.BlockSpec((d, tn), lambda i, j: (0, nj + j)),
            pl.BlockSpec((d, tn), lambda i, j: (0, 2 * nj + j)),
        ],
        out_specs=[
            pl.BlockSpec((bm, tn), lambda i, j: (i, j)),
            pl.BlockSpec((bm, tn), lambda i, j: (i, j)),
        ],
        out_shape=[jax.ShapeDtypeStruct((n_all, d), BF16),
                   jax.ShapeDtypeStruct((n_all, d), BF16)],
        scratch_shapes=[pltpu.VMEM((bm, d), BF16)],
        compiler_params=_cparams(("arbitrary", "arbitrary")),
        name="conv_in",
    )(x2d, c2d, mod, mod, w_in_b, w_in_b, w_in_b)


def _ml_in_kernel(x_ref, sh_ref, sc_ref, w_ref, wkt_ref, wg_ref, bgate_ref,
                  u_ref, kt_ref, g_ref, h_ref):
    j = pl.program_id(1)

    @pl.when(j == 0)
    def _():
        h = (x_ref[...] * (1.0 + sc_ref[0]) + sh_ref[0]).astype(BF16)
        h_ref[...] = h
        g_ref[...] = jnp.dot(h, wg_ref[...], preferred_element_type=F32) + bgate_ref[...]
        kt_ref[...] = lax.dot_general(wkt_ref[...], h, (((1,), (1,)), ((), ())),
                                      preferred_element_type=F32).astype(BF16)

    u_ref[...] = jnp.dot(h_ref[...], w_ref[...], preferred_element_type=F32).astype(BF16)


def _ml_in(xall, mod, w_voq_b, w_kt_b, w_gate_pad, b_gate_pad, n_batch, seq, n_ctx_rows):
    n_all, d = xall.shape
    nu = w_voq_b.shape[1]
    hq = w_kt_b.shape[0]
    bm = min(PROJ_ROWS, seq, n_ctx_rows)
    assert seq % bm == 0 and n_ctx_rows % bm == 0
    tn = hq
    assert nu % tn == 0
    gl = w_gate_pad.shape[1]
    mspec = functools.partial(_mod_spec, d=d, tiles_per_batch=seq // bm, n_batch=n_batch)
    return pl.pallas_call(
        _ml_in_kernel,
        grid=(n_all // bm, nu // tn),
        in_specs=[
            pl.BlockSpec((bm, d), lambda i, j: (i, 0)),
            mspec(0), mspec(1),
            pl.BlockSpec((d, tn), lambda i, j: (0, j)),
            pl.BlockSpec((hq, d), lambda i, j: (0, 0)),
            pl.BlockSpec((d, gl), lambda i, j: (0, 0)),
            pl.BlockSpec((1, gl), lambda i, j: (0, 0)),
        ],
        out_specs=[
            pl.BlockSpec((bm, tn), lambda i, j: (i, j)),
            pl.BlockSpec((hq, bm), lambda i, j: (0, i)),
            pl.BlockSpec((bm, gl), lambda i, j: (i, 0)),
        ],
        out_shape=[jax.ShapeDtypeStruct((n_all, nu), BF16),
                   jax.ShapeDtypeStruct((hq, n_all), BF16),
                   jax.ShapeDtypeStruct((n_all, gl), F32)],
        scratch_shapes=[pltpu.VMEM((bm, d), BF16)],
        compiler_params=_cparams(("arbitrary", "arbitrary")),
        name="mlstm_in",
    )(xall, mod, mod, w_voq_b, w_kt_b, w_gate_pad, b_gate_pad)


def _dot_split3(a_b, x):
    hi = x.astype(BF16)
    r1 = x - hi.astype(F32)
    mid = r1.astype(BF16)
    lo = (r1 - mid.astype(F32)).astype(BF16)
    return (jnp.dot(a_b, hi, preferred_element_type=F32)
            + jnp.dot(a_b, mid, preferred_element_type=F32)
            + jnp.dot(a_b, lo, preferred_element_type=F32))


def _scan_kernel(n_heads, dqk, dv, q_ref, kt_ref, v_ref, g_ref, o_ref, ct_ref, m_ref):
    d = pl.program_id(1)
    s = pl.program_id(2)
    L = q_ref.shape[0]
    assert L == LANES
    scale = dqk ** -0.5

    @pl.when(s == 0)
    def _():
        ct_ref[...] = jnp.zeros_like(ct_ref)
        m_ref[...] = jnp.zeros_like(m_ref)

    H = n_heads
    heads = range(H)
    qi = lax.broadcasted_iota(I32, (L, L), 0)
    si = lax.broadcasted_iota(I32, (L, L), 1)
    fwd = d == 0
    mask = jnp.where(fwd, si - qi, qi - si) <= 0

    g = g_ref[...]
    b_all = _dot_split3(mask.astype(BF16), jax.nn.log_sigmoid(g))
    g_t = g.T
    b_t = b_all.T
    def lane_bcast(x, lane0):
        return jnp.stack([jnp.broadcast_to(x[:, lane0 + h:lane0 + h + 1], (L, LANES))
                          for h in heads])

    b_b = lane_bcast(b_all, H)
    i_b = lane_bcast(g, 0)
    b_end = jnp.where(fwd, b_b[:, L - 1:L, :], b_b[:, 0:1, :])
    m_st = m_ref[:, 0:1, :]
    tile = lambda x, n: jnp.concatenate([x] * (n // LANES), axis=-1)

    q3 = jnp.stack([q_ref[:, h * dqk:(h + 1) * dqk] for h in heads])
    kt3 = jnp.stack([kt_ref[h * dqk:(h + 1) * dqk, :] for h in heads])
    v3 = jnp.stack([v_ref[:, h * dv:(h + 1) * dv] for h in heads])

    r3 = jnp.stack([g_t[h:h + 1, :] - b_t[H + h:H + h + 1, :] for h in heads])
    dm = jnp.where(mask, b_b + r3, -jnp.inf)
    a_inter = b_b + m_st
    m_q = jnp.maximum(a_inter, jnp.max(dm, axis=-1, keepdims=True))
    inter = jnp.exp(a_inter - m_q) * scale
    sc = jnp.einsum("hqd,hds->hqs", q3, kt3, preferred_element_type=F32)
    p = jnp.exp(dm - m_q) * (sc * scale)
    ct = ct_ref[...]
    qc = jnp.einsum("hqd,hdv->hqv", q3, ct.astype(BF16), preferred_element_type=F32)
    v_ext = jnp.concatenate([v3, jnp.ones((H, L, LANES), BF16)], axis=-1)
    pv = jnp.einsum("hqs,hsv->hqv", p.astype(BF16), v_ext, preferred_element_type=F32)
    den = pv[:, :, dv:] + inter * qc[:, :, dv:]
    rden = 1.0 / jnp.maximum(jnp.abs(den), jnp.exp(-m_q))
    hout = (pv[:, :, :dv] + tile(inter, dv) * qc[:, :, :dv]) * tile(rden, dv)
    for h in heads:
        o_ref[:, h * dv:(h + 1) * dv] = hout[h]

    wl = b_end - b_b + i_b
    m_next = jnp.maximum(b_end + m_st, jnp.max(wl, axis=1, keepdims=True))
    decay = jnp.exp(b_end + m_st - m_next)
    w_b = jnp.exp(wl - m_next)
    vw = jnp.concatenate([v3.astype(F32) * tile(w_b, dv), w_b], axis=-1).astype(BF16)
    upd = jnp.einsum("hdl,hlv->hdv", kt3, vw, preferred_element_type=F32)
    ct_ref[...] = tile(decay, dv + LANES) * ct + upd
    m_ref[...] = jnp.broadcast_to(m_next, m_ref.shape)


def _scan(u, kt, gates, n_batch, seq, ctx_len, n_heads, dqk, dv):
    L = SCAN_CHUNK
    nx = n_batch * seq
    ncc = ctx_len // L
    nlc = seq // L
    assert ctx_len % L == 0 and seq % L == 0
    hq = n_heads * dqk
    hv = n_heads * dv
    assert hv == 2 * hq

    def row_blk(b, d, s):
        ctx = (nx + b * ctx_len) // L + jnp.where(d == 0, s, ncc - 1 - s)
        sl = s - ncc
        lat = (b * seq) // L + jnp.where(d == 0, sl, nlc - 1 - sl)
        return jnp.where(s < ncc, ctx, lat)

    def out_blk(b, d, s):
        sl = jnp.maximum(s - ncc, 0)
        return (b * seq) // L + jnp.where(d == 0, sl, nlc - 1 - sl)

    return pl.pallas_call(
        functools.partial(_scan_kernel, n_heads, dqk, dv),
        grid=(n_batch, 2, ncc + nlc),
        in_specs=[
            pl.BlockSpec((L, hq), lambda b, d, s: (row_blk(b, d, s), 2 * hv // hq)),
            pl.BlockSpec((hq, L), lambda b, d, s: (0, row_blk(b, d, s))),
            pl.BlockSpec((L, hv), lambda b, d, s: (row_blk(b, d, s), 0)),
            pl.BlockSpec((L, LANES), lambda b, d, s: (row_blk(b, d, s), d)),
        ],
        out_specs=pl.BlockSpec((None, L, hv), lambda b, d, s: (d, out_blk(b, d, s), 0)),
        out_shape=jax.ShapeDtypeStruct((2, nx, hv), F32),
        scratch_shapes=[pltpu.VMEM((n_heads, dqk, dv + LANES), F32),
                        pltpu.VMEM((n_heads, SUBLANES, LANES), F32)],
        compiler_params=_cparams(("arbitrary", "arbitrary", "arbitrary")),
        name="mlstm_scan",
    )(u, kt, u, gates)


def _mix_epilogue(alpha, rs, a, x_ref, g1_ref, sh2_ref, sc2_ref, wout_ref, lng_ref,
                  lnb_ref, rwt_ref, x1_ref, hx2_ref, logt_ref):
    mx = jnp.dot(a, wout_ref[...], preferred_element_type=F32)
    x1 = _layer_norm_rows(alpha * x_ref[rs, :] + g1_ref[0] * mx, lng_ref[...], lnb_ref[...])
    x1_ref[rs, :] = x1
    hx2 = x1 * (1.0 + sc2_ref[0]) + sh2_ref[0]
    hx2_ref[rs, :] = hx2
    logt_ref[:, rs] = lax.dot_general(rwt_ref[...], hx2.astype(BF16), (((1,), (1,)), ((), ())),
                                      preferred_element_type=F32)


def _sub_rows(tm, sub):
    return [slice(r, r + sub) for r in range(0, tm, sub)]


def _conv_out_kernel(alpha, ctx_mode, n_lat, tpb, sub, z_ref, zp_ref, zn_ref, bg_ref, cw_ref,
                     x_ref, g1_ref, sh2_ref, sc2_ref, wout_ref, lng_ref, lnb_ref, rwt_ref, *rest):
    x1_ref, hx2_ref, logt_ref = rest[-3:]
    i = pl.program_id(0)
    tm, d = z_ref.shape
    half = d // 2
    rows = lax.broadcasted_iota(I32, (sub, 1), 0)

    def conv1(zz, w3, period):
        pos = rows & (period - 1)
        prev = pltpu.roll(zz, 1, 0) * (pos != 0).astype(F32)
        nxt = pltpu.roll(zz, sub - 1, 0) * (pos != period - 1).astype(F32)
        return w3[0:1] * prev + w3[1:2] * zz + w3[2:3] * nxt

    def tile():
        cw = cw_ref[...]
        if not ctx_mode:
            ti = i % tpb
            up = zp_ref[...].astype(F32) * (ti > 0).astype(F32)
            dn = zn_ref[...].astype(F32) * (ti < tpb - 1).astype(F32)
            w3 = cw[:, half:]

        for rs in _sub_rows(tm, sub):
            z = z_ref[rs, :].astype(F32)
            bg = bg_ref[rs, :].astype(F32)
            if ctx_mode:
                a = (bg * conv1(z, cw, sub)).astype(BF16)
            else:
                a_row = (bg[:, :half] * conv1(z[:, :half], cw[:, :half], GRID_W)).astype(BF16)
                r0, r1 = rs.start, rs.stop
                f32_rows = lambda a, b: z_ref[a:b, half:].astype(F32)
                prev = (jnp.concatenate([up, f32_rows(0, r1 - GRID_W)], axis=0) if r0 == 0
                        else f32_rows(r0 - GRID_W, r1 - GRID_W))
                nxt = (jnp.concatenate([f32_rows(r0 + GRID_W, tm), dn], axis=0) if r1 == tm
                       else f32_rows(r0 + GRID_W, r1 + GRID_W))
                y = w3[0:1] * prev + w3[1:2] * z[:, half:] + w3[2:3] * nxt
                a = jnp.concatenate([a_row, (bg[:, half:] * y).astype(BF16)], axis=1)
            _mix_epilogue(alpha, rs, a, x_ref, g1_ref, sh2_ref, sc2_ref, wout_ref, lng_ref,
                          lnb_ref, rwt_ref, x1_ref, hx2_ref, logt_ref)

    if ctx_mode:
        tile()
    else:
        pl.when(i < n_lat)(tile)

        @pl.when(i >= n_lat)
        def _():
            x1_ref[...] = jnp.zeros_like(x1_ref)
            hx2_ref[...] = jnp.zeros_like(hx2_ref)
            logt_ref[...] = jnp.zeros_like(logt_ref)


def _ml_out_kernel(alpha, n_heads, dv, sub, hf_ref, hb_ref, og_ref, ng_ref, x_ref,
                   g1_ref, sh2_ref, sc2_ref, wout_ref, lng_ref, lnb_ref, rwt_ref,
                   x1_ref, hx2_ref, logt_ref):
    for rs in _sub_rows(hf_ref.shape[0], sub):
        parts = []
        for h in range(n_heads):
            sl = slice(h * dv, (h + 1) * dv)
            hs = hf_ref[rs, sl] + hb_ref[rs, sl]
            mu = jnp.mean(hs, axis=-1, keepdims=True)
            c = hs - mu
            var = jnp.mean(c * c, axis=-1, keepdims=True)
            hn = c * lax.rsqrt(var + HEAD_NORM_EPS)
            gate = jax.nn.sigmoid(og_ref[rs, sl].astype(F32))
            parts.append((hn * ng_ref[:, sl] * gate).astype(BF16))
        _mix_epilogue(alpha, rs, jnp.concatenate(parts, axis=1), x_ref, g1_ref, sh2_ref,
                      sc2_ref, wout_ref, lng_ref, lnb_ref, rwt_ref, x1_ref, hx2_ref, logt_ref)


def _mix_out_common(d, n_rows, n_exp, mod_row, tile0=0):
    tm = ROW_TILE
    mspec = lambda chunk: pl.BlockSpec((1, 1, d), lambda i: (mod_row(i), 0, chunk))
    const = lambda shape: pl.BlockSpec(shape, lambda i: (0,) * len(shape))
    in_specs = [mspec(2), mspec(3), mspec(4), const((d, d)), const((1, d)), const((1, d)),
                const((n_exp, d))]
    out_specs = [pl.BlockSpec((tm, d), lambda i: (i + tile0, 0)),
                 pl.BlockSpec((tm, d), lambda i: (i + tile0, 0)),
                 pl.BlockSpec((n_exp, tm), lambda i: (0, i + tile0))]
    out_shape = [jax.ShapeDtypeStruct((n_rows, d), F32),
                 jax.ShapeDtypeStruct((n_rows, d), F32),
                 jax.ShapeDtypeStruct((n_exp, n_rows), F32)]
    return in_specs, out_specs, out_shape


def _conv_out(alpha, z, bg, conv_w, x2d, c2d, mod, w_out_b, ln_g, ln_b, rwt_b, n_batch, seq,
              ctx_len):
    n_all, d = z.shape
    nx = x2d.shape[0]
    tm = ROW_TILE
    assert ctx_len == tm and seq % tm == 0 and MIX_SUB_ROWS % GRID_W == 0
    half = d // 2
    nxa = nx // tm
    tpb = seq // tm
    hpt = tm // GRID_W
    n_exp = rwt_b.shape[0]

    def z_specs(tile0):
        return [
            pl.BlockSpec((tm, d), lambda i: (i + tile0, 0)),
            pl.BlockSpec((GRID_W, half), lambda i: (jnp.maximum((i + tile0) * hpt - 1, 0), 1)),
            pl.BlockSpec((GRID_W, half), lambda i: ((i + tile0 + 1) * hpt - 1, 1)),
            pl.BlockSpec((tm, d), lambda i: (i + tile0, 0)),
            pl.BlockSpec((3, d), lambda i: (0, 0)),
            pl.BlockSpec((tm, d), lambda i: (i, 0)),
        ]

    lat = lambda i: jnp.minimum(i, nxa - 1)
    common_in, out_specs, out_shape = _mix_out_common(d, n_all, n_exp, lambda i: lat(i) // tpb)
    lat_specs = z_specs(0)
    lat_specs[2] = pl.BlockSpec((GRID_W, half), lambda i: ((lat(i) + 1) * hpt, 1))
    lat_specs[5] = pl.BlockSpec((tm, d), lambda i: (lat(i), 0))
    outs = pl.pallas_call(
        functools.partial(_conv_out_kernel, alpha, False, nxa, tpb, MIX_SUB_ROWS),
        grid=(n_all // tm,),
        in_specs=lat_specs + common_in,
        out_specs=out_specs,
        out_shape=out_shape,
        compiler_params=_cparams(("arbitrary",)),
        name="conv_out",
    )(z, z, z, bg, conv_w, x2d, mod, mod, mod, w_out_b, ln_g, ln_b, rwt_b)

    common_in, out_specs, out_shape = _mix_out_common(d, n_all, n_exp, lambda i: n_batch, nxa)
    n_in = 6 + len(common_in)
    keep = [pl.BlockSpec(memory_space=pl.ANY)] * 3
    return pl.pallas_call(
        functools.partial(_conv_out_kernel, alpha, True, None, tpb, tm),
        grid=(c2d.shape[0] // tm,),
        in_specs=z_specs(nxa) + common_in + keep,
        out_specs=out_specs,
        out_shape=out_shape,
        input_output_aliases={n_in: 0, n_in + 1: 1, n_in + 2: 2},
        compiler_params=_cparams(("arbitrary",)),
        name="conv_out_ctx",
    )(z, z, z, bg, conv_w, c2d, mod, mod, mod, w_out_b, ln_g, ln_b, rwt_b, *outs)


def _ml_out(alpha, hfb, u, norm_g, xall, mod, w_out_b, ln_g, ln_b, rwt_b, n_batch, seq,
            n_heads, dv):
    nx, d = hfb.shape[1:]
    tm = ROW_TILE
    tpb = seq // tm
    common_in, out_specs, out_shape = _mix_out_common(d, nx, rwt_b.shape[0], lambda i: i // tpb)
    o_blk = 1
    in_specs = [
        pl.BlockSpec((None, tm, d), lambda i: (0, i, 0)),
        pl.BlockSpec((None, tm, d), lambda i: (1, i, 0)),
        pl.BlockSpec((tm, d), lambda i: (i, o_blk)),
        pl.BlockSpec((1, d), lambda i: (0, 0)),
        pl.BlockSpec((tm, d), lambda i: (i, 0)),
    ] + common_in
    return pl.pallas_call(
        functools.partial(_ml_out_kernel, alpha, n_heads, dv, tm),
        grid=(nx // tm,),
        in_specs=in_specs,
        out_specs=out_specs,
        out_shape=out_shape,
        compiler_params=_cparams(("arbitrary",)),
        name="mlstm_out",
    )(hfb, hfb, u, norm_g, xall, mod, mod, mod, w_out_b, ln_g, ln_b, rwt_b)


def _route_kernel(tile_rows, n_te, logt_ref, rb_ref, pos_ref, wcol_ref, te_ref, meta_ref,
                  carry_ref, before_ref):
    ph = pl.program_id(0)
    i = pl.program_id(1)
    n_steps = pl.num_programs(1)
    n_exp, tr = logt_ref.shape
    epg = n_exp // N_GROUPS

    @pl.when((ph == 0) & (i == 0))
    def _():
        carry_ref[...] = jnp.zeros_like(carry_ref)
        ti = lax.broadcasted_iota(I32, (tr, tr), 0)
        tj = lax.broadcasted_iota(I32, (tr, tr), 1)
        before_ref[...] = (ti < tj).astype(BF16)

    s = jax.nn.sigmoid(logt_ref[...])
    sel = s + rb_ref[...]
    row = lax.broadcasted_iota(I32, (n_exp, tr), 0)
    member = row % epg
    group = row // epg

    def partner(x, k, idx, span, unit):
        wrapped = (idx + k) >= span
        up = pltpu.roll(x, n_exp - k * unit, 0)
        down = pltpu.roll(x, (span - k) * unit, 0)
        return jnp.where(wrapped, down, up), wrapped

    rank_in = jnp.zeros((n_exp, tr), F32)
    for k in range(1, epg):
        p, wrapped = partner(sel, k, member, epg, 1)
        beats = (p > sel) | ((p == sel) & wrapped)
        rank_in = rank_in + beats.astype(F32)
    top = (rank_in < TOP_K).astype(F32)
    gs = sel * top
    score = gs
    for k in range(1, epg):
        p, _ = partner(gs, k, member, epg, 1)
        score = score + p
    n_better = jnp.zeros((n_exp, tr), F32)
    for k in range(1, N_GROUPS):
        p, wrapped = partner(score, k, group, N_GROUPS, epg)
        beats = (p > score) | ((p == score) & wrapped)
        n_better = n_better + beats.astype(F32)
    best = n_better == 0.0
    m1 = (best & (rank_in == 0.0)).astype(F32)
    m2 = (best & (rank_in == 1.0)).astype(F32)
    s1 = jnp.sum(s * m1, axis=0, keepdims=True)
    s2 = jnp.sum(s * m2, axis=0, keepdims=True)
    den = s1 + s2
    oh = m1 + m2

    slot = jnp.dot(oh.astype(BF16), before_ref[...], preferred_element_type=F32) + carry_ref[...]

    @pl.when(ph == 1)
    def _():
        pos_ref[0:1, :] = jnp.sum(m1 * slot, axis=0, keepdims=True).astype(I32)
        pos_ref[1:2, :] = jnp.sum(m2 * slot, axis=0, keepdims=True).astype(I32)
        w8 = jnp.concatenate([s1 / den, s2 / den, jnp.zeros((LANES - 2, tr), F32)], axis=0)
        wcol_ref[...] = w8.T

    carry_ref[...] = carry_ref[...] + jnp.sum(oh, axis=1, keepdims=True)

    @pl.when((ph == 0) & (i == n_steps - 1))
    def _():
        cnt = carry_ref[...]
        ntile = jnp.floor((cnt + (tile_rows - 1)) * (1.0 / tile_rows))
        er = lax.broadcasted_iota(I32, (n_exp, LANES), 0)
        el = lax.broadcasted_iota(I32, (n_exp, LANES), 1)
        eye = (er == el).astype(F32)
        nt_row = jnp.sum(ntile * eye, axis=0, keepdims=True)
        cnt_row = jnp.sum(cnt * eye, axis=0, keepdims=True)
        cum_excl = jnp.sum(nt_row * (el < er).astype(F32), axis=1, keepdims=True)
        off = cum_excl * tile_rows
        off_row = jnp.sum(off * eye, axis=0, keepdims=True)
        total = jnp.sum(nt_row, axis=1, keepdims=True)
        cum_incl = cum_excl + ntile
        tl = lax.broadcasted_iota(I32, (n_exp, n_te), 1).astype(F32)
        te = jnp.sum((cum_incl <= tl).astype(F32), axis=0, keepdims=True)
        elf = el.astype(F32)
        later = (el > er) & (nt_row > 0.0) & (el < n_exp)
        nxt = jnp.min(jnp.where(later, elf, float(n_exp)), axis=1, keepdims=True)
        nxt = jnp.where(nxt == float(n_exp), er[:, 0:1].astype(F32), nxt)
        mine = (cum_excl <= tl) & (tl < cum_incl)
        te_next = jnp.sum(jnp.where(mine, nxt, 0.0), axis=0, keepdims=True)
        te_ref[...] = jnp.concatenate([jnp.minimum(te, n_exp - 1), te_next], axis=0).astype(I32)
        meta_ref[...] = jnp.concatenate(
            [cnt_row, off_row, jnp.broadcast_to(total, (1, LANES)),
             jnp.zeros((SUBLANES - 3, LANES), F32)], axis=0).astype(I32)
        carry_ref[...] = off


def _route(logt, router_b, tile_rows):
    n_exp, n = logt.shape
    tr = ROUTE_COLS
    while n % tr:
        tr //= 2
    n_te = 256
    assert (2 * n) // tile_rows + n_exp <= n_te
    return pl.pallas_call(
        functools.partial(_route_kernel, tile_rows, n_te),
        grid=(2, n // tr),
        in_specs=[pl.BlockSpec((n_exp, tr), lambda p, i: (0, i)),
                  pl.BlockSpec((n_exp, 1), lambda p, i: (0, 0))],
        out_specs=[pl.BlockSpec((2, tr), lambda p, i: (0, i * p)),
                   pl.BlockSpec((tr, LANES), lambda p, i: (i * p, 0)),
                   pl.BlockSpec((2, n_te), lambda p, i: (0, 0)),
                   pl.BlockSpec((SUBLANES, LANES), lambda p, i: (0, 0))],
        out_shape=[jax.ShapeDtypeStruct((2, n), I32),
                   jax.ShapeDtypeStruct((n, LANES), F32),
                   jax.ShapeDtypeStruct((2, n_te), I32),
                   jax.ShapeDtypeStruct((SUBLANES, LANES), I32)],
        scratch_shapes=[pltpu.VMEM((n_exp, 1), F32), pltpu.VMEM((tr, tr), BF16)],
        compiler_params=_cparams(("arbitrary", "arbitrary")),
        name="route",
    )(logt, router_b.reshape(n_exp, 1).astype(F32))


def _tbl_kernel(n_tok, n_exp, tile_rows, meta_ref, pos_ref, tbl_ref):
    i = pl.program_id(0)
    blk = pos_ref.shape[1]

    @pl.when(i == 0)
    def _():
        n_pad = 0
        for e in range(n_exp):
            cnt = meta_ref[e]
            off = meta_ref[n_exp + e]
            up = ((cnt + (tile_rows - 1)) // tile_rows) * tile_rows

            def fill(r, j):
                tbl_ref[off + r] = (2 * n_tok + j) << SLOT_SRC_BITS
                return j + 1

            n_pad = lax.fori_loop(cnt, up, fill, n_pad)

        def fill_tail(r, carry):
            tbl_ref[r] = (2 * n_tok) << SLOT_SRC_BITS
            return carry

        lax.fori_loop(2 * n_tok + n_pad, tbl_ref.shape[0], fill_tail, 0)

    base = i * blk
    first = (base << SLOT_SRC_BITS) | jnp.where(base >= n_tok, base - n_tok, base)
    step = (1 << SLOT_SRC_BITS) + 1
    group = 8

    def body(q, carry):
        r0 = q * group
        slots = [pos_ref[0, r0 + j] for j in range(group)]
        for j in range(group):
            tbl_ref[slots[j]] = first + (r0 + j) * step
        return carry

    lax.fori_loop(0, blk // group, body, 0, unroll=True)


def _build_table(meta1d, pos, n_tok, n_exp, tile_rows, p_pad):
    blk = TBL_BLOCK
    while n_tok % blk:
        blk //= 2
    n_pairs = 2 * n_tok
    grid_spec = pltpu.PrefetchScalarGridSpec(
        num_scalar_prefetch=1,
        grid=(n_pairs // blk,),
        in_specs=[pl.BlockSpec((None, 1, blk), lambda i, m: (i, 0, 0), memory_space=pltpu.SMEM)],
        out_specs=pl.BlockSpec(memory_space=pltpu.SMEM),
    )
    return pl.pallas_call(
        functools.partial(_tbl_kernel, n_tok, n_exp, tile_rows),
        grid_spec=grid_spec,
        out_shape=jax.ShapeDtypeStruct((p_pad,), I32),
        compiler_params=_cparams(("arbitrary",)),
        name="slot_table",
    )(meta1d, pos.reshape(n_pairs // blk, 1, blk))


def _moe_kernel(layer, te_ref, ten_ref, nt_ref, tbl_ref, tbln_ref, tblp_ref, hx_hbm,
                wg_hbm, wu_hbm, wd_hbm, y_hbm, xbuf, ybuf, zbuf, wg_b, wu_b, wd_b,
                wg_s, wu_s, wd_s, gsem, ssem, zsem, wsem):
    t = pl.program_id(0)
    n = nt_ref[0]
    rows = tbl_ref.shape[1]
    lt = hx_hbm.shape[1]
    staged = ((wg_hbm, wg_s, wg_b), (wu_hbm, wu_s, wu_b), (wd_hbm, wd_s, wd_b))

    def weight_copies(e):
        return [pltpu.make_async_copy(hbm.at[layer, e], stage, wsem.at[k])
                for k, (hbm, stage, _) in enumerate(staged)]

    @pl.when(t == 0)
    def _():
        for c in weight_copies(te_ref[0]):
            c.start(priority=WEIGHT_DMA_PRIORITY)

    tl = jnp.minimum(t, n - 1)
    new_expert = (t < n) & ((t == 0) | (te_ref[tl] != te_ref[jnp.maximum(tl - 1, 0)]))

    @pl.when(new_expert)
    def _():
        for c, (_, stage, dst) in zip(weight_copies(te_ref[tl]), staged):
            c.wait()
            dst[...] = stage[...].astype(BF16)
        for c in weight_copies(ten_ref[tl]):
            c.start(priority=WEIGHT_DMA_PRIORITY)

    def start_gather(tref, sl, lo=0, hi=None):
        for r in range(lo, rows if hi is None else hi):
            src = tref[0, r] & SLOT_SRC_MASK
            pltpu.make_async_copy(hx_hbm.at[src], xbuf.at[sl, pl.ds(r * lt, lt), :],
                                  gsem.at[sl]).start()

    def start_scatter(tref, sl, lo=0, hi=None):
        for r in range(lo, rows if hi is None else hi):
            dst = tref[0, r] >> SLOT_SRC_BITS
            pltpu.make_async_copy(ybuf.at[sl, pl.ds(r, 1), :],
                                  y_hbm.at[pl.ds(dst, 1), :], ssem.at[sl]
                                  ).start(priority=SCATTER_DMA_PRIORITY)

    def wait_gather(sl):
        pltpu.make_async_copy(xbuf.at[sl], xbuf.at[sl], gsem.at[sl]).wait()

    def wait_scatter(sl):
        pltpu.make_async_copy(ybuf.at[sl], ybuf.at[sl], ssem.at[sl]).wait()

    @pl.when(t == 0)
    def _():
        zbuf[...] = jnp.zeros_like(zbuf)
        start_gather(tbl_ref, 0)

    @pl.when(t >= n)
    def _():
        dst = y_hbm.at[pl.ds(pl.multiple_of(t * rows, rows), rows), :]
        fill = pltpu.make_async_copy(zbuf, dst, zsem)
        fill.start()
        fill.wait()

    def tile(slot, has_prev):
        other = 1 - slot
        wait_gather(slot)
        f = wg_b.shape[1]
        d = wd_b.shape[1]
        fc, dc = MOE_UP_COLS, MOE_DOWN_COLS
        n_up, n_down = f // fc, d // dc
        n_g = MOE_GATHER_GROUPS
        n_s = n_up + n_down - n_g
        g_bounds = [(k * rows) // n_g for k in range(n_g + 1)]
        s_bounds = [(k * rows) // n_s for k in range(n_s + 1)]

        def copy_group(k):
            if k < n_g:
                start_gather(tbln_ref, other, g_bounds[k], g_bounds[k + 1])
            elif has_prev:
                start_scatter(tblp_ref, other, s_bounds[k - n_g], s_bounds[k - n_g + 1])

        xb = jnp.concatenate(
            [xbuf[slot, pl.ds(c, rows, stride=lt), :].astype(BF16) for c in range(lt)], axis=1)
        hs = []
        for c in range(f // fc):
            cols = slice(c * fc, (c + 1) * fc)
            g = jnp.dot(xb, wg_b[:, cols], preferred_element_type=F32)
            u = jnp.dot(xb, wu_b[:, cols], preferred_element_type=F32)
            hs.append((g * jax.nn.sigmoid(g) * u).astype(BF16))
            copy_group(c)
        h = jnp.concatenate(hs, axis=1)

        @pl.when(t >= 2)
        def _():
            wait_scatter(slot)

        for c in range(d // dc):
            cols = slice(c * dc, (c + 1) * dc)
            ybuf[slot, :, cols] = jnp.dot(h, wd_b[:, cols], preferred_element_type=F32)
            copy_group(f // fc + c)

        @pl.when(t == n - 1)
        def _():
            start_scatter(tbl_ref, slot)
            wait_gather(other)

            @pl.when(t >= 1)
            def _():
                wait_scatter(other)

            wait_scatter(slot)
            for c in weight_copies(ten_ref[tl]):
                c.wait()

    pl.when((t < n) & (t == 0))(functools.partial(tile, 0, False))
    for parity in (0, 1):
        pl.when((t < n) & (t > 0) & (lax.rem(t, 2) == parity))(
            functools.partial(tile, parity, True))


def _moe(te2, nt1d, tbl, hx2, w_gate, w_up, w_down, layer, n_tok, p_pad):
    tmm = MOE_ROWS
    _, n_exp, d, f = w_gate.shape
    t_max = (2 * n_tok) // tmm + n_exp
    last = lambda t, nt: jnp.minimum(t, nt[0] - 1)
    tspec = lambda fn: pl.BlockSpec((None, 1, tmm), lambda t, te, ten, nt: (fn(t, nt), 0, 0),
                                    memory_space=pltpu.SMEM)
    hbm = pl.BlockSpec(memory_space=pl.ANY)
    grid_spec = pltpu.PrefetchScalarGridSpec(
        num_scalar_prefetch=3,
        grid=(t_max,),
        in_specs=[
            tspec(last),
            tspec(lambda t, nt: last(t + 1, nt)),
            tspec(lambda t, nt: jnp.maximum(last(t, nt) - 1, 0)),
            hbm, hbm, hbm, hbm,
        ],
        out_specs=hbm,
        scratch_shapes=[pltpu.VMEM((2, tmm * (d // LANES), LANES), F32),
                        pltpu.VMEM((2, tmm, d), F32),
                        pltpu.VMEM((tmm, d), F32),
                        pltpu.VMEM((d, f), BF16), pltpu.VMEM((d, f), BF16),
                        pltpu.VMEM((f, d), BF16),
                        pltpu.VMEM((d, f), F32), pltpu.VMEM((d, f), F32),
                        pltpu.VMEM((f, d), F32),
                        pltpu.SemaphoreType.DMA((2,)), pltpu.SemaphoreType.DMA((2,)),
                        pltpu.SemaphoreType.DMA(()), pltpu.SemaphoreType.DMA((3,))],
    )
    tbl3 = tbl.reshape(p_pad // tmm, 1, tmm)
    return pl.pallas_call(
        functools.partial(_moe_kernel, layer),
        grid_spec=grid_spec,
        out_shape=jax.ShapeDtypeStruct((t_max * tmm, d), F32),
        compiler_params=_cparams(("arbitrary",)),
        name="moe_experts",
    )(te2[0], te2[1], nt1d, tbl3, tbl3, tbl3, hx2.reshape(n_tok, d // LANES, LANES),
      w_gate, w_up, w_down)


def _comb_kernel(alpha, x_ref, y0_ref, y1_ref, w_ref, g2_ref, lng_ref, lnb_ref, o_ref):
    w = w_ref[...]
    ex = w[:, 0:1] * y0_ref[...] + w[:, 1:2] * y1_ref[...]
    o_ref[...] = _layer_norm_rows(alpha * x_ref[...] + g2_ref[0] * ex, lng_ref[...], lnb_ref[...])


def _combine(alpha, x1, y, wcol, mod, ln_g, ln_b, n_tok, n_out, n_batch, seq):
    d = x1.shape[1]
    tm = COMB_ROWS
    assert n_tok % tm == 0 and n_out % tm == 0 and seq % tm == 0
    nblk = n_tok // tm
    mspec = functools.partial(_mod_spec, d=d, tiles_per_batch=seq // tm, n_batch=n_batch)
    return pl.pallas_call(
        functools.partial(_comb_kernel, alpha),
        grid=(n_out // tm,),
        in_specs=[
            pl.BlockSpec((tm, d), lambda i: (i, 0)),
            pl.BlockSpec((tm, d), lambda i: (i, 0)),
            pl.BlockSpec((tm, d), lambda i: (i + nblk, 0)),
            pl.BlockSpec((tm, LANES), lambda i: (i, 0)),
            mspec(5),
            pl.BlockSpec((1, d), lambda i: (0, 0)),
            pl.BlockSpec((1, d), lambda i: (0, 0)),
        ],
        out_specs=pl.BlockSpec((tm, d), lambda i: (i, 0)),
        out_shape=jax.ShapeDtypeStruct((n_out, d), F32),
        compiler_params=_cparams(("arbitrary",)),
        name="moe_combine",
    )(x1, y, y, wcol, mod, ln_g, ln_b)


def _moe_layer(alpha, x1, hx2, logt, mod, ln_g, ln_b, router_b, w_gate, w_up, w_down, layer,
               n_out, n_batch, seq):
    n_tok = hx2.shape[0]
    n_exp = w_gate.shape[1]
    tmm = MOE_ROWS
    p_max = 2 * n_tok + n_exp * tmm
    p_pad = -(-p_max // TBL_BLOCK) * TBL_BLOCK
    assert n_tok <= 1 << SLOT_SRC_BITS and p_max < 1 << (31 - SLOT_SRC_BITS)
    pos, wcol, te, meta = _route(logt, router_b, tmm)
    meta1d = meta[:2, :n_exp].reshape(-1)
    tbl = _build_table(meta1d, pos, n_tok, n_exp, tmm, p_pad)
    y = _moe(te, meta[2, :1], tbl, hx2, w_gate, w_up, w_down, layer, n_tok, p_pad)
    return _combine(alpha, x1, y, wcol, mod, ln_g, ln_b, n_tok, n_out, n_batch, seq)


def kernel(x, c, ctx, c_ctx, w_ada, b_ada, ln_g, ln_b, conv_w_in, conv_w, conv_w_out, ml_w_in, ml_w_gate, ml_b_gate, ml_norm_g, ml_w_out, router_w, router_b, exp_w_gate, exp_w_up, exp_w_down):
    n_batch, seq, d = x.shape
    ctx_len = ctx.shape[1]
    depth = w_ada.shape[0]
    assert depth == 2, "layer 0 is the conv mixer, layer 1 the mLSTM mixer"
    alpha = (2 * depth) ** 0.25
    n_heads = ml_b_gate.shape[-1] // 4
    dqk = d // (2 * n_heads)
    dv = d // n_heads
    nx = n_batch * seq
    nc = n_batch * ctx_len
    assert n_batch < SUBLANES and 2 * n_heads <= LANES

    x2d = x.reshape(nx, d)
    c2d = ctx.reshape(nc, d)
    cc = jnp.zeros((SUBLANES, d), F32).at[:n_batch].set(c).at[n_batch].set(c_ctx)
    mod = _ada(cc, w_ada, b_ada).reshape(depth, SUBLANES, 1, 6 * d)
    rwt_b = router_w.T.astype(BF16)

    bg, z = _conv_in(x2d, c2d, mod[0], _to_bf16(conv_w_in, 0), n_batch, seq)
    x1, hx2, logt = _conv_out(alpha, z, bg, conv_w[0], x2d, c2d, mod[0],
                              _to_bf16(conv_w_out, 0), ln_g[0, 0:1], ln_b[0, 0:1], rwt_b,
                              n_batch, seq, ctx_len)
    xall = _moe_layer(alpha, x1, hx2, logt, mod[0], ln_g[0, 1:2], ln_b[0, 1:2], router_b,
                      exp_w_gate, exp_w_up, exp_w_down, 0, nx + nc, n_batch, seq)

    w_gate = ml_w_gate[0]
    b_gate = ml_b_gate[0]
    h2 = 2 * n_heads
    w_gate_pad = (jnp.zeros((d, 2 * LANES), F32).at[:, :h2].set(w_gate[:, :h2])
                  .at[:, LANES:LANES + h2].set(w_gate[:, h2:])).astype(BF16)
    b_gate_pad = (jnp.zeros((1, 2 * LANES), F32).at[0, :h2].set(b_gate[:h2])
                  .at[0, LANES:LANES + h2].set(b_gate[h2:]))
    w_voq_b, w_kt_b = _ml_weights(ml_w_in, 0, n_heads * dqk)
    u, kt, gates = _ml_in(xall, mod[1], w_voq_b, w_kt_b, w_gate_pad, b_gate_pad, n_batch,
                          seq, nc)
    hfb = _scan(u, kt, gates, n_batch, seq, ctx_len, n_heads, dqk, dv)
    x1, hx2, logt = _ml_out(alpha, hfb, u, ml_norm_g[0:1], xall, mod[1],
                            _to_bf16(ml_w_out, 0), ln_g[1, 0:1], ln_b[1, 0:1], rwt_b,
                            n_batch, seq, n_heads, dv)
    out = _moe_layer(alpha, x1, hx2, logt, mod[1], ln_g[1, 1:2], ln_b[1, 1:2], router_b,
                     exp_w_gate, exp_w_up, exp_w_down, 1, nx, n_batch, seq)
    return out.reshape(n_batch, seq, d)
```

```python
import functools

import jax
import jax.numpy as jnp
from jax import lax
from jax.experimental import pallas as pl
from jax.experimental.pallas import tpu as pltpu

F32 = jnp.float32
BF16 = jnp.bfloat16
I32 = jnp.int32

GRID_W = 64
N_GROUPS = 4
TOP_K = 2
LN_EPS = 1e-5
HEAD_NORM_EPS = 1e-6

LANES = 128
SUBLANES = 8
VMEM_LIMIT_BYTES = 56 * 1024 * 1024

ROW_TILE = 256
MIX_SUB_ROWS = 128
PROJ_ROWS = 1024
MOE_ROWS = 256
MOE_UP_COLS = 256
MOE_DOWN_COLS = 512
MOE_GATHER_GROUPS = 1
ROUTE_COLS = 1024
COMB_ROWS = 512
SCAN_CHUNK = 128
TBL_BLOCK = 1024
SLOT_SRC_BITS = 15
SLOT_SRC_MASK = (1 << SLOT_SRC_BITS) - 1
CAST_BLOCK_BYTES = 8 * 1024 * 1024
WEIGHT_DMA_PRIORITY = 1
SCATTER_DMA_PRIORITY = 1


def _cparams(sem):
    return pltpu.CompilerParams(dimension_semantics=sem, vmem_limit_bytes=VMEM_LIMIT_BYTES)


def _layer_norm_rows(r, g, b):
    mu = jnp.mean(r, axis=-1, keepdims=True)
    c = r - mu
    var = jnp.mean(c * c, axis=-1, keepdims=True)
    return c * lax.rsqrt(var + LN_EPS) * g + b


def _ada_kernel(cc_ref, w_ref, b_ref, o_ref):
    a = cc_ref[...]
    a = (a * jax.nn.sigmoid(a)).astype(BF16)
    o_ref[0] = jnp.dot(a, w_ref[0].astype(BF16), preferred_element_type=F32) + b_ref[0]


def _ada(cc, w_ada, b_ada):
    depth, d, n6 = w_ada.shape
    tn = 1024
    return pl.pallas_call(
        _ada_kernel,
        grid=(depth, n6 // tn),
        in_specs=[
            pl.BlockSpec((SUBLANES, d), lambda l, j: (0, 0)),
            pl.BlockSpec((1, d, tn), lambda l, j: (l, 0, j)),
            pl.BlockSpec((1, 1, tn), lambda l, j: (l, 0, j)),
        ],
        out_specs=pl.BlockSpec((1, SUBLANES, tn), lambda l, j: (l, 0, j)),
        out_shape=jax.ShapeDtypeStruct((depth, SUBLANES, n6), F32),
        compiler_params=_cparams(("arbitrary", "arbitrary")),
        name="ada_mod",
    )(cc, w_ada, b_ada.reshape(depth, 1, n6))


def _cast_kernel(w_ref, o_ref):
    o_ref[...] = w_ref[...].astype(BF16)


def _to_bf16(w, layer):
    rows, cols = w.shape[-2:]
    w4 = w.reshape(w.shape[0], -1, rows, cols)
    m = w4.shape[1]
    rb = rows
    while rb * cols * 4 > CAST_BLOCK_BYTES and rb % (4 * SUBLANES) == 0:
        rb //= 2
    out = pl.pallas_call(
        _cast_kernel,
        grid=(m, rows // rb),
        in_specs=[pl.BlockSpec((None, None, rb, cols), lambda e, r: (layer, e, r, 0))],
        out_specs=pl.BlockSpec((None, rb, cols), lambda e, r: (e, r, 0)),
        out_shape=jax.ShapeDtypeStruct((m, rows, cols), BF16),
        compiler_params=_cparams(("arbitrary", "arbitrary")),
        name="to_bf16",
    )(w4)
    return out.reshape(w.shape[1:])


def _cast_t_kernel(w_ref, o_ref):
    o_ref[...] = w_ref[...].T.astype(BF16)


def _ml_weights(w, layer, hq):
    _, d, ncol = w.shape
    nb = ncol // hq
    rb = min(d, 512)
    voq = pl.pallas_call(
        _cast_kernel,
        grid=(d // rb, nb - 1),
        in_specs=[pl.BlockSpec((None, rb, hq), lambda r, k: (layer, r, (k + 2) % nb))],
        out_specs=pl.BlockSpec((rb, hq), lambda r, k: (r, k)),
        out_shape=jax.ShapeDtypeStruct((d, ncol - hq), BF16),
        compiler_params=_cparams(("arbitrary", "arbitrary")),
        name="to_bf16_voq",
    )(w)
    kt = pl.pallas_call(
        _cast_t_kernel,
        grid=(d // rb,),
        in_specs=[pl.BlockSpec((None, rb, hq), lambda r: (layer, r, 1))],
        out_specs=pl.BlockSpec((hq, rb), lambda r: (0, r)),
        out_shape=jax.ShapeDtypeStruct((hq, d), BF16),
        compiler_params=_cparams(("arbitrary",)),
        name="to_bf16_kt",
    )(w)
    return voq, kt


def _mod_spec(chunk, d, tiles_per_batch, n_batch):
    return pl.BlockSpec(
        (1, 1, d),
        lambda i, *_: (jnp.minimum(i // tiles_per_batch, n_batch), 0, chunk))


def _conv_in_kernel(nxa, xa_ref, xb_ref, sh_ref, sc_ref, wb_ref, wc_ref, wv_ref,
                    bg_ref, z_ref, h_ref):
    i = pl.program_id(0)
    j = pl.program_id(1)

    @pl.when((j == 0) & (i < nxa))
    def _():
        h_ref[...] = (xa_ref[...] * (1.0 + sc_ref[0]) + sh_ref[0]).astype(BF16)

    @pl.when((j == 0) & (i >= nxa))
    def _():
        h_ref[...] = (xb_ref[...] * (1.0 + sc_ref[0]) + sh_ref[0]).astype(BF16)

    h = h_ref[...]
    bg = jnp.dot(h, wb_ref[...], preferred_element_type=F32)
    cg = jnp.dot(h, wc_ref[...], preferred_element_type=F32)
    v = jnp.dot(h, wv_ref[...], preferred_element_type=F32)
    bg_ref[...] = bg.astype(BF16)
    z_ref[...] = (cg * v).astype(BF16)


def _conv_in(x2d, c2d, mod, w_in_b, n_batch, seq):
    nx, d = x2d.shape
    nc = c2d.shape[0]
    bm = min(PROJ_ROWS, seq, nc)
    assert seq % bm == 0 and nc % bm == 0
    tn = 512 if d % 512 == 0 else d
    nj = d // tn
    nxa = nx // bm
    n_all = nx + nc
    tpb = seq // bm
    mspec = functools.partial(_mod_spec, d=d, tiles_per_batch=tpb, n_batch=n_batch)
    return pl.pallas_call(
        functools.partial(_conv_in_kernel, nxa),
        grid=(n_all // bm, nj),
        in_specs=[
            pl.BlockSpec((bm, d), lambda i, j: (jnp.minimum(i, nxa - 1), 0)),
            pl.BlockSpec((bm, d), lambda i, j: (jnp.maximum(i - nxa, 0), 0),
                         pipeline_mode=pl.Buffered(1)),
            mspec(0), mspec(1),
            pl.BlockSpec((d, tn), lambda i, j: (0, j)),
            pl.BlockSpec((d, tn), lambda i, j: (0, nj + j)),
            pl.BlockSpec((d, tn), lambda i, j: (0, 2 * nj + j)),
        ],
        out_specs=[
            pl.BlockSpec((bm, tn), lambda i, j: (i, j)),
            pl.BlockSpec((bm, tn), lambda i, j: (i, j)),
        ],
        out_shape=[jax.ShapeDtypeStruct((n_all, d), BF16),
                   jax.ShapeDtypeStruct((n_all, d), BF16)],
        scratch_shapes=[pltpu.VMEM((bm, d), BF16)],
        compiler_params=_cparams(("arbitrary", "arbitrary")),
        name="conv_in",
    )(x2d, c2d, mod, mod, w_in_b, w_in_b, w_in_b)


def _ml_in_kernel(x_ref, sh_ref, sc_ref, w_ref, wkt_ref, wg_ref, bgate_ref,
                  u_ref, kt_ref, g_ref, h_ref):
    j = pl.program_id(1)

    @pl.when(j == 0)
    def _():
        h = (x_ref[...] * (1.0 + sc_ref[0]) + sh_ref[0]).astype(BF16)
        h_ref[...] = h
        g_ref[...] = jnp.dot(h, wg_ref[...], preferred_element_type=F32) + bgate_ref[...]
        kt_ref[...] = lax.dot_general(wkt_ref[...], h, (((1,), (1,)), ((), ())),
                                      preferred_element_type=F32).astype(BF16)

    u_ref[...] = jnp.dot(h_ref[...], w_ref[...], preferred_element_type=F32).astype(BF16)


def _ml_in(xall, mod, w_voq_b, w_kt_b, w_gate_pad, b_gate_pad, n_batch, seq, n_ctx_rows):
    n_all, d = xall.shape
    nu = w_voq_b.shape[1]
    hq = w_kt_b.shape[0]
    bm = min(PROJ_ROWS, seq, n_ctx_rows)
    assert seq % bm == 0 and n_ctx_rows % bm == 0
    tn = hq
    assert nu % tn == 0
    gl = w_gate_pad.shape[1]
    mspec = functools.partial(_mod_spec, d=d, tiles_per_batch=seq // bm, n_batch=n_batch)
    return pl.pallas_call(
        _ml_in_kernel,
        grid=(n_all // bm, nu // tn),
        in_specs=[
            pl.BlockSpec((bm, d), lambda i, j: (i, 0)),
            mspec(0), mspec(1),
            pl.BlockSpec((d, tn), lambda i, j: (0, j)),
            pl.BlockSpec((hq, d), lambda i, j: (0, 0)),
            pl.BlockSpec((d, gl), lambda i, j: (0, 0)),
            pl.BlockSpec((1, gl), lambda i, j: (0, 0)),
        ],
        out_specs=[
            pl.BlockSpec((bm, tn), lambda i, j: (i, j)),
            pl.BlockSpec((hq, bm), lambda i, j: (0, i)),
            pl.BlockSpec((bm, gl), lambda i, j: (i, 0)),
        ],
        out_shape=[jax.ShapeDtypeStruct((n_all, nu), BF16),
                   jax.ShapeDtypeStruct((hq, n_all), BF16),
                   jax.ShapeDtypeStruct((n_all, gl), F32)],
        scratch_shapes=[pltpu.VMEM((bm, d), BF16)],
        compiler_params=_cparams(("arbitrary", "arbitrary")),
        name="mlstm_in",
    )(xall, mod, mod, w_voq_b, w_kt_b, w_gate_pad, b_gate_pad)


def _dot_split3(a_b, x):
    hi = x.astype(BF16)
    r1 = x - hi.astype(F32)
    mid = r1.astype(BF16)
    lo = (r1 - mid.astype(F32)).astype(BF16)
    return (jnp.dot(a_b, hi, preferred_element_type=F32)
            + jnp.dot(a_b, mid, preferred_element_type=F32)
            + jnp.dot(a_b, lo, preferred_element_type=F32))


def _scan_kernel(n_heads, dqk, dv, q_ref, kt_ref, v_ref, g_ref, o_ref, ct_ref, m_ref):
    d = pl.program_id(1)
    s = pl.program_id(2)
    L = q_ref.shape[0]
    assert L == LANES
    scale = dqk ** -0.5

    @pl.when(s == 0)
    def _():
        ct_ref[...] = jnp.zeros_like(ct_ref)
        m_ref[...] = jnp.zeros_like(m_ref)

    H = n_heads
    heads = range(H)
    qi = lax.broadcasted_iota(I32, (L, L), 0)
    si = lax.broadcasted_iota(I32, (L, L), 1)
    fwd = d == 0
    mask = jnp.where(fwd, si - qi, qi - si) <= 0

    g = g_ref[...]
    b_all = _dot_split3(mask.astype(BF16), jax.nn.log_sigmoid(g))
    g_t = g.T
    b_t = b_all.T
    def lane_bcast(x, lane0):
        return jnp.stack([jnp.broadcast_to(x[:, lane0 + h:lane0 + h + 1], (L, LANES))
                          for h in heads])

    b_b = lane_bcast(b_all, H)
    i_b = lane_bcast(g, 0)
    b_end = jnp.where(fwd, b_b[:, L - 1:L, :], b_b[:, 0:1, :])
    m_st = m_ref[:, 0:1, :]
    tile = lambda x, n: jnp.concatenate([x] * (n // LANES), axis=-1)

    q3 = jnp.stack([q_ref[:, h * dqk:(h + 1) * dqk] for h in heads])
    kt3 = jnp.stack([kt_ref[h * dqk:(h + 1) * dqk, :] for h in heads])
    v3 = jnp.stack([v_ref[:, h * dv:(h + 1) * dv] for h in heads])

    r3 = jnp.stack([g_t[h:h + 1, :] - b_t[H + h:H + h + 1, :] for h in heads])
    dm = jnp.where(mask, b_b + r3, -jnp.inf)
    a_inter = b_b + m_st
    m_q = jnp.maximum(a_inter, jnp.max(dm, axis=-1, keepdims=True))
    inter = jnp.exp(a_inter - m_q) * scale
    sc = jnp.einsum("hqd,hds->hqs", q3, kt3, preferred_element_type=F32)
    p = jnp.exp(dm - m_q) * (sc * scale)
    ct = ct_ref[...]
    qc = jnp.einsum("hqd,hdv->hqv", q3, ct.astype(BF16), preferred_element_type=F32)
    v_ext = jnp.concatenate([v3, jnp.ones((H, L, LANES), BF16)], axis=-1)
    pv = jnp.einsum("hqs,hsv->hqv", p.astype(BF16), v_ext, preferred_element_type=F32)
    den = pv[:, :, dv:] + inter * qc[:, :, dv:]
    rden = 1.0 / jnp.maximum(jnp.abs(den), jnp.exp(-m_q))
    hout = (pv[:, :, :dv] + tile(inter, dv) * qc[:, :, :dv]) * tile(rden, dv)
    for h in heads:
        o_ref[:, h * dv:(h + 1) * dv] = hout[h]

    wl = b_end - b_b + i_b
    m_next = jnp.maximum(b_end + m_st, jnp.max(wl, axis=1, keepdims=True))
    decay = jnp.exp(b_end + m_st - m_next)
    w_b = jnp.exp(wl - m_next)
    vw = jnp.concatenate([v3.astype(F32) * tile(w_b, dv), w_b], axis=-1).astype(BF16)
    upd = jnp.einsum("hdl,hlv->hdv", kt3, vw, preferred_element_type=F32)
    ct_ref[...] = tile(decay, dv + LANES) * ct + upd
    m_ref[...] = jnp.broadcast_to(m_next, m_ref.shape)


def _scan(u, kt, gates, n_batch, seq, ctx_len, n_heads, dqk, dv):
    L = SCAN_CHUNK
    nx = n_batch * seq
    ncc = ctx_len // L
    nlc = seq // L
    assert ctx_len % L == 0 and seq % L == 0
    hq = n_heads * dqk
    hv = n_heads * dv
    assert hv == 2 * hq

    def row_blk(b, d, s):
        ctx = (nx + b * ctx_len) // L + jnp.where(d == 0, s, ncc - 1 - s)
        sl = s - ncc
        lat = (b * seq) // L + jnp.where(d == 0, sl, nlc - 1 - sl)
        return jnp.where(s < ncc, ctx, lat)

    def out_blk(b, d, s):
        sl = jnp.maximum(s - ncc, 0)
        return (b * seq) // L + jnp.where(d == 0, sl, nlc - 1 - sl)

    return pl.pallas_call(
        functools.partial(_scan_kernel, n_heads, dqk, dv),
        grid=(n_batch, 2, ncc + nlc),
        in_specs=[
            pl.BlockSpec((L, hq), lambda b, d, s: (row_blk(b, d, s), 2 * hv // hq)),
            pl.BlockSpec((hq, L), lambda b, d, s: (0, row_blk(b, d, s))),
            pl.BlockSpec((L, hv), lambda b, d, s: (row_blk(b, d, s), 0)),
            pl.BlockSpec((L, LANES), lambda b, d, s: (row_blk(b, d, s), d)),
        ],
        out_specs=pl.BlockSpec((None, L, hv), lambda b, d, s: (d, out_blk(b, d, s), 0)),
        out_shape=jax.ShapeDtypeStruct((2, nx, hv), F32),
        scratch_shapes=[pltpu.VMEM((n_heads, dqk, dv + LANES), F32),
                        pltpu.VMEM((n_heads, SUBLANES, LANES), F32)],
        compiler_params=_cparams(("arbitrary", "arbitrary", "arbitrary")),
        name="mlstm_scan",
    )(u, kt, u, gates)


def _mix_epilogue(alpha, rs, a, x_ref, g1_ref, sh2_ref, sc2_ref, wout_ref, lng_ref,
                  lnb_ref, rwt_ref, x1_ref, hx2_ref, logt_ref):
    mx = jnp.dot(a, wout_ref[...], preferred_element_type=F32)
    x1 = _layer_norm_rows(alpha * x_ref[rs, :] + g1_ref[0] * mx, lng_ref[...], lnb_ref[...])
    x1_ref[rs, :] = x1
    hx2 = x1 * (1.0 + sc2_ref[0]) + sh2_ref[0]
    hx2_ref[rs, :] = hx2
    logt_ref[:, rs] = lax.dot_general(rwt_ref[...], hx2.astype(BF16), (((1,), (1,)), ((), ())),
                                      preferred_element_type=F32)


def _sub_rows(tm, sub):
    return [slice(r, r + sub) for r in range(0, tm, sub)]


def _conv_out_kernel(alpha, ctx_mode, n_lat, tpb, sub, z_ref, zp_ref, zn_ref, bg_ref, cw_ref,
                     x_ref, g1_ref, sh2_ref, sc2_ref, wout_ref, lng_ref, lnb_ref, rwt_ref, *rest):
    x1_ref, hx2_ref, logt_ref = rest[-3:]
    i = pl.program_id(0)
    tm, d = z_ref.shape
    half = d // 2
    rows = lax.broadcasted_iota(I32, (sub, 1), 0)

    def conv1(zz, w3, period):
        pos = rows & (period - 1)
        prev = pltpu.roll(zz, 1, 0) * (pos != 0).astype(F32)
        nxt = pltpu.roll(zz, sub - 1, 0) * (pos != period - 1).astype(F32)
        return w3[0:1] * prev + w3[1:2] * zz + w3[2:3] * nxt

    def tile():
        cw = cw_ref[...]
        if not ctx_mode:
            ti = i % tpb
            up = zp_ref[...].astype(F32) * (ti > 0).astype(F32)
            dn = zn_ref[...].astype(F32) * (ti < tpb - 1).astype(F32)
            w3 = cw[:, half:]

        for rs in _sub_rows(tm, sub):
            z = z_ref[rs, :].astype(F32)
            bg = bg_ref[rs, :].astype(F32)
            if ctx_mode:
                a = (bg * conv1(z, cw, sub)).astype(BF16)
            else:
                a_row = (bg[:, :half] * conv1(z[:, :half], cw[:, :half], GRID_W)).astype(BF16)
                r0, r1 = rs.start, rs.stop
                f32_rows = lambda a, b: z_ref[a:b, half:].astype(F32)
                prev = (jnp.concatenate([up, f32_rows(0, r1 - GRID_W)], axis=0) if r0 == 0
                        else f32_rows(r0 - GRID_W, r1 - GRID_W))
                nxt = (jnp.concatenate([f32_rows(r0 + GRID_W, tm), dn], axis=0) if r1 == tm
                       else f32_rows(r0 + GRID_W, r1 + GRID_W))
                y = w3[0:1] * prev + w3[1:2] * z[:, half:] + w3[2:3] * nxt
                a = jnp.concatenate([a_row, (bg[:, half:] * y).astype(BF16)], axis=1)
            _mix_epilogue(alpha, rs, a, x_ref, g1_ref, sh2_ref, sc2_ref, wout_ref, lng_ref,
                          lnb_ref, rwt_ref, x1_ref, hx2_ref, logt_ref)

    if ctx_mode:
        tile()
    else:
        pl.when(i < n_lat)(tile)

        @pl.when(i >= n_lat)
        def _():
            x1_ref[...] = jnp.zeros_like(x1_ref)
            hx2_ref[...] = jnp.zeros_like(hx2_ref)
            logt_ref[...] = jnp.zeros_like(logt_ref)


def _ml_out_kernel(alpha, n_heads, dv, sub, hf_ref, hb_ref, og_ref, ng_ref, x_ref,
                   g1_ref, sh2_ref, sc2_ref, wout_ref, lng_ref, lnb_ref, rwt_ref,
                   x1_ref, hx2_ref, logt_ref):
    for rs in _sub_rows(hf_ref.shape[0], sub):
        parts = []
        for h in range(n_heads):
            sl = slice(h * dv, (h + 1) * dv)
            hs = hf_ref[rs, sl] + hb_ref[rs, sl]
            mu = jnp.mean(hs, axis=-1, keepdims=True)
            c = hs - mu
            var = jnp.mean(c * c, axis=-1, keepdims=True)
            hn = c * lax.rsqrt(var + HEAD_NORM_EPS)
            gate = jax.nn.sigmoid(og_ref[rs, sl].astype(F32))
            parts.append((hn * ng_ref[:, sl] * gate).astype(BF16))
        _mix_epilogue(alpha, rs, jnp.concatenate(parts, axis=1), x_ref, g1_ref, sh2_ref,
                      sc2_ref, wout_ref, lng_ref, lnb_ref, rwt_ref, x1_ref, hx2_ref, logt_ref)


def _mix_out_common(d, n_rows, n_exp, mod_row, tile0=0):
    tm = ROW_TILE
    mspec = lambda chunk: pl.BlockSpec((1, 1, d), lambda i: (mod_row(i), 0, chunk))
    const = lambda shape: pl.BlockSpec(shape, lambda i: (0,) * len(shape))
    in_specs = [mspec(2), mspec(3), mspec(4), const((d, d)), const((1, d)), const((1, d)),
                const((n_exp, d))]
    out_specs = [pl.BlockSpec((tm, d), lambda i: (i + tile0, 0)),
                 pl.BlockSpec((tm, d), lambda i: (i + tile0, 0)),
                 pl.BlockSpec((n_exp, tm), lambda i: (0, i + tile0))]
    out_shape = [jax.ShapeDtypeStruct((n_rows, d), F32),
                 jax.ShapeDtypeStruct((n_rows, d), F32),
                 jax.ShapeDtypeStruct((n_exp, n_rows), F32)]
    return in_specs, out_specs, out_shape


def _conv_out(alpha, z, bg, conv_w, x2d, c2d, mod, w_out_b, ln_g, ln_b, rwt_b, n_batch, seq,
              ctx_len):
    n_all, d = z.shape
    nx = x2d.shape[0]
    tm = ROW_TILE
    assert ctx_len == tm and seq % tm == 0 and MIX_SUB_ROWS % GRID_W == 0
    half = d // 2
    nxa = nx // tm
    tpb = seq // tm
    hpt = tm // GRID_W
    n_exp = rwt_b.shape[0]

    def z_specs(tile0):
        return [
            pl.BlockSpec((tm, d), lambda i: (i + tile0, 0)),
            pl.BlockSpec((GRID_W, half), lambda i: (jnp.maximum((i + tile0) * hpt - 1, 0), 1)),
            pl.BlockSpec((GRID_W, half), lambda i: ((i + tile0 + 1) * hpt - 1, 1)),
            pl.BlockSpec((tm, d), lambda i: (i + tile0, 0)),
            pl.BlockSpec((3, d), lambda i: (0, 0)),
            pl.BlockSpec((tm, d), lambda i: (i, 0)),
        ]

    lat = lambda i: jnp.minimum(i, nxa - 1)
    common_in, out_specs, out_shape = _mix_out_common(d, n_all, n_exp, lambda i: lat(i) // tpb)
    lat_specs = z_specs(0)
    lat_specs[2] = pl.BlockSpec((GRID_W, half), lambda i: ((lat(i) + 1) * hpt, 1))
    lat_specs[5] = pl.BlockSpec((tm, d), lambda i: (lat(i), 0))
    outs = pl.pallas_call(
        functools.partial(_conv_out_kernel, alpha, False, nxa, tpb, MIX_SUB_ROWS),
        grid=(n_all // tm,),
        in_specs=lat_specs + common_in,
        out_specs=out_specs,
        out_shape=out_shape,
        compiler_params=_cparams(("arbitrary",)),
        name="conv_out",
    )(z, z, z, bg, conv_w, x2d, mod, mod, mod, w_out_b, ln_g, ln_b, rwt_b)

    common_in, out_specs, out_shape = _mix_out_common(d, n_all, n_exp, lambda i: n_batch, nxa)
    n_in = 6 + len(common_in)
    keep = [pl.BlockSpec(memory_space=pl.ANY)] * 3
    return pl.pallas_call(
        functools.partial(_conv_out_kernel, alpha, True, None, tpb, tm),
        grid=(c2d.shape[0] // tm,),
        in_specs=z_specs(nxa) + common_in + keep,
        out_specs=out_specs,
        out_shape=out_shape,
        input_output_aliases={n_in: 0, n_in + 1: 1, n_in + 2: 2},
        compiler_params=_cparams(("arbitrary",)),
        name="conv_out_ctx",
    )(z, z, z, bg, conv_w, c2d, mod, mod, mod, w_out_b, ln_g, ln_b, rwt_b, *outs)


def _ml_out(alpha, hfb, u, norm_g, xall, mod, w_out_b, ln_g, ln_b, rwt_b, n_batch, seq,
            n_heads, dv):
    nx, d = hfb.shape[1:]
    tm = ROW_TILE
    tpb = seq // tm
    common_in, out_specs, out_shape = _mix_out_common(d, nx, rwt_b.shape[0], lambda i: i // tpb)
    o_blk = 1
    in_specs = [
        pl.BlockSpec((None, tm, d), lambda i: (0, i, 0)),
        pl.BlockSpec((None, tm, d), lambda i: (1, i, 0)),
        pl.BlockSpec((tm, d), lambda i: (i, o_blk)),
        pl.BlockSpec((1, d), lambda i: (0, 0)),
        pl.BlockSpec((tm, d), lambda i: (i, 0)),
    ] + common_in
    return pl.pallas_call(
        functools.partial(_ml_out_kernel, alpha, n_heads, dv, tm),
        grid=(nx // tm,),
        in_specs=in_specs,
        out_specs=out_specs,
        out_shape=out_shape,
        compiler_params=_cparams(("arbitrary",)),
        name="mlstm_out",
    )(hfb, hfb, u, norm_g, xall, mod, mod, mod, w_out_b, ln_g, ln_b, rwt_b)


def _route_kernel(tile_rows, n_te, logt_ref, rb_ref, pos_ref, wcol_ref, te_ref, meta_ref,
                  carry_ref, before_ref):
    ph = pl.program_id(0)
    i = pl.program_id(1)
    n_steps = pl.num_programs(1)
    n_exp, tr = logt_ref.shape
    epg = n_exp // N_GROUPS

    @pl.when((ph == 0) & (i == 0))
    def _():
        carry_ref[...] = jnp.zeros_like(carry_ref)
        ti = lax.broadcasted_iota(I32, (tr, tr), 0)
        tj = lax.broadcasted_iota(I32, (tr, tr), 1)
        before_ref[...] = (ti < tj).astype(BF16)

    s = jax.nn.sigmoid(logt_ref[...])
    sel = s + rb_ref[...]
    row = lax.broadcasted_iota(I32, (n_exp, tr), 0)
    member = row % epg
    group = row // epg

    def partner(x, k, idx, span, unit):
        wrapped = (idx + k) >= span
        up = pltpu.roll(x, n_exp - k * unit, 0)
        down = pltpu.roll(x, (span - k) * unit, 0)
        return jnp.where(wrapped, down, up), wrapped

    rank_in = jnp.zeros((n_exp, tr), F32)
    for k in range(1, epg):
        p, wrapped = partner(sel, k, member, epg, 1)
        beats = (p > sel) | ((p == sel) & wrapped)
        rank_in = rank_in + beats.astype(F32)
    top = (rank_in < TOP_K).astype(F32)
    gs = sel * top
    score = gs
    for k in range(1, epg):
        p, _ = partner(gs, k, member, epg, 1)
        score = score + p
    n_better = jnp.zeros((n_exp, tr), F32)
    for k in range(1, N_GROUPS):
        p, wrapped = partner(score, k, group, N_GROUPS, epg)
        beats = (p > score) | ((p == score) & wrapped)
        n_better = n_better + beats.astype(F32)
    best = n_better == 0.0
    m1 = (best & (rank_in == 0.0)).astype(F32)
    m2 = (best & (rank_in == 1.0)).astype(F32)
    s1 = jnp.sum(s * m1, axis=0, keepdims=True)
    s2 = jnp.sum(s * m2, axis=0, keepdims=True)
    den = s1 + s2
    oh = m1 + m2

    slot = jnp.dot(oh.astype(BF16), before_ref[...], preferred_element_type=F32) + carry_ref[...]

    @pl.when(ph == 1)
    def _():
        pos_ref[0:1, :] = jnp.sum(m1 * slot, axis=0, keepdims=True).astype(I32)
        pos_ref[1:2, :] = jnp.sum(m2 * slot, axis=0, keepdims=True).astype(I32)
        w8 = jnp.concatenate([s1 / den, s2 / den, jnp.zeros((LANES - 2, tr), F32)], axis=0)
        wcol_ref[...] = w8.T

    carry_ref[...] = carry_ref[...] + jnp.sum(oh, axis=1, keepdims=True)

    @pl.when((ph == 0) & (i == n_steps - 1))
    def _():
        cnt = carry_ref[...]
        ntile = jnp.floor((cnt + (tile_rows - 1)) * (1.0 / tile_rows))
        er = lax.broadcasted_iota(I32, (n_exp, LANES), 0)
        el = lax.broadcasted_iota(I32, (n_exp, LANES), 1)
        eye = (er == el).astype(F32)
        nt_row = jnp.sum(ntile * eye, axis=0, keepdims=True)
        cnt_row = jnp.sum(cnt * eye, axis=0, keepdims=True)
        cum_excl = jnp.sum(nt_row * (el < er).astype(F32), axis=1, keepdims=True)
        off = cum_excl * tile_rows
        off_row = jnp.sum(off * eye, axis=0, keepdims=True)
        total = jnp.sum(nt_row, axis=1, keepdims=True)
        cum_incl = cum_excl + ntile
        tl = lax.broadcasted_iota(I32, (n_exp, n_te), 1).astype(F32)
        te = jnp.sum((cum_incl <= tl).astype(F32), axis=0, keepdims=True)
        elf = el.astype(F32)
        later = (el > er) & (nt_row > 0.0) & (el < n_exp)
        nxt = jnp.min(jnp.where(later, elf, float(n_exp)), axis=1, keepdims=True)
        nxt = jnp.where(nxt == float(n_exp), er[:, 0:1].astype(F32), nxt)
        mine = (cum_excl <= tl) & (tl < cum_incl)
        te_next = jnp.sum(jnp.where(mine, nxt, 0.0), axis=0, keepdims=True)
        te_ref[...] = jnp.concatenate([jnp.minimum(te, n_exp - 1), te_next], axis=0).astype(I32)
        meta_ref[...] = jnp.concatenate(
            [cnt_row, off_row, jnp.broadcast_to(total, (1, LANES)),
             jnp.zeros((SUBLANES - 3, LANES), F32)], axis=0).astype(I32)
        carry_ref[...] = off


def _route(logt, router_b, tile_rows):
    n_exp, n = logt.shape
    tr = ROUTE_COLS
    while n % tr:
        tr //= 2
    n_te = 256
    assert (2 * n) // tile_rows + n_exp <= n_te
    return pl.pallas_call(
        functools.partial(_route_kernel, tile_rows, n_te),
        grid=(2, n // tr),
        in_specs=[pl.BlockSpec((n_exp, tr), lambda p, i: (0, i)),
                  pl.BlockSpec((n_exp, 1), lambda p, i: (0, 0))],
        out_specs=[pl.BlockSpec((2, tr), lambda p, i: (0, i * p)),
                   pl.BlockSpec((tr, LANES), lambda p, i: (i * p, 0)),
                   pl.BlockSpec((2, n_te), lambda p, i: (0, 0)),
                   pl.BlockSpec((SUBLANES, LANES), lambda p, i: (0, 0))],
        out_shape=[jax.ShapeDtypeStruct((2, n), I32),
                   jax.ShapeDtypeStruct((n, LANES), F32),
                   jax.ShapeDtypeStruct((2, n_te), I32),
                   jax.ShapeDtypeStruct((SUBLANES, LANES), I32)],
        scratch_shapes=[pltpu.VMEM((n_exp, 1), F32), pltpu.VMEM((tr, tr), BF16)],
        compiler_params=_cparams(("arbitrary", "arbitrary")),
        name="route",
    )(logt, router_b.reshape(n_exp, 1).astype(F32))


def _tbl_kernel(n_tok, n_exp, tile_rows, meta_ref, pos_ref, tbl_ref):
    i = pl.program_id(0)
    blk = pos_ref.shape[1]

    @pl.when(i == 0)
    def _():
        n_pad = 0
        for e in range(n_exp):
            cnt = meta_ref[e]
            off = meta_ref[n_exp + e]
            up = ((cnt + (tile_rows - 1)) // tile_rows) * tile_rows

            def fill(r, j):
                tbl_ref[off + r] = (2 * n_tok + j) << SLOT_SRC_BITS
                return j + 1

            n_pad = lax.fori_loop(cnt, up, fill, n_pad)

        def fill_tail(r, carry):
            tbl_ref[r] = (2 * n_tok) << SLOT_SRC_BITS
            return carry

        lax.fori_loop(2 * n_tok + n_pad, tbl_ref.shape[0], fill_tail, 0)

    base = i * blk
    first = (base << SLOT_SRC_BITS) | jnp.where(base >= n_tok, base - n_tok, base)
    step = (1 << SLOT_SRC_BITS) + 1
    group = 8

    def body(q, carry):
        r0 = q * group
        slots = [pos_ref[0, r0 + j] for j in range(group)]
        for j in range(group):
            tbl_ref[slots[j]] = first + (r0 + j) * step
        return carry

    lax.fori_loop(0, blk // group, body, 0, unroll=True)


def _build_table(meta1d, pos, n_tok, n_exp, tile_rows, p_pad):
    blk = TBL_BLOCK
    while n_tok % blk:
        blk //= 2
    n_pairs = 2 * n_tok
    grid_spec = pltpu.PrefetchScalarGridSpec(
        num_scalar_prefetch=1,
        grid=(n_pairs // blk,),
        in_specs=[pl.BlockSpec((None, 1, blk), lambda i, m: (i, 0, 0), memory_space=pltpu.SMEM)],
        out_specs=pl.BlockSpec(memory_space=pltpu.SMEM),
    )
    return pl.pallas_call(
        functools.partial(_tbl_kernel, n_tok, n_exp, tile_rows),
        grid_spec=grid_spec,
        out_shape=jax.ShapeDtypeStruct((p_pad,), I32),
        compiler_params=_cparams(("arbitrary",)),
        name="slot_table",
    )(meta1d, pos.reshape(n_pairs // blk, 1, blk))


def _moe_kernel(layer, te_ref, ten_ref, nt_ref, tbl_ref, tbln_ref, tblp_ref, hx_hbm,
                wg_hbm, wu_hbm, wd_hbm, y_hbm, xbuf, ybuf, zbuf, wg_b, wu_b, wd_b,
                wg_s, wu_s, wd_s, gsem, ssem, zsem, wsem):
    t = pl.program_id(0)
    n = nt_ref[0]
    rows = tbl_ref.shape[1]
    lt = hx_hbm.shape[1]
    staged = ((wg_hbm, wg_s, wg_b), (wu_hbm, wu_s, wu_b), (wd_hbm, wd_s, wd_b))

    def weight_copies(e):
        return [pltpu.make_async_copy(hbm.at[layer, e], stage, wsem.at[k])
                for k, (hbm, stage, _) in enumerate(staged)]

    @pl.when(t == 0)
    def _():
        for c in weight_copies(te_ref[0]):
            c.start(priority=WEIGHT_DMA_PRIORITY)

    tl = jnp.minimum(t, n - 1)
    new_expert = (t < n) & ((t == 0) | (te_ref[tl] != te_ref[jnp.maximum(tl - 1, 0)]))

    @pl.when(new_expert)
    def _():
        for c, (_, stage, dst) in zip(weight_copies(te_ref[tl]), staged):
            c.wait()
            dst[...] = stage[...].astype(BF16)
        for c in weight_copies(ten_ref[tl]):
            c.start(priority=WEIGHT_DMA_PRIORITY)

    def start_gather(tref, sl, lo=0, hi=None):
        for r in range(lo, rows if hi is None else hi):
            src = tref[0, r] & SLOT_SRC_MASK
            pltpu.make_async_copy(hx_hbm.at[src], xbuf.at[sl, pl.ds(r * lt, lt), :],
                                  gsem.at[sl]).start()

    def start_scatter(tref, sl, lo=0, hi=None):
        for r in range(lo, rows if hi is None else hi):
            dst = tref[0, r] >> SLOT_SRC_BITS
            pltpu.make_async_copy(ybuf.at[sl, pl.ds(r, 1), :],
                                  y_hbm.at[pl.ds(dst, 1), :], ssem.at[sl]
                                  ).start(priority=SCATTER_DMA_PRIORITY)

    def wait_gather(sl):
        pltpu.make_async_copy(xbuf.at[sl], xbuf.at[sl], gsem.at[sl]).wait()

    def wait_scatter(sl):
        pltpu.make_async_copy(ybuf.at[sl], ybuf.at[sl], ssem.at[sl]).wait()

    @pl.when(t == 0)
    def _():
        zbuf[...] = jnp.zeros_like(zbuf)
        start_gather(tbl_ref, 0)

    @pl.when(t >= n)
    def _():
        dst = y_hbm.at[pl.ds(pl.multiple_of(t * rows, rows), rows), :]
        fill = pltpu.make_async_copy(zbuf, dst, zsem)
        fill.start()
        fill.wait()

    def tile(slot, has_prev):
        other = 1 - slot
        wait_gather(slot)
        f = wg_b.shape[1]
        d = wd_b.shape[1]
        fc, dc = MOE_UP_COLS, MOE_DOWN_COLS
        n_up, n_down = f // fc, d // dc
        n_g = MOE_GATHER_GROUPS
        n_s = n_up + n_down - n_g
        g_bounds = [(k * rows) // n_g for k in range(n_g + 1)]
        s_bounds = [(k * rows) // n_s for k in range(n_s + 1)]

        def copy_group(k):
            if k < n_g:
                start_gather(tbln_ref, other, g_bounds[k], g_bounds[k + 1])
            elif has_prev:
                start_scatter(tblp_ref, other, s_bounds[k - n_g], s_bounds[k - n_g + 1])

        xb = jnp.concatenate(
            [xbuf[slot, pl.ds(c, rows, stride=lt), :].astype(BF16) for c in range(lt)], axis=1)
        hs = []
        for c in range(f // fc):
            cols = slice(c * fc, (c + 1) * fc)
            g = jnp.dot(xb, wg_b[:, cols], preferred_element_type=F32)
            u = jnp.dot(xb, wu_b[:, cols], preferred_element_type=F32)
            hs.append((g * jax.nn.sigmoid(g) * u).astype(BF16))
            copy_group(c)
        h = jnp.concatenate(hs, axis=1)

        @pl.when(t >= 2)
        def _():
            wait_scatter(slot)

        for c in range(d // dc):
            cols = slice(c * dc, (c + 1) * dc)
            ybuf[slot, :, cols] = jnp.dot(h, wd_b[:, cols], preferred_element_type=F32)
            copy_group(f // fc + c)

        @pl.when(t == n - 1)
        def _():
            start_scatter(tbl_ref, slot)
            wait_gather(other)

            @pl.when(t >= 1)
            def _():
                wait_scatter(other)

            wait_scatter(slot)
            for c in weight_copies(ten_ref[tl]):
                c.wait()

    pl.when((t < n) & (t == 0))(functools.partial(tile, 0, False))
    for parity in (0, 1):
        pl.when((t < n) & (t > 0) & (lax.rem(t, 2) == parity))(
            functools.partial(tile, parity, True))


def _moe(te2, nt1d, tbl, hx2, w_gate, w_up, w_down, layer, n_tok, p_pad):
    tmm = MOE_ROWS
    _, n_exp, d, f = w_gate.shape
    t_max = (2 * n_tok) // tmm + n_exp
    last = lambda t, nt: jnp.minimum(t, nt[0] - 1)
    tspec = lambda fn: pl.BlockSpec((None, 1, tmm), lambda t, te, ten, nt: (fn(t, nt), 0, 0),
                                    memory_space=pltpu.SMEM)
    hbm = pl.BlockSpec(memory_space=pl.ANY)
    grid_spec = pltpu.PrefetchScalarGridSpec(
        num_scalar_prefetch=3,
        grid=(t_max,),
        in_specs=[
            tspec(last),
            tspec(lambda t, nt: last(t + 1, nt)),
            tspec(lambda t, nt: jnp.maximum(last(t, nt) - 1, 0)),
            hbm, hbm, hbm, hbm,
        ],
        out_specs=hbm,
        scratch_shapes=[pltpu.VMEM((2, tmm * (d // LANES), LANES), F32),
                        pltpu.VMEM((2, tmm, d), F32),
                        pltpu.VMEM((tmm, d), F32),
                        pltpu.VMEM((d, f), BF16), pltpu.VMEM((d, f), BF16),
                        pltpu.VMEM((f, d), BF16),
                        pltpu.VMEM((d, f), F32), pltpu.VMEM((d, f), F32),
                        pltpu.VMEM((f, d), F32),
                        pltpu.SemaphoreType.DMA((2,)), pltpu.SemaphoreType.DMA((2,)),
                        pltpu.SemaphoreType.DMA(()), pltpu.SemaphoreType.DMA((3,))],
    )
    tbl3 = tbl.reshape(p_pad // tmm, 1, tmm)
    return pl.pallas_call(
        functools.partial(_moe_kernel, layer),
        grid_spec=grid_spec,
        out_shape=jax.ShapeDtypeStruct((t_max * tmm, d), F32),
        compiler_params=_cparams(("arbitrary",)),
        name="moe_experts",
    )(te2[0], te2[1], nt1d, tbl3, tbl3, tbl3, hx2.reshape(n_tok, d // LANES, LANES),
      w_gate, w_up, w_down)


def _comb_kernel(alpha, x_ref, y0_ref, y1_ref, w_ref, g2_ref, lng_ref, lnb_ref, o_ref):
    w = w_ref[...]
    ex = w[:, 0:1] * y0_ref[...] + w[:, 1:2] * y1_ref[...]
    o_ref[...] = _layer_norm_rows(alpha * x_ref[...] + g2_ref[0] * ex, lng_ref[...], lnb_ref[...])


def _combine(alpha, x1, y, wcol, mod, ln_g, ln_b, n_tok, n_out, n_batch, seq):
    d = x1.shape[1]
    tm = COMB_ROWS
    assert n_tok % tm == 0 and n_out % tm == 0 and seq % tm == 0
    nblk = n_tok // tm
    mspec = functools.partial(_mod_spec, d=d, tiles_per_batch=seq // tm, n_batch=n_batch)
    return pl.pallas_call(
        functools.partial(_comb_kernel, alpha),
        grid=(n_out // tm,),
        in_specs=[
            pl.BlockSpec((tm, d), lambda i: (i, 0)),
            pl.BlockSpec((tm, d), lambda i: (i, 0)),
            pl.BlockSpec((tm, d), lambda i: (i + nblk, 0)),
            pl.BlockSpec((tm, LANES), lambda i: (i, 0)),
            mspec(5),
            pl.BlockSpec((1, d), lambda i: (0, 0)),
            pl.BlockSpec((1, d), lambda i: (0, 0)),
        ],
        out_specs=pl.BlockSpec((tm, d), lambda i: (i, 0)),
        out_shape=jax.ShapeDtypeStruct((n_out, d), F32),
        compiler_params=_cparams(("arbitrary",)),
        name="moe_combine",
    )(x1, y, y, wcol, mod, ln_g, ln_b)


def _moe_layer(alpha, x1, hx2, logt, mod, ln_g, ln_b, router_b, w_gate, w_up, w_down, layer,
               n_out, n_batch, seq):
    n_tok = hx2.shape[0]
    n_exp = w_gate.shape[1]
    tmm = MOE_ROWS
    p_max = 2 * n_tok + n_exp * tmm
    p_pad = -(-p_max // TBL_BLOCK) * TBL_BLOCK
    assert n_tok <= 1 << SLOT_SRC_BITS and p_max < 1 << (31 - SLOT_SRC_BITS)
    pos, wcol, te, meta = _route(logt, router_b, tmm)
    meta1d = meta[:2, :n_exp].reshape(-1)
    tbl = _build_table(meta1d, pos, n_tok, n_exp, tmm, p_pad)
    y = _moe(te, meta[2, :1], tbl, hx2, w_gate, w_up, w_down, layer, n_tok, p_pad)
    return _combine(alpha, x1, y, wcol, mod, ln_g, ln_b, n_tok, n_out, n_batch, seq)


def kernel(x, c, ctx, c_ctx, w_ada, b_ada, ln_g, ln_b, conv_w_in, conv_w, conv_w_out, ml_w_in, ml_w_gate, ml_b_gate, ml_norm_g, ml_w_out, router_w, router_b, exp_w_gate, exp_w_up, exp_w_down):
    n_batch, seq, d = x.shape
    ctx_len = ctx.shape[1]
    depth = w_ada.shape[0]
    assert depth == 2, "layer 0 is the conv mixer, layer 1 the mLSTM mixer"
    alpha = (2 * depth) ** 0.25
    n_heads = ml_b_gate.shape[-1] // 4
    dqk = d // (2 * n_heads)
    dv = d // n_heads
    nx = n_batch * seq
    nc = n_batch * ctx_len
    assert n_batch < SUBLANES and 2 * n_heads <= LANES

    x2d = x.reshape(nx, d)
    c2d = ctx.reshape(nc, d)
    cc = jnp.zeros((SUBLANES, d), F32).at[:n_batch].set(c).at[n_batch].set(c_ctx)
    mod = _ada(cc, w_ada, b_ada).reshape(depth, SUBLANES, 1, 6 * d)
    rwt_b = router_w.T.astype(BF16)

    bg, z = _conv_in(x2d, c2d, mod[0], _to_bf16(conv_w_in, 0), n_batch, seq)
    x1, hx2, logt = _conv_out(alpha, z, bg, conv_w[0], x2d, c2d, mod[0],
                              _to_bf16(conv_w_out, 0), ln_g[0, 0:1], ln_b[0, 0:1], rwt_b,
                              n_batch, seq, ctx_len)
    xall = _moe_layer(alpha, x1, hx2, logt, mod[0], ln_g[0, 1:2], ln_b[0, 1:2], router_b,
                      exp_w_gate, exp_w_up, exp_w_down, 0, nx + nc, n_batch, seq)

    w_gate = ml_w_gate[0]
    b_gate = ml_b_gate[0]
    h2 = 2 * n_heads
    w_gate_pad = (jnp.zeros((d, 2 * LANES), F32).at[:, :h2].set(w_gate[:, :h2])
                  .at[:, LANES:LANES + h2].set(w_gate[:, h2:])).astype(BF16)
    b_gate_pad = (jnp.zeros((1, 2 * LANES), F32).at[0, :h2].set(b_gate[:h2])
                  .at[0, LANES:LANES + h2].set(b_gate[h2:]))
    w_voq_b, w_kt_b = _ml_weights(ml_w_in, 0, n_heads * dqk)
    u, kt, gates = _ml_in(xall, mod[1], w_voq_b, w_kt_b, w_gate_pad, b_gate_pad, n_batch,
                          seq, nc)
    hfb = _scan(u, kt, gates, n_batch, seq, ctx_len, n_heads, dqk, dv)
    x1, hx2, logt = _ml_out(alpha, hfb, u, ml_norm_g[0:1], xall, mod[1],
                            _to_bf16(ml_w_out, 0), ln_g[1, 0:1], ln_b[1, 0:1], rwt_b,
                            n_batch, seq, n_heads, dv)
    out = _moe_layer(alpha, x1, hx2, logt, mod[1], ln_g[1, 1:2], ln_b[1, 1:2], router_b,
                     exp_w_gate, exp_w_up, exp_w_down, 1, nx, n_batch, seq)
    return out.reshape(n_batch, seq, d)
```
